```python
import jax, jax.numpy as jnp
from jax import lax
import numpy as np

D_MODEL = 1024
BATCH = 8
SEQ = 16384
DEPTH = 2

N_A = DEPTH // 2
N_B = DEPTH - N_A
CHUNK = 64
LEFT_CHUNKS = 8
BAND = (LEFT_CHUNKS + 1) * CHUNK
PAD = LEFT_CHUNKS * CHUNK
D_RNN = D_MODEL
LRU_BLOCKS = 8
LRU_BW = D_RNN // LRU_BLOCKS
CONV_W = 4
LRU_C = 8.0
N_HEADS = 16
HEAD_DIM = D_MODEL // N_HEADS
MAX_REL = 2 * CHUNK
MIN_REL = -(CHUNK - 1)
NREL = MAX_REL - MIN_REL + 1
D_FF = 4 * D_MODEL
EPS = 1e-6

kernel_name = 'yoco_rglru_chunk_relbias_hybrid'


def rmsnorm(x, g):
    xf = x.astype(jnp.float32)
    y = xf * lax.rsqrt(jnp.mean(xf * xf, axis=-1, keepdims=True) + EPS)
    return (y * g.astype(jnp.float32)).astype(x.dtype)


def causal_depthwise_conv(x, w, b):
    s = x.shape[1]
    xp = jnp.pad(x, ((0, 0), (CONV_W - 1, 0), (0, 0)))
    out = b + xp[:, 0:s] * w[0]
    for k in range(1, CONV_W):
        out = out + xp[:, k:k + s] * w[k]
    return out


def rg_lru(x, w_gate, b_gate, lam):
    bsz, s, _ = x.shape
    xf = x.astype(jnp.float32)
    xb = xf.reshape(bsz, s, LRU_BLOCKS, LRU_BW)
    g = jnp.einsum('bsnd,nde->bsne', xb, w_gate.astype(jnp.float32)) + b_gate.astype(jnp.float32)
    r = jax.nn.sigmoid(g[..., :LRU_BW]).reshape(bsz, s, D_RNN)
    i = jax.nn.sigmoid(g[..., LRU_BW:]).reshape(bsz, s, D_RNN)
    log_a = -LRU_C * r * jax.nn.softplus(-lam.astype(jnp.float32))
    a = jnp.exp(log_a)
    b = jnp.sqrt(-jnp.expm1(2.0 * log_a)) * (i * xf)

    def combine(left, right):
        a1, b1 = left
        a2, b2 = right
        return a1 * a2, a2 * b1 + b2

    _, h = lax.associative_scan(combine, (a, b), axis=1)
    return h.astype(x.dtype)


def recurrent_block(x, w_in, conv_w, conv_b, w_gate, b_gate, lam, w_out):
    u = x @ w_in
    gate, rec = u[..., :D_RNN], u[..., D_RNN:]
    rec = causal_depthwise_conv(rec, conv_w, conv_b)
    h = rg_lru(rec, w_gate, b_gate, lam)
    return (jax.nn.gelu(gate) * h) @ w_out


def shared_kv(x, kv_norm, w_kv, k_norm):
    bsz, s, _ = x.shape
    h = rmsnorm(x, kv_norm)
    kv = (h @ w_kv).reshape(bsz, s, 2, N_HEADS, HEAD_DIM)
    k = rmsnorm(kv[:, :, 0], k_norm)
    v = kv[:, :, 1]
    k = jnp.pad(k.transpose(0, 2, 1, 3), ((0, 0), (0, 0), (PAD, 0), (0, 0)))
    v = jnp.pad(v.transpose(0, 2, 1, 3), ((0, 0), (0, 0), (PAD, 0), (0, 0)))
    return k, v


def chunk_band_attention(q, k_pad, v_pad, rel_bias):
    bsz, s = q.shape[:2]
    nc = s // CHUNK
    qc = q.reshape(bsz, nc, CHUNK, N_HEADS, HEAD_DIM).transpose(1, 0, 3, 2, 4)
    qi = jnp.arange(CHUNK)[:, None]
    kj = jnp.arange(BAND)[None, :]
    dist = qi + PAD - kj
    idx = jnp.clip(dist, MIN_REL, MAX_REL) - MIN_REL
    bias = rel_bias.astype(jnp.float32)[:, idx]
    scale = HEAD_DIM ** -0.5

    def one_chunk(args):
        c, qb = args
        kb = lax.dynamic_slice_in_dim(k_pad, c * CHUNK, BAND, axis=2)
        vb = lax.dynamic_slice_in_dim(v_pad, c * CHUNK, BAND, axis=2)
        sc = jnp.einsum('bhqd,bhkd->bhqk', qb, kb).astype(jnp.float32) * scale + bias
        valid = (c * CHUNK - PAD + jnp.arange(BAND)) >= 0
        sc = jnp.where(valid, sc, -jnp.inf)
        p = jax.nn.softmax(sc, axis=-1).astype(vb.dtype)
        return jnp.einsum('bhqk,bhkd->bhqd', p, vb)

    o = lax.map(one_chunk, (jnp.arange(nc), qc))
    return o.transpose(1, 0, 3, 2, 4).reshape(bsz, s, N_HEADS * HEAD_DIM)


def sqrelu_mlp(x, w_up, w_down):
    return jnp.square(jax.nn.relu(x @ w_up)) @ w_down


def _fwd_setup_inputs(seed: int = 0) -> dict:
    key = jax.random.key(seed)
    ks = jax.random.split(key, 24)
    f32 = jnp.float32

    def nrm(k, shape, fan_in):
        return jax.random.normal(k, shape, f32) * (fan_in ** -0.5)

    def gain(k, shape):
        return 1.0 + 0.05 * jax.random.normal(k, shape, f32)

    x = jax.random.normal(ks[0], (BATCH, SEQ, D_MODEL), f32)
    a_norm = gain(ks[1], (N_A, D_MODEL))
    a_w_in = nrm(ks[2], (N_A, D_MODEL, 2 * D_RNN), D_MODEL)
    a_conv_w = nrm(ks[3], (N_A, CONV_W, D_RNN), CONV_W)
    a_conv_b = 0.01 * jax.random.normal(ks[4], (N_A, D_RNN), f32)
    a_w_gate = nrm(ks[5], (N_A, LRU_BLOCKS, LRU_BW, 2 * LRU_BW), LRU_BW)
    a_b_gate = 0.01 * jax.random.normal(ks[6], (N_A, LRU_BLOCKS, 2 * LRU_BW), f32)
    u = jax.random.uniform(ks[7], (N_A, D_RNN), f32, 0.9, 0.999)
    base = u ** (1.0 / LRU_C)
    a_lambda = jnp.log(base) - jnp.log1p(-base)
    a_w_out = nrm(ks[8], (N_A, D_RNN, D_MODEL), D_RNN)
    kv_norm = gain(ks[9], (D_MODEL,))
    w_kv = nrm(ks[10], (D_MODEL, 2 * N_HEADS * HEAD_DIM), D_MODEL)
    k_norm = gain(ks[11], (HEAD_DIM,))
    b_norm = gain(ks[12], (N_B, D_MODEL))
    b_w_q = nrm(ks[13], (N_B, D_MODEL, N_HEADS * HEAD_DIM), D_MODEL)
    b_q_norm = gain(ks[14], (N_B, HEAD_DIM))
    b_rel_bias = 0.1 * jax.random.normal(ks[15], (N_B, N_HEADS, NREL), f32)
    b_w_o = nrm(ks[16], (N_B, N_HEADS * HEAD_DIM, D_MODEL), N_HEADS * HEAD_DIM)
    mlp_norm = gain(ks[17], (DEPTH, D_MODEL))
    w_up = nrm(ks[18], (DEPTH, D_MODEL, D_FF), D_MODEL)
    w_down = nrm(ks[19], (DEPTH, D_FF, D_MODEL), D_FF)
    return {'x': x, 'a_norm': a_norm, 'a_w_in': a_w_in, 'a_conv_w': a_conv_w,
            'a_conv_b': a_conv_b, 'a_w_gate': a_w_gate, 'a_b_gate': a_b_gate,
            'a_lambda': a_lambda, 'a_w_out': a_w_out, 'kv_norm': kv_norm, 'w_kv': w_kv,
            'k_norm': k_norm, 'b_norm': b_norm, 'b_w_q': b_w_q, 'b_q_norm': b_q_norm,
            'b_rel_bias': b_rel_bias, 'b_w_o': b_w_o, 'mlp_norm': mlp_norm,
            'w_up': w_up, 'w_down': w_down}


def _fwd_reference(x, a_norm, a_w_in, a_conv_w, a_conv_b, a_w_gate, a_b_gate, a_lambda,
              a_w_out, kv_norm, w_kv, k_norm, b_norm, b_w_q, b_q_norm, b_rel_bias,
              b_w_o, mlp_norm, w_up, w_down):
    bsz, s, _ = x.shape
    h = x
    k_pad = None
    v_pad = None
    for l in range(DEPTH):
        if l < N_A:
            h = h + recurrent_block(rmsnorm(h, a_norm[l]), a_w_in[l], a_conv_w[l], a_conv_b[l],
                                    a_w_gate[l], a_b_gate[l], a_lambda[l], a_w_out[l])
        else:
            if l == N_A:
                k_pad, v_pad = shared_kv(h, kv_norm, w_kv, k_norm)
            j = l - N_A
            q = (rmsnorm(h, b_norm[j]) @ b_w_q[j]).reshape(bsz, s, N_HEADS, HEAD_DIM)
            q = rmsnorm(q, b_q_norm[j])
            o = chunk_band_attention(q, k_pad, v_pad, b_rel_bias[j])
            h = h + o @ b_w_o[j]
        h = h + sqrelu_mlp(rmsnorm(h, mlp_norm[l]), w_up[l], w_down[l])
    return h


import jax as _jax
import jax.numpy as _jnp

TWIN_FORMAT = 'train_step'
FWD_PARAMS = ['x', 'a_norm', 'a_w_in', 'a_conv_w', 'a_conv_b', 'a_w_gate', 'a_b_gate', 'a_lambda', 'a_w_out', 'kv_norm', 'w_kv', 'k_norm', 'b_norm', 'b_w_q', 'b_q_norm', 'b_rel_bias', 'b_w_o', 'mlp_norm', 'w_up', 'w_down']
TWIN_WEIGHTS = ['a_norm', 'a_w_in', 'a_conv_w', 'a_conv_b', 'a_w_gate', 'a_b_gate', 'a_lambda', 'a_w_out', 'kv_norm', 'w_kv', 'k_norm', 'b_norm', 'b_w_q', 'b_q_norm', 'b_rel_bias', 'b_w_o', 'mlp_norm', 'w_up', 'w_down']
TWIN_DIFF_INPUT = 'x'
TWIN_INPUTS = ['x', 'a_norm', 'a_w_in', 'a_conv_w', 'a_conv_b', 'a_w_gate', 'a_b_gate', 'a_lambda', 'a_w_out', 'kv_norm', 'w_kv', 'k_norm', 'b_norm', 'b_w_q', 'b_q_norm', 'b_rel_bias', 'b_w_o', 'mlp_norm', 'w_up', 'w_down', 'loss_target', 'm_a_norm', 'm_a_w_in', 'm_a_conv_w', 'm_a_conv_b', 'm_a_w_gate', 'm_a_b_gate', 'm_a_lambda', 'm_a_w_out', 'm_kv_norm', 'm_w_kv', 'm_k_norm', 'm_b_norm', 'm_b_w_q', 'm_b_q_norm', 'm_b_rel_bias', 'm_b_w_o', 'm_mlp_norm', 'm_w_up', 'm_w_down', 'v_a_norm', 'v_a_w_in', 'v_a_conv_w', 'v_a_conv_b', 'v_a_w_gate', 'v_a_b_gate', 'v_a_lambda', 'v_a_w_out', 'v_kv_norm', 'v_w_kv', 'v_k_norm', 'v_b_norm', 'v_b_w_q', 'v_b_q_norm', 'v_b_rel_bias', 'v_b_w_o', 'v_mlp_norm', 'v_w_up', 'v_w_down']
TWIN_OUTPUTS = ['loss', 'grad_x', 'grad_a_norm', 'grad_a_w_in', 'grad_a_conv_w', 'grad_a_conv_b', 'grad_a_w_gate', 'grad_a_b_gate', 'grad_a_lambda', 'grad_a_w_out', 'grad_kv_norm', 'grad_w_kv', 'grad_k_norm', 'grad_b_norm', 'grad_b_w_q', 'grad_b_q_norm', 'grad_b_rel_bias', 'grad_b_w_o', 'grad_mlp_norm', 'grad_w_up', 'grad_w_down', 'delta_a_norm', 'delta_a_w_in', 'delta_a_conv_w', 'delta_a_conv_b', 'delta_a_w_gate', 'delta_a_b_gate', 'delta_a_lambda', 'delta_a_w_out', 'delta_kv_norm', 'delta_w_kv', 'delta_k_norm', 'delta_b_norm', 'delta_b_w_q', 'delta_b_q_norm', 'delta_b_rel_bias', 'delta_b_w_o', 'delta_mlp_norm', 'delta_w_up', 'delta_w_down', 'new_m_a_norm', 'new_m_a_w_in', 'new_m_a_conv_w', 'new_m_a_conv_b', 'new_m_a_w_gate', 'new_m_a_b_gate', 'new_m_a_lambda', 'new_m_a_w_out', 'new_m_kv_norm', 'new_m_w_kv', 'new_m_k_norm', 'new_m_b_norm', 'new_m_b_w_q', 'new_m_b_q_norm', 'new_m_b_rel_bias', 'new_m_b_w_o', 'new_m_mlp_norm', 'new_m_w_up', 'new_m_w_down', 'new_v_a_norm', 'new_v_a_w_in', 'new_v_a_conv_w', 'new_v_a_conv_b', 'new_v_a_w_gate', 'new_v_a_b_gate', 'new_v_a_lambda', 'new_v_a_w_out', 'new_v_kv_norm', 'new_v_w_kv', 'new_v_k_norm', 'new_v_b_norm', 'new_v_b_w_q', 'new_v_b_q_norm', 'new_v_b_rel_bias', 'new_v_b_w_o', 'new_v_mlp_norm', 'new_v_w_up', 'new_v_w_down']
TWIN_LEAF_KINDS = {'loss': 'loss', 'grad_x': 'grad_x', 'grad_a_norm': 'grad_w', 'grad_a_w_in': 'grad_w', 'grad_a_conv_w': 'grad_w', 'grad_a_conv_b': 'grad_w', 'grad_a_w_gate': 'grad_w', 'grad_a_b_gate': 'grad_w', 'grad_a_lambda': 'grad_w', 'grad_a_w_out': 'grad_w', 'grad_kv_norm': 'grad_w', 'grad_w_kv': 'grad_w', 'grad_k_norm': 'grad_w', 'grad_b_norm': 'grad_w', 'grad_b_w_q': 'grad_w', 'grad_b_q_norm': 'grad_w', 'grad_b_rel_bias': 'grad_w', 'grad_b_w_o': 'grad_w', 'grad_mlp_norm': 'grad_w', 'grad_w_up': 'grad_w', 'grad_w_down': 'grad_w', 'delta_a_norm': 'delta_w', 'delta_a_w_in': 'delta_w', 'delta_a_conv_w': 'delta_w', 'delta_a_conv_b': 'delta_w', 'delta_a_w_gate': 'delta_w', 'delta_a_b_gate': 'delta_w', 'delta_a_lambda': 'delta_w', 'delta_a_w_out': 'delta_w', 'delta_kv_norm': 'delta_w', 'delta_w_kv': 'delta_w', 'delta_k_norm': 'delta_w', 'delta_b_norm': 'delta_w', 'delta_b_w_q': 'delta_w', 'delta_b_q_norm': 'delta_w', 'delta_b_rel_bias': 'delta_w', 'delta_b_w_o': 'delta_w', 'delta_mlp_norm': 'delta_w', 'delta_w_up': 'delta_w', 'delta_w_down': 'delta_w', 'new_m_a_norm': 'new_m', 'new_m_a_w_in': 'new_m', 'new_m_a_conv_w': 'new_m', 'new_m_a_conv_b': 'new_m', 'new_m_a_w_gate': 'new_m', 'new_m_a_b_gate': 'new_m', 'new_m_a_lambda': 'new_m', 'new_m_a_w_out': 'new_m', 'new_m_kv_norm': 'new_m', 'new_m_w_kv': 'new_m', 'new_m_k_norm': 'new_m', 'new_m_b_norm': 'new_m', 'new_m_b_w_q': 'new_m', 'new_m_b_q_norm': 'new_m', 'new_m_b_rel_bias': 'new_m', 'new_m_b_w_o': 'new_m', 'new_m_mlp_norm': 'new_m', 'new_m_w_up': 'new_m', 'new_m_w_down': 'new_m', 'new_v_a_norm': 'new_v', 'new_v_a_w_in': 'new_v', 'new_v_a_conv_w': 'new_v', 'new_v_a_conv_b': 'new_v', 'new_v_a_w_gate': 'new_v', 'new_v_a_b_gate': 'new_v', 'new_v_a_lambda': 'new_v', 'new_v_a_w_out': 'new_v', 'new_v_kv_norm': 'new_v', 'new_v_w_kv': 'new_v', 'new_v_k_norm': 'new_v', 'new_v_b_norm': 'new_v', 'new_v_b_w_q': 'new_v', 'new_v_b_q_norm': 'new_v', 'new_v_b_rel_bias': 'new_v', 'new_v_b_w_o': 'new_v', 'new_v_mlp_norm': 'new_v', 'new_v_w_up': 'new_v', 'new_v_w_down': 'new_v'}


def _forward(args):
    return _fwd_reference(*[args[k] for k in FWD_PARAMS])


def _output_shape():
    def fwd():
        inp = _fwd_setup_inputs(0)
        return _fwd_reference(*[inp[k] for k in FWD_PARAMS])
    out = _jax.eval_shape(fwd)
    return out.shape, out.dtype

N_MICROBATCH = 1
ADAM_LR = 0.001
ADAM_B1 = 0.9
ADAM_B2 = 0.999
ADAM_EPS = 1e-08
ADAM_WD = 0.01
ADAM_STEP = 10
PER_EXAMPLE_BATCH_AXIS = {'x': 0, 'loss_target': 0}
SHARED_INPUTS = []
_WEIGHT_DTYPES = {'a_norm': _jnp.float32, 'a_w_in': _jnp.float32, 'a_conv_w': _jnp.float32, 'a_conv_b': _jnp.float32, 'a_w_gate': _jnp.float32, 'a_b_gate': _jnp.float32, 'a_lambda': _jnp.float32, 'a_w_out': _jnp.float32, 'kv_norm': _jnp.float32, 'w_kv': _jnp.float32, 'k_norm': _jnp.float32, 'b_norm': _jnp.float32, 'b_w_q': _jnp.float32, 'b_q_norm': _jnp.float32, 'b_rel_bias': _jnp.float32, 'b_w_o': _jnp.float32, 'mlp_norm': _jnp.float32, 'w_up': _jnp.float32, 'w_down': _jnp.float32}
MOMENT_SCALE = {'a_norm': 3.883912e+01, 'a_w_in': 2.789165e+00, 'a_conv_w': 2.040124e+01, 'a_conv_b': 2.610041e+02, 'a_w_gate': 9.919072e+00, 'a_b_gate': 7.186714e+00, 'a_lambda': 5.296945e+00, 'a_w_out': 1.545591e+01, 'kv_norm': 5.260849e+01, 'w_kv': 3.356478e+01, 'k_norm': 4.102211e+00, 'b_norm': 8.690143e-01, 'b_w_q': 9.919424e-01, 'b_q_norm': 4.053527e+00, 'b_rel_bias': 1.009612e-01, 'b_w_o': 4.411928e+01, 'mlp_norm': 3.921067e+02, 'w_up': 2.015841e+01, 'w_down': 7.877620e+01}


def _to_microbatches(a, axis):
    t = _jnp.moveaxis(a, axis, 0)
    t = t.reshape((N_MICROBATCH, t.shape[0] // N_MICROBATCH) + t.shape[1:])
    return _jnp.moveaxis(t, 1, axis + 1)


def setup_inputs(seed: int = 0) -> dict:
    inp = _fwd_setup_inputs(seed)
    key = _jax.random.fold_in(_jax.random.key(seed), 7919)
    shape, _ = _output_shape()
    out = dict(inp)
    out["loss_target"] = _jax.random.normal(_jax.random.fold_in(key, 0), shape, _jnp.float32)
    for i, name in enumerate(TWIN_WEIGHTS):
        w = inp[name].astype(_jnp.float32)
        if MOMENT_SCALE is None:
            s = _jnp.sqrt(_jnp.mean(_jnp.square(w)) + 1e-30)
        else:
            s = MOMENT_SCALE[name]
        km, kv = _jax.random.split(_jax.random.fold_in(key, i + 1))
        out[name] = w
        out["m_" + name] = s * _jax.random.normal(km, w.shape, _jnp.float32)
        out["v_" + name] = (s * s) * _jax.random.uniform(kv, w.shape, _jnp.float32, 0.5, 1.5)
    if N_MICROBATCH > 1:
        for name, axis in PER_EXAMPLE_BATCH_AXIS.items():
            out[name] = _to_microbatches(out[name], axis)
    return {'x': out['x'], 'a_norm': out['a_norm'], 'a_w_in': out['a_w_in'], 'a_conv_w': out['a_conv_w'], 'a_conv_b': out['a_conv_b'], 'a_w_gate': out['a_w_gate'], 'a_b_gate': out['a_b_gate'], 'a_lambda': out['a_lambda'], 'a_w_out': out['a_w_out'], 'kv_norm': out['kv_norm'], 'w_kv': out['w_kv'], 'k_norm': out['k_norm'], 'b_norm': out['b_norm'], 'b_w_q': out['b_w_q'], 'b_q_norm': out['b_q_norm'], 'b_rel_bias': out['b_rel_bias'], 'b_w_o': out['b_w_o'], 'mlp_norm': out['mlp_norm'], 'w_up': out['w_up'], 'w_down': out['w_down'], 'loss_target': out['loss_target'], 'm_a_norm': out['m_a_norm'], 'm_a_w_in': out['m_a_w_in'], 'm_a_conv_w': out['m_a_conv_w'], 'm_a_conv_b': out['m_a_conv_b'], 'm_a_w_gate': out['m_a_w_gate'], 'm_a_b_gate': out['m_a_b_gate'], 'm_a_lambda': out['m_a_lambda'], 'm_a_w_out': out['m_a_w_out'], 'm_kv_norm': out['m_kv_norm'], 'm_w_kv': out['m_w_kv'], 'm_k_norm': out['m_k_norm'], 'm_b_norm': out['m_b_norm'], 'm_b_w_q': out['m_b_w_q'], 'm_b_q_norm': out['m_b_q_norm'], 'm_b_rel_bias': out['m_b_rel_bias'], 'm_b_w_o': out['m_b_w_o'], 'm_mlp_norm': out['m_mlp_norm'], 'm_w_up': out['m_w_up'], 'm_w_down': out['m_w_down'], 'v_a_norm': out['v_a_norm'], 'v_a_w_in': out['v_a_w_in'], 'v_a_conv_w': out['v_a_conv_w'], 'v_a_conv_b': out['v_a_conv_b'], 'v_a_w_gate': out['v_a_w_gate'], 'v_a_b_gate': out['v_a_b_gate'], 'v_a_lambda': out['v_a_lambda'], 'v_a_w_out': out['v_a_w_out'], 'v_kv_norm': out['v_kv_norm'], 'v_w_kv': out['v_w_kv'], 'v_k_norm': out['v_k_norm'], 'v_b_norm': out['v_b_norm'], 'v_b_w_q': out['v_b_w_q'], 'v_b_q_norm': out['v_b_q_norm'], 'v_b_rel_bias': out['v_b_rel_bias'], 'v_b_w_o': out['v_b_w_o'], 'v_mlp_norm': out['v_mlp_norm'], 'v_w_up': out['v_w_up'], 'v_w_down': out['v_w_down']}


def _loss(weights, diff, rest, loss_target):
    with _jax.named_scope("forward"):
        args = {**rest, TWIN_DIFF_INPUT: diff, **{k: w.astype(_WEIGHT_DTYPES[k]) for k, w in weights.items()}}
        y = _forward(args)
    with _jax.named_scope("loss_head"):
        err = _jnp.square(y.astype(_jnp.float32) - loss_target)
        return 0.5 * _jnp.sum(_jnp.mean(err, axis=-1)) if err.ndim else 0.5 * err


def _adamw(w, g, m, v):
    m = ADAM_B1 * m + (1.0 - ADAM_B1) * g
    v = ADAM_B2 * v + (1.0 - ADAM_B2) * _jnp.square(g)
    m_hat = m / (1.0 - ADAM_B1 ** ADAM_STEP)
    v_hat = v / (1.0 - ADAM_B2 ** ADAM_STEP)
    delta = -ADAM_LR * (m_hat / (_jnp.sqrt(v_hat) + ADAM_EPS) + ADAM_WD * w)
    return delta, m, v


def reference(x, a_norm, a_w_in, a_conv_w, a_conv_b, a_w_gate, a_b_gate, a_lambda, a_w_out, kv_norm, w_kv, k_norm, b_norm, b_w_q, b_q_norm, b_rel_bias, b_w_o, mlp_norm, w_up, w_down, loss_target, m_a_norm, m_a_w_in, m_a_conv_w, m_a_conv_b, m_a_w_gate, m_a_b_gate, m_a_lambda, m_a_w_out, m_kv_norm, m_w_kv, m_k_norm, m_b_norm, m_b_w_q, m_b_q_norm, m_b_rel_bias, m_b_w_o, m_mlp_norm, m_w_up, m_w_down, v_a_norm, v_a_w_in, v_a_conv_w, v_a_conv_b, v_a_w_gate, v_a_b_gate, v_a_lambda, v_a_w_out, v_kv_norm, v_w_kv, v_k_norm, v_b_norm, v_b_w_q, v_b_q_norm, v_b_rel_bias, v_b_w_o, v_mlp_norm, v_w_up, v_w_down):
    given = dict(x=x, a_norm=a_norm, a_w_in=a_w_in, a_conv_w=a_conv_w, a_conv_b=a_conv_b, a_w_gate=a_w_gate, a_b_gate=a_b_gate, a_lambda=a_lambda, a_w_out=a_w_out, kv_norm=kv_norm, w_kv=w_kv, k_norm=k_norm, b_norm=b_norm, b_w_q=b_w_q, b_q_norm=b_q_norm, b_rel_bias=b_rel_bias, b_w_o=b_w_o, mlp_norm=mlp_norm, w_up=w_up, w_down=w_down, loss_target=loss_target, m_a_norm=m_a_norm, m_a_w_in=m_a_w_in, m_a_conv_w=m_a_conv_w, m_a_conv_b=m_a_conv_b, m_a_w_gate=m_a_w_gate, m_a_b_gate=m_a_b_gate, m_a_lambda=m_a_lambda, m_a_w_out=m_a_w_out, m_kv_norm=m_kv_norm, m_w_kv=m_w_kv, m_k_norm=m_k_norm, m_b_norm=m_b_norm, m_b_w_q=m_b_w_q, m_b_q_norm=m_b_q_norm, m_b_rel_bias=m_b_rel_bias, m_b_w_o=m_b_w_o, m_mlp_norm=m_mlp_norm, m_w_up=m_w_up, m_w_down=m_w_down, v_a_norm=v_a_norm, v_a_w_in=v_a_w_in, v_a_conv_w=v_a_conv_w, v_a_conv_b=v_a_conv_b, v_a_w_gate=v_a_w_gate, v_a_b_gate=v_a_b_gate, v_a_lambda=v_a_lambda, v_a_w_out=v_a_w_out, v_kv_norm=v_kv_norm, v_w_kv=v_w_kv, v_k_norm=v_k_norm, v_b_norm=v_b_norm, v_b_w_q=v_b_w_q, v_b_q_norm=v_b_q_norm, v_b_rel_bias=v_b_rel_bias, v_b_w_o=v_b_w_o, v_mlp_norm=v_mlp_norm, v_w_up=v_w_up, v_w_down=v_w_down)
    weights = {n: given[n] for n in TWIN_WEIGHTS}
    shared = {n: given[n] for n in SHARED_INPUTS}
    per_example = {n: given[n] for n in ['x']}
    grad_fn = _jax.value_and_grad(_loss, argnums=(0, 1))

    def one_microbatch(ex, loss_target):
        ex = dict(ex)
        diff = ex.pop(TWIN_DIFF_INPUT)
        return grad_fn(weights, diff, {**shared, **ex}, loss_target)

    if N_MICROBATCH == 1:
        loss, (grad_w, grad_x) = one_microbatch(per_example, given["loss_target"])
    else:
        def body(carry, xs):
            loss_sum, grad_sum = carry
            l_k, (gw_k, gx_k) = one_microbatch(xs[0], xs[1])
            with _jax.named_scope("update"):
                return (loss_sum + l_k, _jax.tree.map(_jnp.add, grad_sum, gw_k)), gx_k

        init = (_jnp.zeros((), _jnp.float32), _jax.tree.map(_jnp.zeros_like, weights))
        (loss, grad_w), grad_x = _jax.lax.scan(body, init, (per_example, given["loss_target"]))
    with _jax.named_scope("update"):
        delta_w, new_m, new_v = {}, {}, {}
        for n in TWIN_WEIGHTS:
            delta_w[n], new_m[n], new_v[n] = _adamw(weights[n], grad_w[n], given["m_" + n], given["v_" + n])
    return (loss, grad_x, *[grad_w[n] for n in TWIN_WEIGHTS], *[delta_w[n] for n in TWIN_WEIGHTS],
            *[new_m[n] for n in TWIN_WEIGHTS], *[new_v[n] for n in TWIN_WEIGHTS])
```

```python
import functools

import numpy as np
import jax
import jax.numpy as jnp
from jax import lax
from jax.experimental import pallas as pl
from jax.experimental.pallas import tpu as pltpu

F32 = jnp.float32
BF16 = jnp.bfloat16

D = 1024
NH = 16
HD = 64
FF = 4096
NBLK = 8
BW = 128
CHUNK = 64
PADK = 512
NREL = 192
EPS = 1e-6
LRU_C = 8.0
NDEV = 8

V7X_VMEM_LIMIT = 56 * 1024 * 1024
TM = 512
TMM = 256
TL = 256
QB = 256
KB = QB + PADK
NEG = -1e30

ADAM_LR, ADAM_B1, ADAM_B2, ADAM_EPS, ADAM_WD, ADAM_STEP = 0.001, 0.9, 0.999, 1e-08, 0.01, 10

_NT = (((1,), (1,)), ((), ()))
_TN = (((0,), (0,)), ((), ()))


def _cp(sem=None):
    return pltpu.CompilerParams(dimension_semantics=sem, vmem_limit_bytes=V7X_VMEM_LIMIT)


def _full(shape):
    n = len(shape)
    return pl.BlockSpec(shape, lambda *a: (0,) * n, pipeline_mode=pl.Buffered(1))


def _rows(tm, width):
    return pl.BlockSpec((tm, width), lambda i: (i, 0))


def _rstd(h):
    return lax.rsqrt(jnp.mean(h * h, axis=-1, keepdims=True) + EPS)


def _sigmoid(x):
    return 1.0 / (1.0 + jnp.exp(-x))


def _expm1(x):
    small = x * (1.0 + x * (0.5 + x * (1.0 / 6.0 + x * (1.0 / 24.0))))
    return jnp.where(jnp.abs(x) < 0.03, small, jnp.exp(x) - 1.0)


def _softplus_neg(lam):
    e = jnp.exp(-jnp.abs(lam))
    series = e * (1.0 - e * (0.5 - e * (1.0 / 3.0 - e * 0.25)))
    return jnp.maximum(-lam, 0.0) + jnp.where(e < 0.01, series, jnp.log(1.0 + e))


_GELU_K = 0.7978845608028654


def _gelu(x):
    return 0.5 * x * (1.0 + jnp.tanh(_GELU_K * (x + 0.044715 * x * x * x)))


def _gelu_grad(x):
    t = jnp.tanh(_GELU_K * (x + 0.044715 * x * x * x))
    return 0.5 * (1.0 + t) + 0.5 * x * (1.0 - t * t) * _GELU_K * (1.0 + 3.0 * 0.044715 * x * x)


def _sum8(x):
    r, c = x.shape
    return jnp.sum(x.reshape(r // 8, 8, c), axis=0)


def _shift_down(x, s, fill, rows):
    return jnp.where(rows >= s, pltpu.roll(x, s, axis=0), fill)


def _shift_up(x, s, fill, rows, n):
    return jnp.where(rows < n - s, pltpu.roll(x, n - s, axis=0), fill)


def _scan_fwd(a, b):
    n = a.shape[0]
    rows = lax.broadcasted_iota(jnp.int32, a.shape, 0)
    s = 1
    while s < n:
        a_sh = _shift_down(a, s, 1.0, rows)
        b_sh = _shift_down(b, s, 0.0, rows)
        b = a * b_sh + b
        a = a * a_sh
        s *= 2
    return a, b


def _scan_bwd(a, b):
    n = a.shape[0]
    rows = lax.broadcasted_iota(jnp.int32, a.shape, 0)
    s = 1
    while s < n:
        a_sh = _shift_up(a, s, 1.0, rows, n)
        b_sh = _shift_up(b, s, 0.0, rows, n)
        b = a * b_sh + b
        a = a * a_sh
        s *= 2
    return a, b


def _lru_gates(rc, wg_n, bg_n, sp_n):
    g = jnp.dot(rc.astype(BF16), wg_n, preferred_element_type=F32) + bg_n
    rg = _sigmoid(g[:, :BW])
    ig = _sigmoid(g[:, BW:])
    la = (-LRU_C) * rg * sp_n
    a = jnp.exp(la)
    mult = jnp.sqrt(-_expm1(2.0 * la))
    return rg, ig, a, mult


def _conv(ext_ref, cw_ref, cb_ref, sl, n):
    out = cb_ref[:, sl] + cw_ref[0:1, sl] * ext_ref[5:5 + n, sl]
    for k in range(1, 4):
        out = out + cw_ref[k:k + 1, sl] * ext_ref[5 + k:5 + k + n, sl]
    return out


def _lru_fwd(h0, a_norm, w_in, conv_w, conv_b, wg, bg, lam):
    s = h0.shape[0]

    def body(h0_ref, an_ref, win_ref, cw_ref, cb_ref, wg_ref, bg_ref, lam_ref,
             gate_ref, rec_ref, hs_ref, y_ref, ext_ref, hc_ref):
        i = pl.program_id(0)

        @pl.when(i == 0)
        def _():
            ext_ref[0:8, :] = jnp.zeros((8, D), F32)
            hc_ref[...] = jnp.zeros_like(hc_ref)

        h = h0_ref[...]
        n1 = (h * _rstd(h) * an_ref[...]).astype(BF16)
        u = jnp.dot(n1, win_ref[...], preferred_element_type=F32)
        gate_ref[...] = u[:, :D]
        rec_ref[...] = u[:, D:]
        ext_ref[8:8 + TL, :] = u[:, D:]
        sp = _softplus_neg(lam_ref[...])
        for n in range(NBLK):
            sl = slice(n * BW, (n + 1) * BW)
            rc = _conv(ext_ref, cw_ref, cb_ref, sl, TL)
            rg, ig, a, mult = _lru_gates(rc, wg_ref[n], bg_ref[n:n + 1, :], sp[:, sl])
            acum, bcum = _scan_fwd(a, mult * (ig * rc))
            hh = acum * hc_ref[0:1, sl] + bcum
            hc_ref[0:1, sl] = hh[TL - 1:TL, :]
            hs_ref[:, sl] = hh
            y_ref[:, sl] = (_gelu(gate_ref[:, sl]) * hh).astype(BF16)
        ext_ref[0:8, :] = ext_ref[TL:TL + 8, :]

    row = _rows(TL, D)
    return pl.pallas_call(
        body, name="lru_fwd", grid=(s // TL,),
        in_specs=[row, _full((1, D)), _full((D, 2 * D)), _full((4, D)), _full((1, D)),
                  _full((NBLK, BW, 2 * BW)), _full((NBLK, 2 * BW)), _full((1, D))],
        out_specs=[row, row, row, row],
        out_shape=[jax.ShapeDtypeStruct((s, D), F32), jax.ShapeDtypeStruct((s, D), F32),
                   jax.ShapeDtypeStruct((s, D), F32), jax.ShapeDtypeStruct((s, D), BF16)],
        scratch_shapes=[pltpu.VMEM((TL + 8, D), F32), pltpu.VMEM((8, D), F32)],
        compiler_params=_cp(("arbitrary",)),
    )(h0, a_norm, w_in, conv_w, conv_b, wg, bg, lam)


def _mlp_fwd(res, px, pw, g, wu, wd, name):
    s = res.shape[0]
    fj = 512

    def body(res_ref, px_ref, pw_ref, g_ref, wu_ref, wd_ref, hin_ref, hout_ref, up_ref, n_ref):
        hin = res_ref[...] + jnp.dot(px_ref[...], pw_ref[...], preferred_element_type=F32)
        hin_ref[...] = hin
        hout_ref[...] = hin
        n_ref[...] = (hin * _rstd(hin) * g_ref[...]).astype(BF16)
        for j in range(FF // fj):
            sl = slice(j * fj, (j + 1) * fj)
            up = jnp.dot(n_ref[...], wu_ref[:, sl], preferred_element_type=F32)
            up_ref[:, sl] = up
            rl = jnp.maximum(up, 0.0)
            hout_ref[...] += jnp.dot((rl * rl).astype(BF16), wd_ref[sl, :], preferred_element_type=F32)

    row = _rows(TMM, D)
    return pl.pallas_call(
        body, name=name, grid=(s // TMM,),
        in_specs=[row, row, _full((D, D)), _full((1, D)), _full((D, FF)), _full((FF, D))],
        out_specs=[row, row, _rows(TMM, FF)],
        out_shape=[jax.ShapeDtypeStruct((s, D), F32), jax.ShapeDtypeStruct((s, D), F32),
                   jax.ShapeDtypeStruct((s, FF), F32)],
        scratch_shapes=[pltpu.VMEM((TMM, D), BF16)],
        compiler_params=_cp(("parallel",)),
    )(res, px, pw, g, wu, wd)


def _head_rstd(x2, lo):
    sq = x2 * x2
    s_lo = jnp.sum(jnp.where(lo, sq, 0.0), axis=-1, keepdims=True)
    s_hi = jnp.sum(jnp.where(lo, 0.0, sq), axis=-1, keepdims=True)
    return lax.rsqrt(jnp.where(lo, s_lo, s_hi) * (1.0 / HD) + EPS)


def _kvq_fwd(h2, kv_norm, b_norm, w_kv, w_q, k_norm_t, q_norm_t):
    s = h2.shape[0]
    assert PADK == TM

    def body(h_ref, gkv_ref, gb_ref, wkv_ref, wq_ref, kn_ref, qn_ref,
             kraw_ref, qraw_ref, k_ref, v_ref, q_ref):
        i = pl.program_id(0)

        @pl.when(i == 0)
        def _():
            k_ref[...] = jnp.zeros_like(k_ref)
            v_ref[...] = jnp.zeros_like(v_ref)

        @pl.when(i > 0)
        def _():
            h = h_ref[...]
            xhat = h * _rstd(h)
            kv = jnp.dot((xhat * gkv_ref[...]).astype(BF16), wkv_ref[...], preferred_element_type=F32)
            qr = jnp.dot((xhat * gb_ref[...]).astype(BF16), wq_ref[...], preferred_element_type=F32)
            kraw_ref[...] = kv[:, :D]
            qraw_ref[...] = qr
            v_ref[...] = kv[:, D:].astype(BF16)
            lo = lax.broadcasted_iota(jnp.int32, (1, 128), 1) < HD
            for p in range(D // 128):
                sl = slice(p * 128, (p + 1) * 128)
                k2 = kv[:, sl]
                k_ref[:, sl] = (k2 * _head_rstd(k2, lo) * kn_ref[:, sl]).astype(BF16)
                q2 = qr[:, sl]
                q_ref[:, sl] = (q2 * _head_rstd(q2, lo) * qn_ref[:, sl]).astype(BF16)

    prev = pl.BlockSpec((TM, D), lambda i: (jnp.maximum(i - 1, 0), 0))
    cur = pl.BlockSpec((TM, D), lambda i: (i, 0))
    return pl.pallas_call(
        body, name="kvq_fwd", grid=(s // TM + 1,),
        in_specs=[prev, _full((1, D)), _full((1, D)), _full((D, 2 * D)), _full((D, D)), _full((1, D)), _full((1, D))],
        out_specs=[prev, prev, cur, cur, prev],
        out_shape=[jax.ShapeDtypeStruct((s, D), F32), jax.ShapeDtypeStruct((s, D), F32),
                   jax.ShapeDtypeStruct((s + PADK, D), BF16), jax.ShapeDtypeStruct((s + PADK, D), BF16),
                   jax.ShapeDtypeStruct((s, D), BF16)],
        compiler_params=_cp(("arbitrary",)),
    )(h2, kv_norm, b_norm, w_kv, w_q, k_norm_t, q_norm_t)


def _attn_scores(q2, kcat, bias_a, lo_a, valid):
    qa = jnp.where(lo_a, q2, jnp.zeros_like(q2))
    sc = lax.dot_general(qa, kcat, _NT, preferred_element_type=F32) * (HD ** -0.5) + bias_a
    sc = jnp.where(valid, sc, NEG)
    e = jnp.exp(sc - jnp.max(sc, axis=-1, keepdims=True))
    return e * (1.0 / jnp.sum(e, axis=-1, keepdims=True))


def _attn_specs(nqb):
    qspec = pl.BlockSpec((QB, 128), lambda p, j: (jnp.minimum(j, nqb - 1), p))
    kspecs = [pl.BlockSpec((QB, 128), functools.partial(lambda p, j, t: (jnp.minimum(j, nqb - 1) + t, p), t=t))
              for t in range(KB // QB)]
    bspec = pl.BlockSpec((2, QB, KB), lambda p, j: (p, 0, 0))
    return qspec, kspecs, bspec


def _attn_fwd(q, kpad, vpad, bias):
    s = q.shape[0]
    nqb = s // QB
    nk = KB // QB

    def body(q_ref, k0, k1, k2, v0, v1, v2, bias_ref, o_ref):
        j = pl.program_id(1)
        kcat = jnp.concatenate([k0[...], k1[...], k2[...]], axis=0)
        vcat = jnp.concatenate([v0[...], v1[...], v2[...]], axis=0)
        q2 = q_ref[...]
        lo = lax.broadcasted_iota(jnp.int32, (1, 128), 1) < HD
        valid = lax.broadcasted_iota(jnp.int32, (QB, KB), 1) + j * QB >= PADK
        outs = []
        for a in range(2):
            p = _attn_scores(q2, kcat, bias_ref[a], lo if a == 0 else jnp.logical_not(lo), valid)
            outs.append(jnp.dot(p.astype(BF16), vcat, preferred_element_type=F32))
        o_ref[...] = jnp.where(lo, outs[0], outs[1]).astype(BF16)

    qspec, kspecs, bspec = _attn_specs(nqb)
    assert nk == 3
    return pl.pallas_call(
        body, name="attn_fwd", grid=(D // 128, nqb),
        in_specs=[qspec] + kspecs + kspecs + [bspec],
        out_specs=qspec,
        out_shape=jax.ShapeDtypeStruct((s, D), BF16),
        compiler_params=_cp(("parallel", "arbitrary")),
    )(q, kpad, kpad, kpad, vpad, vpad, vpad, bias)


def _loss_grad(h4, tgt):
    s = h4.shape[0]

    def body(h_ref, t_ref, g_ref, l_ref):
        @pl.when(pl.program_id(0) == 0)
        def _():
            l_ref[...] = jnp.zeros_like(l_ref)

        d = h_ref[...] - t_ref[...]
        g_ref[...] = d * (1.0 / D)
        l_ref[...] += _sum8(d * d) * (0.5 / D)

    row = _rows(TM, D)
    return pl.pallas_call(
        body, name="loss_grad", grid=(s // TM,),
        in_specs=[row, row], out_specs=[row, pl.BlockSpec((8, D), lambda i: (0, 0))],
        out_shape=[jax.ShapeDtypeStruct((s, D), F32), jax.ShapeDtypeStruct((8, D), F32)],
        compiler_params=_cp(("arbitrary",)),
    )(h4, tgt)


def _rms_bwd(dn, xhat, r, g):
    dng = dn * g
    return r * (dng - xhat * jnp.mean(dng * xhat, axis=-1, keepdims=True))


def _acc_spec():
    return pl.BlockSpec((8, D), lambda i: (0, 0))


def _mlp_bwd(gout, hin, up, g, wu, wd, name):
    s = gout.shape[0]
    fj = 512

    def body(go_ref, hin_ref, up_ref, g_ref, wu_ref, wd_ref, gin_ref, dup_ref, act_ref, n_ref, dg_ref,
             gob_ref, dn_ref):
        @pl.when(pl.program_id(0) == 0)
        def _():
            dg_ref[...] = jnp.zeros_like(dg_ref)

        gob_ref[...] = go_ref[...].astype(BF16)
        for j in range(FF // fj):
            sl = slice(j * fj, (j + 1) * fj)
            rl = jnp.maximum(up_ref[:, sl], 0.0)
            act_ref[:, sl] = (rl * rl).astype(BF16)
            dact = lax.dot_general(gob_ref[...], wd_ref[sl, :], _NT, preferred_element_type=F32)
            dupj = (dact * (2.0 * rl)).astype(BF16)
            dup_ref[:, sl] = dupj
            part = lax.dot_general(dupj, wu_ref[:, sl], _NT, preferred_element_type=F32)
            if j == 0:
                dn_ref[...] = part
            else:
                dn_ref[...] += part
        hin = hin_ref[...]
        r = _rstd(hin)
        xhat = hin * r
        n_ref[...] = (xhat * g_ref[...]).astype(BF16)
        dn = dn_ref[...]
        gin_ref[...] = go_ref[...] + _rms_bwd(dn, xhat, r, g_ref[...])
        dg_ref[...] += _sum8(dn * xhat)

    row = _rows(TMM, D)
    wide = _rows(TMM, FF)
    return pl.pallas_call(
        body, name=name, grid=(s // TMM,),
        in_specs=[row, row, wide, _full((1, D)), _full((D, FF)), _full((FF, D))],
        out_specs=[row, wide, wide, row, _acc_spec()],
        out_shape=[jax.ShapeDtypeStruct((s, D), F32), jax.ShapeDtypeStruct((s, FF), BF16),
                   jax.ShapeDtypeStruct((s, FF), BF16), jax.ShapeDtypeStruct((s, D), BF16),
                   jax.ShapeDtypeStruct((8, D), F32)],
        scratch_shapes=[pltpu.VMEM((TMM, D), BF16), pltpu.VMEM((TMM, D), F32)],
        compiler_params=_cp(("arbitrary",)),
    )(gout, hin, up, g, wu, wd)


def _matmul_tn(a, b, slab, out_dtype, name):
    s, m = a.shape
    n = b.shape[1]
    tn = n // NDEV if slab else min(n, 1024)
    tm = min(m, 1024)
    ts = 512
    nk = s // ts

    def body(a_ref, b_ref, o_ref, acc_ref):
        k = pl.program_id(2)

        @pl.when(k == 0)
        def _():
            acc_ref[...] = jnp.zeros_like(acc_ref)

        acc_ref[...] += lax.dot_general(a_ref[...].astype(BF16), b_ref[...].astype(BF16), _TN,
                                        preferred_element_type=F32)

        @pl.when(k == nk - 1)
        def _():
            o_ref[...] = acc_ref[...].astype(out_dtype)

    if slab:
        out_shape = jax.ShapeDtypeStruct((NDEV, m, tn), out_dtype)
        out_spec = pl.BlockSpec((None, tm, tn), lambda i, j, k: (j, i, 0))
    else:
        out_shape = jax.ShapeDtypeStruct((m, n), out_dtype)
        out_spec = pl.BlockSpec((tm, tn), lambda i, j, k: (i, j))
    return pl.pallas_call(
        body, name=name, grid=(m // tm, n // tn, nk),
        in_specs=[pl.BlockSpec((ts, tm), lambda i, j, k: (k, i)), pl.BlockSpec((ts, tn), lambda i, j, k: (k, j))],
        out_specs=out_spec, out_shape=out_shape,
        scratch_shapes=[pltpu.VMEM((tm, tn), F32)],
        compiler_params=_cp(("parallel", "parallel", "arbitrary")),
    )(a, b)


def _matmul_nt(x, w, name):
    s, n = x.shape
    k = w.shape[0]

    def body(x_ref, w_ref, o_ref):
        o_ref[...] = lax.dot_general(x_ref[...].astype(BF16), w_ref[...], _NT,
                                     preferred_element_type=F32).astype(BF16)

    return pl.pallas_call(
        body, name=name, grid=(s // TM,),
        in_specs=[_rows(TM, n), _full((k, n))], out_specs=_rows(TM, k),
        out_shape=jax.ShapeDtypeStruct((s, k), BF16),
        compiler_params=_cp(("parallel",)),
    )(x, w)


def _attn_bwd(q, kpad, vpad, do, bias):
    s = q.shape[0]
    nqb = s // QB

    def body(q_ref, k0, k1, k2, v0, v1, v2, do_ref, bias_ref, dq_ref, dk_ref, dv_ref, db_ref, dka_ref, dva_ref):
        j = pl.program_id(1)

        @pl.when(j == 0)
        def _():
            dka_ref[...] = jnp.zeros_like(dka_ref)
            dva_ref[...] = jnp.zeros_like(dva_ref)
            db_ref[...] = jnp.zeros_like(db_ref)

        @pl.when(j < nqb)
        def _():
            kcat = jnp.concatenate([k0[...], k1[...], k2[...]], axis=0)
            vcat = jnp.concatenate([v0[...], v1[...], v2[...]], axis=0)
            q2 = q_ref[...]
            do2 = do_ref[...]
            lo = lax.broadcasted_iota(jnp.int32, (1, 128), 1) < HD
            valid = lax.broadcasted_iota(jnp.int32, (QB, KB), 1) + j * QB >= PADK
            dq, dk, dv = [], [], []
            for a in range(2):
                lo_a = lo if a == 0 else jnp.logical_not(lo)
                p = _attn_scores(q2, kcat, bias_ref[a], lo_a, valid)
                doa = jnp.where(lo_a, do2, jnp.zeros_like(do2))
                dp = lax.dot_general(doa, vcat, _NT, preferred_element_type=F32)
                ds = p * (dp - jnp.sum(p * dp, axis=-1, keepdims=True))
                db_ref[a] += ds
                dsb = (ds * (HD ** -0.5)).astype(BF16)
                dq.append(jnp.dot(dsb, kcat, preferred_element_type=F32))
                dk.append(lax.dot_general(dsb, q2, _TN, preferred_element_type=F32))
                dv.append(lax.dot_general(p.astype(BF16), do2, _TN, preferred_element_type=F32))
            dq_ref[...] = jnp.where(lo, dq[0], dq[1])
            dka_ref[...] += jnp.where(lo, dk[0], dk[1])
            dva_ref[...] += jnp.where(lo, dv[0], dv[1])

        dk_ref[...] = dka_ref[0:QB, :]
        dv_ref[...] = dva_ref[0:QB, :]
        dka_ref[0:KB - QB, :] = dka_ref[QB:KB, :]
        dva_ref[0:KB - QB, :] = dva_ref[QB:KB, :]
        dka_ref[KB - QB:KB, :] = jnp.zeros((QB, 128), F32)
        dva_ref[KB - QB:KB, :] = jnp.zeros((QB, 128), F32)

    qspec, kspecs, bspec = _attn_specs(nqb)
    kout = pl.BlockSpec((QB, 128), lambda p, j: (jnp.maximum(j - PADK // QB, 0), p))
    sd = jax.ShapeDtypeStruct((s, D), F32)
    return pl.pallas_call(
        body, name="attn_bwd", grid=(D // 128, nqb + PADK // QB),
        in_specs=[qspec] + kspecs + kspecs + [qspec, bspec],
        out_specs=[qspec, kout, kout, bspec],
        out_shape=[sd, sd, sd, jax.ShapeDtypeStruct((NH, QB, KB), F32)],
        scratch_shapes=[pltpu.VMEM((KB, 128), F32), pltpu.VMEM((KB, 128), F32)],
        compiler_params=_cp(("parallel", "arbitrary")),
    )(q, kpad, kpad, kpad, vpad, vpad, vpad, do, bias)


def _head_norm_bwd(dy2, x2, g2, lo):
    rr = _head_rstd(x2, lo)
    xhat = x2 * rr
    t = dy2 * g2 * xhat
    m_lo = jnp.sum(jnp.where(lo, t, 0.0), axis=-1, keepdims=True)
    m_hi = jnp.sum(jnp.where(lo, 0.0, t), axis=-1, keepdims=True)
    m = jnp.where(lo, m_lo, m_hi) * (1.0 / HD)
    return rr * (dy2 * g2 - xhat * m), dy2 * xhat


def _kvq_bwd(dq, dk, dv, qraw, kraw, h2, g3, kv_norm, b_norm, w_kv, w_q, k_norm_t, q_norm_t):
    s = h2.shape[0]

    def body(dq_ref, dk_ref, dv_ref, qraw_ref, kraw_ref, h_ref, g3_ref, gkv_ref, gb_ref, wkv_ref, wq_ref,
             kn_ref, qn_ref, g2_ref, dqr_ref, dkv_ref, nb_ref, nk_ref, dgq_ref, dgk_ref, dgb_ref, dgkv_ref):
        @pl.when(pl.program_id(0) == 0)
        def _():
            for r in (dgq_ref, dgk_ref, dgb_ref, dgkv_ref):
                r[...] = jnp.zeros_like(r)

        lo = lax.broadcasted_iota(jnp.int32, (1, 128), 1) < HD
        for p in range(D // 128):
            sl = slice(p * 128, (p + 1) * 128)
            dx, dgp = _head_norm_bwd(dq_ref[:, sl], qraw_ref[:, sl], qn_ref[:, sl], lo)
            dqr_ref[:, sl] = dx.astype(BF16)
            dgq_ref[:, sl] += _sum8(dgp)
            dx, dgp = _head_norm_bwd(dk_ref[:, sl], kraw_ref[:, sl], kn_ref[:, sl], lo)
            dkv_ref[:, sl] = dx.astype(BF16)
            dgk_ref[:, sl] += _sum8(dgp)
        dkv_ref[:, D:] = dv_ref[...].astype(BF16)
        dnb = lax.dot_general(dqr_ref[...], wq_ref[...], _NT, preferred_element_type=F32)
        dnk = lax.dot_general(dkv_ref[...], wkv_ref[...], _NT, preferred_element_type=F32)
        h = h_ref[...]
        r = _rstd(h)
        xhat = h * r
        dxg = dnb * gb_ref[...] + dnk * gkv_ref[...]
        g2_ref[...] = g3_ref[...] + r * (dxg - xhat * jnp.mean(dxg * xhat, axis=-1, keepdims=True))
        dgb_ref[...] += _sum8(dnb * xhat)
        dgkv_ref[...] += _sum8(dnk * xhat)
        nb_ref[...] = (xhat * gb_ref[...]).astype(BF16)
        nk_ref[...] = (xhat * gkv_ref[...]).astype(BF16)

    row = _rows(TM, D)
    sd = jax.ShapeDtypeStruct((s, D), BF16)
    acc = jax.ShapeDtypeStruct((8, D), F32)
    return pl.pallas_call(
        body, name="kvq_bwd", grid=(s // TM,),
        in_specs=[row] * 7 + [_full((1, D)), _full((1, D)), _full((D, 2 * D)), _full((D, D)), _full((1, D)),
                              _full((1, D))],
        out_specs=[row, row, _rows(TM, 2 * D), row, row] + [_acc_spec()] * 4,
        out_shape=[jax.ShapeDtypeStruct((s, D), F32), sd, jax.ShapeDtypeStruct((s, 2 * D), BF16), sd, sd,
                   acc, acc, acc, acc],
        compiler_params=_cp(("arbitrary",)),
    )(dq, dk, dv, qraw, kraw, h2, g3, kv_norm, b_norm, w_kv, w_q, k_norm_t, q_norm_t)


def _lru_bwd(g1, gate, rec, hs, w_out, conv_w, conv_b, wg, bg, lam):
    s = g1.shape[0]
    nt = s // TL

    def body(g1_ref, gate_ref, rec_ref, recp_ref, hs_ref, hsp_ref, wo_ref, cw_ref, cb_ref, wg_ref, bg_ref, lam_ref,
             du_ref, dcw_ref, dcb_ref, dwg_ref, dbg_ref, dlam_ref, ext_ref, dext_ref, cg_ref):
        i = pl.program_id(0)
        first_tile = i == nt - 1

        @pl.when(i == 0)
        def _():
            dext_ref[TL:TL + 8, :] = jnp.zeros((8, D), F32)
            cg_ref[...] = jnp.zeros_like(cg_ref)
            for r in (dcw_ref, dcb_ref, dwg_ref, dbg_ref, dlam_ref):
                r[...] = jnp.zeros_like(r)

        keep = jnp.where(first_tile, 0.0, 1.0)
        ext_ref[0:8, :] = recp_ref[...] * keep
        ext_ref[8:8 + TL, :] = rec_ref[...]
        dy = lax.dot_general(g1_ref[...].astype(BF16), wo_ref[...], _NT, preferred_element_type=F32)
        lam_v = lam_ref[...]
        sp = _softplus_neg(lam_v)
        dsp_dlam = -_sigmoid(-lam_v)
        rows = lax.broadcasted_iota(jnp.int32, (TL, BW), 0)
        for n in range(NBLK):
            sl = slice(n * BW, (n + 1) * BW)
            rc = _conv(ext_ref, cw_ref, cb_ref, sl, TL)
            rg, ig, a, mult = _lru_gates(rc, wg_ref[n], bg_ref[n:n + 1, :], sp[:, sl])
            h = hs_ref[:, sl]
            hprev = _shift_down(h, 1, hsp_ref[7:8, sl] * keep, rows)
            gt = gate_ref[:, sl]
            dyn = dy[:, sl]
            du_ref[:, sl] = (dyn * h * _gelu_grad(gt)).astype(BF16)
            dh = dyn * _gelu(gt)
            dh = dh + jnp.where(rows == TL - 1, cg_ref[0:1, sl], 0.0)
            _, gsc = _scan_bwd(_shift_up(a, 1, 0.0, rows, TL), dh)
            cg_ref[0:1, sl] = a[0:1, :] * gsc[0:1, :]
            da = gsc * hprev
            d_mult = gsc * ig * rc
            d_ig = gsc * mult * rc
            d_rc = gsc * mult * ig
            d_la = da * a - d_mult * (a * a) / mult
            d_rg = d_la * ((-LRU_C) * sp[:, sl])
            dlam_ref[:, sl] += _sum8(d_la * ((-LRU_C) * rg)) * dsp_dlam[:, sl]
            dg = jnp.concatenate([d_rg * rg * (1.0 - rg), d_ig * ig * (1.0 - ig)], axis=1)
            dgb = dg.astype(BF16)
            d_rc = d_rc + lax.dot_general(dgb, wg_ref[n], _NT, preferred_element_type=F32)
            dwg_ref[n] += lax.dot_general(rc.astype(BF16), dgb, _TN, preferred_element_type=F32)
            dbg_ref[n] += _sum8(dg)
            dext_ref[0:TL, sl] = d_rc
            dcb_ref[:, sl] += _sum8(d_rc)
            for k in range(4):
                dcw_ref[k, :, sl] += _sum8(d_rc * ext_ref[5 + k:5 + k + TL, sl])
        for k in range(4):
            part = cw_ref[3 - k:4 - k, :] * dext_ref[k:k + TL, :]
            acc = part if k == 0 else acc + part
        du_ref[:, D:] = acc.astype(BF16)
        dext_ref[TL:TL + 8, :] = dext_ref[0:8, :]

    rev = pl.BlockSpec((TL, D), lambda i: (nt - 1 - i, 0))
    rev8 = pl.BlockSpec((8, D), lambda i: (jnp.maximum((nt - 1 - i) * (TL // 8) - 1, 0), 0))
    acc = jax.ShapeDtypeStruct((8, D), F32)
    return pl.pallas_call(
        body, name="lru_bwd", grid=(nt,),
        in_specs=[rev, rev, rev, rev8, rev, rev8, _full((D, D)), _full((4, D)), _full((1, D)),
                  _full((NBLK, BW, 2 * BW)), _full((NBLK, 2 * BW)), _full((1, D))],
        out_specs=[pl.BlockSpec((TL, 2 * D), lambda i: (nt - 1 - i, 0)),
                   pl.BlockSpec((4, 8, D), lambda i: (0, 0, 0)), _acc_spec(),
                   pl.BlockSpec((NBLK, BW, 2 * BW), lambda i: (0, 0, 0)),
                   pl.BlockSpec((NBLK, 8, 2 * BW), lambda i: (0, 0, 0)), _acc_spec()],
        out_shape=[jax.ShapeDtypeStruct((s, 2 * D), BF16), jax.ShapeDtypeStruct((4, 8, D), F32), acc,
                   jax.ShapeDtypeStruct((NBLK, BW, 2 * BW), F32), jax.ShapeDtypeStruct((NBLK, 8, 2 * BW), F32), acc],
        scratch_shapes=[pltpu.VMEM((TL + 8, D), F32), pltpu.VMEM((TL + 8, D), F32), pltpu.VMEM((8, D), F32)],
        compiler_params=_cp(("arbitrary",)),
    )(g1, gate, rec, rec, hs, hs, w_out, conv_w, conv_b, wg, bg, lam)


def _a_in_bwd(du, h0, g1, a_norm, w_in):
    s = h0.shape[0]

    def body(du_ref, h_ref, g1_ref, an_ref, win_ref, gx_ref, n1_ref, dg_ref):
        @pl.when(pl.program_id(0) == 0)
        def _():
            dg_ref[...] = jnp.zeros_like(dg_ref)

        dn = lax.dot_general(du_ref[...], win_ref[...], _NT, preferred_element_type=F32)
        h = h_ref[...]
        r = _rstd(h)
        xhat = h * r
        gx_ref[...] = g1_ref[...] + _rms_bwd(dn, xhat, r, an_ref[...])
        n1_ref[...] = (xhat * an_ref[...]).astype(BF16)
        dg_ref[...] += _sum8(dn * xhat)

    row = _rows(TM, D)
    return pl.pallas_call(
        body, name="a_in_bwd", grid=(s // TM,),
        in_specs=[_rows(TM, 2 * D), row, row, _full((1, D)), _full((D, 2 * D))],
        out_specs=[row, row, _acc_spec()],
        out_shape=[jax.ShapeDtypeStruct((s, D), F32), jax.ShapeDtypeStruct((s, D), BF16),
                   jax.ShapeDtypeStruct((8, D), F32)],
        compiler_params=_cp(("arbitrary",)),
    )(du, h0, g1, a_norm, w_in)


_TOEP = QB + KB


def _rel_onehot():
    c = np.arange(_TOEP)
    idx = np.clip(PADK + QB - 1 - c, -(CHUNK - 1), 2 * CHUNK) + (CHUNK - 1)
    return (idx[None, :] == np.arange(NREL)[:, None]).astype(np.float32)


def _band_mask():
    i = np.arange(QB)[:, None] // CHUNK
    j = np.arange(KB)[None, :] // CHUNK
    return np.where((j >= i) & (j <= i + PADK // CHUNK), 0.0, NEG).astype(np.float32)


def _bias_block(rel_bias):
    diag = jnp.dot(rel_bias, jnp.asarray(_rel_onehot()), precision=lax.Precision.HIGHEST)
    hankel = jnp.tile(diag, (1, QB + 1))[:, :QB * (_TOEP + 1)].reshape(NH, QB, _TOEP + 1)
    return hankel[:, ::-1, :KB] + jnp.asarray(_band_mask())[None]


def _rel_bias_grad(dbsum):
    rows = QB + 8
    flat = jnp.pad(dbsum[:, ::-1, :], ((0, 0), (0, 0), (0, _TOEP + 1 - KB))).reshape(NH, QB * (_TOEP + 1))
    z = jnp.pad(flat, ((0, 0), (0, rows * _TOEP - QB * (_TOEP + 1)))).reshape(NH, rows, _TOEP)
    oh = np.zeros((_TOEP, 256), np.float32)
    oh[:, :NREL] = _rel_onehot().T

    def body(z_ref, oh_ref, o_ref):
        d = jnp.sum(z_ref[...], axis=0, keepdims=True)
        hi = d.astype(BF16)
        mid = (d - hi.astype(F32)).astype(BF16)
        lo = (d - hi.astype(F32) - mid.astype(F32)).astype(BF16)
        ohb = oh_ref[...].astype(BF16)
        acc = jnp.zeros((8, 256), F32)
        for piece in (lo, mid, hi):
            acc = acc + jnp.dot(jnp.broadcast_to(piece, (8, _TOEP)), ohb, preferred_element_type=F32)
        o_ref[...] = acc

    out = pl.pallas_call(
        body, name="rel_bias_grad", grid=(NH,),
        in_specs=[pl.BlockSpec((None, rows, _TOEP), lambda h: (h, 0, 0)), pl.BlockSpec((_TOEP, 256), lambda h: (0, 0))],
        out_specs=pl.BlockSpec((None, 8, 256), lambda h: (h, 0, 0)),
        out_shape=jax.ShapeDtypeStruct((NH, 8, 256), F32),
        compiler_params=_cp(("parallel",)),
    )(z, jnp.asarray(oh))
    return out[:, 0, :NREL]


def _exchange(arrays, scatter, name):
    n = len(arrays)

    def body(*refs):
        ins, outs = refs[:n], refs[n:2 * n]
        send_sems, recv_sems, local_sems = refs[2 * n:]
        x, y, c = lax.axis_index("x"), lax.axis_index("y"), lax.axis_index("c")
        me = 4 * x + 2 * y + c

        def peer_of(r):
            rx, ry, rc = (r >> 2) & 1, (r >> 1) & 1, r & 1
            px = 1 - x if rx else x
            py = 1 - y if ry else y
            pc = 1 - c if rc else c
            return (px, py, pc), 4 * px + 2 * py + pc

        local, sent = [], []
        for k in range(n):
            cp = pltpu.make_async_copy(ins[k].at[me] if scatter else ins[k], outs[k].at[me], local_sems.at[k])
            cp.start()
            local.append(cp)
            for r in range(1, NDEV):
                peer, peer_lin = peer_of(r)
                cp = pltpu.make_async_remote_copy(
                    src_ref=ins[k].at[peer_lin] if scatter else ins[k], dst_ref=outs[k].at[me],
                    send_sem=send_sems.at[k, r - 1], recv_sem=recv_sems.at[k, r - 1],
                    device_id=peer, device_id_type=pl.DeviceIdType.MESH)
                cp.start()
                sent.append(cp)
        for k in range(n):
            for r in range(1, NDEV):
                peer, peer_lin = peer_of(r)
                pltpu.make_async_remote_copy(
                    src_ref=ins[k].at[peer_lin] if scatter else ins[k], dst_ref=outs[k].at[peer_lin],
                    send_sem=send_sems.at[k, r - 1], recv_sem=recv_sems.at[k, r - 1],
                    device_id=peer, device_id_type=pl.DeviceIdType.MESH).wait_recv()
        for cp in sent:
            cp.wait_send()
        for cp in local:
            cp.wait()

    def slot_shape(a):
        return (NDEV,) + (a.shape[1:] if scatter else a.shape)

    anyspec = pl.BlockSpec(memory_space=pl.ANY)
    return pl.pallas_call(
        body, name=name,
        in_specs=[anyspec] * n, out_specs=[anyspec] * n,
        out_shape=[jax.ShapeDtypeStruct(slot_shape(a), a.dtype) for a in arrays],
        scratch_shapes=[pltpu.SemaphoreType.DMA((n, NDEV - 1)), pltpu.SemaphoreType.DMA((n, NDEV - 1)),
                        pltpu.SemaphoreType.DMA((n,))],
        compiler_params=pltpu.CompilerParams(has_side_effects=True),
    )(*arrays)


def _sum_slots(st, name):
    _, r, c = st.shape

    def body(s_ref, o_ref):
        acc = s_ref[0]
        for d in range(1, NDEV):
            acc = acc + s_ref[d]
        o_ref[...] = acc

    return pl.pallas_call(
        body, name=name, out_shape=jax.ShapeDtypeStruct((r, c), F32),
        in_specs=[pl.BlockSpec((NDEV, r, c), lambda: (0, 0, 0))], out_specs=pl.BlockSpec((r, c), lambda: (0, 0)),
    )(st)


def _adamw(w, m, v, gst, name):
    r, c = w.shape
    ns = gst.shape[0]
    tr = min(r, 256)
    c1 = 1.0 - ADAM_B1 ** ADAM_STEP
    c2 = 1.0 - ADAM_B2 ** ADAM_STEP

    def body(w_ref, m_ref, v_ref, g_ref, go_ref, d_ref, mo_ref, vo_ref):
        g = g_ref[0].astype(F32)
        for d in range(1, ns):
            g = g + g_ref[d].astype(F32)
        m2 = ADAM_B1 * m_ref[...] + (1.0 - ADAM_B1) * g
        v2 = ADAM_B2 * v_ref[...] + (1.0 - ADAM_B2) * (g * g)
        go_ref[...] = g
        mo_ref[...] = m2
        vo_ref[...] = v2
        d_ref[...] = (-ADAM_LR) * ((m2 / c1) / (jnp.sqrt(v2 / c2) + ADAM_EPS) + ADAM_WD * w_ref[...])

    blk = pl.BlockSpec((tr, c), lambda i: (i, 0))
    sd = jax.ShapeDtypeStruct((r, c), F32)
    return pl.pallas_call(
        body, name=name, grid=(r // tr,),
        in_specs=[blk, blk, blk, pl.BlockSpec((ns, tr, c), lambda i: (0, i, 0))],
        out_specs=[blk, blk, blk, blk], out_shape=[sd, sd, sd, sd],
        compiler_params=_cp(("parallel",)),
    )(w, m, v, gst)


def _pack(pieces, rows):
    flat = jnp.concatenate([p.reshape(-1).astype(F32) for p in pieces])
    return jnp.pad(flat, (0, rows * 128 - flat.shape[0])).reshape(rows, 128)


def _unpack(flat, shapes):
    out, off = [], 0
    for shp in shapes:
        size = int(np.prod(shp))
        out.append(flat[off:off + size].reshape(shp))
        off += size
    return out


def _cols(full, me, width):
    return lax.dynamic_slice_in_dim(full, me * width, width, axis=full.ndim - 1)


def kernel(x, a_norm, a_w_in, a_conv_w, a_conv_b, a_w_gate, a_b_gate, a_lambda, a_w_out, kv_norm, w_kv, k_norm, b_norm, b_w_q, b_q_norm, b_rel_bias, b_w_o, mlp_norm, w_up, w_down, loss_target, m_a_norm, m_a_w_in, m_a_conv_w, m_a_conv_b, m_a_w_gate, m_a_b_gate, m_a_lambda, m_a_w_out, m_kv_norm, m_w_kv, m_k_norm, m_b_norm, m_b_w_q, m_b_q_norm, m_b_rel_bias, m_b_w_o, m_mlp_norm, m_w_up, m_w_down, v_a_norm, v_a_w_in, v_a_conv_w, v_a_conv_b, v_a_w_gate, v_a_b_gate, v_a_lambda, v_a_w_out, v_kv_norm, v_w_kv, v_k_norm, v_b_norm, v_b_w_q, v_b_q_norm, v_b_rel_bias, v_b_w_o, v_mlp_norm, v_w_up, v_w_down):
    me = 4 * lax.axis_index("x") + 2 * lax.axis_index("y") + lax.axis_index("c")
    sh = D // NDEV

    big_w = [a_w_in[0], a_w_out[0], w_kv, b_w_q[0], b_w_o[0], w_up[0], w_up[1], w_down[0], w_down[1]]
    small_sharded = [a_norm, a_conv_w, a_conv_b, a_b_gate, a_lambda, a_w_gate]
    small_rows = 272
    got = _exchange([w.astype(BF16) for w in big_w] + [_pack(small_sharded, small_rows)], False, "gather_weights")
    w_in = got[0].transpose(1, 0, 2).reshape(D, 2 * D)
    w_out = got[1].reshape(D, D)
    wkv = got[2].transpose(1, 0, 2).reshape(D, 2 * D)
    w_q = got[3].reshape(D, D)
    w_o = got[4].reshape(D, D)
    wu = [got[5].transpose(1, 0, 2).reshape(D, FF), got[6].transpose(1, 0, 2).reshape(D, FF)]
    wd = [got[7].reshape(FF, D), got[8].reshape(FF, D)]
    sm = got[9].reshape(NDEV, small_rows * 128)
    an_f = sm[:, 0:128].reshape(1, D)
    cw_f = sm[:, 128:640].reshape(NDEV, 4, sh).transpose(1, 0, 2).reshape(4, D)
    cb_f = sm[:, 640:768].reshape(1, D)
    bg_f = sm[:, 768:1024].reshape(NDEV, NBLK, 2 * BW // NDEV).transpose(1, 0, 2).reshape(NBLK, 2 * BW)
    lam_f = sm[:, 1024:1152].reshape(1, D)
    wg_f = sm[:, 1152:1152 + NBLK * BW * 32].reshape(NDEV, NBLK, BW, 32).transpose(1, 2, 0, 3)
    wg_f = wg_f.reshape(NBLK, BW, 2 * BW).astype(BF16)
    kn_t = jnp.tile(k_norm, NH).reshape(1, D)
    qn_t = jnp.tile(b_q_norm[0], NH).reshape(1, D)
    kvn = kv_norm.reshape(1, D)
    bias = _bias_block(b_rel_bias[0])

    h0 = x[0]
    gate, rec, hs, y = _lru_fwd(h0, an_f, w_in, cw_f, cb_f, wg_f, bg_f, lam_f)
    h1, h2, up0 = _mlp_fwd(h0, y, w_out, mlp_norm[0:1], wu[0], wd[0], "mlp_fwd0")
    kraw, qraw, kpad, vpad, q = _kvq_fwd(h2, kvn, b_norm, wkv, w_q, kn_t, qn_t)
    o = _attn_fwd(q, kpad, vpad, bias)
    h3, h4, up1 = _mlp_fwd(h2, o, w_o, mlp_norm[1:2], wu[1], wd[1], "mlp_fwd1")
    g4, lpart = _loss_grad(h4, loss_target[0])
    loss = lax.psum(jnp.sum(lpart), ("x", "y", "c"))

    g3, dup1, act1, n3, dgm1 = _mlp_bwd(g4, h3, up1, mlp_norm[1:2], wu[1], wd[1], "mlp_bwd1")
    d_wd1 = _matmul_tn(act1, g4, False, BF16, "dw_down1").reshape(NDEV, FF // NDEV, D)
    d_wu1 = _matmul_tn(n3, dup1, True, BF16, "dw_up1")
    do = _matmul_nt(g3, w_o, "do_proj")
    d_wo = _matmul_tn(o, g3, False, BF16, "dw_o").reshape(NDEV, sh, D)
    dq, dk, dv, dbsum = _attn_bwd(q, kpad, vpad, do, bias)
    g2, dqr, dkv, nb, nk, dgq, dgk, dgb, dgkv = _kvq_bwd(dq, dk, dv, qraw, kraw, h2, g3, kvn, b_norm, wkv, w_q,
                                                       kn_t, qn_t)
    d_wq = _matmul_tn(nb, dqr, False, BF16, "dw_q").reshape(NDEV, sh, D)
    d_wkv = _matmul_tn(nk, dkv, True, BF16, "dw_kv")
    g1, dup0, act0, n2, dgm0 = _mlp_bwd(g2, h1, up0, mlp_norm[0:1], wu[0], wd[0], "mlp_bwd0")
    d_wd0 = _matmul_tn(act0, g2, False, BF16, "dw_down0").reshape(NDEV, FF // NDEV, D)
    d_wu0 = _matmul_tn(n2, dup0, True, BF16, "dw_up0")
    du, dcw, dcb, dwg, dbg, dlam = _lru_bwd(g1, gate, rec, hs, w_out, cw_f, cb_f, wg_f, bg_f, lam_f)
    d_wout = _matmul_tn(y, g1, False, BF16, "dw_out").reshape(NDEV, sh, D)
    gx, n1, dga = _a_in_bwd(du, h0, g1, an_f, w_in)
    d_win = _matmul_tn(n1, du, True, BF16, "dw_in")
    d_rel = _rel_bias_grad(dbsum)

    dwg_slab = dwg.reshape(NBLK, BW, NDEV, 32).transpose(2, 0, 1, 3).reshape(NDEV, 256, 128)
    big_g = [d_win, d_wout, d_wkv, d_wq, d_wo, d_wu0, d_wu1, d_wd0, d_wd1, dwg_slab]
    recv = _exchange(big_g, True, "scatter_grads")
    small_full = [dga.sum(0), dcw.sum(1), dcb.sum(0), dbg.sum(1), dlam.sum(0), dgkv.sum(0),
                  dgk.sum(0).reshape(NH, HD).sum(0), dgb.sum(0), dgq.sum(0).reshape(NH, HD).sum(0), d_rel,
                  jnp.stack([dgm0.sum(0), dgm1.sum(0)])]
    small_g_rows = 136
    (gsm,) = _exchange([_pack(small_full, small_g_rows)], False, "gather_small_grads")
    gs = _unpack(_sum_slots(gsm, "sum_small_grads").reshape(-1),
                 [(1, D), (4, D), (1, D), (NBLK, 2 * BW), (1, D), (D,), (HD,), (1, D), (1, HD), (1, NH, NREL), (2, D)])
    g_small = [_cols(gs[0], me, sh), _cols(gs[1], me, sh)[None], _cols(gs[2], me, sh),
               _cols(gs[3], me, 2 * BW // NDEV)[None], _cols(gs[4], me, sh)] + gs[5:]

    names = ["a_w_in", "a_w_out", "w_kv", "b_w_q", "b_w_o", "w_up0", "w_up1", "w_down0", "w_down1", "a_w_gate"]
    big_m = [m_a_w_in[0], m_a_w_out[0], m_w_kv, m_b_w_q[0], m_b_w_o[0], m_w_up[0], m_w_up[1], m_w_down[0],
             m_w_down[1], m_a_w_gate.reshape(256, 128)]
    big_v = [v_a_w_in[0], v_a_w_out[0], v_w_kv, v_b_w_q[0], v_b_w_o[0], v_w_up[0], v_w_up[1], v_w_down[0],
             v_w_down[1], v_a_w_gate.reshape(256, 128)]
    res = [_adamw(w, m, v, g, "adamw_" + nm)
           for w, m, v, g, nm in zip(big_w + [a_w_gate.reshape(256, 128)], big_m, big_v, recv, names)]
    small_w = [a_norm, a_conv_w, a_conv_b, a_b_gate, a_lambda, kv_norm, k_norm, b_norm, b_q_norm, b_rel_bias, mlp_norm]
    small_m = [m_a_norm, m_a_conv_w, m_a_conv_b, m_a_b_gate, m_a_lambda, m_kv_norm, m_k_norm, m_b_norm, m_b_q_norm,
               m_b_rel_bias, m_mlp_norm]
    small_v = [v_a_norm, v_a_conv_w, v_a_conv_b, v_a_b_gate, v_a_lambda, v_kv_norm, v_k_norm, v_b_norm, v_b_q_norm,
               v_b_rel_bias, v_mlp_norm]
    pr = 72
    res_small = _adamw(_pack(small_w, pr), _pack(small_m, pr), _pack(small_v, pr), _pack(g_small, pr)[None],
                       "adamw_small")
    small_shapes = [w.shape for w in small_w]
    res_small = [_unpack(r.reshape(-1), small_shapes) for r in res_small]

    def assemble(t):
        b = [r[t] for r in res]
        s_ = res_small[t]
        return [s_[0], b[0][None], s_[1], s_[2], b[9].reshape(a_w_gate.shape), s_[3], s_[4], b[1][None],
                s_[5], b[2], s_[6], s_[7], b[3][None], s_[8], s_[9], b[4][None], s_[10],
                jnp.stack([b[5], b[6]]), jnp.stack([b[7], b[8]])]

    return tuple([loss, gx[None]] + assemble(0) + assemble(1) + assemble(2) + assemble(3))
```

```python
import functools

import numpy as np
import jax
import jax.numpy as jnp
from jax import lax
from jax.experimental import pallas as pl
from jax.experimental.pallas import tpu as pltpu

F32 = jnp.float32
BF16 = jnp.bfloat16

D = 1024
NH = 16
HD = 64
FF = 4096
NBLK = 8
BW = 128
CHUNK = 64
PADK = 512
NREL = 192
EPS = 1e-6
LRU_C = 8.0
NDEV = 8

V7X_VMEM_LIMIT = 56 * 1024 * 1024
TM = 512
TMM = 256
TL = 256
QB = 256
KB = QB + PADK
NEG = -1e30

ADAM_LR, ADAM_B1, ADAM_B2, ADAM_EPS, ADAM_WD, ADAM_STEP = 0.001, 0.9, 0.999, 1e-08, 0.01, 10

_NT = (((1,), (1,)), ((), ()))
_TN = (((0,), (0,)), ((), ()))


def _cp(sem=None):
    return pltpu.CompilerParams(dimension_semantics=sem, vmem_limit_bytes=V7X_VMEM_LIMIT)


def _full(shape):
    n = len(shape)
    return pl.BlockSpec(shape, lambda *a: (0,) * n, pipeline_mode=pl.Buffered(1))


def _rows(tm, width):
    return pl.BlockSpec((tm, width), lambda i: (i, 0))


def _rstd(h):
    return lax.rsqrt(jnp.mean(h * h, axis=-1, keepdims=True) + EPS)


def _sigmoid(x):
    return 1.0 / (1.0 + jnp.exp(-x))


def _expm1(x):
    small = x * (1.0 + x * (0.5 + x * (1.0 / 6.0 + x * (1.0 / 24.0))))
    return jnp.where(jnp.abs(x) < 0.03, small, jnp.exp(x) - 1.0)


def _softplus_neg(lam):
    e = jnp.exp(-jnp.abs(lam))
    series = e * (1.0 - e * (0.5 - e * (1.0 / 3.0 - e * 0.25)))
    return jnp.maximum(-lam, 0.0) + jnp.where(e < 0.01, series, jnp.log(1.0 + e))


_GELU_K = 0.7978845608028654


def _gelu(x):
    return 0.5 * x * (1.0 + jnp.tanh(_GELU_K * (x + 0.044715 * x * x * x)))


def _gelu_grad(x):
    t = jnp.tanh(_GELU_K * (x + 0.044715 * x * x * x))
    return 0.5 * (1.0 + t) + 0.5 * x * (1.0 - t * t) * _GELU_K * (1.0 + 3.0 * 0.044715 * x * x)


def _sum8(x):
    r, c = x.shape
    return jnp.sum(x.reshape(r // 8, 8, c), axis=0)


def _shift_down(x, s, fill, rows):
    return jnp.where(rows >= s, pltpu.roll(x, s, axis=0), fill)


def _shift_up(x, s, fill, rows, n):
    return jnp.where(rows < n - s, pltpu.roll(x, n - s, axis=0), fill)


def _scan_fwd(a, b):
    n = a.shape[0]
    rows = lax.broadcasted_iota(jnp.int32, a.shape, 0)
    s = 1
    while s < n:
        a_sh = _shift_down(a, s, 1.0, rows)
        b_sh = _shift_down(b, s, 0.0, rows)
        b = a * b_sh + b
        a = a * a_sh
        s *= 2
    return a, b


def _scan_bwd(a, b):
    n = a.shape[0]
    rows = lax.broadcasted_iota(jnp.int32, a.shape, 0)
    s = 1
    while s < n:
        a_sh = _shift_up(a, s, 1.0, rows, n)
        b_sh = _shift_up(b, s, 0.0, rows, n)
        b = a * b_sh + b
        a = a * a_sh
        s *= 2
    return a, b


def _lru_gates(rc, wg_n, bg_n, sp_n):
    g = jnp.dot(rc.astype(BF16), wg_n, preferred_element_type=F32) + bg_n
    rg = _sigmoid(g[:, :BW])
    ig = _sigmoid(g[:, BW:])
    la = (-LRU_C) * rg * sp_n
    a = jnp.exp(la)
    mult = jnp.sqrt(-_expm1(2.0 * la))
    return rg, ig, a, mult


def _conv(ext_ref, cw_ref, cb_ref, sl, n):
    out = cb_ref[:, sl] + cw_ref[0:1, sl] * ext_ref[5:5 + n, sl]
    for k in range(1, 4):
        out = out + cw_ref[k:k + 1, sl] * ext_ref[5 + k:5 + k + n, sl]
    return out


def _lru_fwd(h0, a_norm, w_in, conv_w, conv_b, wg, bg, lam):
    s = h0.shape[0]

    def body(h0_ref, an_ref, win_ref, cw_ref, cb_ref, wg_ref, bg_ref, lam_ref,
             gate_ref, rec_ref, hs_ref, y_ref, ext_ref, hc_ref):
        i = pl.program_id(0)

        @pl.when(i == 0)
        def _():
            ext_ref[0:8, :] = jnp.zeros((8, D), F32)
            hc_ref[...] = jnp.zeros_like(hc_ref)

        h = h0_ref[...]
        n1 = (h * _rstd(h) * an_ref[...]).astype(BF16)
        u = jnp.dot(n1, win_ref[...], preferred_element_type=F32)
        gate_ref[...] = u[:, :D]
        rec_ref[...] = u[:, D:]
        ext_ref[8:8 + TL, :] = u[:, D:]
        sp = _softplus_neg(lam_ref[...])
        for n in range(NBLK):
            sl = slice(n * BW, (n + 1) * BW)
            rc = _conv(ext_ref, cw_ref, cb_ref, sl, TL)
            rg, ig, a, mult = _lru_gates(rc, wg_ref[n], bg_ref[n:n + 1, :], sp[:, sl])
            acum, bcum = _scan_fwd(a, mult * (ig * rc))
            hh = acum * hc_ref[0:1, sl] + bcum
            hc_ref[0:1, sl] = hh[TL - 1:TL, :]
            hs_ref[:, sl] = hh
            y_ref[:, sl] = (_gelu(gate_ref[:, sl]) * hh).astype(BF16)
        ext_ref[0:8, :] = ext_ref[TL:TL + 8, :]

    row = _rows(TL, D)
    return pl.pallas_call(
        body, name="lru_fwd", grid=(s // TL,),
        in_specs=[row, _full((1, D)), _full((D, 2 * D)), _full((4, D)), _full((1, D)),
                  _full((NBLK, BW, 2 * BW)), _full((NBLK, 2 * BW)), _full((1, D))],
        out_specs=[row, row, row, row],
        out_shape=[jax.ShapeDtypeStruct((s, D), F32), jax.ShapeDtypeStruct((s, D), F32),
                   jax.ShapeDtypeStruct((s, D), F32), jax.ShapeDtypeStruct((s, D), BF16)],
        scratch_shapes=[pltpu.VMEM((TL + 8, D), F32), pltpu.VMEM((8, D), F32)],
        compiler_params=_cp(("arbitrary",)),
    )(h0, a_norm, w_in, conv_w, conv_b, wg, bg, lam)


def _mlp_fwd(res, px, pw, g, wu, wd, name):
    s = res.shape[0]
    fj = 512

    def body(res_ref, px_ref, pw_ref, g_ref, wu_ref, wd_ref, hin_ref, hout_ref, up_ref, n_ref):
        hin = res_ref[...] + jnp.dot(px_ref[...], pw_ref[...], preferred_element_type=F32)
        hin_ref[...] = hin
        hout_ref[...] = hin
        n_ref[...] = (hin * _rstd(hin) * g_ref[...]).astype(BF16)
        for j in range(FF // fj):
            sl = slice(j * fj, (j + 1) * fj)
            up = jnp.dot(n_ref[...], wu_ref[:, sl], preferred_element_type=F32)
            up_ref[:, sl] = up
            rl = jnp.maximum(up, 0.0)
            hout_ref[...] += jnp.dot((rl * rl).astype(BF16), wd_ref[sl, :], preferred_element_type=F32)

    row = _rows(TMM, D)
    return pl.pallas_call(
        body, name=name, grid=(s // TMM,),
        in_specs=[row, row, _full((D, D)), _full((1, D)), _full((D, FF)), _full((FF, D))],
        out_specs=[row, row, _rows(TMM, FF)],
        out_shape=[jax.ShapeDtypeStruct((s, D), F32), jax.ShapeDtypeStruct((s, D), F32),
                   jax.ShapeDtypeStruct((s, FF), F32)],
        scratch_shapes=[pltpu.VMEM((TMM, D), BF16)],
        compiler_params=_cp(("parallel",)),
    )(res, px, pw, g, wu, wd)


def _head_rstd(x2, lo):
    sq = x2 * x2
    s_lo = jnp.sum(jnp.where(lo, sq, 0.0), axis=-1, keepdims=True)
    s_hi = jnp.sum(jnp.where(lo, 0.0, sq), axis=-1, keepdims=True)
    return lax.rsqrt(jnp.where(lo, s_lo, s_hi) * (1.0 / HD) + EPS)


def _kvq_fwd(h2, kv_norm, b_norm, w_kv, w_q, k_norm_t, q_norm_t):
    s = h2.shape[0]
    assert PADK == TM

    def body(h_ref, gkv_ref, gb_ref, wkv_ref, wq_ref, kn_ref, qn_ref,
             kraw_ref, qraw_ref, k_ref, v_ref, q_ref):
        i = pl.program_id(0)

        @pl.when(i == 0)
        def _():
            k_ref[...] = jnp.zeros_like(k_ref)
            v_ref[...] = jnp.zeros_like(v_ref)

        @pl.when(i > 0)
        def _():
            h = h_ref[...]
            xhat = h * _rstd(h)
            kv = jnp.dot((xhat * gkv_ref[...]).astype(BF16), wkv_ref[...], preferred_element_type=F32)
            qr = jnp.dot((xhat * gb_ref[...]).astype(BF16), wq_ref[...], preferred_element_type=F32)
            kraw_ref[...] = kv[:, :D]
            qraw_ref[...] = qr
            v_ref[...] = kv[:, D:].astype(BF16)
            lo = lax.broadcasted_iota(jnp.int32, (1, 128), 1) < HD
            for p in range(D // 128):
                sl = slice(p * 128, (p + 1) * 128)
                k2 = kv[:, sl]
                k_ref[:, sl] = (k2 * _head_rstd(k2, lo) * kn_ref[:, sl]).astype(BF16)
                q2 = qr[:, sl]
                q_ref[:, sl] = (q2 * _head_rstd(q2, lo) * qn_ref[:, sl] * (HD ** -0.5)).astype(BF16)

    prev = pl.BlockSpec((TM, D), lambda i: (jnp.maximum(i - 1, 0), 0))
    cur = pl.BlockSpec((TM, D), lambda i: (i, 0))
    return pl.pallas_call(
        body, name="kvq_fwd", grid=(s // TM + 1,),
        in_specs=[prev, _full((1, D)), _full((1, D)), _full((D, 2 * D)), _full((D, D)), _full((1, D)), _full((1, D))],
        out_specs=[prev, prev, cur, cur, prev],
        out_shape=[jax.ShapeDtypeStruct((s, D), F32), jax.ShapeDtypeStruct((s, D), F32),
                   jax.ShapeDtypeStruct((s + PADK, D), BF16), jax.ShapeDtypeStruct((s + PADK, D), BF16),
                   jax.ShapeDtypeStruct((s, D), BF16)],
        compiler_params=_cp(("arbitrary",)),
    )(h2, kv_norm, b_norm, w_kv, w_q, k_norm_t, q_norm_t)


_TOEP = QB + KB


def _attn_exp(q2, kcat, bias_a, lo_a, valid):
    qa = jnp.where(lo_a, q2, jnp.zeros_like(q2))
    sc = lax.dot_general(qa, kcat, _NT, preferred_element_type=F32) + bias_a
    if valid is not None:
        sc = jnp.where(valid, sc, NEG)
    e = jnp.exp(sc - jnp.max(sc, axis=-1, keepdims=True))
    return e, 1.0 / jnp.sum(e, axis=-1, keepdims=True)


def _bias_from_diag(diag_ref, bias_ref):
    row8 = lax.broadcasted_iota(jnp.int32, (8, _TOEP), 0)
    kchunk = lax.broadcasted_iota(jnp.int32, (8, KB), 1) // CHUNK
    for a in range(2):
        v = jnp.broadcast_to(diag_ref[a:a + 1, :], (8, _TOEP))
        z0 = v
        for b in range(1, 8):
            z0 = jnp.where(row8 == b, pltpu.roll(v, b, axis=1), z0)
        for t in range(QB // 8):
            slab = z0 if t == 0 else pltpu.roll(z0, 8 * t, axis=1)
            qchunk = (8 * t) // CHUNK
            band = jnp.logical_and(kchunk >= qchunk, kchunk <= qchunk + PADK // CHUNK)
            bias_ref[a, 8 * t:8 * t + 8, :] = jnp.where(band, slab[:, :KB], NEG)


def _diag_sums(db_ref, a):
    row8 = lax.broadcasted_iota(jnp.int32, (8, _TOEP), 0)
    z = jnp.zeros((8, _TOEP), F32)
    for t in range(QB // 8):
        slab = jnp.concatenate([db_ref[a, 8 * t:8 * t + 8, :], jnp.zeros((8, _TOEP - KB), F32)], axis=1)
        z = z + (slab if t == 0 else pltpu.roll(slab, _TOEP - 8 * t, axis=1))
    e = z
    for b in range(1, 8):
        e = jnp.where(row8 == b, pltpu.roll(z, _TOEP - b, axis=1), e)
    return e


def _attn_specs(nqb):
    qspec = pl.BlockSpec((QB, 128), lambda p, j: (jnp.minimum(j, nqb - 1), p))
    kspecs = [pl.BlockSpec((QB, 128), functools.partial(lambda p, j, t: (jnp.minimum(j, nqb - 1) + t, p), t=t))
              for t in range(KB // QB)]
    dspec = pl.BlockSpec((None, 2, _TOEP), lambda p, j: (p, 0, 0))
    return qspec, kspecs, dspec


def _attn_fwd(q, kpad, vpad, diag):
    s = q.shape[0]
    nqb = s // QB
    npad = PADK // QB

    def body(q_ref, k0, k1, k2, v0, v1, v2, diag_ref, o_ref, bias_ref):
        j = pl.program_id(1)

        @pl.when(j == 0)
        def _():
            _bias_from_diag(diag_ref, bias_ref)

        def block(masked):
            kcat = jnp.concatenate([k0[...], k1[...], k2[...]], axis=0)
            vcat = jnp.concatenate([v0[...], v1[...], v2[...]], axis=0)
            q2 = q_ref[...]
            lo = lax.broadcasted_iota(jnp.int32, (1, 128), 1) < HD
            valid = (lax.broadcasted_iota(jnp.int32, (QB, KB), 1) + j * QB >= PADK) if masked else None
            outs = []
            for a in range(2):
                e, rl = _attn_exp(q2, kcat, bias_ref[a], lo if a == 0 else jnp.logical_not(lo), valid)
                outs.append(jnp.dot(e.astype(BF16), vcat, preferred_element_type=F32) * rl)
            o_ref[...] = jnp.where(lo, outs[0], outs[1]).astype(BF16)

        pl.when(j < npad)(functools.partial(block, True))
        pl.when(j >= npad)(functools.partial(block, False))

    qspec, kspecs, dspec = _attn_specs(nqb)
    assert len(kspecs) == 3
    return pl.pallas_call(
        body, name="attn_fwd", grid=(D // 128, nqb),
        in_specs=[qspec] + kspecs + kspecs + [dspec],
        out_specs=qspec,
        out_shape=jax.ShapeDtypeStruct((s, D), BF16),
        scratch_shapes=[pltpu.VMEM((2, QB, KB), F32)],
        compiler_params=_cp(("parallel", "arbitrary")),
    )(q, kpad, kpad, kpad, vpad, vpad, vpad, diag)


def _loss_grad(h4, tgt):
    s = h4.shape[0]

    def body(h_ref, t_ref, g_ref, l_ref):
        @pl.when(pl.program_id(0) == 0)
        def _():
            l_ref[...] = jnp.zeros_like(l_ref)

        d = h_ref[...] - t_ref[...]
        g_ref[...] = d * (1.0 / D)
        l_ref[...] += _sum8(d * d) * (0.5 / D)

    row = _rows(TM, D)
    return pl.pallas_call(
        body, name="loss_grad", grid=(s // TM,),
        in_specs=[row, row], out_specs=[row, pl.BlockSpec((8, D), lambda i: (0, 0))],
        out_shape=[jax.ShapeDtypeStruct((s, D), F32), jax.ShapeDtypeStruct((8, D), F32)],
        compiler_params=_cp(("arbitrary",)),
    )(h4, tgt)


def _rms_bwd(dn, xhat, r, g):
    dng = dn * g
    return r * (dng - xhat * jnp.mean(dng * xhat, axis=-1, keepdims=True))


def _acc_spec():
    return pl.BlockSpec((8, D), lambda i: (0, 0))


def _mlp_bwd(gout, hin, up, g, wu, wd, name):
    s = gout.shape[0]
    fj = 512

    def body(go_ref, hin_ref, up_ref, g_ref, wu_ref, wd_ref, gin_ref, dup_ref, act_ref, n_ref, dg_ref,
             gob_ref, dn_ref):
        @pl.when(pl.program_id(0) == 0)
        def _():
            dg_ref[...] = jnp.zeros_like(dg_ref)

        gob_ref[...] = go_ref[...].astype(BF16)
        for j in range(FF // fj):
            sl = slice(j * fj, (j + 1) * fj)
            rl = jnp.maximum(up_ref[:, sl], 0.0)
            act_ref[:, sl] = (rl * rl).astype(BF16)
            dact = lax.dot_general(gob_ref[...], wd_ref[sl, :], _NT, preferred_element_type=F32)
            dupj = (dact * (2.0 * rl)).astype(BF16)
            dup_ref[:, sl] = dupj
            part = lax.dot_general(dupj, wu_ref[:, sl], _NT, preferred_element_type=F32)
            if j == 0:
                dn_ref[...] = part
            else:
                dn_ref[...] += part
        hin = hin_ref[...]
        r = _rstd(hin)
        xhat = hin * r
        n_ref[...] = (xhat * g_ref[...]).astype(BF16)
        dn = dn_ref[...]
        gin_ref[...] = go_ref[...] + _rms_bwd(dn, xhat, r, g_ref[...])
        dg_ref[...] += _sum8(dn * xhat)

    row = _rows(TMM, D)
    wide = _rows(TMM, FF)
    return pl.pallas_call(
        body, name=name, grid=(s // TMM,),
        in_specs=[row, row, wide, _full((1, D)), _full((D, FF)), _full((FF, D))],
        out_specs=[row, wide, wide, row, _acc_spec()],
        out_shape=[jax.ShapeDtypeStruct((s, D), F32), jax.ShapeDtypeStruct((s, FF), BF16),
                   jax.ShapeDtypeStruct((s, FF), BF16), jax.ShapeDtypeStruct((s, D), BF16),
                   jax.ShapeDtypeStruct((8, D), F32)],
        scratch_shapes=[pltpu.VMEM((TMM, D), BF16), pltpu.VMEM((TMM, D), F32)],
        compiler_params=_cp(("arbitrary",)),
    )(gout, hin, up, g, wu, wd)


def _matmul_tn(a, b, slab, out_dtype, name):
    s, m = a.shape
    n = b.shape[1]
    ts = min(s, 512 if n > 2048 else 1024)
    nk = s // ts
    nc = 512
    w = n // NDEV

    def body(a_ref, b_ref, o_ref, at_ref, acc_ref):
        k = pl.program_id(0)
        at_ref[...] = a_ref[...].astype(BF16).T
        for c in range(n // nc):
            sl = slice(c * nc, (c + 1) * nc)
            part = jnp.dot(at_ref[...], b_ref[:, sl].astype(BF16), preferred_element_type=F32)

            @pl.when(k == 0)
            def _():
                acc_ref[:, sl] = part

            @pl.when(k > 0)
            def _():
                acc_ref[:, sl] += part

        @pl.when(k == nk - 1)
        def _():
            if slab:
                for d in range(NDEV):
                    o_ref[d] = acc_ref[:, d * w:(d + 1) * w].astype(out_dtype)
            else:
                o_ref[...] = acc_ref[...].astype(out_dtype)

    if slab:
        out_shape = jax.ShapeDtypeStruct((NDEV, m, w), out_dtype)
        out_spec = pl.BlockSpec((NDEV, m, w), lambda k: (0, 0, 0), pipeline_mode=pl.Buffered(1))
    else:
        out_shape = jax.ShapeDtypeStruct((m, n), out_dtype)
        out_spec = pl.BlockSpec((m, n), lambda k: (0, 0), pipeline_mode=pl.Buffered(1))
    return pl.pallas_call(
        body, name=name, grid=(nk,),
        in_specs=[pl.BlockSpec((ts, m), lambda k: (k, 0)), pl.BlockSpec((ts, n), lambda k: (k, 0))],
        out_specs=out_spec, out_shape=out_shape,
        scratch_shapes=[pltpu.VMEM((m, ts), BF16), pltpu.VMEM((m, n), F32)],
        compiler_params=_cp(("arbitrary",)),
    )(a, b)


def _matmul_nt(x, w, name):
    s, n = x.shape
    k = w.shape[0]

    def body(x_ref, w_ref, o_ref):
        o_ref[...] = lax.dot_general(x_ref[...].astype(BF16), w_ref[...], _NT,
                                     preferred_element_type=F32).astype(BF16)

    return pl.pallas_call(
        body, name=name, grid=(s // TM,),
        in_specs=[_rows(TM, n), _full((k, n))], out_specs=_rows(TM, k),
        out_shape=jax.ShapeDtypeStruct((s, k), BF16),
        compiler_params=_cp(("parallel",)),
    )(x, w)


def _attn_bwd(q, kpad, vpad, do, diag):
    s = q.shape[0]
    nqb = s // QB
    npad = PADK // QB

    def body(q_ref, k0, k1, k2, v0, v1, v2, do_ref, diag_ref, dq_ref, dk_ref, dv_ref, dd_ref,
             bias_ref, db_ref, dka_ref, dva_ref):
        j = pl.program_id(1)

        @pl.when(j == 0)
        def _():
            _bias_from_diag(diag_ref, bias_ref)
            dka_ref[...] = jnp.zeros_like(dka_ref)
            dva_ref[...] = jnp.zeros_like(dva_ref)
            db_ref[...] = jnp.zeros_like(db_ref)

        def block(masked):
            kcat = jnp.concatenate([k0[...], k1[...], k2[...]], axis=0)
            vcat = jnp.concatenate([v0[...], v1[...], v2[...]], axis=0)
            q2 = q_ref[...]
            do2 = do_ref[...]
            lo = lax.broadcasted_iota(jnp.int32, (1, 128), 1) < HD
            valid = (lax.broadcasted_iota(jnp.int32, (QB, KB), 1) + j * QB >= PADK) if masked else None
            dq, dk, dv = [], [], []
            for a in range(2):
                lo_a = lo if a == 0 else jnp.logical_not(lo)
                e, rl = _attn_exp(q2, kcat, bias_ref[a], lo_a, valid)
                p = e * rl
                doa = jnp.where(lo_a, do2, jnp.zeros_like(do2))
                dp = lax.dot_general(doa, vcat, _NT, preferred_element_type=F32)
                ds = p * (dp - jnp.sum(p * dp, axis=-1, keepdims=True))
                db_ref[a] += ds
                dsb = ds.astype(BF16)
                dq.append(jnp.dot(dsb, kcat, preferred_element_type=F32))
                dk.append(lax.dot_general(dsb, q2, _TN, preferred_element_type=F32))
                dv.append(lax.dot_general(p.astype(BF16), do2, _TN, preferred_element_type=F32))
            dq_ref[...] = jnp.where(lo, dq[0], dq[1]) * (HD ** -0.5)
            dka_ref[...] += jnp.where(lo, dk[0], dk[1])
            dva_ref[...] += jnp.where(lo, dv[0], dv[1])

        pl.when(j < npad)(functools.partial(block, True))
        pl.when(jnp.logical_and(j >= npad, j < nqb))(functools.partial(block, False))

        @pl.when(j == nqb - 1)
        def _():
            for a in range(2):
                dd_ref[a] = _diag_sums(db_ref, a)

        dk_ref[...] = dka_ref[0:QB, :]
        dv_ref[...] = dva_ref[0:QB, :]
        dka_ref[0:KB - QB, :] = dka_ref[QB:KB, :]
        dva_ref[0:KB - QB, :] = dva_ref[QB:KB, :]
        dka_ref[KB - QB:KB, :] = jnp.zeros((QB, 128), F32)
        dva_ref[KB - QB:KB, :] = jnp.zeros((QB, 128), F32)

    qspec, kspecs, dspec = _attn_specs(nqb)
    kout = pl.BlockSpec((QB, 128), lambda p, j: (jnp.maximum(j - npad, 0), p))
    sd = jax.ShapeDtypeStruct((s, D), F32)
    return pl.pallas_call(
        body, name="attn_bwd", grid=(D // 128, nqb + npad),
        in_specs=[qspec] + kspecs + kspecs + [qspec, dspec],
        out_specs=[qspec, kout, kout, pl.BlockSpec((None, 2, 8, _TOEP), lambda p, j: (p, 0, 0, 0))],
        out_shape=[sd, sd, sd, jax.ShapeDtypeStruct((NH // 2, 2, 8, _TOEP), F32)],
        scratch_shapes=[pltpu.VMEM((2, QB, KB), F32), pltpu.VMEM((2, QB, KB), F32),
                        pltpu.VMEM((KB, 128), F32), pltpu.VMEM((KB, 128), F32)],
        compiler_params=_cp(("parallel", "arbitrary")),
    )(q, kpad, kpad, kpad, vpad, vpad, vpad, do, diag)


def _head_norm_bwd(dy2, x2, g2, lo):
    rr = _head_rstd(x2, lo)
    xhat = x2 * rr
    t = dy2 * g2 * xhat
    m_lo = jnp.sum(jnp.where(lo, t, 0.0), axis=-1, keepdims=True)
    m_hi = jnp.sum(jnp.where(lo, 0.0, t), axis=-1, keepdims=True)
    m = jnp.where(lo, m_lo, m_hi) * (1.0 / HD)
    return rr * (dy2 * g2 - xhat * m), dy2 * xhat


def _kvq_bwd(dq, dk, dv, qraw, kraw, h2, g3, kv_norm, b_norm, w_kv, w_q, k_norm_t, q_norm_t):
    s = h2.shape[0]

    def body(dq_ref, dk_ref, dv_ref, qraw_ref, kraw_ref, h_ref, g3_ref, gkv_ref, gb_ref, wkv_ref, wq_ref,
             kn_ref, qn_ref, g2_ref, dqr_ref, dkv_ref, nb_ref, nk_ref, dgq_ref, dgk_ref, dgb_ref, dgkv_ref):
        @pl.when(pl.program_id(0) == 0)
        def _():
            for r in (dgq_ref, dgk_ref, dgb_ref, dgkv_ref):
                r[...] = jnp.zeros_like(r)

        lo = lax.broadcasted_iota(jnp.int32, (1, 128), 1) < HD
        for p in range(D // 128):
            sl = slice(p * 128, (p + 1) * 128)
            dx, dgp = _head_norm_bwd(dq_ref[:, sl], qraw_ref[:, sl], qn_ref[:, sl], lo)
            dqr_ref[:, sl] = dx.astype(BF16)
            dgq_ref[:, sl] += _sum8(dgp)
            dx, dgp = _head_norm_bwd(dk_ref[:, sl], kraw_ref[:, sl], kn_ref[:, sl], lo)
            dkv_ref[:, sl] = dx.astype(BF16)
            dgk_ref[:, sl] += _sum8(dgp)
        dkv_ref[:, D:] = dv_ref[...].astype(BF16)
        dnb = lax.dot_general(dqr_ref[...], wq_ref[...], _NT, preferred_element_type=F32)
        dnk = lax.dot_general(dkv_ref[...], wkv_ref[...], _NT, preferred_element_type=F32)
        h = h_ref[...]
        r = _rstd(h)
        xhat = h * r
        dxg = dnb * gb_ref[...] + dnk * gkv_ref[...]
        g2_ref[...] = g3_ref[...] + r * (dxg - xhat * jnp.mean(dxg * xhat, axis=-1, keepdims=True))
        dgb_ref[...] += _sum8(dnb * xhat)
        dgkv_ref[...] += _sum8(dnk * xhat)
        nb_ref[...] = (xhat * gb_ref[...]).astype(BF16)
        nk_ref[...] = (xhat * gkv_ref[...]).astype(BF16)

    row = _rows(TM, D)
    sd = jax.ShapeDtypeStruct((s, D), BF16)
    acc = jax.ShapeDtypeStruct((8, D), F32)
    return pl.pallas_call(
        body, name="kvq_bwd", grid=(s // TM,),
        in_specs=[row] * 7 + [_full((1, D)), _full((1, D)), _full((D, 2 * D)), _full((D, D)), _full((1, D)),
                              _full((1, D))],
        out_specs=[row, row, _rows(TM, 2 * D), row, row] + [_acc_spec()] * 4,
        out_shape=[jax.ShapeDtypeStruct((s, D), F32), sd, jax.ShapeDtypeStruct((s, 2 * D), BF16), sd, sd,
                   acc, acc, acc, acc],
        compiler_params=_cp(("arbitrary",)),
    )(dq, dk, dv, qraw, kraw, h2, g3, kv_norm, b_norm, w_kv, w_q, k_norm_t, q_norm_t)


def _lru_bwd(g1, gate, rec, hs, w_out, conv_w, conv_b, wg, bg, lam):
    s = g1.shape[0]
    nt = s // TL

    def body(g1_ref, gate_ref, rec_ref, recp_ref, hs_ref, hsp_ref, wo_ref, cw_ref, cb_ref, wg_ref, bg_ref, lam_ref,
             du_ref, dcw_ref, dcb_ref, dwg_ref, dbg_ref, dlam_ref, ext_ref, dext_ref, cg_ref):
        i = pl.program_id(0)
        first_tile = i == nt - 1

        @pl.when(i == 0)
        def _():
            dext_ref[TL:TL + 8, :] = jnp.zeros((8, D), F32)
            cg_ref[...] = jnp.zeros_like(cg_ref)
            for r in (dcw_ref, dcb_ref, dwg_ref, dbg_ref, dlam_ref):
                r[...] = jnp.zeros_like(r)

        keep = jnp.where(first_tile, 0.0, 1.0)
        ext_ref[0:8, :] = recp_ref[...] * keep
        ext_ref[8:8 + TL, :] = rec_ref[...]
        dy = lax.dot_general(g1_ref[...].astype(BF16), wo_ref[...], _NT, preferred_element_type=F32)
        lam_v = lam_ref[...]
        sp = _softplus_neg(lam_v)
        dsp_dlam = -_sigmoid(-lam_v)
        rows = lax.broadcasted_iota(jnp.int32, (TL, BW), 0)
        for n in range(NBLK):
            sl = slice(n * BW, (n + 1) * BW)
            rc = _conv(ext_ref, cw_ref, cb_ref, sl, TL)
            rg, ig, a, mult = _lru_gates(rc, wg_ref[n], bg_ref[n:n + 1, :], sp[:, sl])
            h = hs_ref[:, sl]
            hprev = _shift_down(h, 1, hsp_ref[7:8, sl] * keep, rows)
            gt = gate_ref[:, sl]
            dyn = dy[:, sl]
            du_ref[:, sl] = (dyn * h * _gelu_grad(gt)).astype(BF16)
            dh = dyn * _gelu(gt)
            dh = dh + jnp.where(rows == TL - 1, cg_ref[0:1, sl], 0.0)
            _, gsc = _scan_bwd(_shift_up(a, 1, 0.0, rows, TL), dh)
            cg_ref[0:1, sl] = a[0:1, :] * gsc[0:1, :]
            da = gsc * hprev
            d_mult = gsc * ig * rc
            d_ig = gsc * mult * rc
            d_rc = gsc * mult * ig
            d_la = da * a - d_mult * (a * a) / mult
            d_rg = d_la * ((-LRU_C) * sp[:, sl])
            dlam_ref[:, sl] += _sum8(d_la * ((-LRU_C) * rg)) * dsp_dlam[:, sl]
            dg = jnp.concatenate([d_rg * rg * (1.0 - rg), d_ig * ig * (1.0 - ig)], axis=1)
            dgb = dg.astype(BF16)
            d_rc = d_rc + lax.dot_general(dgb, wg_ref[n], _NT, preferred_element_type=F32)
            dwg_ref[n] += lax.dot_general(rc.astype(BF16), dgb, _TN, preferred_element_type=F32)
            dbg_ref[n] += _sum8(dg)
            dext_ref[0:TL, sl] = d_rc
            dcb_ref[:, sl] += _sum8(d_rc)
            for k in range(4):
                dcw_ref[k, :, sl] += _sum8(d_rc * ext_ref[5 + k:5 + k + TL, sl])
        for k in range(4):
            part = cw_ref[3 - k:4 - k, :] * dext_ref[k:k + TL, :]
            acc = part if k == 0 else acc + part
        du_ref[:, D:] = acc.astype(BF16)
        dext_ref[TL:TL + 8, :] = dext_ref[0:8, :]

    rev = pl.BlockSpec((TL, D), lambda i: (nt - 1 - i, 0))
    rev8 = pl.BlockSpec((8, D), lambda i: (jnp.maximum((nt - 1 - i) * (TL // 8) - 1, 0), 0))
    acc = jax.ShapeDtypeStruct((8, D), F32)
    return pl.pallas_call(
        body, name="lru_bwd", grid=(nt,),
        in_specs=[rev, rev, rev, rev8, rev, rev8, _full((D, D)), _full((4, D)), _full((1, D)),
                  _full((NBLK, BW, 2 * BW)), _full((NBLK, 2 * BW)), _full((1, D))],
        out_specs=[pl.BlockSpec((TL, 2 * D), lambda i: (nt - 1 - i, 0)),
                   pl.BlockSpec((4, 8, D), lambda i: (0, 0, 0)), _acc_spec(),
                   pl.BlockSpec((NBLK, BW, 2 * BW), lambda i: (0, 0, 0)),
                   pl.BlockSpec((NBLK, 8, 2 * BW), lambda i: (0, 0, 0)), _acc_spec()],
        out_shape=[jax.ShapeDtypeStruct((s, 2 * D), BF16), jax.ShapeDtypeStruct((4, 8, D), F32), acc,
                   jax.ShapeDtypeStruct((NBLK, BW, 2 * BW), F32), jax.ShapeDtypeStruct((NBLK, 8, 2 * BW), F32), acc],
        scratch_shapes=[pltpu.VMEM((TL + 8, D), F32), pltpu.VMEM((TL + 8, D), F32), pltpu.VMEM((8, D), F32)],
        compiler_params=_cp(("arbitrary",)),
    )(g1, gate, rec, rec, hs, hs, w_out, conv_w, conv_b, wg, bg, lam)


def _a_in_bwd(du, h0, g1, a_norm, w_in):
    s = h0.shape[0]

    def body(du_ref, h_ref, g1_ref, an_ref, win_ref, gx_ref, n1_ref, dg_ref):
        @pl.when(pl.program_id(0) == 0)
        def _():
            dg_ref[...] = jnp.zeros_like(dg_ref)

        dn = lax.dot_general(du_ref[...], win_ref[...], _NT, preferred_element_type=F32)
        h = h_ref[...]
        r = _rstd(h)
        xhat = h * r
        gx_ref[...] = g1_ref[...] + _rms_bwd(dn, xhat, r, an_ref[...])
        n1_ref[...] = (xhat * an_ref[...]).astype(BF16)
        dg_ref[...] += _sum8(dn * xhat)

    row = _rows(TM, D)
    return pl.pallas_call(
        body, name="a_in_bwd", grid=(s // TM,),
        in_specs=[_rows(TM, 2 * D), row, row, _full((1, D)), _full((D, 2 * D))],
        out_specs=[row, row, _acc_spec()],
        out_shape=[jax.ShapeDtypeStruct((s, D), F32), jax.ShapeDtypeStruct((s, D), BF16),
                   jax.ShapeDtypeStruct((8, D), F32)],
        compiler_params=_cp(("arbitrary",)),
    )(du, h0, g1, a_norm, w_in)


def _rel_onehot():
    m = np.arange(_TOEP)
    signed = np.where(m < KB, m, m - _TOEP)
    idx = np.clip(PADK - signed, -(CHUNK - 1), 2 * CHUNK) + (CHUNK - 1)
    return (idx[None, :] == np.arange(NREL)[:, None]).astype(np.float32)


def _bias_diagonals(rel_bias):
    diag = jnp.dot(rel_bias, jnp.asarray(_rel_onehot()), precision=lax.Precision.HIGHEST)
    return diag.reshape(NH // 2, 2, _TOEP)


def _rel_bias_grad(dd):
    rows = 8
    z = dd
    oh = np.zeros((_TOEP, 256), np.float32)
    oh[:, :NREL] = _rel_onehot().T

    def body(z_ref, oh_ref, o_ref):
        d = jnp.sum(z_ref[...], axis=0, keepdims=True)
        hi = d.astype(BF16)
        mid = (d - hi.astype(F32)).astype(BF16)
        lo = (d - hi.astype(F32) - mid.astype(F32)).astype(BF16)
        ohb = oh_ref[...].astype(BF16)
        acc = jnp.zeros((8, 256), F32)
        for piece in (lo, mid, hi):
            acc = acc + jnp.dot(jnp.broadcast_to(piece, (8, _TOEP)), ohb, preferred_element_type=F32)
        o_ref[...] = acc

    out = pl.pallas_call(
        body, name="rel_bias_grad", grid=(NH,),
        in_specs=[pl.BlockSpec((None, rows, _TOEP), lambda h: (h, 0, 0)), pl.BlockSpec((_TOEP, 256), lambda h: (0, 0))],
        out_specs=pl.BlockSpec((None, 8, 256), lambda h: (h, 0, 0)),
        out_shape=jax.ShapeDtypeStruct((NH, 8, 256), F32),
        compiler_params=_cp(("parallel",)),
    )(z, jnp.asarray(oh))
    return out[:, 0, :NREL]


def _exchange(arrays, scatter, name):
    n = len(arrays)

    def body(*refs):
        ins, outs = refs[:n], refs[n:2 * n]
        send_sems, recv_sems, local_sems = refs[2 * n:]
        x, y, c = lax.axis_index("x"), lax.axis_index("y"), lax.axis_index("c")
        me = 4 * x + 2 * y + c

        def peer_of(r):
            rx, ry, rc = (r >> 2) & 1, (r >> 1) & 1, r & 1
            px = 1 - x if rx else x
            py = 1 - y if ry else y
            pc = 1 - c if rc else c
            return (px, py, pc), 4 * px + 2 * py + pc

        local, sent = [], []
        for k in range(n):
            cp = pltpu.make_async_copy(ins[k].at[me] if scatter else ins[k], outs[k].at[me], local_sems.at[k])
            cp.start()
            local.append(cp)
            for r in range(1, NDEV):
                peer, peer_lin = peer_of(r)
                cp = pltpu.make_async_remote_copy(
                    src_ref=ins[k].at[peer_lin] if scatter else ins[k], dst_ref=outs[k].at[me],
                    send_sem=send_sems.at[k, r - 1], recv_sem=recv_sems.at[k, r - 1],
                    device_id=peer, device_id_type=pl.DeviceIdType.MESH)
                cp.start()
                sent.append(cp)
        for k in range(n):
            for r in range(1, NDEV):
                peer, peer_lin = peer_of(r)
                pltpu.make_async_remote_copy(
                    src_ref=ins[k].at[peer_lin] if scatter else ins[k], dst_ref=outs[k].at[peer_lin],
                    send_sem=send_sems.at[k, r - 1], recv_sem=recv_sems.at[k, r - 1],
                    device_id=peer, device_id_type=pl.DeviceIdType.MESH).wait_recv()
        for cp in sent:
            cp.wait_send()
        for cp in local:
            cp.wait()

    def slot_shape(a):
        return (NDEV,) + (a.shape[1:] if scatter else a.shape)

    anyspec = pl.BlockSpec(memory_space=pl.ANY)
    return pl.pallas_call(
        body, name=name,
        in_specs=[anyspec] * n, out_specs=[anyspec] * n,
        out_shape=[jax.ShapeDtypeStruct(slot_shape(a), a.dtype) for a in arrays],
        scratch_shapes=[pltpu.SemaphoreType.DMA((n, NDEV - 1)), pltpu.SemaphoreType.DMA((n, NDEV - 1)),
                        pltpu.SemaphoreType.DMA((n,))],
        compiler_params=pltpu.CompilerParams(has_side_effects=True),
    )(*arrays)


def _sum_slots(st, name):
    _, r, c = st.shape

    def body(s_ref, o_ref):
        acc = s_ref[0]
        for d in range(1, NDEV):
            acc = acc + s_ref[d]
        o_ref[...] = acc

    return pl.pallas_call(
        body, name=name, out_shape=jax.ShapeDtypeStruct((r, c), F32),
        in_specs=[pl.BlockSpec((NDEV, r, c), lambda: (0, 0, 0))], out_specs=pl.BlockSpec((r, c), lambda: (0, 0)),
    )(st)


def _adamw(w, m, v, gst, name):
    r, c = w.shape
    ns = gst.shape[0]
    tr = min(r, 256)
    c1 = 1.0 - ADAM_B1 ** ADAM_STEP
    c2 = 1.0 - ADAM_B2 ** ADAM_STEP

    def body(w_ref, m_ref, v_ref, g_ref, go_ref, d_ref, mo_ref, vo_ref):
        g = g_ref[0].astype(F32)
        for d in range(1, ns):
            g = g + g_ref[d].astype(F32)
        m2 = ADAM_B1 * m_ref[...] + (1.0 - ADAM_B1) * g
        v2 = ADAM_B2 * v_ref[...] + (1.0 - ADAM_B2) * (g * g)
        go_ref[...] = g
        mo_ref[...] = m2
        vo_ref[...] = v2
        d_ref[...] = (-ADAM_LR) * ((m2 / c1) / (jnp.sqrt(v2 / c2) + ADAM_EPS) + ADAM_WD * w_ref[...])

    blk = pl.BlockSpec((tr, c), lambda i: (i, 0))
    sd = jax.ShapeDtypeStruct((r, c), F32)
    return pl.pallas_call(
        body, name=name, grid=(r // tr,),
        in_specs=[blk, blk, blk, pl.BlockSpec((ns, tr, c), lambda i: (0, i, 0))],
        out_specs=[blk, blk, blk, blk], out_shape=[sd, sd, sd, sd],
        compiler_params=_cp(("parallel",)),
    )(w, m, v, gst)


def _pack(pieces, rows):
    flat = jnp.concatenate([p.reshape(-1).astype(F32) for p in pieces])
    return jnp.pad(flat, (0, rows * 128 - flat.shape[0])).reshape(rows, 128)


def _unpack(flat, shapes):
    out, off = [], 0
    for shp in shapes:
        size = int(np.prod(shp))
        out.append(flat[off:off + size].reshape(shp))
        off += size
    return out


def _cols(full, me, width):
    return lax.dynamic_slice_in_dim(full, me * width, width, axis=full.ndim - 1)


def kernel(x, a_norm, a_w_in, a_conv_w, a_conv_b, a_w_gate, a_b_gate, a_lambda, a_w_out, kv_norm, w_kv, k_norm, b_norm, b_w_q, b_q_norm, b_rel_bias, b_w_o, mlp_norm, w_up, w_down, loss_target, m_a_norm, m_a_w_in, m_a_conv_w, m_a_conv_b, m_a_w_gate, m_a_b_gate, m_a_lambda, m_a_w_out, m_kv_norm, m_w_kv, m_k_norm, m_b_norm, m_b_w_q, m_b_q_norm, m_b_rel_bias, m_b_w_o, m_mlp_norm, m_w_up, m_w_down, v_a_norm, v_a_w_in, v_a_conv_w, v_a_conv_b, v_a_w_gate, v_a_b_gate, v_a_lambda, v_a_w_out, v_kv_norm, v_w_kv, v_k_norm, v_b_norm, v_b_w_q, v_b_q_norm, v_b_rel_bias, v_b_w_o, v_mlp_norm, v_w_up, v_w_down):
    me = 4 * lax.axis_index("x") + 2 * lax.axis_index("y") + lax.axis_index("c")
    sh = D // NDEV

    big_w = [a_w_in[0], a_w_out[0], w_kv, b_w_q[0], b_w_o[0], w_up[0], w_up[1], w_down[0], w_down[1]]
    small_sharded = [a_norm, a_conv_w, a_conv_b, a_b_gate, a_lambda, a_w_gate]
    small_rows = 272
    got = _exchange([w.astype(BF16) for w in big_w] + [_pack(small_sharded, small_rows)], False, "gather_weights")
    w_in = got[0].transpose(1, 0, 2).reshape(D, 2 * D)
    w_out = got[1].reshape(D, D)
    wkv = got[2].transpose(1, 0, 2).reshape(D, 2 * D)
    w_q = got[3].reshape(D, D)
    w_o = got[4].reshape(D, D)
    wu = [got[5].transpose(1, 0, 2).reshape(D, FF), got[6].transpose(1, 0, 2).reshape(D, FF)]
    wd = [got[7].reshape(FF, D), got[8].reshape(FF, D)]
    sm = got[9].reshape(NDEV, small_rows * 128)
    an_f = sm[:, 0:128].reshape(1, D)
    cw_f = sm[:, 128:640].reshape(NDEV, 4, sh).transpose(1, 0, 2).reshape(4, D)
    cb_f = sm[:, 640:768].reshape(1, D)
    bg_f = sm[:, 768:1024].reshape(NDEV, NBLK, 2 * BW // NDEV).transpose(1, 0, 2).reshape(NBLK, 2 * BW)
    lam_f = sm[:, 1024:1152].reshape(1, D)
    wg_f = sm[:, 1152:1152 + NBLK * BW * 32].reshape(NDEV, NBLK, BW, 32).transpose(1, 2, 0, 3)
    wg_f = wg_f.reshape(NBLK, BW, 2 * BW).astype(BF16)
    kn_t = jnp.tile(k_norm, NH).reshape(1, D)
    qn_t = jnp.tile(b_q_norm[0], NH).reshape(1, D)
    kvn = kv_norm.reshape(1, D)
    diag = _bias_diagonals(b_rel_bias[0])

    h0 = x[0]
    gate, rec, hs, y = _lru_fwd(h0, an_f, w_in, cw_f, cb_f, wg_f, bg_f, lam_f)
    h1, h2, up0 = _mlp_fwd(h0, y, w_out, mlp_norm[0:1], wu[0], wd[0], "mlp_fwd0")
    kraw, qraw, kpad, vpad, q = _kvq_fwd(h2, kvn, b_norm, wkv, w_q, kn_t, qn_t)
    o = _attn_fwd(q, kpad, vpad, diag)
    h3, h4, up1 = _mlp_fwd(h2, o, w_o, mlp_norm[1:2], wu[1], wd[1], "mlp_fwd1")
    g4, lpart = _loss_grad(h4, loss_target[0])
    loss = lax.psum(jnp.sum(lpart), ("x", "y", "c"))

    g3, dup1, act1, n3, dgm1 = _mlp_bwd(g4, h3, up1, mlp_norm[1:2], wu[1], wd[1], "mlp_bwd1")
    d_wd1 = _matmul_tn(g4, act1, True, BF16, "dw_down1")
    d_wu1 = _matmul_tn(n3, dup1, True, BF16, "dw_up1")
    do = _matmul_nt(g3, w_o, "do_proj")
    d_wo = _matmul_tn(o, g3, False, BF16, "dw_o").reshape(NDEV, sh, D)
    dq, dk, dv, dd = _attn_bwd(q, kpad, vpad, do, diag)
    g2, dqr, dkv, nb, nk, dgq, dgk, dgb, dgkv = _kvq_bwd(dq, dk, dv, qraw, kraw, h2, g3, kvn, b_norm, wkv, w_q,
                                                       kn_t, qn_t)
    d_wq = _matmul_tn(nb, dqr, False, BF16, "dw_q").reshape(NDEV, sh, D)
    d_wkv = _matmul_tn(nk, dkv, True, BF16, "dw_kv")
    g1, dup0, act0, n2, dgm0 = _mlp_bwd(g2, h1, up0, mlp_norm[0:1], wu[0], wd[0], "mlp_bwd0")
    d_wd0 = _matmul_tn(g2, act0, True, BF16, "dw_down0")
    d_wu0 = _matmul_tn(n2, dup0, True, BF16, "dw_up0")
    du, dcw, dcb, dwg, dbg, dlam = _lru_bwd(g1, gate, rec, hs, w_out, cw_f, cb_f, wg_f, bg_f, lam_f)
    d_wout = _matmul_tn(y, g1, False, BF16, "dw_out").reshape(NDEV, sh, D)
    gx, n1, dga = _a_in_bwd(du, h0, g1, an_f, w_in)
    d_win = _matmul_tn(n1, du, True, BF16, "dw_in")
    d_rel = _rel_bias_grad(dd.reshape(NH, 8, _TOEP))

    dwg_slab = dwg.reshape(NBLK, BW, NDEV, 32).transpose(2, 0, 1, 3).reshape(NDEV, 256, 128)
    big_g = [d_win, d_wout, d_wkv, d_wq, d_wo, d_wu0, d_wu1, d_wd0, d_wd1, dwg_slab]
    recv = list(_exchange(big_g, True, "scatter_grads"))
    recv[7] = recv[7].transpose(0, 2, 1)
    recv[8] = recv[8].transpose(0, 2, 1)
    small_full = [dga.sum(0), dcw.sum(1), dcb.sum(0), dbg.sum(1), dlam.sum(0), dgkv.sum(0),
                  dgk.sum(0).reshape(NH, HD).sum(0), dgb.sum(0), dgq.sum(0).reshape(NH, HD).sum(0), d_rel,
                  jnp.stack([dgm0.sum(0), dgm1.sum(0)])]
    small_g_rows = 136
    (gsm,) = _exchange([_pack(small_full, small_g_rows)], False, "gather_small_grads")
    gs = _unpack(_sum_slots(gsm, "sum_small_grads").reshape(-1),
                 [(1, D), (4, D), (1, D), (NBLK, 2 * BW), (1, D), (D,), (HD,), (1, D), (1, HD), (1, NH, NREL), (2, D)])
    g_small = [_cols(gs[0], me, sh), _cols(gs[1], me, sh)[None], _cols(gs[2], me, sh),
               _cols(gs[3], me, 2 * BW // NDEV)[None], _cols(gs[4], me, sh)] + gs[5:]

    names = ["a_w_in", "a_w_out", "w_kv", "b_w_q", "b_w_o", "w_up0", "w_up1", "w_down0", "w_down1", "a_w_gate"]
    big_m = [m_a_w_in[0], m_a_w_out[0], m_w_kv, m_b_w_q[0], m_b_w_o[0], m_w_up[0], m_w_up[1], m_w_down[0],
             m_w_down[1], m_a_w_gate.reshape(256, 128)]
    big_v = [v_a_w_in[0], v_a_w_out[0], v_w_kv, v_b_w_q[0], v_b_w_o[0], v_w_up[0], v_w_up[1], v_w_down[0],
             v_w_down[1], v_a_w_gate.reshape(256, 128)]
    res = [_adamw(w, m, v, g, "adamw_" + nm)
           for w, m, v, g, nm in zip(big_w + [a_w_gate.reshape(256, 128)], big_m, big_v, recv, names)]
    small_w = [a_norm, a_conv_w, a_conv_b, a_b_gate, a_lambda, kv_norm, k_norm, b_norm, b_q_norm, b_rel_bias, mlp_norm]
    small_m = [m_a_norm, m_a_conv_w, m_a_conv_b, m_a_b_gate, m_a_lambda, m_kv_norm, m_k_norm, m_b_norm, m_b_q_norm,
               m_b_rel_bias, m_mlp_norm]
    small_v = [v_a_norm, v_a_conv_w, v_a_conv_b, v_a_b_gate, v_a_lambda, v_kv_norm, v_k_norm, v_b_norm, v_b_q_norm,
               v_b_rel_bias, v_mlp_norm]
    pr = 72
    res_small = _adamw(_pack(small_w, pr), _pack(small_m, pr), _pack(small_v, pr), _pack(g_small, pr)[None],
                       "adamw_small")
    small_shapes = [w.shape for w in small_w]
    res_small = [_unpack(r.reshape(-1), small_shapes) for r in res_small]

    def assemble(t):
        b = [r[t] for r in res]
        s_ = res_small[t]
        return [s_[0], b[0][None], s_[1], s_[2], b[9].reshape(a_w_gate.shape), s_[3], s_[4], b[1][None],
                s_[5], b[2], s_[6], s_[7], b[3][None], s_[8], s_[9], b[4][None], s_[10],
                jnp.stack([b[5], b[6]]), jnp.stack([b[7], b[8]])]

    return tuple([loss, gx[None]] + assemble(0) + assemble(1) + assemble(2) + assemble(3))
```

```python
import functools

import numpy as np
import jax
import jax.numpy as jnp
from jax import lax
from jax.experimental import pallas as pl
from jax.experimental.pallas import tpu as pltpu

F32 = jnp.float32
BF16 = jnp.bfloat16

D = 1024
NH = 16
HD = 64
FF = 4096
NBLK = 8
BW = 128
CHUNK = 64
PADK = 512
NREL = 192
EPS = 1e-6
LRU_C = 8.0
NDEV = 8

V7X_VMEM_LIMIT = 56 * 1024 * 1024
TM = 512
TMM = 256
TL = 256
QB = 256
KB = QB + PADK
NEG = -1e30

ADAM_LR, ADAM_B1, ADAM_B2, ADAM_EPS, ADAM_WD, ADAM_STEP = 0.001, 0.9, 0.999, 1e-08, 0.01, 10

_NT = (((1,), (1,)), ((), ()))
_TN = (((0,), (0,)), ((), ()))


def _cp(sem=None):
    return pltpu.CompilerParams(dimension_semantics=sem, vmem_limit_bytes=V7X_VMEM_LIMIT)


def _full(shape):
    n = len(shape)
    return pl.BlockSpec(shape, lambda *a: (0,) * n, pipeline_mode=pl.Buffered(1))


def _rows(tm, width):
    return pl.BlockSpec((tm, width), lambda i: (i, 0))


def _rstd(h):
    return lax.rsqrt(jnp.mean(h * h, axis=-1, keepdims=True) + EPS)


def _sigmoid(x):
    return 1.0 / (1.0 + jnp.exp(-x))


def _expm1(x):
    small = x * (1.0 + x * (0.5 + x * (1.0 / 6.0 + x * (1.0 / 24.0))))
    return jnp.where(jnp.abs(x) < 0.03, small, jnp.exp(x) - 1.0)


def _softplus_neg(lam):
    e = jnp.exp(-jnp.abs(lam))
    series = e * (1.0 - e * (0.5 - e * (1.0 / 3.0 - e * 0.25)))
    return jnp.maximum(-lam, 0.0) + jnp.where(e < 0.01, series, jnp.log(1.0 + e))


_GELU_K = 0.7978845608028654


def _gelu(x):
    return 0.5 * x * (1.0 + jnp.tanh(_GELU_K * (x + 0.044715 * x * x * x)))


def _gelu_grad(x):
    t = jnp.tanh(_GELU_K * (x + 0.044715 * x * x * x))
    return 0.5 * (1.0 + t) + 0.5 * x * (1.0 - t * t) * _GELU_K * (1.0 + 3.0 * 0.044715 * x * x)


def _sum8(x):
    r, c = x.shape
    return jnp.sum(x.reshape(r // 8, 8, c), axis=0)


def _shift_down(x, s, fill, rows):
    return jnp.where(rows >= s, pltpu.roll(x, s, axis=0), fill)


def _shift_up(x, s, fill, rows, n):
    return jnp.where(rows < n - s, pltpu.roll(x, n - s, axis=0), fill)


def _scan_fwd(a, b):
    n = a.shape[0]
    rows = lax.broadcasted_iota(jnp.int32, a.shape, 0)
    s = 1
    while s < n:
        a_sh = _shift_down(a, s, 1.0, rows)
        b_sh = _shift_down(b, s, 0.0, rows)
        b = a * b_sh + b
        a = a * a_sh
        s *= 2
    return a, b


def _scan_bwd(a, b):
    n = a.shape[0]
    rows = lax.broadcasted_iota(jnp.int32, a.shape, 0)
    s = 1
    while s < n:
        a_sh = _shift_up(a, s, 1.0, rows, n)
        b_sh = _shift_up(b, s, 0.0, rows, n)
        b = a * b_sh + b
        a = a * a_sh
        s *= 2
    return a, b


def _lru_gates(rc, wg_n, bg_n, sp_n):
    g = jnp.dot(rc.astype(BF16), wg_n, preferred_element_type=F32) + bg_n
    rg = _sigmoid(g[:, :BW])
    ig = _sigmoid(g[:, BW:])
    la = (-LRU_C) * rg * sp_n
    a = jnp.exp(la)
    mult = jnp.sqrt(-_expm1(2.0 * la))
    return rg, ig, a, mult


def _conv(ext_ref, cw_ref, cb_ref, sl, n):
    out = cb_ref[:, sl] + cw_ref[0:1, sl] * ext_ref[5:5 + n, sl]
    for k in range(1, 4):
        out = out + cw_ref[k:k + 1, sl] * ext_ref[5 + k:5 + k + n, sl]
    return out


def _lru_fwd(h0, a_norm, w_in, conv_w, conv_b, wg, bg, lam):
    s = h0.shape[0]

    def body(h0_ref, an_ref, win_ref, cw_ref, cb_ref, wg_ref, bg_ref, lam_ref,
             gate_ref, rec_ref, hs_ref, y_ref, ext_ref, hc_ref):
        i = pl.program_id(0)

        @pl.when(i == 0)
        def _():
            ext_ref[0:8, :] = jnp.zeros((8, D), F32)
            hc_ref[...] = jnp.zeros_like(hc_ref)

        h = h0_ref[...]
        n1 = (h * _rstd(h) * an_ref[...]).astype(BF16)
        u = jnp.dot(n1, win_ref[...], preferred_element_type=F32)
        gate_ref[...] = u[:, :D]
        rec_ref[...] = u[:, D:]
        ext_ref[8:8 + TL, :] = u[:, D:]
        sp = _softplus_neg(lam_ref[...])
        for n in range(NBLK):
            sl = slice(n * BW, (n + 1) * BW)
            rc = _conv(ext_ref, cw_ref, cb_ref, sl, TL)
            rg, ig, a, mult = _lru_gates(rc, wg_ref[n], bg_ref[n:n + 1, :], sp[:, sl])
            acum, bcum = _scan_fwd(a, mult * (ig * rc))
            hh = acum * hc_ref[0:1, sl] + bcum
            hc_ref[0:1, sl] = hh[TL - 1:TL, :]
            hs_ref[:, sl] = hh
            y_ref[:, sl] = (_gelu(gate_ref[:, sl]) * hh).astype(BF16)
        ext_ref[0:8, :] = ext_ref[TL:TL + 8, :]

    row = _rows(TL, D)
    return pl.pallas_call(
        body, name="lru_fwd", grid=(s // TL,),
        in_specs=[row, _full((1, D)), _full((D, 2 * D)), _full((4, D)), _full((1, D)),
                  _full((NBLK, BW, 2 * BW)), _full((NBLK, 2 * BW)), _full((1, D))],
        out_specs=[row, row, row, row],
        out_shape=[jax.ShapeDtypeStruct((s, D), F32), jax.ShapeDtypeStruct((s, D), F32),
                   jax.ShapeDtypeStruct((s, D), F32), jax.ShapeDtypeStruct((s, D), BF16)],
        scratch_shapes=[pltpu.VMEM((TL + 8, D), F32), pltpu.VMEM((8, D), F32)],
        compiler_params=_cp(("arbitrary",)),
    )(h0, a_norm, w_in, conv_w, conv_b, wg, bg, lam)


def _mlp_fwd(res, px, pw, g, wu, wd, name):
    s = res.shape[0]
    fj = 512

    def body(res_ref, px_ref, pw_ref, g_ref, wu_ref, wd_ref, hin_ref, hout_ref, up_ref, n_ref):
        hin = res_ref[...] + jnp.dot(px_ref[...], pw_ref[...], preferred_element_type=F32)
        hin_ref[...] = hin
        hout_ref[...] = hin
        n_ref[...] = (hin * _rstd(hin) * g_ref[...]).astype(BF16)
        for j in range(FF // fj):
            sl = slice(j * fj, (j + 1) * fj)
            up = jnp.dot(n_ref[...], wu_ref[:, sl], preferred_element_type=F32)
            up_ref[:, sl] = up
            rl = jnp.maximum(up, 0.0)
            hout_ref[...] += jnp.dot((rl * rl).astype(BF16), wd_ref[sl, :], preferred_element_type=F32)

    row = _rows(TMM, D)
    return pl.pallas_call(
        body, name=name, grid=(s // TMM,),
        in_specs=[row, row, _full((D, D)), _full((1, D)), _full((D, FF)), _full((FF, D))],
        out_specs=[row, row, _rows(TMM, FF)],
        out_shape=[jax.ShapeDtypeStruct((s, D), F32), jax.ShapeDtypeStruct((s, D), F32),
                   jax.ShapeDtypeStruct((s, FF), F32)],
        scratch_shapes=[pltpu.VMEM((TMM, D), BF16)],
        compiler_params=_cp(("parallel",)),
    )(res, px, pw, g, wu, wd)


def _head_rstd(x2, lo):
    sq = x2 * x2
    s_lo = jnp.sum(jnp.where(lo, sq, 0.0), axis=-1, keepdims=True)
    s_hi = jnp.sum(jnp.where(lo, 0.0, sq), axis=-1, keepdims=True)
    return lax.rsqrt(jnp.where(lo, s_lo, s_hi) * (1.0 / HD) + EPS)


def _kvq_fwd(h2, kv_norm, b_norm, w_kv, w_q, k_norm_t, q_norm_t):
    s = h2.shape[0]
    assert PADK == TM

    def body(h_ref, gkv_ref, gb_ref, wkv_ref, wq_ref, kn_ref, qn_ref,
             kraw_ref, qraw_ref, k_ref, v_ref, q_ref):
        i = pl.program_id(0)

        @pl.when(i == 0)
        def _():
            k_ref[...] = jnp.zeros_like(k_ref)
            v_ref[...] = jnp.zeros_like(v_ref)

        @pl.when(i > 0)
        def _():
            h = h_ref[...]
            xhat = h * _rstd(h)
            kv = jnp.dot((xhat * gkv_ref[...]).astype(BF16), wkv_ref[...], preferred_element_type=F32)
            qr = jnp.dot((xhat * gb_ref[...]).astype(BF16), wq_ref[...], preferred_element_type=F32)
            kraw_ref[...] = kv[:, :D]
            qraw_ref[...] = qr
            v_ref[...] = kv[:, D:].astype(BF16)
            lo = lax.broadcasted_iota(jnp.int32, (1, 128), 1) < HD
            for p in range(D // 128):
                sl = slice(p * 128, (p + 1) * 128)
                k2 = kv[:, sl]
                k_ref[:, sl] = (k2 * _head_rstd(k2, lo) * kn_ref[:, sl]).astype(BF16)
                q2 = qr[:, sl]
                q_ref[:, sl] = (q2 * _head_rstd(q2, lo) * qn_ref[:, sl] * (HD ** -0.5)).astype(BF16)

    prev = pl.BlockSpec((TM, D), lambda i: (jnp.maximum(i - 1, 0), 0))
    cur = pl.BlockSpec((TM, D), lambda i: (i, 0))
    return pl.pallas_call(
        body, name="kvq_fwd", grid=(s // TM + 1,),
        in_specs=[prev, _full((1, D)), _full((1, D)), _full((D, 2 * D)), _full((D, D)), _full((1, D)), _full((1, D))],
        out_specs=[prev, prev, cur, cur, prev],
        out_shape=[jax.ShapeDtypeStruct((s, D), F32), jax.ShapeDtypeStruct((s, D), F32),
                   jax.ShapeDtypeStruct((s + PADK, D), BF16), jax.ShapeDtypeStruct((s + PADK, D), BF16),
                   jax.ShapeDtypeStruct((s, D), BF16)],
        compiler_params=_cp(("arbitrary",)),
    )(h2, kv_norm, b_norm, w_kv, w_q, k_norm_t, q_norm_t)


_TOEP = QB + KB


def _attn_exp(q2, kcat, bias_a, lo_a, valid):
    qa = jnp.where(lo_a, q2, jnp.zeros_like(q2))
    sc = lax.dot_general(qa, kcat, _NT, preferred_element_type=F32) + bias_a
    if valid is not None:
        sc = jnp.where(valid, sc, NEG)
    e = jnp.exp(sc - jnp.max(sc, axis=-1, keepdims=True))
    return e, 1.0 / jnp.sum(e, axis=-1, keepdims=True)


def _bias_from_diag(diag_ref, bias_ref):
    row8 = lax.broadcasted_iota(jnp.int32, (8, _TOEP), 0)
    kchunk = lax.broadcasted_iota(jnp.int32, (8, KB), 1) // CHUNK
    for a in range(2):
        v = jnp.broadcast_to(diag_ref[a:a + 1, :], (8, _TOEP))
        z0 = v
        for b in range(1, 8):
            z0 = jnp.where(row8 == b, pltpu.roll(v, b, axis=1), z0)
        for t in range(QB // 8):
            slab = z0 if t == 0 else pltpu.roll(z0, 8 * t, axis=1)
            qchunk = (8 * t) // CHUNK
            band = jnp.logical_and(kchunk >= qchunk, kchunk <= qchunk + PADK // CHUNK)
            bias_ref[a, 8 * t:8 * t + 8, :] = jnp.where(band, slab[:, :KB], NEG)


def _diag_sums(db_ref, a):
    row8 = lax.broadcasted_iota(jnp.int32, (8, _TOEP), 0)
    z = jnp.zeros((8, _TOEP), F32)
    for t in range(QB // 8):
        slab = jnp.concatenate([db_ref[a, 8 * t:8 * t + 8, :], jnp.zeros((8, _TOEP - KB), F32)], axis=1)
        z = z + (slab if t == 0 else pltpu.roll(slab, _TOEP - 8 * t, axis=1))
    e = z
    for b in range(1, 8):
        e = jnp.where(row8 == b, pltpu.roll(z, _TOEP - b, axis=1), e)
    return e


def _attn_specs(nqb):
    qspec = pl.BlockSpec((QB, 128), lambda p, j: (jnp.minimum(j, nqb - 1), p))
    kspecs = [pl.BlockSpec((QB, 128), functools.partial(lambda p, j, t: (jnp.minimum(j, nqb - 1) + t, p), t=t))
              for t in range(KB // QB)]
    dspec = pl.BlockSpec((None, 2, _TOEP), lambda p, j: (p, 0, 0))
    return qspec, kspecs, dspec


def _attn_fwd(q, kpad, vpad, diag):
    s = q.shape[0]
    nqb = s // QB
    npad = PADK // QB

    def body(q_ref, k0, k1, k2, v0, v1, v2, diag_ref, o_ref, bias_ref):
        j = pl.program_id(1)

        @pl.when(j == 0)
        def _():
            _bias_from_diag(diag_ref, bias_ref)

        def block(masked):
            kcat = jnp.concatenate([k0[...], k1[...], k2[...]], axis=0)
            vcat = jnp.concatenate([v0[...], v1[...], v2[...]], axis=0)
            q2 = q_ref[...]
            lo = lax.broadcasted_iota(jnp.int32, (1, 128), 1) < HD
            valid = (lax.broadcasted_iota(jnp.int32, (QB, KB), 1) + j * QB >= PADK) if masked else None
            outs = []
            for a in range(2):
                e, rl = _attn_exp(q2, kcat, bias_ref[a], lo if a == 0 else jnp.logical_not(lo), valid)
                outs.append(jnp.dot(e.astype(BF16), vcat, preferred_element_type=F32) * rl)
            o_ref[...] = jnp.where(lo, outs[0], outs[1]).astype(BF16)

        pl.when(j < npad)(functools.partial(block, True))
        pl.when(j >= npad)(functools.partial(block, False))

    qspec, kspecs, dspec = _attn_specs(nqb)
    assert len(kspecs) == 3
    return pl.pallas_call(
        body, name="attn_fwd", grid=(D // 128, nqb),
        in_specs=[qspec] + kspecs + kspecs + [dspec],
        out_specs=qspec,
        out_shape=jax.ShapeDtypeStruct((s, D), BF16),
        scratch_shapes=[pltpu.VMEM((2, QB, KB), F32)],
        compiler_params=_cp(("parallel", "arbitrary")),
    )(q, kpad, kpad, kpad, vpad, vpad, vpad, diag)


def _loss_grad(h4, tgt):
    s = h4.shape[0]

    def body(h_ref, t_ref, g_ref, l_ref):
        @pl.when(pl.program_id(0) == 0)
        def _():
            l_ref[...] = jnp.zeros_like(l_ref)

        d = h_ref[...] - t_ref[...]
        g_ref[...] = d * (1.0 / D)
        l_ref[...] += _sum8(d * d) * (0.5 / D)

    row = _rows(TM, D)
    return pl.pallas_call(
        body, name="loss_grad", grid=(s // TM,),
        in_specs=[row, row], out_specs=[row, pl.BlockSpec((8, D), lambda i: (0, 0))],
        out_shape=[jax.ShapeDtypeStruct((s, D), F32), jax.ShapeDtypeStruct((8, D), F32)],
        compiler_params=_cp(("arbitrary",)),
    )(h4, tgt)


def _rms_bwd(dn, xhat, r, g):
    dng = dn * g
    return r * (dng - xhat * jnp.mean(dng * xhat, axis=-1, keepdims=True))


def _acc_spec():
    return pl.BlockSpec((8, D), lambda i: (0, 0))


def _mlp_bwd(gout, hin, up, g, wu, wd, name):
    s = gout.shape[0]
    fj = 512

    def body(go_ref, hin_ref, up_ref, g_ref, wu_ref, wd_ref, gin_ref, dup_ref, act_ref, n_ref, dg_ref,
             gob_ref, dn_ref):
        @pl.when(pl.program_id(0) == 0)
        def _():
            dg_ref[...] = jnp.zeros_like(dg_ref)

        gob_ref[...] = go_ref[...].astype(BF16)
        for j in range(FF // fj):
            sl = slice(j * fj, (j + 1) * fj)
            rl = jnp.maximum(up_ref[:, sl], 0.0)
            act_ref[:, sl] = (rl * rl).astype(BF16)
            dact = lax.dot_general(gob_ref[...], wd_ref[sl, :], _NT, preferred_element_type=F32)
            dupj = (dact * (2.0 * rl)).astype(BF16)
            dup_ref[:, sl] = dupj
            part = lax.dot_general(dupj, wu_ref[:, sl], _NT, preferred_element_type=F32)
            if j == 0:
                dn_ref[...] = part
            else:
                dn_ref[...] += part
        hin = hin_ref[...]
        r = _rstd(hin)
        xhat = hin * r
        n_ref[...] = (xhat * g_ref[...]).astype(BF16)
        dn = dn_ref[...]
        gin_ref[...] = go_ref[...] + _rms_bwd(dn, xhat, r, g_ref[...])
        dg_ref[...] += _sum8(dn * xhat)

    row = _rows(TMM, D)
    wide = _rows(TMM, FF)
    return pl.pallas_call(
        body, name=name, grid=(s // TMM,),
        in_specs=[row, row, wide, _full((1, D)), _full((D, FF)), _full((FF, D))],
        out_specs=[row, wide, wide, row, _acc_spec()],
        out_shape=[jax.ShapeDtypeStruct((s, D), F32), jax.ShapeDtypeStruct((s, FF), BF16),
                   jax.ShapeDtypeStruct((s, FF), BF16), jax.ShapeDtypeStruct((s, D), BF16),
                   jax.ShapeDtypeStruct((8, D), F32)],
        scratch_shapes=[pltpu.VMEM((TMM, D), BF16), pltpu.VMEM((TMM, D), F32)],
        compiler_params=_cp(("arbitrary",)),
    )(gout, hin, up, g, wu, wd)


def _matmul_tn(a, b, slab, out_dtype, name):
    s, m = a.shape
    n = b.shape[1]
    ts = min(s, 512 if n > 2048 else 1024)
    nk = s // ts
    nc = 512
    w = n // NDEV

    def body(a_ref, b_ref, o_ref, at_ref, acc_ref):
        k = pl.program_id(0)
        at_ref[...] = a_ref[...].astype(BF16).T
        for c in range(n // nc):
            sl = slice(c * nc, (c + 1) * nc)
            part = jnp.dot(at_ref[...], b_ref[:, sl].astype(BF16), preferred_element_type=F32)

            @pl.when(k == 0)
            def _():
                acc_ref[:, sl] = part

            @pl.when(k > 0)
            def _():
                acc_ref[:, sl] += part

        @pl.when(k == nk - 1)
        def _():
            if slab:
                for d in range(NDEV):
                    o_ref[d] = acc_ref[:, d * w:(d + 1) * w].astype(out_dtype)
            else:
                o_ref[...] = acc_ref[...].astype(out_dtype)

    if slab:
        out_shape = jax.ShapeDtypeStruct((NDEV, m, w), out_dtype)
        out_spec = pl.BlockSpec((NDEV, m, w), lambda k: (0, 0, 0), pipeline_mode=pl.Buffered(1))
    else:
        out_shape = jax.ShapeDtypeStruct((m, n), out_dtype)
        out_spec = pl.BlockSpec((m, n), lambda k: (0, 0), pipeline_mode=pl.Buffered(1))
    return pl.pallas_call(
        body, name=name, grid=(nk,),
        in_specs=[pl.BlockSpec((ts, m), lambda k: (k, 0)), pl.BlockSpec((ts, n), lambda k: (k, 0))],
        out_specs=out_spec, out_shape=out_shape,
        scratch_shapes=[pltpu.VMEM((m, ts), BF16), pltpu.VMEM((m, n), F32)],
        compiler_params=_cp(("arbitrary",)),
    )(a, b)


def _matmul_nt(x, w, name):
    s, n = x.shape
    k = w.shape[0]

    def body(x_ref, w_ref, o_ref):
        o_ref[...] = lax.dot_general(x_ref[...].astype(BF16), w_ref[...], _NT,
                                     preferred_element_type=F32).astype(BF16)

    return pl.pallas_call(
        body, name=name, grid=(s // TM,),
        in_specs=[_rows(TM, n), _full((k, n))], out_specs=_rows(TM, k),
        out_shape=jax.ShapeDtypeStruct((s, k), BF16),
        compiler_params=_cp(("parallel",)),
    )(x, w)


def _attn_bwd(q, kpad, vpad, do, diag):
    s = q.shape[0]
    nqb = s // QB
    npad = PADK // QB

    def body(q_ref, k0, k1, k2, v0, v1, v2, do_ref, diag_ref, dq_ref, dk_ref, dv_ref, dd_ref,
             bias_ref, db_ref, dka_ref, dva_ref):
        j = pl.program_id(1)

        @pl.when(j == 0)
        def _():
            _bias_from_diag(diag_ref, bias_ref)
            dka_ref[...] = jnp.zeros_like(dka_ref)
            dva_ref[...] = jnp.zeros_like(dva_ref)
            db_ref[...] = jnp.zeros_like(db_ref)

        def block(masked):
            kcat = jnp.concatenate([k0[...], k1[...], k2[...]], axis=0)
            vcat = jnp.concatenate([v0[...], v1[...], v2[...]], axis=0)
            q2 = q_ref[...]
            do2 = do_ref[...]
            lo = lax.broadcasted_iota(jnp.int32, (1, 128), 1) < HD
            valid = (lax.broadcasted_iota(jnp.int32, (QB, KB), 1) + j * QB >= PADK) if masked else None
            dq, dk, dv = [], [], []
            for a in range(2):
                lo_a = lo if a == 0 else jnp.logical_not(lo)
                e, rl = _attn_exp(q2, kcat, bias_ref[a], lo_a, valid)
                p = e * rl
                doa = jnp.where(lo_a, do2, jnp.zeros_like(do2))
                dp = lax.dot_general(doa, vcat, _NT, preferred_element_type=F32)
                ds = p * (dp - jnp.sum(p * dp, axis=-1, keepdims=True))
                db_ref[a] += ds
                dsb = ds.astype(BF16)
                dq.append(jnp.dot(dsb, kcat, preferred_element_type=F32))
                dk.append(lax.dot_general(dsb, q2, _TN, preferred_element_type=F32))
                dv.append(lax.dot_general(p.astype(BF16), do2, _TN, preferred_element_type=F32))
            dq_ref[...] = jnp.where(lo, dq[0], dq[1]) * (HD ** -0.5)
            dka_ref[...] += jnp.where(lo, dk[0], dk[1])
            dva_ref[...] += jnp.where(lo, dv[0], dv[1])

        pl.when(j < npad)(functools.partial(block, True))
        pl.when(jnp.logical_and(j >= npad, j < nqb))(functools.partial(block, False))

        @pl.when(j == nqb - 1)
        def _():
            for a in range(2):
                dd_ref[a] = _diag_sums(db_ref, a)

        dk_ref[...] = dka_ref[0:QB, :]
        dv_ref[...] = dva_ref[0:QB, :]
        dka_ref[0:KB - QB, :] = dka_ref[QB:KB, :]
        dva_ref[0:KB - QB, :] = dva_ref[QB:KB, :]
        dka_ref[KB - QB:KB, :] = jnp.zeros((QB, 128), F32)
        dva_ref[KB - QB:KB, :] = jnp.zeros((QB, 128), F32)

    qspec, kspecs, dspec = _attn_specs(nqb)
    kout = pl.BlockSpec((QB, 128), lambda p, j: (jnp.maximum(j - npad, 0), p))
    sd = jax.ShapeDtypeStruct((s, D), F32)
    return pl.pallas_call(
        body, name="attn_bwd", grid=(D // 128, nqb + npad),
        in_specs=[qspec] + kspecs + kspecs + [qspec, dspec],
        out_specs=[qspec, kout, kout, pl.BlockSpec((None, 2, 8, _TOEP), lambda p, j: (p, 0, 0, 0))],
        out_shape=[sd, sd, sd, jax.ShapeDtypeStruct((NH // 2, 2, 8, _TOEP), F32)],
        scratch_shapes=[pltpu.VMEM((2, QB, KB), F32), pltpu.VMEM((2, QB, KB), F32),
                        pltpu.VMEM((KB, 128), F32), pltpu.VMEM((KB, 128), F32)],
        compiler_params=_cp(("parallel", "arbitrary")),
    )(q, kpad, kpad, kpad, vpad, vpad, vpad, do, diag)


def _head_norm_bwd(dy2, x2, g2, lo):
    rr = _head_rstd(x2, lo)
    xhat = x2 * rr
    t = dy2 * g2 * xhat
    m_lo = jnp.sum(jnp.where(lo, t, 0.0), axis=-1, keepdims=True)
    m_hi = jnp.sum(jnp.where(lo, 0.0, t), axis=-1, keepdims=True)
    m = jnp.where(lo, m_lo, m_hi) * (1.0 / HD)
    return rr * (dy2 * g2 - xhat * m), dy2 * xhat


def _kvq_bwd(dq, dk, dv, qraw, kraw, h2, g3, kv_norm, b_norm, w_kv, w_q, k_norm_t, q_norm_t):
    s = h2.shape[0]

    def body(dq_ref, dk_ref, dv_ref, qraw_ref, kraw_ref, h_ref, g3_ref, gkv_ref, gb_ref, wkv_ref, wq_ref,
             kn_ref, qn_ref, g2_ref, dqr_ref, dkv_ref, nb_ref, nk_ref, dgq_ref, dgk_ref, dgb_ref, dgkv_ref):
        @pl.when(pl.program_id(0) == 0)
        def _():
            for r in (dgq_ref, dgk_ref, dgb_ref, dgkv_ref):
                r[...] = jnp.zeros_like(r)

        lo = lax.broadcasted_iota(jnp.int32, (1, 128), 1) < HD
        for p in range(D // 128):
            sl = slice(p * 128, (p + 1) * 128)
            dx, dgp = _head_norm_bwd(dq_ref[:, sl], qraw_ref[:, sl], qn_ref[:, sl], lo)
            dqr_ref[:, sl] = dx.astype(BF16)
            dgq_ref[:, sl] += _sum8(dgp)
            dx, dgp = _head_norm_bwd(dk_ref[:, sl], kraw_ref[:, sl], kn_ref[:, sl], lo)
            dkv_ref[:, sl] = dx.astype(BF16)
            dgk_ref[:, sl] += _sum8(dgp)
        dkv_ref[:, D:] = dv_ref[...].astype(BF16)
        dnb = lax.dot_general(dqr_ref[...], wq_ref[...], _NT, preferred_element_type=F32)
        dnk = lax.dot_general(dkv_ref[...], wkv_ref[...], _NT, preferred_element_type=F32)
        h = h_ref[...]
        r = _rstd(h)
        xhat = h * r
        dxg = dnb * gb_ref[...] + dnk * gkv_ref[...]
        g2_ref[...] = g3_ref[...] + r * (dxg - xhat * jnp.mean(dxg * xhat, axis=-1, keepdims=True))
        dgb_ref[...] += _sum8(dnb * xhat)
        dgkv_ref[...] += _sum8(dnk * xhat)
        nb_ref[...] = (xhat * gb_ref[...]).astype(BF16)
        nk_ref[...] = (xhat * gkv_ref[...]).astype(BF16)

    row = _rows(TM, D)
    sd = jax.ShapeDtypeStruct((s, D), BF16)
    acc = jax.ShapeDtypeStruct((8, D), F32)
    return pl.pallas_call(
        body, name="kvq_bwd", grid=(s // TM,),
        in_specs=[row] * 7 + [_full((1, D)), _full((1, D)), _full((D, 2 * D)), _full((D, D)), _full((1, D)),
                              _full((1, D))],
        out_specs=[row, row, _rows(TM, 2 * D), row, row] + [_acc_spec()] * 4,
        out_shape=[jax.ShapeDtypeStruct((s, D), F32), sd, jax.ShapeDtypeStruct((s, 2 * D), BF16), sd, sd,
                   acc, acc, acc, acc],
        compiler_params=_cp(("arbitrary",)),
    )(dq, dk, dv, qraw, kraw, h2, g3, kv_norm, b_norm, w_kv, w_q, k_norm_t, q_norm_t)


def _lru_bwd(g1, gate, rec, hs, w_out, conv_w, conv_b, wg, bg, lam):
    s = g1.shape[0]
    nt = s // TL

    def body(g1_ref, gate_ref, rec_ref, recp_ref, hs_ref, hsp_ref, wo_ref, cw_ref, cb_ref, wg_ref, bg_ref, lam_ref,
             du_ref, dcw_ref, dcb_ref, dwg_ref, dbg_ref, dlam_ref, ext_ref, dext_ref, cg_ref):
        i = pl.program_id(0)
        first_tile = i == nt - 1

        @pl.when(i == 0)
        def _():
            dext_ref[TL:TL + 8, :] = jnp.zeros((8, D), F32)
            cg_ref[...] = jnp.zeros_like(cg_ref)
            for r in (dcw_ref, dcb_ref, dwg_ref, dbg_ref, dlam_ref):
                r[...] = jnp.zeros_like(r)

        keep = jnp.where(first_tile, 0.0, 1.0)
        ext_ref[0:8, :] = recp_ref[...] * keep
        ext_ref[8:8 + TL, :] = rec_ref[...]
        dy = lax.dot_general(g1_ref[...].astype(BF16), wo_ref[...], _NT, preferred_element_type=F32)
        lam_v = lam_ref[...]
        sp = _softplus_neg(lam_v)
        dsp_dlam = -_sigmoid(-lam_v)
        rows = lax.broadcasted_iota(jnp.int32, (TL, BW), 0)
        for n in range(NBLK):
            sl = slice(n * BW, (n + 1) * BW)
            rc = _conv(ext_ref, cw_ref, cb_ref, sl, TL)
            rg, ig, a, mult = _lru_gates(rc, wg_ref[n], bg_ref[n:n + 1, :], sp[:, sl])
            h = hs_ref[:, sl]
            hprev = _shift_down(h, 1, hsp_ref[7:8, sl] * keep, rows)
            gt = gate_ref[:, sl]
            dyn = dy[:, sl]
            du_ref[:, sl] = (dyn * h * _gelu_grad(gt)).astype(BF16)
            dh = dyn * _gelu(gt)
            dh = dh + jnp.where(rows == TL - 1, cg_ref[0:1, sl], 0.0)
            _, gsc = _scan_bwd(_shift_up(a, 1, 0.0, rows, TL), dh)
            cg_ref[0:1, sl] = a[0:1, :] * gsc[0:1, :]
            da = gsc * hprev
            d_mult = gsc * ig * rc
            d_ig = gsc * mult * rc
            d_rc = gsc * mult * ig
            d_la = da * a - d_mult * (a * a) / mult
            d_rg = d_la * ((-LRU_C) * sp[:, sl])
            dlam_ref[:, sl] += _sum8(d_la * ((-LRU_C) * rg)) * dsp_dlam[:, sl]
            dg = jnp.concatenate([d_rg * rg * (1.0 - rg), d_ig * ig * (1.0 - ig)], axis=1)
            dgb = dg.astype(BF16)
            d_rc = d_rc + lax.dot_general(dgb, wg_ref[n], _NT, preferred_element_type=F32)
            dwg_ref[n] += lax.dot_general(rc.astype(BF16), dgb, _TN, preferred_element_type=F32)
            dbg_ref[n] += _sum8(dg)
            dext_ref[0:TL, sl] = d_rc
            dcb_ref[:, sl] += _sum8(d_rc)
            for k in range(4):
                dcw_ref[k, :, sl] += _sum8(d_rc * ext_ref[5 + k:5 + k + TL, sl])
        for k in range(4):
            part = cw_ref[3 - k:4 - k, :] * dext_ref[k:k + TL, :]
            acc = part if k == 0 else acc + part
        du_ref[:, D:] = acc.astype(BF16)
        dext_ref[TL:TL + 8, :] = dext_ref[0:8, :]

    rev = pl.BlockSpec((TL, D), lambda i: (nt - 1 - i, 0))
    rev8 = pl.BlockSpec((8, D), lambda i: (jnp.maximum((nt - 1 - i) * (TL // 8) - 1, 0), 0))
    acc = jax.ShapeDtypeStruct((8, D), F32)
    return pl.pallas_call(
        body, name="lru_bwd", grid=(nt,),
        in_specs=[rev, rev, rev, rev8, rev, rev8, _full((D, D)), _full((4, D)), _full((1, D)),
                  _full((NBLK, BW, 2 * BW)), _full((NBLK, 2 * BW)), _full((1, D))],
        out_specs=[pl.BlockSpec((TL, 2 * D), lambda i: (nt - 1 - i, 0)),
                   pl.BlockSpec((4, 8, D), lambda i: (0, 0, 0)), _acc_spec(),
                   pl.BlockSpec((NBLK, BW, 2 * BW), lambda i: (0, 0, 0)),
                   pl.BlockSpec((NBLK, 8, 2 * BW), lambda i: (0, 0, 0)), _acc_spec()],
        out_shape=[jax.ShapeDtypeStruct((s, 2 * D), BF16), jax.ShapeDtypeStruct((4, 8, D), F32), acc,
                   jax.ShapeDtypeStruct((NBLK, BW, 2 * BW), F32), jax.ShapeDtypeStruct((NBLK, 8, 2 * BW), F32), acc],
        scratch_shapes=[pltpu.VMEM((TL + 8, D), F32), pltpu.VMEM((TL + 8, D), F32), pltpu.VMEM((8, D), F32)],
        compiler_params=_cp(("arbitrary",)),
    )(g1, gate, rec, rec, hs, hs, w_out, conv_w, conv_b, wg, bg, lam)


def _a_in_bwd(du, h0, g1, a_norm, w_in):
    s = h0.shape[0]

    def body(du_ref, h_ref, g1_ref, an_ref, win_ref, gx_ref, n1_ref, dg_ref):
        @pl.when(pl.program_id(0) == 0)
        def _():
            dg_ref[...] = jnp.zeros_like(dg_ref)

        dn = lax.dot_general(du_ref[...], win_ref[...], _NT, preferred_element_type=F32)
        h = h_ref[...]
        r = _rstd(h)
        xhat = h * r
        gx_ref[...] = g1_ref[...] + _rms_bwd(dn, xhat, r, an_ref[...])
        n1_ref[...] = (xhat * an_ref[...]).astype(BF16)
        dg_ref[...] += _sum8(dn * xhat)

    row = _rows(TM, D)
    return pl.pallas_call(
        body, name="a_in_bwd", grid=(s // TM,),
        in_specs=[_rows(TM, 2 * D), row, row, _full((1, D)), _full((D, 2 * D))],
        out_specs=[row, row, _acc_spec()],
        out_shape=[jax.ShapeDtypeStruct((s, D), F32), jax.ShapeDtypeStruct((s, D), BF16),
                   jax.ShapeDtypeStruct((8, D), F32)],
        compiler_params=_cp(("arbitrary",)),
    )(du, h0, g1, a_norm, w_in)


def _rel_onehot():
    m = np.arange(_TOEP)
    signed = np.where(m < KB, m, m - _TOEP)
    idx = np.clip(PADK - signed, -(CHUNK - 1), 2 * CHUNK) + (CHUNK - 1)
    return (idx[None, :] == np.arange(NREL)[:, None]).astype(np.float32)


def _bias_diagonals(rel_bias):
    diag = jnp.dot(rel_bias, jnp.asarray(_rel_onehot()), precision=lax.Precision.HIGHEST)
    return diag.reshape(NH // 2, 2, _TOEP)


def _rel_bias_grad(dd):
    rows = 8
    z = dd
    oh = np.zeros((_TOEP, 256), np.float32)
    oh[:, :NREL] = _rel_onehot().T

    def body(z_ref, oh_ref, o_ref):
        d = jnp.sum(z_ref[...], axis=0, keepdims=True)
        hi = d.astype(BF16)
        mid = (d - hi.astype(F32)).astype(BF16)
        lo = (d - hi.astype(F32) - mid.astype(F32)).astype(BF16)
        ohb = oh_ref[...].astype(BF16)
        acc = jnp.zeros((8, 256), F32)
        for piece in (lo, mid, hi):
            acc = acc + jnp.dot(jnp.broadcast_to(piece, (8, _TOEP)), ohb, preferred_element_type=F32)
        o_ref[...] = acc

    out = pl.pallas_call(
        body, name="rel_bias_grad", grid=(NH,),
        in_specs=[pl.BlockSpec((None, rows, _TOEP), lambda h: (h, 0, 0)), pl.BlockSpec((_TOEP, 256), lambda h: (0, 0))],
        out_specs=pl.BlockSpec((None, 8, 256), lambda h: (h, 0, 0)),
        out_shape=jax.ShapeDtypeStruct((NH, 8, 256), F32),
        compiler_params=_cp(("parallel",)),
    )(z, jnp.asarray(oh))
    return out[:, 0, :NREL]


def _exchange(arrays, scatter, name):
    n = len(arrays)

    def body(*refs):
        ins, outs = refs[:n], refs[n:2 * n]
        token, (send_sems, recv_sems, local_sems) = refs[2 * n], refs[2 * n + 1:]
        token[...] = jnp.zeros_like(token)
        x, y, c = lax.axis_index("x"), lax.axis_index("y"), lax.axis_index("c")
        me = 4 * x + 2 * y + c

        def peer_of(r):
            rx, ry, rc = (r >> 2) & 1, (r >> 1) & 1, r & 1
            px = 1 - x if rx else x
            py = 1 - y if ry else y
            pc = 1 - c if rc else c
            return (px, py, pc), 4 * px + 2 * py + pc

        local, sent = [], []
        for k in range(n):
            cp = pltpu.make_async_copy(ins[k].at[me] if scatter else ins[k], outs[k].at[me], local_sems.at[k])
            cp.start()
            local.append(cp)
            for r in range(1, NDEV):
                peer, peer_lin = peer_of(r)
                cp = pltpu.make_async_remote_copy(
                    src_ref=ins[k].at[peer_lin] if scatter else ins[k], dst_ref=outs[k].at[me],
                    send_sem=send_sems.at[k, r - 1], recv_sem=recv_sems.at[k, r - 1],
                    device_id=peer, device_id_type=pl.DeviceIdType.MESH)
                cp.start()
                sent.append(cp)
        for k in range(n):
            for r in range(1, NDEV):
                peer, peer_lin = peer_of(r)
                pltpu.make_async_remote_copy(
                    src_ref=ins[k].at[peer_lin] if scatter else ins[k], dst_ref=outs[k].at[peer_lin],
                    send_sem=send_sems.at[k, r - 1], recv_sem=recv_sems.at[k, r - 1],
                    device_id=peer, device_id_type=pl.DeviceIdType.MESH).wait_recv()
        for cp in sent:
            cp.wait_send()
        for cp in local:
            cp.wait()

    def slot_shape(a):
        return (NDEV,) + (a.shape[1:] if scatter else a.shape)

    anyspec = pl.BlockSpec(memory_space=pl.ANY)
    outs = pl.pallas_call(
        body, name=name,
        in_specs=[anyspec] * n, out_specs=[anyspec] * n + [pl.BlockSpec(memory_space=pltpu.VMEM)],
        out_shape=[jax.ShapeDtypeStruct(slot_shape(a), a.dtype) for a in arrays]
        + [jax.ShapeDtypeStruct((8, 128), F32)],
        scratch_shapes=[pltpu.SemaphoreType.DMA((n, NDEV - 1)), pltpu.SemaphoreType.DMA((n, NDEV - 1)),
                        pltpu.SemaphoreType.DMA((n,))],
        compiler_params=pltpu.CompilerParams(has_side_effects=True),
    )(*arrays)
    return outs[:n], outs[n]


def _peer(r):
    x, y, c = lax.axis_index("x"), lax.axis_index("y"), lax.axis_index("c")
    px = 1 - x if (r >> 2) & 1 else x
    py = 1 - y if (r >> 1) & 1 else y
    pc = 1 - c if r & 1 else c
    return (px, py, pc), 4 * px + 2 * py + pc


def _my_index():
    return 4 * lax.axis_index("x") + 2 * lax.axis_index("y") + lax.axis_index("c")


_HBM_SPEC = pl.BlockSpec(memory_space=pltpu.HBM)
_SEM_SPEC = pl.BlockSpec(memory_space=pltpu.SEMAPHORE)


_NPEER = NDEV - 1


def _exchange_start(a, scatter, name):
    slot = (NDEV,) + (a.shape[1:] if scatter else a.shape)

    def body(src, land, *rest):
        send_sems, recv_sems = rest[:_NPEER], rest[_NPEER:2 * _NPEER]
        token = rest[-1]
        me = _my_index()
        for r in range(1, NDEV):
            peer, peer_lin = _peer(r)
            pltpu.make_async_remote_copy(
                src_ref=src.at[peer_lin] if scatter else src, dst_ref=land.at[me],
                send_sem=send_sems[r - 1], recv_sem=recv_sems[r - 1],
                device_id=peer, device_id_type=pl.DeviceIdType.MESH).start()
        token[...] = jnp.zeros_like(token)

    sem = pltpu.SemaphoreType.DMA(())
    outs = pl.pallas_call(
        body, name=name,
        out_shape=(*[sem] * (2 * _NPEER), pltpu.HBM(a.shape, a.dtype), pltpu.HBM(slot, a.dtype),
                   jax.ShapeDtypeStruct((8, 128), F32)),
        in_specs=[_HBM_SPEC, _HBM_SPEC],
        out_specs=(*[_SEM_SPEC] * (2 * _NPEER), _HBM_SPEC, _HBM_SPEC, pl.BlockSpec(memory_space=pltpu.VMEM)),
        input_output_aliases={0: 2 * _NPEER, 1: 2 * _NPEER + 1},
        compiler_params=pltpu.CompilerParams(has_side_effects=pltpu.SideEffectType.DATAFLOW_SIDE_EFFECTING),
    )(pltpu.with_memory_space_constraint(a, pltpu.HBM),
      pltpu.with_memory_space_constraint(lax.empty(slot, a.dtype), pltpu.HBM))
    return outs[:_NPEER], outs[_NPEER:2 * _NPEER], outs[2 * _NPEER], outs[2 * _NPEER + 1], outs[-1]


def _exchange_wait(started, after, scatter, name):
    send_sems, recv_sems, src, land, _ = started

    def body(src_ref, land_ref, *rest):
        ssem, rsem = rest[:_NPEER], rest[_NPEER:2 * _NPEER]
        for r in range(1, NDEV):
            peer, peer_lin = _peer(r)
            cp = pltpu.make_async_remote_copy(
                src_ref=src_ref.at[peer_lin] if scatter else src_ref, dst_ref=land_ref.at[peer_lin],
                send_sem=ssem[r - 1], recv_sem=rsem[r - 1],
                device_id=peer, device_id_type=pl.DeviceIdType.MESH)
            cp.wait_send()
            cp.wait_recv()

    outs = pl.pallas_call(
        body, name=name,
        out_shape=(pltpu.HBM(src.shape, src.dtype), pltpu.HBM(land.shape, land.dtype)),
        in_specs=[_HBM_SPEC, _HBM_SPEC] + [_SEM_SPEC] * (2 * _NPEER) + [pl.BlockSpec(memory_space=pl.ANY)],
        out_specs=(_HBM_SPEC, _HBM_SPEC),
        input_output_aliases={0: 0, 1: 1},
        compiler_params=pltpu.CompilerParams(has_side_effects=pltpu.SideEffectType.DATAFLOW_SIDE_EFFECTING),
    )(src, land, *send_sems, *recv_sems, after)
    return outs[0], outs[1]


def _fill_own(land, own, me):
    return lax.dynamic_update_slice(land, own, (me,) + (0,) * (land.ndim - 1))


def _as_rows(a):
    return a.reshape(a.shape[:-2] + (a.shape[-2] * a.shape[-1] // D, D))


def _sum_slots(st, name):
    _, r, c = st.shape

    def body(s_ref, o_ref):
        acc = s_ref[0]
        for d in range(1, NDEV):
            acc = acc + s_ref[d]
        o_ref[...] = acc

    return pl.pallas_call(
        body, name=name, out_shape=jax.ShapeDtypeStruct((r, c), F32),
        in_specs=[pl.BlockSpec((NDEV, r, c), lambda: (0, 0, 0))], out_specs=pl.BlockSpec((r, c), lambda: (0, 0)),
    )(st)


def _adamw(w, m, v, gst, name):
    r, c = w.shape
    ns = gst.shape[0]
    tr = min(r, 256)
    c1 = 1.0 - ADAM_B1 ** ADAM_STEP
    c2 = 1.0 - ADAM_B2 ** ADAM_STEP

    def body(w_ref, m_ref, v_ref, g_ref, go_ref, d_ref, mo_ref, vo_ref):
        g = g_ref[0].astype(F32)
        for d in range(1, ns):
            g = g + g_ref[d].astype(F32)
        m2 = ADAM_B1 * m_ref[...] + (1.0 - ADAM_B1) * g
        v2 = ADAM_B2 * v_ref[...] + (1.0 - ADAM_B2) * (g * g)
        go_ref[...] = g
        mo_ref[...] = m2
        vo_ref[...] = v2
        d_ref[...] = (-ADAM_LR) * ((m2 / c1) / (jnp.sqrt(v2 / c2) + ADAM_EPS) + ADAM_WD * w_ref[...])

    blk = pl.BlockSpec((tr, c), lambda i: (i, 0))
    sd = jax.ShapeDtypeStruct((r, c), F32)
    return pl.pallas_call(
        body, name=name, grid=(r // tr,),
        in_specs=[blk, blk, blk, pl.BlockSpec((ns, tr, c), lambda i: (0, i, 0))],
        out_specs=[blk, blk, blk, blk], out_shape=[sd, sd, sd, sd],
        compiler_params=_cp(("parallel",)),
    )(w, m, v, gst)


def _pack(pieces, rows):
    flat = jnp.concatenate([p.reshape(-1).astype(F32) for p in pieces])
    return jnp.pad(flat, (0, rows * 128 - flat.shape[0])).reshape(rows, 128)


def _unpack(flat, shapes):
    out, off = [], 0
    for shp in shapes:
        size = int(np.prod(shp))
        out.append(flat[off:off + size].reshape(shp))
        off += size
    return out


def _cols(full, me, width):
    return lax.dynamic_slice_in_dim(full, me * width, width, axis=full.ndim - 1)


def kernel(x, a_norm, a_w_in, a_conv_w, a_conv_b, a_w_gate, a_b_gate, a_lambda, a_w_out, kv_norm, w_kv, k_norm, b_norm, b_w_q, b_q_norm, b_rel_bias, b_w_o, mlp_norm, w_up, w_down, loss_target, m_a_norm, m_a_w_in, m_a_conv_w, m_a_conv_b, m_a_w_gate, m_a_b_gate, m_a_lambda, m_a_w_out, m_kv_norm, m_w_kv, m_k_norm, m_b_norm, m_b_w_q, m_b_q_norm, m_b_rel_bias, m_b_w_o, m_mlp_norm, m_w_up, m_w_down, v_a_norm, v_a_w_in, v_a_conv_w, v_a_conv_b, v_a_w_gate, v_a_b_gate, v_a_lambda, v_a_w_out, v_kv_norm, v_w_kv, v_k_norm, v_b_norm, v_b_w_q, v_b_q_norm, v_b_rel_bias, v_b_w_o, v_mlp_norm, v_w_up, v_w_down):
    me = 4 * lax.axis_index("x") + 2 * lax.axis_index("y") + lax.axis_index("c")
    sh = D // NDEV

    big_w = [a_w_in[0], a_w_out[0], w_kv, b_w_q[0], b_w_o[0], w_up[0], w_up[1], w_down[0], w_down[1]]
    small_sharded = [a_norm, a_conv_w, a_conv_b, a_b_gate, a_lambda, a_w_gate]
    small_rows = 272
    def to_bf16(w, token):
        return (w + token[0, 0]).astype(BF16)

    got, tok_a = _exchange([a_w_in[0].astype(BF16), _pack(small_sharded, small_rows)], False, "gather_a")
    st_b1 = _exchange_start(jnp.concatenate([_as_rows(to_bf16(w, tok_a)) for w in (a_w_out[0], w_up[0], w_down[0])]),
                            False, "gather_b1_start")
    st_b2 = _exchange_start(jnp.concatenate([_as_rows(to_bf16(w, st_b1[4]))
                                             for w in (w_kv, b_w_q[0], b_w_o[0], w_up[1], w_down[1])]),
                            False, "gather_b2_start")
    w_in = got[0].transpose(1, 0, 2).reshape(D, 2 * D)
    sm = got[1].reshape(NDEV, small_rows * 128)
    an_f = sm[:, 0:128].reshape(1, D) + st_b2[4][0:1, 0:1]
    cw_f = sm[:, 128:640].reshape(NDEV, 4, sh).transpose(1, 0, 2).reshape(4, D)
    cb_f = sm[:, 640:768].reshape(1, D)
    bg_f = sm[:, 768:1024].reshape(NDEV, NBLK, 2 * BW // NDEV).transpose(1, 0, 2).reshape(NBLK, 2 * BW)
    lam_f = sm[:, 1024:1152].reshape(1, D)
    wg_f = sm[:, 1152:1152 + NBLK * BW * 32].reshape(NDEV, NBLK, BW, 32).transpose(1, 2, 0, 3)
    wg_f = wg_f.reshape(NBLK, BW, 2 * BW).astype(BF16)
    kn_t = jnp.tile(k_norm, NH).reshape(1, D)
    qn_t = jnp.tile(b_q_norm[0], NH).reshape(1, D)
    kvn = kv_norm.reshape(1, D)
    diag = _bias_diagonals(b_rel_bias[0])

    h0 = x[0]
    gate, rec, hs, y = _lru_fwd(h0, an_f, w_in, cw_f, cb_f, wg_f, bg_f, lam_f)
    def cols_of(land, lo, hi, width):
        return land[:, lo:hi].reshape(NDEV, D, width // NDEV).transpose(1, 0, 2).reshape(D, width)

    def rows_of(land, lo, hi):
        return land[:, lo:hi].reshape(NDEV * (hi - lo), D)

    own, land = _exchange_wait(st_b1, y, False, "gather_b1_wait")
    land = _fill_own(land, own[None], me)
    w_out = rows_of(land, 0, 128)
    wu = [cols_of(land, 128, 640, FF), None]
    wd = [rows_of(land, 640, 1152), None]
    h1, h2, up0 = _mlp_fwd(h0, y, w_out, mlp_norm[0:1], wu[0], wd[0], "mlp_fwd0")
    own, land = _exchange_wait(st_b2, h2, False, "gather_b2_wait")
    land = _fill_own(land, own[None], me)
    wkv = cols_of(land, 0, 256, 2 * D)
    w_q = rows_of(land, 256, 384)
    w_o = rows_of(land, 384, 512)
    wu[1] = cols_of(land, 512, 1024, FF)
    wd[1] = rows_of(land, 1024, 1536)
    kraw, qraw, kpad, vpad, q = _kvq_fwd(h2, kvn, b_norm, wkv, w_q, kn_t, qn_t)
    o = _attn_fwd(q, kpad, vpad, diag)
    h3, h4, up1 = _mlp_fwd(h2, o, w_o, mlp_norm[1:2], wu[1], wd[1], "mlp_fwd1")
    g4, lpart = _loss_grad(h4, loss_target[0])
    loss = lax.psum(jnp.sum(lpart), ("x", "y", "c"))

    g3, dup1, act1, n3, dgm1 = _mlp_bwd(g4, h3, up1, mlp_norm[1:2], wu[1], wd[1], "mlp_bwd1")
    d_wd1 = _matmul_tn(g4, act1, True, BF16, "dw_down1")
    d_wu1 = _matmul_tn(n3, dup1, True, BF16, "dw_up1")
    do = _matmul_nt(g3, w_o, "do_proj")
    d_wo = _matmul_tn(o, g3, False, BF16, "dw_o").reshape(NDEV, sh, D)
    dq, dk, dv, dd = _attn_bwd(q, kpad, vpad, do, diag)
    g2, dqr, dkv, nb, nk, dgq, dgk, dgb, dgkv = _kvq_bwd(dq, dk, dv, qraw, kraw, h2, g3, kvn, b_norm, wkv, w_q,
                                                       kn_t, qn_t)
    d_wq = _matmul_tn(nb, dqr, False, BF16, "dw_q").reshape(NDEV, sh, D)
    d_wkv = _matmul_tn(nk, dkv, True, BF16, "dw_kv")
    st_r1 = _exchange_start(jnp.concatenate([_as_rows(a) for a in (d_wd1, d_wu1, d_wo, d_wq, d_wkv)], axis=1), True,
                            "scatter_r1_start")
    g1, dup0, act0, n2, dgm0 = _mlp_bwd(g2, h1, up0, mlp_norm[0:1] + st_r1[4][0:1, 0:1], wu[0], wd[0], "mlp_bwd0")
    d_wd0 = _matmul_tn(g2, act0, True, BF16, "dw_down0")
    d_wu0 = _matmul_tn(n2, dup0, True, BF16, "dw_up0")
    st_r2 = _exchange_start(jnp.concatenate([_as_rows(d_wu0), _as_rows(d_wd0)], axis=1), True, "scatter_r2_start")
    du, dcw, dcb, dwg, dbg, dlam = _lru_bwd(g1, gate, rec, hs, w_out, cw_f, cb_f, wg_f, bg_f,
                                            lam_f + st_r2[4][0:1, 0:1])
    d_wout = _matmul_tn(y, g1, False, BF16, "dw_out").reshape(NDEV, sh, D)
    gx, n1, dga = _a_in_bwd(du, h0, g1, an_f, w_in)
    d_win = _matmul_tn(n1, du, True, BF16, "dw_in")
    d_rel = _rel_bias_grad(dd.reshape(NH, 8, _TOEP))

    dwg_slab = dwg.reshape(NBLK, BW, NDEV, 32).transpose(2, 0, 1, 3).reshape(NDEV, 256, 128)
    recv3, _ = _exchange([d_win, d_wout, dwg_slab], True, "scatter_r3")
    small_full = [dga.sum(0), dcw.sum(1), dcb.sum(0), dbg.sum(1), dlam.sum(0), dgkv.sum(0),
                  dgk.sum(0).reshape(NH, HD).sum(0), dgb.sum(0), dgq.sum(0).reshape(NH, HD).sum(0), d_rel,
                  jnp.stack([dgm0.sum(0), dgm1.sum(0)])]
    small_g_rows = 136
    (gsm,), _ = _exchange([_pack(small_full, small_g_rows)], False, "gather_small_grads")
    src, recv1 = _exchange_wait(st_r1, gsm, True, "scatter_r1_wait")
    recv1 = _fill_own(recv1, lax.dynamic_slice_in_dim(src, me, 1, 0), me)
    src, recv2 = _exchange_wait(st_r2, recv1, True, "scatter_r2_wait")
    recv2 = _fill_own(recv2, lax.dynamic_slice_in_dim(src, me, 1, 0), me)

    def slabs(land, lo, hi, shape):
        return land[:, lo:hi].reshape((NDEV,) + shape)

    recv = [recv3[0], recv3[1], slabs(recv1, 1280, 1536, (D, 2 * D // NDEV)), slabs(recv1, 1152, 1280, (sh, D)),
            slabs(recv1, 1024, 1152, (sh, D)), slabs(recv2, 0, 512, (D, FF // NDEV)),
            slabs(recv1, 512, 1024, (D, FF // NDEV)),
            slabs(recv2, 512, 1024, (D, FF // NDEV)).transpose(0, 2, 1),
            slabs(recv1, 0, 512, (D, FF // NDEV)).transpose(0, 2, 1), recv3[2]]
    gs = _unpack(_sum_slots(gsm, "sum_small_grads").reshape(-1),
                 [(1, D), (4, D), (1, D), (NBLK, 2 * BW), (1, D), (D,), (HD,), (1, D), (1, HD), (1, NH, NREL), (2, D)])
    g_small = [_cols(gs[0], me, sh), _cols(gs[1], me, sh)[None], _cols(gs[2], me, sh),
               _cols(gs[3], me, 2 * BW // NDEV)[None], _cols(gs[4], me, sh)] + gs[5:]

    names = ["a_w_in", "a_w_out", "w_kv", "b_w_q", "b_w_o", "w_up0", "w_up1", "w_down0", "w_down1", "a_w_gate"]
    big_m = [m_a_w_in[0], m_a_w_out[0], m_w_kv, m_b_w_q[0], m_b_w_o[0], m_w_up[0], m_w_up[1], m_w_down[0],
             m_w_down[1], m_a_w_gate.reshape(256, 128)]
    big_v = [v_a_w_in[0], v_a_w_out[0], v_w_kv, v_b_w_q[0], v_b_w_o[0], v_w_up[0], v_w_up[1], v_w_down[0],
             v_w_down[1], v_a_w_gate.reshape(256, 128)]
    res = [_adamw(w, m, v, g, "adamw_" + nm)
           for w, m, v, g, nm in zip(big_w + [a_w_gate.reshape(256, 128)], big_m, big_v, recv, names)]
    small_w = [a_norm, a_conv_w, a_conv_b, a_b_gate, a_lambda, kv_norm, k_norm, b_norm, b_q_norm, b_rel_bias, mlp_norm]
    small_m = [m_a_norm, m_a_conv_w, m_a_conv_b, m_a_b_gate, m_a_lambda, m_kv_norm, m_k_norm, m_b_norm, m_b_q_norm,
               m_b_rel_bias, m_mlp_norm]
    small_v = [v_a_norm, v_a_conv_w, v_a_conv_b, v_a_b_gate, v_a_lambda, v_kv_norm, v_k_norm, v_b_norm, v_b_q_norm,
               v_b_rel_bias, v_mlp_norm]
    pr = 72
    res_small = _adamw(_pack(small_w, pr), _pack(small_m, pr), _pack(small_v, pr), _pack(g_small, pr)[None],
                       "adamw_small")
    small_shapes = [w.shape for w in small_w]
    res_small = [_unpack(r.reshape(-1), small_shapes) for r in res_small]

    def assemble(t):
        b = [r[t] for r in res]
        s_ = res_small[t]
        return [s_[0], b[0][None], s_[1], s_[2], b[9].reshape(a_w_gate.shape), s_[3], s_[4], b[1][None],
                s_[5], b[2], s_[6], s_[7], b[3][None], s_[8], s_[9], b[4][None], s_[10],
                jnp.stack([b[5], b[6]]), jnp.stack([b[7], b[8]])]

    return tuple([loss, gx[None]] + assemble(0) + assemble(1) + assemble(2) + assemble(3))
```

```python
import functools

import numpy as np
import jax
import jax.numpy as jnp
from jax import lax
from jax.experimental import pallas as pl
from jax.experimental.pallas import tpu as pltpu

F32 = jnp.float32
BF16 = jnp.bfloat16

D = 1024
NH = 16
HD = 64
FF = 4096
NBLK = 8
BW = 128
CHUNK = 64
PADK = 512
NREL = 192
EPS = 1e-6
LRU_C = 8.0
NDEV = 8

V7X_VMEM_LIMIT = 56 * 1024 * 1024
TM = 512
TMM = 256
TL = 256
QB = 256
KB = QB + PADK
NEG = -1e30

ADAM_LR, ADAM_B1, ADAM_B2, ADAM_EPS, ADAM_WD, ADAM_STEP = 0.001, 0.9, 0.999, 1e-08, 0.01, 10

_NT = (((1,), (1,)), ((), ()))
_TN = (((0,), (0,)), ((), ()))


def _cp(sem=None):
    return pltpu.CompilerParams(dimension_semantics=sem, vmem_limit_bytes=V7X_VMEM_LIMIT)


def _full(shape):
    n = len(shape)
    return pl.BlockSpec(shape, lambda *a: (0,) * n, pipeline_mode=pl.Buffered(1))


def _rows(tm, width):
    return pl.BlockSpec((tm, width), lambda i: (i, 0))


def _rstd(h):
    return lax.rsqrt(jnp.mean(h * h, axis=-1, keepdims=True) + EPS)


def _sigmoid(x):
    return 1.0 / (1.0 + jnp.exp(-x))


def _expm1(x):
    small = x * (1.0 + x * (0.5 + x * (1.0 / 6.0 + x * (1.0 / 24.0))))
    return jnp.where(jnp.abs(x) < 0.03, small, jnp.exp(x) - 1.0)


def _softplus_neg(lam):
    e = jnp.exp(-jnp.abs(lam))
    series = e * (1.0 - e * (0.5 - e * (1.0 / 3.0 - e * 0.25)))
    return jnp.maximum(-lam, 0.0) + jnp.where(e < 0.01, series, jnp.log(1.0 + e))


_GELU_K = 0.7978845608028654


def _gelu(x):
    return 0.5 * x * (1.0 + jnp.tanh(_GELU_K * (x + 0.044715 * x * x * x)))


def _gelu_grad(x):
    t = jnp.tanh(_GELU_K * (x + 0.044715 * x * x * x))
    return 0.5 * (1.0 + t) + 0.5 * x * (1.0 - t * t) * _GELU_K * (1.0 + 3.0 * 0.044715 * x * x)


def _sum8(x):
    r, c = x.shape
    return jnp.sum(x.reshape(r // 8, 8, c), axis=0)


def _shift_down(x, s, fill, rows):
    return jnp.where(rows >= s, pltpu.roll(x, s, axis=0), fill)


def _shift_up(x, s, fill, rows, n):
    return jnp.where(rows < n - s, pltpu.roll(x, n - s, axis=0), fill)


def _scan_fwd(a, b):
    n = a.shape[0]
    rows = lax.broadcasted_iota(jnp.int32, a.shape, 0)
    s = 1
    while s < n:
        a_sh = _shift_down(a, s, 1.0, rows)
        b_sh = _shift_down(b, s, 0.0, rows)
        b = a * b_sh + b
        a = a * a_sh
        s *= 2
    return a, b


def _scan_bwd(a, b):
    n = a.shape[0]
    rows = lax.broadcasted_iota(jnp.int32, a.shape, 0)
    s = 1
    while s < n:
        a_sh = _shift_up(a, s, 1.0, rows, n)
        b_sh = _shift_up(b, s, 0.0, rows, n)
        b = a * b_sh + b
        a = a * a_sh
        s *= 2
    return a, b


def _lru_gates(rc, wg_n, bg_n, sp_n):
    g = jnp.dot(rc.astype(BF16), wg_n, preferred_element_type=F32) + bg_n
    rg = _sigmoid(g[:, :BW])
    ig = _sigmoid(g[:, BW:])
    la = (-LRU_C) * rg * sp_n
    a = jnp.exp(la)
    mult = jnp.sqrt(-_expm1(2.0 * la))
    return rg, ig, a, mult


def _conv(ext_ref, cw_ref, cb_ref, sl, n):
    out = cb_ref[:, sl] + cw_ref[0:1, sl] * ext_ref[5:5 + n, sl]
    for k in range(1, 4):
        out = out + cw_ref[k:k + 1, sl] * ext_ref[5 + k:5 + k + n, sl]
    return out


def _lru_fwd(h0, a_norm, w_in, conv_w, conv_b, wg, bg, lam):
    s = h0.shape[0]

    def body(h0_ref, an_ref, win_ref, cw_ref, cb_ref, wg_ref, bg_ref, lam_ref,
             gate_ref, rec_ref, hs_ref, y_ref, ext_ref, hc_ref):
        i = pl.program_id(0)

        @pl.when(i == 0)
        def _():
            ext_ref[0:8, :] = jnp.zeros((8, D), F32)
            hc_ref[...] = jnp.zeros_like(hc_ref)

        h = h0_ref[...]
        n1 = (h * _rstd(h) * an_ref[...]).astype(BF16)
        cw = 2 * D // NDEV
        for d in range(NDEV):
            ud = jnp.dot(n1, win_ref[d], preferred_element_type=F32)
            if d < NDEV // 2:
                gate_ref[:, d * cw:(d + 1) * cw] = ud
            else:
                rec_ref[:, d * cw - D:(d + 1) * cw - D] = ud
                ext_ref[8:8 + TL, d * cw - D:(d + 1) * cw - D] = ud
        sp = _softplus_neg(lam_ref[...])
        for n in range(NBLK):
            sl = slice(n * BW, (n + 1) * BW)
            rc = _conv(ext_ref, cw_ref, cb_ref, sl, TL)
            rg, ig, a, mult = _lru_gates(rc, wg_ref[n], bg_ref[n:n + 1, :], sp[:, sl])
            acum, bcum = _scan_fwd(a, mult * (ig * rc))
            hh = acum * hc_ref[0:1, sl] + bcum
            hc_ref[0:1, sl] = hh[TL - 1:TL, :]
            hs_ref[:, sl] = hh
            y_ref[:, sl] = (_gelu(gate_ref[:, sl]) * hh).astype(BF16)
        ext_ref[0:8, :] = ext_ref[TL:TL + 8, :]

    row = _rows(TL, D)
    return pl.pallas_call(
        body, name="lru_fwd", grid=(s // TL,),
        in_specs=[row, _full((1, D)), _full((NDEV, D, 2 * D // NDEV)), _full((4, D)), _full((1, D)),
                  _full((NBLK, BW, 2 * BW)), _full((NBLK, 2 * BW)), _full((1, D))],
        out_specs=[row, row, row, row],
        out_shape=[jax.ShapeDtypeStruct((s, D), F32), jax.ShapeDtypeStruct((s, D), F32),
                   jax.ShapeDtypeStruct((s, D), F32), jax.ShapeDtypeStruct((s, D), BF16)],
        scratch_shapes=[pltpu.VMEM((TL + 8, D), F32), pltpu.VMEM((8, D), F32)],
        compiler_params=_cp(("arbitrary",)),
    )(h0, a_norm, w_in, conv_w, conv_b, wg, bg, lam)


def _mlp_fwd(res, px, pw, g, wu, wd, name):
    s = res.shape[0]
    fj = 512

    def body(res_ref, px_ref, pw_ref, g_ref, wu_ref, wd_ref, hin_ref, hout_ref, up_ref, n_ref):
        hin = res_ref[...] + jnp.dot(px_ref[...], pw_ref[...], preferred_element_type=F32)
        hin_ref[...] = hin
        hout_ref[...] = hin
        n_ref[...] = (hin * _rstd(hin) * g_ref[...]).astype(BF16)
        for j in range(FF // fj):
            sl = slice(j * fj, (j + 1) * fj)
            up = jnp.dot(n_ref[...], wu_ref[j], preferred_element_type=F32)
            up_ref[:, sl] = up
            rl = jnp.maximum(up, 0.0)
            hout_ref[...] += jnp.dot((rl * rl).astype(BF16), wd_ref[sl, :], preferred_element_type=F32)

    row = _rows(TMM, D)
    return pl.pallas_call(
        body, name=name, grid=(s // TMM,),
        in_specs=[row, row, _full((D, D)), _full((1, D)), _full((NDEV, D, fj)), _full((FF, D))],
        out_specs=[row, row, _rows(TMM, FF)],
        out_shape=[jax.ShapeDtypeStruct((s, D), F32), jax.ShapeDtypeStruct((s, D), F32),
                   jax.ShapeDtypeStruct((s, FF), F32)],
        scratch_shapes=[pltpu.VMEM((TMM, D), BF16)],
        compiler_params=_cp(("parallel",)),
    )(res, px, pw, g, wu, wd)


def _head_rstd(x2, lo):
    sq = x2 * x2
    s_lo = jnp.sum(jnp.where(lo, sq, 0.0), axis=-1, keepdims=True)
    s_hi = jnp.sum(jnp.where(lo, 0.0, sq), axis=-1, keepdims=True)
    return lax.rsqrt(jnp.where(lo, s_lo, s_hi) * (1.0 / HD) + EPS)


def _kvq_fwd(h2, kv_norm, b_norm, w_kv, w_q, k_norm_t, q_norm_t):
    s = h2.shape[0]
    assert PADK == TM

    def body(h_ref, gkv_ref, gb_ref, wkv_ref, wq_ref, kn_ref, qn_ref,
             kraw_ref, qraw_ref, k_ref, v_ref, q_ref):
        i = pl.program_id(0)

        @pl.when(i == 0)
        def _():
            k_ref[...] = jnp.zeros_like(k_ref)
            v_ref[...] = jnp.zeros_like(v_ref)

        @pl.when(i > 0)
        def _():
            h = h_ref[...]
            xhat = h * _rstd(h)
            nk = (xhat * gkv_ref[...]).astype(BF16)
            qr = jnp.dot((xhat * gb_ref[...]).astype(BF16), wq_ref[...], preferred_element_type=F32)
            qraw_ref[...] = qr
            lo = lax.broadcasted_iota(jnp.int32, (1, 128), 1) < HD
            cw = 2 * D // NDEV
            for d in range(NDEV):
                kvd = jnp.dot(nk, wkv_ref[d], preferred_element_type=F32)
                if d < NDEV // 2:
                    kraw_ref[:, d * cw:(d + 1) * cw] = kvd
                    for p in range(cw // 128):
                        sl = slice(d * cw + p * 128, d * cw + (p + 1) * 128)
                        k2 = kvd[:, p * 128:(p + 1) * 128]
                        k_ref[:, sl] = (k2 * _head_rstd(k2, lo) * kn_ref[:, sl]).astype(BF16)
                else:
                    v_ref[:, d * cw - D:(d + 1) * cw - D] = kvd.astype(BF16)
            for p in range(D // 128):
                sl = slice(p * 128, (p + 1) * 128)
                q2 = qr[:, sl]
                q_ref[:, sl] = (q2 * _head_rstd(q2, lo) * qn_ref[:, sl] * (HD ** -0.5)).astype(BF16)

    prev = pl.BlockSpec((TM, D), lambda i: (jnp.maximum(i - 1, 0), 0))
    cur = pl.BlockSpec((TM, D), lambda i: (i, 0))
    return pl.pallas_call(
        body, name="kvq_fwd", grid=(s // TM + 1,),
        in_specs=[prev, _full((1, D)), _full((1, D)), _full((NDEV, D, 2 * D // NDEV)), _full((D, D)), _full((1, D)),
                  _full((1, D))],
        out_specs=[prev, prev, cur, cur, prev],
        out_shape=[jax.ShapeDtypeStruct((s, D), F32), jax.ShapeDtypeStruct((s, D), F32),
                   jax.ShapeDtypeStruct((s + PADK, D), BF16), jax.ShapeDtypeStruct((s + PADK, D), BF16),
                   jax.ShapeDtypeStruct((s, D), BF16)],
        compiler_params=_cp(("arbitrary",)),
    )(h2, kv_norm, b_norm, w_kv, w_q, k_norm_t, q_norm_t)


_TOEP = QB + KB


def _attn_exp(q2, kcat, bias_a, lo_a, valid):
    qa = jnp.where(lo_a, q2, jnp.zeros_like(q2))
    sc = lax.dot_general(qa, kcat, _NT, preferred_element_type=F32) + bias_a
    if valid is not None:
        sc = jnp.where(valid, sc, NEG)
    e = jnp.exp(sc - jnp.max(sc, axis=-1, keepdims=True))
    return e, 1.0 / jnp.sum(e, axis=-1, keepdims=True)


def _bias_from_diag(diag_ref, bias_ref):
    row8 = lax.broadcasted_iota(jnp.int32, (8, _TOEP), 0)
    kchunk = lax.broadcasted_iota(jnp.int32, (8, KB), 1) // CHUNK
    for a in range(2):
        v = jnp.broadcast_to(diag_ref[a:a + 1, :], (8, _TOEP))
        z0 = v
        for b in range(1, 8):
            z0 = jnp.where(row8 == b, pltpu.roll(v, b, axis=1), z0)
        for t in range(QB // 8):
            slab = z0 if t == 0 else pltpu.roll(z0, 8 * t, axis=1)
            qchunk = (8 * t) // CHUNK
            band = jnp.logical_and(kchunk >= qchunk, kchunk <= qchunk + PADK // CHUNK)
            bias_ref[a, 8 * t:8 * t + 8, :] = jnp.where(band, slab[:, :KB], NEG)


def _diag_sums(db_ref, a):
    row8 = lax.broadcasted_iota(jnp.int32, (8, _TOEP), 0)
    z = jnp.zeros((8, _TOEP), F32)
    for t in range(QB // 8):
        slab = jnp.concatenate([db_ref[a, 8 * t:8 * t + 8, :], jnp.zeros((8, _TOEP - KB), F32)], axis=1)
        z = z + (slab if t == 0 else pltpu.roll(slab, _TOEP - 8 * t, axis=1))
    e = z
    for b in range(1, 8):
        e = jnp.where(row8 == b, pltpu.roll(z, _TOEP - b, axis=1), e)
    return e


def _attn_specs(nqb):
    qspec = pl.BlockSpec((QB, 128), lambda p, j: (jnp.minimum(j, nqb - 1), p))
    kspecs = [pl.BlockSpec((QB, 128), functools.partial(lambda p, j, t: (jnp.minimum(j, nqb - 1) + t, p), t=t))
              for t in range(KB // QB)]
    dspec = pl.BlockSpec((None, 2, _TOEP), lambda p, j: (p, 0, 0))
    return qspec, kspecs, dspec


def _attn_fwd(q, kpad, vpad, diag):
    s = q.shape[0]
    nqb = s // QB
    npad = PADK // QB

    def body(q_ref, k0, k1, k2, v0, v1, v2, diag_ref, o_ref, bias_ref):
        j = pl.program_id(1)

        @pl.when(j == 0)
        def _():
            _bias_from_diag(diag_ref, bias_ref)

        def block(masked):
            kcat = jnp.concatenate([k0[...], k1[...], k2[...]], axis=0)
            vcat = jnp.concatenate([v0[...], v1[...], v2[...]], axis=0)
            q2 = q_ref[...]
            lo = lax.broadcasted_iota(jnp.int32, (1, 128), 1) < HD
            valid = (lax.broadcasted_iota(jnp.int32, (QB, KB), 1) + j * QB >= PADK) if masked else None
            outs = []
            for a in range(2):
                e, rl = _attn_exp(q2, kcat, bias_ref[a], lo if a == 0 else jnp.logical_not(lo), valid)
                outs.append(jnp.dot(e.astype(BF16), vcat, preferred_element_type=F32) * rl)
            o_ref[...] = jnp.where(lo, outs[0], outs[1]).astype(BF16)

        pl.when(j < npad)(functools.partial(block, True))
        pl.when(j >= npad)(functools.partial(block, False))

    qspec, kspecs, dspec = _attn_specs(nqb)
    assert len(kspecs) == 3
    return pl.pallas_call(
        body, name="attn_fwd", grid=(D // 128, nqb),
        in_specs=[qspec] + kspecs + kspecs + [dspec],
        out_specs=qspec,
        out_shape=jax.ShapeDtypeStruct((s, D), BF16),
        scratch_shapes=[pltpu.VMEM((2, QB, KB), F32)],
        compiler_params=_cp(("parallel", "arbitrary")),
    )(q, kpad, kpad, kpad, vpad, vpad, vpad, diag)


def _loss_grad(h4, tgt):
    s = h4.shape[0]

    def body(h_ref, t_ref, g_ref, l_ref):
        @pl.when(pl.program_id(0) == 0)
        def _():
            l_ref[...] = jnp.zeros_like(l_ref)

        d = h_ref[...] - t_ref[...]
        g_ref[...] = d * (1.0 / D)
        l_ref[...] += _sum8(d * d) * (0.5 / D)

    row = _rows(TM, D)
    return pl.pallas_call(
        body, name="loss_grad", grid=(s // TM,),
        in_specs=[row, row], out_specs=[row, pl.BlockSpec((8, D), lambda i: (0, 0))],
        out_shape=[jax.ShapeDtypeStruct((s, D), F32), jax.ShapeDtypeStruct((8, D), F32)],
        compiler_params=_cp(("arbitrary",)),
    )(h4, tgt)


def _rms_bwd(dn, xhat, r, g):
    dng = dn * g
    return r * (dng - xhat * jnp.mean(dng * xhat, axis=-1, keepdims=True))


def _acc_spec():
    return pl.BlockSpec((8, D), lambda i: (0, 0))


def _mlp_bwd(gout, hin, up, g, wu, wd, name):
    s = gout.shape[0]
    fj = 512

    def body(go_ref, hin_ref, up_ref, g_ref, wu_ref, wd_ref, gin_ref, dup_ref, act_ref, n_ref, dg_ref,
             gob_ref, dn_ref):
        @pl.when(pl.program_id(0) == 0)
        def _():
            dg_ref[...] = jnp.zeros_like(dg_ref)

        gob_ref[...] = go_ref[...].astype(BF16)
        for j in range(FF // fj):
            sl = slice(j * fj, (j + 1) * fj)
            rl = jnp.maximum(up_ref[:, sl], 0.0)
            act_ref[:, sl] = (rl * rl).astype(BF16)
            dact = lax.dot_general(gob_ref[...], wd_ref[sl, :], _NT, preferred_element_type=F32)
            dupj = (dact * (2.0 * rl)).astype(BF16)
            dup_ref[:, sl] = dupj
            part = lax.dot_general(dupj, wu_ref[j], _NT, preferred_element_type=F32)
            if j == 0:
                dn_ref[...] = part
            else:
                dn_ref[...] += part
        hin = hin_ref[...]
        r = _rstd(hin)
        xhat = hin * r
        n_ref[...] = (xhat * g_ref[...]).astype(BF16)
        dn = dn_ref[...]
        gin_ref[...] = go_ref[...] + _rms_bwd(dn, xhat, r, g_ref[...])
        dg_ref[...] += _sum8(dn * xhat)

    row = _rows(TMM, D)
    wide = _rows(TMM, FF)
    return pl.pallas_call(
        body, name=name, grid=(s // TMM,),
        in_specs=[row, row, wide, _full((1, D)), _full((NDEV, D, fj)), _full((FF, D))],
        out_specs=[row, wide, wide, row, _acc_spec()],
        out_shape=[jax.ShapeDtypeStruct((s, D), F32), jax.ShapeDtypeStruct((s, FF), BF16),
                   jax.ShapeDtypeStruct((s, FF), BF16), jax.ShapeDtypeStruct((s, D), BF16),
                   jax.ShapeDtypeStruct((8, D), F32)],
        scratch_shapes=[pltpu.VMEM((TMM, D), BF16), pltpu.VMEM((TMM, D), F32)],
        compiler_params=_cp(("arbitrary",)),
    )(gout, hin, up, g, wu, wd)


def _matmul_tn(a, b, slab, out_dtype, name):
    s, m = a.shape
    n = b.shape[1]
    ts = min(s, 512 if n > 2048 else 1024)
    nk = s // ts
    nc = 512
    w = n // NDEV

    def body(a_ref, b_ref, o_ref, at_ref, acc_ref):
        k = pl.program_id(0)
        at_ref[...] = a_ref[...].astype(BF16).T

        @pl.when(k == 0)
        def _():
            acc_ref[...] = jnp.zeros_like(acc_ref)

        for c in range(n // nc):
            sl = slice(c * nc, (c + 1) * nc)
            acc_ref[:, sl] += jnp.dot(at_ref[...], b_ref[:, sl].astype(BF16), preferred_element_type=F32)

        @pl.when(k == nk - 1)
        def _():
            if slab:
                for d in range(NDEV):
                    o_ref[d] = acc_ref[:, d * w:(d + 1) * w].astype(out_dtype)
            else:
                o_ref[...] = acc_ref[...].astype(out_dtype)

    if slab:
        out_shape = jax.ShapeDtypeStruct((NDEV, m, w), out_dtype)
        out_spec = pl.BlockSpec((NDEV, m, w), lambda k: (0, 0, 0), pipeline_mode=pl.Buffered(1))
    else:
        out_shape = jax.ShapeDtypeStruct((m, n), out_dtype)
        out_spec = pl.BlockSpec((m, n), lambda k: (0, 0), pipeline_mode=pl.Buffered(1))
    return pl.pallas_call(
        body, name=name, grid=(nk,),
        in_specs=[pl.BlockSpec((ts, m), lambda k: (k, 0)), pl.BlockSpec((ts, n), lambda k: (k, 0))],
        out_specs=out_spec, out_shape=out_shape,
        scratch_shapes=[pltpu.VMEM((m, ts), BF16), pltpu.VMEM((m, n), F32)],
        compiler_params=_cp(("arbitrary",)),
    )(a, b)


def _matmul_nt(x, w, name):
    s, n = x.shape
    k = w.shape[0]

    def body(x_ref, w_ref, o_ref):
        o_ref[...] = lax.dot_general(x_ref[...].astype(BF16), w_ref[...], _NT,
                                     preferred_element_type=F32).astype(BF16)

    return pl.pallas_call(
        body, name=name, grid=(s // TM,),
        in_specs=[_rows(TM, n), _full((k, n))], out_specs=_rows(TM, k),
        out_shape=jax.ShapeDtypeStruct((s, k), BF16),
        compiler_params=_cp(("parallel",)),
    )(x, w)


def _attn_bwd(q, kpad, vpad, do, diag):
    s = q.shape[0]
    nqb = s // QB
    npad = PADK // QB

    def body(q_ref, k0, k1, k2, v0, v1, v2, do_ref, diag_ref, dq_ref, dk_ref, dv_ref, dd_ref,
             bias_ref, db_ref, dka_ref, dva_ref):
        j = pl.program_id(1)

        @pl.when(j == 0)
        def _():
            _bias_from_diag(diag_ref, bias_ref)
            dka_ref[...] = jnp.zeros_like(dka_ref)
            dva_ref[...] = jnp.zeros_like(dva_ref)
            db_ref[...] = jnp.zeros_like(db_ref)

        def block(masked):
            kcat = jnp.concatenate([k0[...], k1[...], k2[...]], axis=0)
            vcat = jnp.concatenate([v0[...], v1[...], v2[...]], axis=0)
            q2 = q_ref[...]
            do2 = do_ref[...]
            lo = lax.broadcasted_iota(jnp.int32, (1, 128), 1) < HD
            valid = (lax.broadcasted_iota(jnp.int32, (QB, KB), 1) + j * QB >= PADK) if masked else None
            dq, dk, dv = [], [], []
            for a in range(2):
                lo_a = lo if a == 0 else jnp.logical_not(lo)
                e, rl = _attn_exp(q2, kcat, bias_ref[a], lo_a, valid)
                p = e * rl
                doa = jnp.where(lo_a, do2, jnp.zeros_like(do2))
                dp = lax.dot_general(doa, vcat, _NT, preferred_element_type=F32)
                ds = p * (dp - jnp.sum(p * dp, axis=-1, keepdims=True))
                db_ref[a] += ds
                dsb = ds.astype(BF16)
                dq.append(jnp.dot(dsb, kcat, preferred_element_type=F32))
                dk.append(lax.dot_general(dsb, q2, _TN, preferred_element_type=F32))
                dv.append(lax.dot_general(p.astype(BF16), do2, _TN, preferred_element_type=F32))
            dq_ref[...] = jnp.where(lo, dq[0], dq[1]) * (HD ** -0.5)
            dka_ref[...] += jnp.where(lo, dk[0], dk[1])
            dva_ref[...] += jnp.where(lo, dv[0], dv[1])

        pl.when(j < npad)(functools.partial(block, True))
        pl.when(jnp.logical_and(j >= npad, j < nqb))(functools.partial(block, False))

        @pl.when(j == nqb - 1)
        def _():
            for a in range(2):
                dd_ref[a] = _diag_sums(db_ref, a)

        dk_ref[...] = dka_ref[0:QB, :]
        dv_ref[...] = dva_ref[0:QB, :]
        dka_ref[0:KB - QB, :] = dka_ref[QB:KB, :]
        dva_ref[0:KB - QB, :] = dva_ref[QB:KB, :]
        dka_ref[KB - QB:KB, :] = jnp.zeros((QB, 128), F32)
        dva_ref[KB - QB:KB, :] = jnp.zeros((QB, 128), F32)

    qspec, kspecs, dspec = _attn_specs(nqb)
    kout = pl.BlockSpec((QB, 128), lambda p, j: (jnp.maximum(j - npad, 0), p))
    sd = jax.ShapeDtypeStruct((s, D), F32)
    return pl.pallas_call(
        body, name="attn_bwd", grid=(D // 128, nqb + npad),
        in_specs=[qspec] + kspecs + kspecs + [qspec, dspec],
        out_specs=[qspec, kout, kout, pl.BlockSpec((None, 2, 8, _TOEP), lambda p, j: (p, 0, 0, 0))],
        out_shape=[sd, sd, sd, jax.ShapeDtypeStruct((NH // 2, 2, 8, _TOEP), F32)],
        scratch_shapes=[pltpu.VMEM((2, QB, KB), F32), pltpu.VMEM((2, QB, KB), F32),
                        pltpu.VMEM((KB, 128), F32), pltpu.VMEM((KB, 128), F32)],
        compiler_params=_cp(("parallel", "arbitrary")),
    )(q, kpad, kpad, kpad, vpad, vpad, vpad, do, diag)


def _head_norm_bwd(dy2, x2, g2, lo):
    rr = _head_rstd(x2, lo)
    xhat = x2 * rr
    t = dy2 * g2 * xhat
    m_lo = jnp.sum(jnp.where(lo, t, 0.0), axis=-1, keepdims=True)
    m_hi = jnp.sum(jnp.where(lo, 0.0, t), axis=-1, keepdims=True)
    m = jnp.where(lo, m_lo, m_hi) * (1.0 / HD)
    return rr * (dy2 * g2 - xhat * m), dy2 * xhat


def _kvq_bwd(dq, dk, dv, qraw, kraw, h2, g3, kv_norm, b_norm, w_kv, w_q, k_norm_t, q_norm_t):
    s = h2.shape[0]

    def body(dq_ref, dk_ref, dv_ref, qraw_ref, kraw_ref, h_ref, g3_ref, gkv_ref, gb_ref, wkv_ref, wq_ref,
             kn_ref, qn_ref, g2_ref, dqr_ref, dkv_ref, nb_ref, nk_ref, dgq_ref, dgk_ref, dgb_ref, dgkv_ref):
        @pl.when(pl.program_id(0) == 0)
        def _():
            for r in (dgq_ref, dgk_ref, dgb_ref, dgkv_ref):
                r[...] = jnp.zeros_like(r)

        lo = lax.broadcasted_iota(jnp.int32, (1, 128), 1) < HD
        for p in range(D // 128):
            sl = slice(p * 128, (p + 1) * 128)
            dx, dgp = _head_norm_bwd(dq_ref[:, sl], qraw_ref[:, sl], qn_ref[:, sl], lo)
            dqr_ref[:, sl] = dx.astype(BF16)
            dgq_ref[:, sl] += _sum8(dgp)
            dx, dgp = _head_norm_bwd(dk_ref[:, sl], kraw_ref[:, sl], kn_ref[:, sl], lo)
            dkv_ref[:, sl] = dx.astype(BF16)
            dgk_ref[:, sl] += _sum8(dgp)
        dkv_ref[:, D:] = dv_ref[...].astype(BF16)
        dnb = lax.dot_general(dqr_ref[...], wq_ref[...], _NT, preferred_element_type=F32)
        cw = 2 * D // NDEV
        dnk = lax.dot_general(dkv_ref[:, 0:cw], wkv_ref[0], _NT, preferred_element_type=F32)
        for d in range(1, NDEV):
            dnk = dnk + lax.dot_general(dkv_ref[:, d * cw:(d + 1) * cw], wkv_ref[d], _NT, preferred_element_type=F32)
        h = h_ref[...]
        r = _rstd(h)
        xhat = h * r
        dxg = dnb * gb_ref[...] + dnk * gkv_ref[...]
        g2_ref[...] = g3_ref[...] + r * (dxg - xhat * jnp.mean(dxg * xhat, axis=-1, keepdims=True))
        dgb_ref[...] += _sum8(dnb * xhat)
        dgkv_ref[...] += _sum8(dnk * xhat)
        nb_ref[...] = (xhat * gb_ref[...]).astype(BF16)
        nk_ref[...] = (xhat * gkv_ref[...]).astype(BF16)

    row = _rows(TM, D)
    sd = jax.ShapeDtypeStruct((s, D), BF16)
    acc = jax.ShapeDtypeStruct((8, D), F32)
    return pl.pallas_call(
        body, name="kvq_bwd", grid=(s // TM,),
        in_specs=[row] * 7 + [_full((1, D)), _full((1, D)), _full((NDEV, D, 2 * D // NDEV)), _full((D, D)),
                              _full((1, D)), _full((1, D))],
        out_specs=[row, row, _rows(TM, 2 * D), row, row] + [_acc_spec()] * 4,
        out_shape=[jax.ShapeDtypeStruct((s, D), F32), sd, jax.ShapeDtypeStruct((s, 2 * D), BF16), sd, sd,
                   acc, acc, acc, acc],
        compiler_params=_cp(("arbitrary",)),
    )(dq, dk, dv, qraw, kraw, h2, g3, kv_norm, b_norm, w_kv, w_q, k_norm_t, q_norm_t)


def _lru_bwd(g1, gate, rec, hs, w_out, conv_w, conv_b, wg, bg, lam):
    s = g1.shape[0]
    nt = s // TL

    def body(g1_ref, gate_ref, rec_ref, recp_ref, hs_ref, hsp_ref, wo_ref, cw_ref, cb_ref, wg_ref, bg_ref, lam_ref,
             du_ref, dcw_ref, dcb_ref, dwg_ref, dbg_ref, dlam_ref, ext_ref, dext_ref, cg_ref):
        i = pl.program_id(0)
        first_tile = i == nt - 1

        @pl.when(i == 0)
        def _():
            dext_ref[TL:TL + 8, :] = jnp.zeros((8, D), F32)
            cg_ref[...] = jnp.zeros_like(cg_ref)
            for r in (dcw_ref, dcb_ref, dwg_ref, dbg_ref, dlam_ref):
                r[...] = jnp.zeros_like(r)

        keep = jnp.where(first_tile, 0.0, 1.0)
        ext_ref[0:8, :] = recp_ref[...] * keep
        ext_ref[8:8 + TL, :] = rec_ref[...]
        dy = lax.dot_general(g1_ref[...].astype(BF16), wo_ref[...], _NT, preferred_element_type=F32)
        lam_v = lam_ref[...]
        sp = _softplus_neg(lam_v)
        dsp_dlam = -_sigmoid(-lam_v)
        rows = lax.broadcasted_iota(jnp.int32, (TL, BW), 0)
        for n in range(NBLK):
            sl = slice(n * BW, (n + 1) * BW)
            rc = _conv(ext_ref, cw_ref, cb_ref, sl, TL)
            rg, ig, a, mult = _lru_gates(rc, wg_ref[n], bg_ref[n:n + 1, :], sp[:, sl])
            h = hs_ref[:, sl]
            hprev = _shift_down(h, 1, hsp_ref[7:8, sl] * keep, rows)
            gt = gate_ref[:, sl]
            dyn = dy[:, sl]
            du_ref[:, sl] = (dyn * h * _gelu_grad(gt)).astype(BF16)
            dh = dyn * _gelu(gt)
            dh = dh + jnp.where(rows == TL - 1, cg_ref[0:1, sl], 0.0)
            _, gsc = _scan_bwd(_shift_up(a, 1, 0.0, rows, TL), dh)
            cg_ref[0:1, sl] = a[0:1, :] * gsc[0:1, :]
            da = gsc * hprev
            d_mult = gsc * ig * rc
            d_ig = gsc * mult * rc
            d_rc = gsc * mult * ig
            d_la = da * a - d_mult * (a * a) / mult
            d_rg = d_la * ((-LRU_C) * sp[:, sl])
            dlam_ref[:, sl] += _sum8(d_la * ((-LRU_C) * rg)) * dsp_dlam[:, sl]
            dg = jnp.concatenate([d_rg * rg * (1.0 - rg), d_ig * ig * (1.0 - ig)], axis=1)
            dgb = dg.astype(BF16)
            d_rc = d_rc + lax.dot_general(dgb, wg_ref[n], _NT, preferred_element_type=F32)
            dwg_ref[n] += lax.dot_general(rc.astype(BF16), dgb, _TN, preferred_element_type=F32)
            dbg_ref[n] += _sum8(dg)
            dext_ref[0:TL, sl] = d_rc
            dcb_ref[:, sl] += _sum8(d_rc)
            for k in range(4):
                dcw_ref[k, :, sl] += _sum8(d_rc * ext_ref[5 + k:5 + k + TL, sl])
        for k in range(4):
            part = cw_ref[3 - k:4 - k, :] * dext_ref[k:k + TL, :]
            acc = part if k == 0 else acc + part
        du_ref[:, D:] = acc.astype(BF16)
        dext_ref[TL:TL + 8, :] = dext_ref[0:8, :]

    rev = pl.BlockSpec((TL, D), lambda i: (nt - 1 - i, 0))
    rev8 = pl.BlockSpec((8, D), lambda i: (jnp.maximum((nt - 1 - i) * (TL // 8) - 1, 0), 0))
    acc = jax.ShapeDtypeStruct((8, D), F32)
    return pl.pallas_call(
        body, name="lru_bwd", grid=(nt,),
        in_specs=[rev, rev, rev, rev8, rev, rev8, _full((D, D)), _full((4, D)), _full((1, D)),
                  _full((NBLK, BW, 2 * BW)), _full((NBLK, 2 * BW)), _full((1, D))],
        out_specs=[pl.BlockSpec((TL, 2 * D), lambda i: (nt - 1 - i, 0)),
                   pl.BlockSpec((4, 8, D), lambda i: (0, 0, 0)), _acc_spec(),
                   pl.BlockSpec((NBLK, BW, 2 * BW), lambda i: (0, 0, 0)),
                   pl.BlockSpec((NBLK, 8, 2 * BW), lambda i: (0, 0, 0)), _acc_spec()],
        out_shape=[jax.ShapeDtypeStruct((s, 2 * D), BF16), jax.ShapeDtypeStruct((4, 8, D), F32), acc,
                   jax.ShapeDtypeStruct((NBLK, BW, 2 * BW), F32), jax.ShapeDtypeStruct((NBLK, 8, 2 * BW), F32), acc],
        scratch_shapes=[pltpu.VMEM((TL + 8, D), F32), pltpu.VMEM((TL + 8, D), F32), pltpu.VMEM((8, D), F32)],
        compiler_params=_cp(("arbitrary",)),
    )(g1, gate, rec, rec, hs, hs, w_out, conv_w, conv_b, wg, bg, lam)


def _a_in_bwd(du, h0, g1, a_norm, w_in):
    s = h0.shape[0]

    def body(du_ref, h_ref, g1_ref, an_ref, win_ref, gx_ref, n1_ref, dg_ref):
        @pl.when(pl.program_id(0) == 0)
        def _():
            dg_ref[...] = jnp.zeros_like(dg_ref)

        cw = 2 * D // NDEV
        dn = lax.dot_general(du_ref[:, 0:cw], win_ref[0], _NT, preferred_element_type=F32)
        for d in range(1, NDEV):
            dn = dn + lax.dot_general(du_ref[:, d * cw:(d + 1) * cw], win_ref[d], _NT, preferred_element_type=F32)
        h = h_ref[...]
        r = _rstd(h)
        xhat = h * r
        gx_ref[...] = g1_ref[...] + _rms_bwd(dn, xhat, r, an_ref[...])
        n1_ref[...] = (xhat * an_ref[...]).astype(BF16)
        dg_ref[...] += _sum8(dn * xhat)

    row = _rows(TM, D)
    return pl.pallas_call(
        body, name="a_in_bwd", grid=(s // TM,),
        in_specs=[_rows(TM, 2 * D), row, row, _full((1, D)), _full((NDEV, D, 2 * D // NDEV))],
        out_specs=[row, row, _acc_spec()],
        out_shape=[jax.ShapeDtypeStruct((s, D), F32), jax.ShapeDtypeStruct((s, D), BF16),
                   jax.ShapeDtypeStruct((8, D), F32)],
        compiler_params=_cp(("arbitrary",)),
    )(du, h0, g1, a_norm, w_in)


def _rel_onehot():
    m = np.arange(_TOEP)
    signed = np.where(m < KB, m, m - _TOEP)
    idx = np.clip(PADK - signed, -(CHUNK - 1), 2 * CHUNK) + (CHUNK - 1)
    return (idx[None, :] == np.arange(NREL)[:, None]).astype(np.float32)


def _bias_diagonals(rel_bias):
    diag = jnp.dot(rel_bias, jnp.asarray(_rel_onehot()), precision=lax.Precision.HIGHEST)
    return diag.reshape(NH // 2, 2, _TOEP)


def _rel_bias_grad(dd):
    rows = 8
    z = dd
    oh = np.zeros((_TOEP, 256), np.float32)
    oh[:, :NREL] = _rel_onehot().T

    def body(z_ref, oh_ref, o_ref):
        d = jnp.sum(z_ref[...], axis=0, keepdims=True)
        hi = d.astype(BF16)
        mid = (d - hi.astype(F32)).astype(BF16)
        lo = (d - hi.astype(F32) - mid.astype(F32)).astype(BF16)
        ohb = oh_ref[...].astype(BF16)
        acc = jnp.zeros((8, 256), F32)
        for piece in (lo, mid, hi):
            acc = acc + jnp.dot(jnp.broadcast_to(piece, (8, _TOEP)), ohb, preferred_element_type=F32)
        o_ref[...] = acc

    out = pl.pallas_call(
        body, name="rel_bias_grad", grid=(NH,),
        in_specs=[pl.BlockSpec((None, rows, _TOEP), lambda h: (h, 0, 0)), pl.BlockSpec((_TOEP, 256), lambda h: (0, 0))],
        out_specs=pl.BlockSpec((None, 8, 256), lambda h: (h, 0, 0)),
        out_shape=jax.ShapeDtypeStruct((NH, 8, 256), F32),
        compiler_params=_cp(("parallel",)),
    )(z, jnp.asarray(oh))
    return out[:, 0, :NREL]


def _exchange(arrays, scatter, name):
    n = len(arrays)

    def body(*refs):
        ins, outs = refs[:n], refs[n:2 * n]
        token, (send_sems, recv_sems, local_sems) = refs[2 * n], refs[2 * n + 1:]
        token[...] = jnp.zeros_like(token)
        x, y, c = lax.axis_index("x"), lax.axis_index("y"), lax.axis_index("c")
        me = 4 * x + 2 * y + c

        def peer_of(r):
            rx, ry, rc = (r >> 2) & 1, (r >> 1) & 1, r & 1
            px = 1 - x if rx else x
            py = 1 - y if ry else y
            pc = 1 - c if rc else c
            return (px, py, pc), 4 * px + 2 * py + pc

        local, sent = [], []
        for k in range(n):
            cp = pltpu.make_async_copy(ins[k].at[me] if scatter else ins[k], outs[k].at[me], local_sems.at[k])
            cp.start()
            local.append(cp)
            for r in range(1, NDEV):
                peer, peer_lin = peer_of(r)
                cp = pltpu.make_async_remote_copy(
                    src_ref=ins[k].at[peer_lin] if scatter else ins[k], dst_ref=outs[k].at[me],
                    send_sem=send_sems.at[k, r - 1], recv_sem=recv_sems.at[k, r - 1],
                    device_id=peer, device_id_type=pl.DeviceIdType.MESH)
                cp.start()
                sent.append(cp)
        for k in range(n):
            for r in range(1, NDEV):
                peer, peer_lin = peer_of(r)
                pltpu.make_async_remote_copy(
                    src_ref=ins[k].at[peer_lin] if scatter else ins[k], dst_ref=outs[k].at[peer_lin],
                    send_sem=send_sems.at[k, r - 1], recv_sem=recv_sems.at[k, r - 1],
                    device_id=peer, device_id_type=pl.DeviceIdType.MESH).wait_recv()
        for cp in sent:
            cp.wait_send()
        for cp in local:
            cp.wait()

    def slot_shape(a):
        return (NDEV,) + (a.shape[1:] if scatter else a.shape)

    anyspec = pl.BlockSpec(memory_space=pl.ANY)
    outs = pl.pallas_call(
        body, name=name,
        in_specs=[anyspec] * n, out_specs=[anyspec] * n + [pl.BlockSpec(memory_space=pltpu.VMEM)],
        out_shape=[jax.ShapeDtypeStruct(slot_shape(a), a.dtype) for a in arrays]
        + [jax.ShapeDtypeStruct((8, 128), F32)],
        scratch_shapes=[pltpu.SemaphoreType.DMA((n, NDEV - 1)), pltpu.SemaphoreType.DMA((n, NDEV - 1)),
                        pltpu.SemaphoreType.DMA((n,))],
        compiler_params=pltpu.CompilerParams(has_side_effects=True),
    )(*arrays)
    return outs[:n], outs[n]


def _peer(r):
    x, y, c = lax.axis_index("x"), lax.axis_index("y"), lax.axis_index("c")
    px = 1 - x if (r >> 2) & 1 else x
    py = 1 - y if (r >> 1) & 1 else y
    pc = 1 - c if r & 1 else c
    return (px, py, pc), 4 * px + 2 * py + pc


def _my_index():
    return 4 * lax.axis_index("x") + 2 * lax.axis_index("y") + lax.axis_index("c")


_HBM_SPEC = pl.BlockSpec(memory_space=pltpu.HBM)
_SEM_SPEC = pl.BlockSpec(memory_space=pltpu.SEMAPHORE)


_NPEER = NDEV - 1


def _exchange_start(arrays, scatter, name):
    n = len(arrays)
    ns = n * _NPEER
    slots = [(NDEV,) + (a.shape[1:] if scatter else a.shape) for a in arrays]

    def body(*refs):
        srcs, lands = refs[:n], refs[n:2 * n]
        send_sems, recv_sems = refs[2 * n:2 * n + ns], refs[2 * n + ns:2 * n + 2 * ns]
        token = refs[-1]
        me = _my_index()
        for k in range(n):
            for r in range(1, NDEV):
                peer, peer_lin = _peer(r)
                pltpu.make_async_remote_copy(
                    src_ref=srcs[k].at[peer_lin] if scatter else srcs[k], dst_ref=lands[k].at[me],
                    send_sem=send_sems[k * _NPEER + r - 1], recv_sem=recv_sems[k * _NPEER + r - 1],
                    device_id=peer, device_id_type=pl.DeviceIdType.MESH).start()
        token[...] = jnp.zeros_like(token)

    sem = pltpu.SemaphoreType.DMA(())
    outs = pl.pallas_call(
        body, name=name,
        out_shape=(*[sem] * (2 * ns), *[pltpu.HBM(a.shape, a.dtype) for a in arrays],
                   *[pltpu.HBM(s, a.dtype) for s, a in zip(slots, arrays)], jax.ShapeDtypeStruct((8, 128), F32)),
        in_specs=[_HBM_SPEC] * (2 * n),
        out_specs=(*[_SEM_SPEC] * (2 * ns), *[_HBM_SPEC] * (2 * n), pl.BlockSpec(memory_space=pltpu.VMEM)),
        input_output_aliases={k: 2 * ns + k for k in range(2 * n)},
        compiler_params=pltpu.CompilerParams(has_side_effects=pltpu.SideEffectType.DATAFLOW_SIDE_EFFECTING),
    )(*[pltpu.with_memory_space_constraint(a, pltpu.HBM) for a in arrays],
      *[pltpu.with_memory_space_constraint(lax.empty(s, a.dtype), pltpu.HBM) for s, a in zip(slots, arrays)])
    return outs[:ns], outs[ns:2 * ns], outs[2 * ns:2 * ns + n], outs[2 * ns + n:2 * ns + 2 * n], outs[-1]


def _exchange_wait(started, after, scatter, name):
    send_sems, recv_sems, srcs, lands, _ = started
    n = len(srcs)
    ns = n * _NPEER

    def body(*refs):
        src_refs, land_refs = refs[:n], refs[n:2 * n]
        ssem, rsem = refs[2 * n:2 * n + ns], refs[2 * n + ns:2 * n + 2 * ns]
        for k in range(n):
            for r in range(1, NDEV):
                peer, peer_lin = _peer(r)
                cp = pltpu.make_async_remote_copy(
                    src_ref=src_refs[k].at[peer_lin] if scatter else src_refs[k], dst_ref=land_refs[k].at[peer_lin],
                    send_sem=ssem[k * _NPEER + r - 1], recv_sem=rsem[k * _NPEER + r - 1],
                    device_id=peer, device_id_type=pl.DeviceIdType.MESH)
                cp.wait_send()
                cp.wait_recv()

    outs = pl.pallas_call(
        body, name=name,
        out_shape=tuple(pltpu.HBM(a.shape, a.dtype) for a in list(srcs) + list(lands)),
        in_specs=[_HBM_SPEC] * (2 * n) + [_SEM_SPEC] * (2 * ns) + [pl.BlockSpec(memory_space=pl.ANY)],
        out_specs=tuple([_HBM_SPEC] * (2 * n)),
        input_output_aliases={k: k for k in range(2 * n)},
        compiler_params=pltpu.CompilerParams(has_side_effects=pltpu.SideEffectType.DATAFLOW_SIDE_EFFECTING),
    )(*srcs, *lands, *send_sems, *recv_sems, after)
    return list(outs[:n]), list(outs[n:])


def _fill_own(lands, owns, me):
    return [lax.dynamic_update_slice(z, o, (me,) + (0,) * (z.ndim - 1)) for z, o in zip(lands, owns)]


def _sum_slots(st, name):
    _, r, c = st.shape

    def body(s_ref, o_ref):
        acc = s_ref[0]
        for d in range(1, NDEV):
            acc = acc + s_ref[d]
        o_ref[...] = acc

    return pl.pallas_call(
        body, name=name, out_shape=jax.ShapeDtypeStruct((r, c), F32),
        in_specs=[pl.BlockSpec((NDEV, r, c), lambda: (0, 0, 0))], out_specs=pl.BlockSpec((r, c), lambda: (0, 0)),
    )(st)


def _adamw(w, m, v, gst, name, transposed=False):
    r, c = w.shape
    ns = gst.shape[0]
    tr = min(r, 256)
    c1 = 1.0 - ADAM_B1 ** ADAM_STEP
    c2 = 1.0 - ADAM_B2 ** ADAM_STEP

    def body(w_ref, m_ref, v_ref, g_ref, go_ref, d_ref, mo_ref, vo_ref):
        g = g_ref[0].astype(F32)
        for d in range(1, ns):
            g = g + g_ref[d].astype(F32)
        if transposed:
            g = g.T
        m2 = ADAM_B1 * m_ref[...] + (1.0 - ADAM_B1) * g
        v2 = ADAM_B2 * v_ref[...] + (1.0 - ADAM_B2) * (g * g)
        go_ref[...] = g
        mo_ref[...] = m2
        vo_ref[...] = v2
        d_ref[...] = (-ADAM_LR) * ((m2 / c1) / (jnp.sqrt(v2 / c2) + ADAM_EPS) + ADAM_WD * w_ref[...])

    blk = pl.BlockSpec((tr, c), lambda i: (i, 0))
    sd = jax.ShapeDtypeStruct((r, c), F32)
    return pl.pallas_call(
        body, name=name, grid=(r // tr,),
        in_specs=[blk, blk, blk, pl.BlockSpec((ns, c, tr), lambda i: (0, 0, i)) if transposed
                  else pl.BlockSpec((ns, tr, c), lambda i: (0, i, 0))],
        out_specs=[blk, blk, blk, blk], out_shape=[sd, sd, sd, sd],
        compiler_params=_cp(("parallel",)),
    )(w, m, v, gst)


def _pack(pieces, rows):
    flat = jnp.concatenate([p.reshape(-1).astype(F32) for p in pieces])
    return jnp.pad(flat, (0, rows * 128 - flat.shape[0])).reshape(rows, 128)


def _unpack(flat, shapes):
    out, off = [], 0
    for shp in shapes:
        size = int(np.prod(shp))
        out.append(flat[off:off + size].reshape(shp))
        off += size
    return out


def _cols(full, me, width):
    return lax.dynamic_slice_in_dim(full, me * width, width, axis=full.ndim - 1)


def kernel(x, a_norm, a_w_in, a_conv_w, a_conv_b, a_w_gate, a_b_gate, a_lambda, a_w_out, kv_norm, w_kv, k_norm, b_norm, b_w_q, b_q_norm, b_rel_bias, b_w_o, mlp_norm, w_up, w_down, loss_target, m_a_norm, m_a_w_in, m_a_conv_w, m_a_conv_b, m_a_w_gate, m_a_b_gate, m_a_lambda, m_a_w_out, m_kv_norm, m_w_kv, m_k_norm, m_b_norm, m_b_w_q, m_b_q_norm, m_b_rel_bias, m_b_w_o, m_mlp_norm, m_w_up, m_w_down, v_a_norm, v_a_w_in, v_a_conv_w, v_a_conv_b, v_a_w_gate, v_a_b_gate, v_a_lambda, v_a_w_out, v_kv_norm, v_w_kv, v_k_norm, v_b_norm, v_b_w_q, v_b_q_norm, v_b_rel_bias, v_b_w_o, v_mlp_norm, v_w_up, v_w_down):
    me = 4 * lax.axis_index("x") + 2 * lax.axis_index("y") + lax.axis_index("c")
    sh = D // NDEV

    big_w = [a_w_in[0], a_w_out[0], w_kv, b_w_q[0], b_w_o[0], w_up[0], w_up[1], w_down[0], w_down[1]]
    small_sharded = [a_norm, a_conv_w, a_conv_b, a_b_gate, a_lambda, a_w_gate]
    small_rows = 272
    def to_bf16(w, token):
        return (w + token[0, 0]).astype(BF16)

    got, tok_a = _exchange([a_w_in[0].astype(BF16), _pack(small_sharded, small_rows)], False, "gather_a")
    own_b1 = [to_bf16(w, tok_a) for w in (a_w_out[0], w_up[0], w_down[0])]
    st_b1 = _exchange_start(own_b1, False, "gather_b1_start")
    own_b2 = [to_bf16(w, st_b1[4]) for w in (w_kv, b_w_q[0], b_w_o[0], w_up[1], w_down[1])]
    st_b2 = _exchange_start(own_b2, False, "gather_b2_start")
    w_in = got[0]
    sm = got[1].reshape(NDEV, small_rows * 128)
    an_f = sm[:, 0:128].reshape(1, D) + st_b2[4][0:1, 0:1]
    cw_f = sm[:, 128:640].reshape(NDEV, 4, sh).transpose(1, 0, 2).reshape(4, D)
    cb_f = sm[:, 640:768].reshape(1, D)
    bg_f = sm[:, 768:1024].reshape(NDEV, NBLK, 2 * BW // NDEV).transpose(1, 0, 2).reshape(NBLK, 2 * BW)
    lam_f = sm[:, 1024:1152].reshape(1, D)
    wg_f = sm[:, 1152:1152 + NBLK * BW * 32].reshape(NDEV, NBLK, BW, 32).transpose(1, 2, 0, 3)
    wg_f = wg_f.reshape(NBLK, BW, 2 * BW).astype(BF16)
    kn_t = jnp.tile(k_norm, NH).reshape(1, D)
    qn_t = jnp.tile(b_q_norm[0], NH).reshape(1, D)
    kvn = kv_norm.reshape(1, D)
    diag = _bias_diagonals(b_rel_bias[0])

    h0 = x[0]
    gate, rec, hs, y = _lru_fwd(h0, an_f, w_in, cw_f, cb_f, wg_f, bg_f, lam_f)
    own, land = _exchange_wait(st_b1, y, False, "gather_b1_wait")
    land = _fill_own(land, [o[None] for o in own], me)
    w_out = land[0].reshape(D, D)
    wu = [land[1], None]
    wd = [land[2].reshape(FF, D), None]
    h1, h2, up0 = _mlp_fwd(h0, y, w_out, mlp_norm[0:1], wu[0], wd[0], "mlp_fwd0")
    own, land = _exchange_wait(st_b2, h2, False, "gather_b2_wait")
    land = _fill_own(land, [o[None] for o in own], me)
    wkv = land[0]
    w_q = land[1].reshape(D, D)
    w_o = land[2].reshape(D, D)
    wu[1] = land[3]
    wd[1] = land[4].reshape(FF, D)
    kraw, qraw, kpad, vpad, q = _kvq_fwd(h2, kvn, b_norm, wkv, w_q, kn_t, qn_t)
    o = _attn_fwd(q, kpad, vpad, diag)
    h3, h4, up1 = _mlp_fwd(h2, o, w_o, mlp_norm[1:2], wu[1], wd[1], "mlp_fwd1")
    g4, lpart = _loss_grad(h4, loss_target[0])
    loss = lax.psum(jnp.sum(lpart), ("x", "y", "c"))

    g3, dup1, act1, n3, dgm1 = _mlp_bwd(g4, h3, up1, mlp_norm[1:2], wu[1], wd[1], "mlp_bwd1")
    d_wd1 = _matmul_tn(g4, act1, True, BF16, "dw_down1")
    d_wu1 = _matmul_tn(n3, dup1, True, BF16, "dw_up1")
    do = _matmul_nt(g3, w_o, "do_proj")
    d_wo = _matmul_tn(o, g3, False, BF16, "dw_o").reshape(NDEV, sh, D)
    dq, dk, dv, dd = _attn_bwd(q, kpad, vpad, do, diag)
    g2, dqr, dkv, nb, nk, dgq, dgk, dgb, dgkv = _kvq_bwd(dq, dk, dv, qraw, kraw, h2, g3, kvn, b_norm, wkv, w_q,
                                                       kn_t, qn_t)
    d_wq = _matmul_tn(nb, dqr, False, BF16, "dw_q").reshape(NDEV, sh, D)
    d_wkv = _matmul_tn(nk, dkv, True, BF16, "dw_kv")
    st_r1 = _exchange_start([d_wd1, d_wu1, d_wo, d_wq, d_wkv], True, "scatter_r1_start")
    g1, dup0, act0, n2, dgm0 = _mlp_bwd(g2, h1, up0, mlp_norm[0:1] + st_r1[4][0:1, 0:1], wu[0], wd[0], "mlp_bwd0")
    d_wd0 = _matmul_tn(g2, act0, True, BF16, "dw_down0")
    d_wu0 = _matmul_tn(n2, dup0, True, BF16, "dw_up0")
    st_r2 = _exchange_start([d_wu0, d_wd0], True, "scatter_r2_start")
    du, dcw, dcb, dwg, dbg, dlam = _lru_bwd(g1, gate, rec, hs, w_out, cw_f, cb_f, wg_f, bg_f,
                                            lam_f + st_r2[4][0:1, 0:1])
    d_wout = _matmul_tn(y, g1, False, BF16, "dw_out").reshape(NDEV, sh, D)
    gx, n1, dga = _a_in_bwd(du, h0, g1, an_f, w_in)
    d_win = _matmul_tn(n1, du, True, BF16, "dw_in")
    d_rel = _rel_bias_grad(dd.reshape(NH, 8, _TOEP))

    dwg_slab = dwg.reshape(NBLK, BW, NDEV, 32).transpose(2, 0, 1, 3).reshape(NDEV, 256, 128)
    recv3, _ = _exchange([d_win, d_wout, dwg_slab], True, "scatter_r3")
    small_full = [dga.sum(0), dcw.sum(1), dcb.sum(0), dbg.sum(1), dlam.sum(0), dgkv.sum(0),
                  dgk.sum(0).reshape(NH, HD).sum(0), dgb.sum(0), dgq.sum(0).reshape(NH, HD).sum(0), d_rel,
                  jnp.stack([dgm0.sum(0), dgm1.sum(0)])]
    small_g_rows = 136
    (gsm,), _ = _exchange([_pack(small_full, small_g_rows)], False, "gather_small_grads")
    src, recv1 = _exchange_wait(st_r1, gsm, True, "scatter_r1_wait")
    recv1 = _fill_own(recv1, [lax.dynamic_slice_in_dim(a, me, 1, 0) for a in src], me)
    src, recv2 = _exchange_wait(st_r2, recv1[0], True, "scatter_r2_wait")
    recv2 = _fill_own(recv2, [lax.dynamic_slice_in_dim(a, me, 1, 0) for a in src], me)
    recv = [recv3[0], recv3[1], recv1[4], recv1[3], recv1[2], recv2[0], recv1[1], recv2[1], recv1[0], recv3[2]]
    gs = _unpack(_sum_slots(gsm, "sum_small_grads").reshape(-1),
                 [(1, D), (4, D), (1, D), (NBLK, 2 * BW), (1, D), (D,), (HD,), (1, D), (1, HD), (1, NH, NREL), (2, D)])
    g_small = [_cols(gs[0], me, sh), _cols(gs[1], me, sh)[None], _cols(gs[2], me, sh),
               _cols(gs[3], me, 2 * BW // NDEV)[None], _cols(gs[4], me, sh)] + gs[5:]

    names = ["a_w_in", "a_w_out", "w_kv", "b_w_q", "b_w_o", "w_up0", "w_up1", "w_down0", "w_down1", "a_w_gate"]
    big_m = [m_a_w_in[0], m_a_w_out[0], m_w_kv, m_b_w_q[0], m_b_w_o[0], m_w_up[0], m_w_up[1], m_w_down[0],
             m_w_down[1], m_a_w_gate.reshape(256, 128)]
    big_v = [v_a_w_in[0], v_a_w_out[0], v_w_kv, v_b_w_q[0], v_b_w_o[0], v_w_up[0], v_w_up[1], v_w_down[0],
             v_w_down[1], v_a_w_gate.reshape(256, 128)]
    res = [_adamw(w, m, v, g, "adamw_" + nm, transposed=nm.startswith("w_down"))
           for w, m, v, g, nm in zip(big_w + [a_w_gate.reshape(256, 128)], big_m, big_v, recv, names)]
    small_w = [a_norm, a_conv_w, a_conv_b, a_b_gate, a_lambda, kv_norm, k_norm, b_norm, b_q_norm, b_rel_bias, mlp_norm]
    small_m = [m_a_norm, m_a_conv_w, m_a_conv_b, m_a_b_gate, m_a_lambda, m_kv_norm, m_k_norm, m_b_norm, m_b_q_norm,
               m_b_rel_bias, m_mlp_norm]
    small_v = [v_a_norm, v_a_conv_w, v_a_conv_b, v_a_b_gate, v_a_lambda, v_kv_norm, v_k_norm, v_b_norm, v_b_q_norm,
               v_b_rel_bias, v_mlp_norm]
    pr = 72
    res_small = _adamw(_pack(small_w, pr), _pack(small_m, pr), _pack(small_v, pr), _pack(g_small, pr)[None],
                       "adamw_small")
    small_shapes = [w.shape for w in small_w]
    res_small = [_unpack(r.reshape(-1), small_shapes) for r in res_small]

    def assemble(t):
        b = [r[t] for r in res]
        s_ = res_small[t]
        return [s_[0], b[0][None], s_[1], s_[2], b[9].reshape(a_w_gate.shape), s_[3], s_[4], b[1][None],
                s_[5], b[2], s_[6], s_[7], b[3][None], s_[8], s_[9], b[4][None], s_[10],
                jnp.stack([b[5], b[6]]), jnp.stack([b[7], b[8]])]

    return tuple([loss, gx[None]] + assemble(0) + assemble(1) + assemble(2) + assemble(3))
```

```python
import functools

import numpy as np
import jax
import jax.numpy as jnp
from jax import lax
from jax.experimental import pallas as pl
from jax.experimental.pallas import tpu as pltpu

F32 = jnp.float32
BF16 = jnp.bfloat16

D = 1024
NH = 16
HD = 64
FF = 4096
NBLK = 8
BW = 128
CHUNK = 64
PADK = 512
NREL = 192
EPS = 1e-6
LRU_C = 8.0
NDEV = 8

V7X_VMEM_LIMIT = 56 * 1024 * 1024
TM = 512
TMM = 256
TL = 256
QB = 256
ATT_RC = 32
KB = QB + PADK
NEG = -1e30

ADAM_LR, ADAM_B1, ADAM_B2, ADAM_EPS, ADAM_WD, ADAM_STEP = 0.001, 0.9, 0.999, 1e-08, 0.01, 10

_NT = (((1,), (1,)), ((), ()))
_TN = (((0,), (0,)), ((), ()))


def _cp(sem=None):
    return pltpu.CompilerParams(dimension_semantics=sem, vmem_limit_bytes=V7X_VMEM_LIMIT)


def _full(shape):
    n = len(shape)
    return pl.BlockSpec(shape, lambda *a: (0,) * n, pipeline_mode=pl.Buffered(1))


def _rows(tm, width):
    return pl.BlockSpec((tm, width), lambda i: (i, 0))


def _rstd(h):
    return lax.rsqrt(jnp.mean(h * h, axis=-1, keepdims=True) + EPS)


def _sigmoid(x):
    return 1.0 / (1.0 + jnp.exp(-x))


def _expm1(x):
    small = x * (1.0 + x * (0.5 + x * (1.0 / 6.0 + x * (1.0 / 24.0))))
    return jnp.where(jnp.abs(x) < 0.03, small, jnp.exp(x) - 1.0)


def _softplus_neg(lam):
    e = jnp.exp(-jnp.abs(lam))
    series = e * (1.0 - e * (0.5 - e * (1.0 / 3.0 - e * 0.25)))
    return jnp.maximum(-lam, 0.0) + jnp.where(e < 0.01, series, jnp.log(1.0 + e))


_GELU_K = 0.7978845608028654


def _gelu(x):
    return 0.5 * x * (1.0 + jnp.tanh(_GELU_K * (x + 0.044715 * x * x * x)))


def _gelu_grad(x):
    t = jnp.tanh(_GELU_K * (x + 0.044715 * x * x * x))
    return 0.5 * (1.0 + t) + 0.5 * x * (1.0 - t * t) * _GELU_K * (1.0 + 3.0 * 0.044715 * x * x)


def _sum8(x):
    r, c = x.shape
    return jnp.sum(x.reshape(r // 8, 8, c), axis=0)


def _shift_down(x, s, fill, rows):
    return jnp.where(rows >= s, pltpu.roll(x, s, axis=0), fill)


def _shift_up(x, s, fill, rows, n):
    return jnp.where(rows < n - s, pltpu.roll(x, n - s, axis=0), fill)


def _scan_fwd(a, b):
    n = a.shape[0]
    rows = lax.broadcasted_iota(jnp.int32, a.shape, 0)
    s = 1
    while s < n:
        a_sh = _shift_down(a, s, 1.0, rows)
        b_sh = _shift_down(b, s, 0.0, rows)
        b = a * b_sh + b
        a = a * a_sh
        s *= 2
    return a, b


def _scan_bwd(a, b):
    n = a.shape[0]
    rows = lax.broadcasted_iota(jnp.int32, a.shape, 0)
    s = 1
    while s < n:
        a_sh = _shift_up(a, s, 1.0, rows, n)
        b_sh = _shift_up(b, s, 0.0, rows, n)
        b = a * b_sh + b
        a = a * a_sh
        s *= 2
    return a, b


def _lru_gates(rc, wg_n, bg_n, sp_n):
    g = jnp.dot(rc.astype(BF16), wg_n, preferred_element_type=F32) + bg_n
    rg = _sigmoid(g[:, :BW])
    ig = _sigmoid(g[:, BW:])
    la = (-LRU_C) * rg * sp_n
    a = jnp.exp(la)
    mult = jnp.sqrt(-_expm1(2.0 * la))
    return rg, ig, a, mult


def _conv(ext_ref, cw_ref, cb_ref, sl, n):
    out = cb_ref[:, sl] + cw_ref[0:1, sl] * ext_ref[5:5 + n, sl]
    for k in range(1, 4):
        out = out + cw_ref[k:k + 1, sl] * ext_ref[5 + k:5 + k + n, sl]
    return out


def _lru_fwd(h0, a_norm, w_in, conv_w, conv_b, wg, bg, lam):
    s = h0.shape[0]

    def body(h0_ref, an_ref, win_ref, cw_ref, cb_ref, wg_ref, bg_ref, lam_ref,
             gate_ref, rec_ref, hs_ref, y_ref, ext_ref, hc_ref):
        i = pl.program_id(0)

        @pl.when(i == 0)
        def _():
            ext_ref[0:8, :] = jnp.zeros((8, D), F32)
            hc_ref[...] = jnp.zeros_like(hc_ref)

        h = h0_ref[...]
        n1 = (h * _rstd(h) * an_ref[...]).astype(BF16)
        cw = 2 * D // NDEV
        for d in range(NDEV):
            ud = jnp.dot(n1, win_ref[d], preferred_element_type=F32)
            if d < NDEV // 2:
                gate_ref[:, d * cw:(d + 1) * cw] = ud
            else:
                rec_ref[:, d * cw - D:(d + 1) * cw - D] = ud
                ext_ref[8:8 + TL, d * cw - D:(d + 1) * cw - D] = ud
        sp = _softplus_neg(lam_ref[...])
        for n in range(NBLK):
            sl = slice(n * BW, (n + 1) * BW)
            rc = _conv(ext_ref, cw_ref, cb_ref, sl, TL)
            rg, ig, a, mult = _lru_gates(rc, wg_ref[n], bg_ref[n:n + 1, :], sp[:, sl])
            acum, bcum = _scan_fwd(a, mult * (ig * rc))
            hh = acum * hc_ref[0:1, sl] + bcum
            hc_ref[0:1, sl] = hh[TL - 1:TL, :]
            hs_ref[:, sl] = hh
            y_ref[:, sl] = (_gelu(gate_ref[:, sl]) * hh).astype(BF16)
        ext_ref[0:8, :] = ext_ref[TL:TL + 8, :]

    row = _rows(TL, D)
    return pl.pallas_call(
        body, name="lru_fwd", grid=(s // TL,),
        in_specs=[row, _full((1, D)), _full((NDEV, D, 2 * D // NDEV)), _full((4, D)), _full((1, D)),
                  _full((NBLK, BW, 2 * BW)), _full((NBLK, 2 * BW)), _full((1, D))],
        out_specs=[row, row, row, row],
        out_shape=[jax.ShapeDtypeStruct((s, D), F32), jax.ShapeDtypeStruct((s, D), F32),
                   jax.ShapeDtypeStruct((s, D), F32), jax.ShapeDtypeStruct((s, D), BF16)],
        scratch_shapes=[pltpu.VMEM((TL + 8, D), F32), pltpu.VMEM((8, D), F32)],
        compiler_params=_cp(("arbitrary",)),
    )(h0, a_norm, w_in, conv_w, conv_b, wg, bg, lam)


def _mlp_fwd(res, px, pw, g, wu, wd, name):
    s = res.shape[0]
    fj = 512

    def body(res_ref, px_ref, pw_ref, g_ref, wu_ref, wd_ref, hin_ref, hout_ref, up_ref, n_ref):
        hin = res_ref[...] + jnp.dot(px_ref[...], pw_ref[...], preferred_element_type=F32)
        hin_ref[...] = hin
        hout_ref[...] = hin
        n_ref[...] = (hin * _rstd(hin) * g_ref[...]).astype(BF16)
        for j in range(FF // fj):
            sl = slice(j * fj, (j + 1) * fj)
            up = jnp.dot(n_ref[...], wu_ref[j], preferred_element_type=F32)
            up_ref[:, sl] = up
            rl = jnp.maximum(up, 0.0)
            hout_ref[...] += jnp.dot((rl * rl).astype(BF16), wd_ref[sl, :], preferred_element_type=F32)

    row = _rows(TMM, D)
    return pl.pallas_call(
        body, name=name, grid=(s // TMM,),
        in_specs=[row, row, _full((D, D)), _full((1, D)), _full((NDEV, D, fj)), _full((FF, D))],
        out_specs=[row, row, _rows(TMM, FF)],
        out_shape=[jax.ShapeDtypeStruct((s, D), F32), jax.ShapeDtypeStruct((s, D), F32),
                   jax.ShapeDtypeStruct((s, FF), F32)],
        scratch_shapes=[pltpu.VMEM((TMM, D), BF16)],
        compiler_params=_cp(("parallel",)),
    )(res, px, pw, g, wu, wd)


def _head_rstd(x2, lo):
    sq = x2 * x2
    s_lo = jnp.sum(jnp.where(lo, sq, 0.0), axis=-1, keepdims=True)
    s_hi = jnp.sum(jnp.where(lo, 0.0, sq), axis=-1, keepdims=True)
    return lax.rsqrt(jnp.where(lo, s_lo, s_hi) * (1.0 / HD) + EPS)


def _kvq_fwd(h2, kv_norm, b_norm, w_kv, w_q, k_norm_t, q_norm_t):
    s = h2.shape[0]
    assert PADK == TM

    def body(h_ref, gkv_ref, gb_ref, wkv_ref, wq_ref, kn_ref, qn_ref,
             kraw_ref, qraw_ref, k_ref, v_ref, q_ref):
        i = pl.program_id(0)

        @pl.when(i == 0)
        def _():
            k_ref[...] = jnp.zeros_like(k_ref)
            v_ref[...] = jnp.zeros_like(v_ref)

        @pl.when(i > 0)
        def _():
            h = h_ref[...]
            xhat = h * _rstd(h)
            nk = (xhat * gkv_ref[...]).astype(BF16)
            qr = jnp.dot((xhat * gb_ref[...]).astype(BF16), wq_ref[...], preferred_element_type=F32)
            qraw_ref[...] = qr
            lo = lax.broadcasted_iota(jnp.int32, (1, 128), 1) < HD
            cw = 2 * D // NDEV
            for d in range(NDEV):
                kvd = jnp.dot(nk, wkv_ref[d], preferred_element_type=F32)
                if d < NDEV // 2:
                    kraw_ref[:, d * cw:(d + 1) * cw] = kvd
                    for p in range(cw // 128):
                        sl = slice(d * cw + p * 128, d * cw + (p + 1) * 128)
                        k2 = kvd[:, p * 128:(p + 1) * 128]
                        k_ref[:, sl] = (k2 * _head_rstd(k2, lo) * kn_ref[:, sl]).astype(BF16)
                else:
                    v_ref[:, d * cw - D:(d + 1) * cw - D] = kvd.astype(BF16)
            for p in range(D // 128):
                sl = slice(p * 128, (p + 1) * 128)
                q2 = qr[:, sl]
                q_ref[:, sl] = (q2 * _head_rstd(q2, lo) * qn_ref[:, sl] * (HD ** -0.5)).astype(BF16)

    prev = pl.BlockSpec((TM, D), lambda i: (jnp.maximum(i - 1, 0), 0))
    cur = pl.BlockSpec((TM, D), lambda i: (i, 0))
    return pl.pallas_call(
        body, name="kvq_fwd", grid=(s // TM + 1,),
        in_specs=[prev, _full((1, D)), _full((1, D)), _full((NDEV, D, 2 * D // NDEV)), _full((D, D)), _full((1, D)),
                  _full((1, D))],
        out_specs=[prev, prev, cur, cur, prev],
        out_shape=[jax.ShapeDtypeStruct((s, D), F32), jax.ShapeDtypeStruct((s, D), F32),
                   jax.ShapeDtypeStruct((s + PADK, D), BF16), jax.ShapeDtypeStruct((s + PADK, D), BF16),
                   jax.ShapeDtypeStruct((s, D), BF16)],
        compiler_params=_cp(("arbitrary",)),
    )(h2, kv_norm, b_norm, w_kv, w_q, k_norm_t, q_norm_t)


_TOEP = QB + KB


def _bias_from_diag(diag_ref, bias_ref):
    row8 = lax.broadcasted_iota(jnp.int32, (8, _TOEP), 0)
    kchunk = lax.broadcasted_iota(jnp.int32, (8, KB), 1) // CHUNK
    for a in range(2):
        v = jnp.broadcast_to(diag_ref[a:a + 1, :], (8, _TOEP))
        z0 = v
        for b in range(1, 8):
            z0 = jnp.where(row8 == b, pltpu.roll(v, b, axis=1), z0)
        for t in range(QB // 8):
            slab = z0 if t == 0 else pltpu.roll(z0, 8 * t, axis=1)
            qchunk = (8 * t) // CHUNK
            band = jnp.logical_and(kchunk >= qchunk, kchunk <= qchunk + PADK // CHUNK)
            bias_ref[a, 8 * t:8 * t + 8, :] = jnp.where(band, slab[:, :KB], NEG)


def _diag_sums(db_ref, a):
    row8 = lax.broadcasted_iota(jnp.int32, (8, _TOEP), 0)
    z = jnp.zeros((8, _TOEP), F32)
    for t in range(QB // 8):
        slab = jnp.concatenate([db_ref[a, 8 * t:8 * t + 8, :], jnp.zeros((8, _TOEP - KB), F32)], axis=1)
        z = z + (slab if t == 0 else pltpu.roll(slab, _TOEP - 8 * t, axis=1))
    e = z
    for b in range(1, 8):
        e = jnp.where(row8 == b, pltpu.roll(z, _TOEP - b, axis=1), e)
    return e


def _attn_specs(nqb):
    qspec = pl.BlockSpec((QB, 128), lambda p, j: (jnp.minimum(j, nqb - 1), p))
    kspecs = [pl.BlockSpec((QB, 128), functools.partial(lambda p, j, t: (jnp.minimum(j, nqb - 1) + t, p), t=t))
              for t in range(KB // QB)]
    dspec = pl.BlockSpec((None, 2, _TOEP), lambda p, j: (p, 0, 0))
    return qspec, kspecs, dspec


def _attn_fwd(q, kpad, vpad, diag):
    s = q.shape[0]
    nqb = s // QB
    npad = PADK // QB

    def body(q_ref, k0, k1, k2, v0, v1, v2, diag_ref, o_ref, bias_ref, sc_ref, eb_ref, rl_ref):
        j = pl.program_id(1)

        @pl.when(j == 0)
        def _():
            _bias_from_diag(diag_ref, bias_ref)

        def block(masked):
            kcat = jnp.concatenate([k0[...], k1[...], k2[...]], axis=0)
            vcat = jnp.concatenate([v0[...], v1[...], v2[...]], axis=0)
            q2 = q_ref[...]
            lo = lax.broadcasted_iota(jnp.int32, (1, 128), 1) < HD
            valid = (lax.broadcasted_iota(jnp.int32, (1, KB), 1) + j * QB >= PADK) if masked else None
            outs = []
            for a in range(2):
                lo_a = lo if a == 0 else jnp.logical_not(lo)
                sc_ref[a] = lax.dot_general(jnp.where(lo_a, q2, jnp.zeros_like(q2)), kcat, _NT,
                                            preferred_element_type=F32)
            for a in range(2):
                for c in range(QB // ATT_RC):
                    r = slice(c * ATT_RC, (c + 1) * ATT_RC)
                    sc = sc_ref[a, r, :] + bias_ref[a, r, :]
                    if masked:
                        sc = jnp.where(valid, sc, NEG)
                    e = jnp.exp(sc - jnp.max(sc, axis=-1, keepdims=True))
                    eb_ref[a, r, :] = e.astype(BF16)
                    rl_ref[a, r, :] = jnp.broadcast_to(1.0 / jnp.sum(e, axis=-1, keepdims=True), (ATT_RC, 128))
                outs.append(jnp.dot(eb_ref[a], vcat, preferred_element_type=F32) * rl_ref[a])
            o_ref[...] = jnp.where(lo, outs[0], outs[1]).astype(BF16)

        pl.when(j < npad)(functools.partial(block, True))
        pl.when(j >= npad)(functools.partial(block, False))

    qspec, kspecs, dspec = _attn_specs(nqb)
    assert len(kspecs) == 3
    return pl.pallas_call(
        body, name="attn_fwd", grid=(D // 128, nqb),
        in_specs=[qspec] + kspecs + kspecs + [dspec],
        out_specs=qspec,
        out_shape=jax.ShapeDtypeStruct((s, D), BF16),
        scratch_shapes=[pltpu.VMEM((2, QB, KB), F32), pltpu.VMEM((2, QB, KB), F32), pltpu.VMEM((2, QB, KB), BF16),
                        pltpu.VMEM((2, QB, 128), F32)],
        compiler_params=_cp(("parallel", "arbitrary")),
    )(q, kpad, kpad, kpad, vpad, vpad, vpad, diag)


def _loss_grad(h4, tgt):
    s = h4.shape[0]

    def body(h_ref, t_ref, g_ref, l_ref):
        @pl.when(pl.program_id(0) == 0)
        def _():
            l_ref[...] = jnp.zeros_like(l_ref)

        d = h_ref[...] - t_ref[...]
        g_ref[...] = d * (1.0 / D)
        l_ref[...] += _sum8(d * d) * (0.5 / D)

    row = _rows(TM, D)
    return pl.pallas_call(
        body, name="loss_grad", grid=(s // TM,),
        in_specs=[row, row], out_specs=[row, pl.BlockSpec((8, D), lambda i: (0, 0))],
        out_shape=[jax.ShapeDtypeStruct((s, D), F32), jax.ShapeDtypeStruct((8, D), F32)],
        compiler_params=_cp(("arbitrary",)),
    )(h4, tgt)


def _rms_bwd(dn, xhat, r, g):
    dng = dn * g
    return r * (dng - xhat * jnp.mean(dng * xhat, axis=-1, keepdims=True))


def _acc_spec():
    return pl.BlockSpec((8, D), lambda i: (0, 0))


def _mlp_bwd(gout, hin, up, g, wu, wd, name):
    s = gout.shape[0]
    fj = 512

    def body(go_ref, hin_ref, up_ref, g_ref, wu_ref, wd_ref, gin_ref, dup_ref, act_ref, n_ref, dg_ref,
             gob_ref, dn_ref):
        @pl.when(pl.program_id(0) == 0)
        def _():
            dg_ref[...] = jnp.zeros_like(dg_ref)

        gob_ref[...] = go_ref[...].astype(BF16)
        for j in range(FF // fj):
            sl = slice(j * fj, (j + 1) * fj)
            rl = jnp.maximum(up_ref[:, sl], 0.0)
            act_ref[:, sl] = (rl * rl).astype(BF16)
            dact = lax.dot_general(gob_ref[...], wd_ref[sl, :], _NT, preferred_element_type=F32)
            dupj = (dact * (2.0 * rl)).astype(BF16)
            dup_ref[:, sl] = dupj
            part = lax.dot_general(dupj, wu_ref[j], _NT, preferred_element_type=F32)
            if j == 0:
                dn_ref[...] = part
            else:
                dn_ref[...] += part
        hin = hin_ref[...]
        r = _rstd(hin)
        xhat = hin * r
        n_ref[...] = (xhat * g_ref[...]).astype(BF16)
        dn = dn_ref[...]
        gin_ref[...] = go_ref[...] + _rms_bwd(dn, xhat, r, g_ref[...])
        dg_ref[...] += _sum8(dn * xhat)

    row = _rows(TMM, D)
    wide = _rows(TMM, FF)
    return pl.pallas_call(
        body, name=name, grid=(s // TMM,),
        in_specs=[row, row, wide, _full((1, D)), _full((NDEV, D, fj)), _full((FF, D))],
        out_specs=[row, wide, wide, row, _acc_spec()],
        out_shape=[jax.ShapeDtypeStruct((s, D), F32), jax.ShapeDtypeStruct((s, FF), BF16),
                   jax.ShapeDtypeStruct((s, FF), BF16), jax.ShapeDtypeStruct((s, D), BF16),
                   jax.ShapeDtypeStruct((8, D), F32)],
        scratch_shapes=[pltpu.VMEM((TMM, D), BF16), pltpu.VMEM((TMM, D), F32)],
        compiler_params=_cp(("arbitrary",)),
    )(gout, hin, up, g, wu, wd)


def _matmul_tn(a, b, slab, out_dtype, name):
    s, m = a.shape
    n = b.shape[1]
    ts = min(s, 512 if n > 2048 else 1024)
    nk = s // ts
    nc = 512
    w = n // NDEV

    def body(a_ref, b_ref, o_ref, at_ref, acc_ref):
        k = pl.program_id(0)
        at_ref[...] = a_ref[...].astype(BF16).T

        @pl.when(k == 0)
        def _():
            acc_ref[...] = jnp.zeros_like(acc_ref)

        for c in range(n // nc):
            sl = slice(c * nc, (c + 1) * nc)
            acc_ref[:, sl] += jnp.dot(at_ref[...], b_ref[:, sl].astype(BF16), preferred_element_type=F32)

        @pl.when(k == nk - 1)
        def _():
            if slab:
                for d in range(NDEV):
                    o_ref[d] = acc_ref[:, d * w:(d + 1) * w].astype(out_dtype)
            else:
                o_ref[...] = acc_ref[...].astype(out_dtype)

    if slab:
        out_shape = jax.ShapeDtypeStruct((NDEV, m, w), out_dtype)
        out_spec = pl.BlockSpec((NDEV, m, w), lambda k: (0, 0, 0), pipeline_mode=pl.Buffered(1))
    else:
        out_shape = jax.ShapeDtypeStruct((m, n), out_dtype)
        out_spec = pl.BlockSpec((m, n), lambda k: (0, 0), pipeline_mode=pl.Buffered(1))
    return pl.pallas_call(
        body, name=name, grid=(nk,),
        in_specs=[pl.BlockSpec((ts, m), lambda k: (k, 0)), pl.BlockSpec((ts, n), lambda k: (k, 0))],
        out_specs=out_spec, out_shape=out_shape,
        scratch_shapes=[pltpu.VMEM((m, ts), BF16), pltpu.VMEM((m, n), F32)],
        compiler_params=_cp(("arbitrary",)),
    )(a, b)


def _matmul_nt(x, w, name):
    s, n = x.shape
    k = w.shape[0]

    def body(x_ref, w_ref, o_ref):
        o_ref[...] = lax.dot_general(x_ref[...].astype(BF16), w_ref[...], _NT,
                                     preferred_element_type=F32).astype(BF16)

    return pl.pallas_call(
        body, name=name, grid=(s // TM,),
        in_specs=[_rows(TM, n), _full((k, n))], out_specs=_rows(TM, k),
        out_shape=jax.ShapeDtypeStruct((s, k), BF16),
        compiler_params=_cp(("parallel",)),
    )(x, w)


def _attn_bwd(q, kpad, vpad, do, diag):
    s = q.shape[0]
    nqb = s // QB
    npad = PADK // QB

    def body(q_ref, k0, k1, k2, v0, v1, v2, do_ref, diag_ref, dq_ref, dk_ref, dv_ref, dd_ref,
             bias_ref, db_ref, dka_ref, dva_ref, sc_ref, dp_ref, dsb_ref, pb_ref):
        j = pl.program_id(1)

        @pl.when(j == 0)
        def _():
            _bias_from_diag(diag_ref, bias_ref)
            dka_ref[...] = jnp.zeros_like(dka_ref)
            dva_ref[...] = jnp.zeros_like(dva_ref)
            db_ref[...] = jnp.zeros_like(db_ref)

        def block(masked):
            kcat = jnp.concatenate([k0[...], k1[...], k2[...]], axis=0)
            vcat = jnp.concatenate([v0[...], v1[...], v2[...]], axis=0)
            q2 = q_ref[...]
            do2 = do_ref[...]
            lo = lax.broadcasted_iota(jnp.int32, (1, 128), 1) < HD
            valid = (lax.broadcasted_iota(jnp.int32, (1, KB), 1) + j * QB >= PADK) if masked else None
            qt = q2.T
            dot_ = do2.T
            dq = []
            for a in range(2):
                lo_a = lo if a == 0 else jnp.logical_not(lo)
                sc_ref[a] = lax.dot_general(jnp.where(lo_a, q2, jnp.zeros_like(q2)), kcat, _NT,
                                            preferred_element_type=F32)
                dp_ref[a] = lax.dot_general(jnp.where(lo_a, do2, jnp.zeros_like(do2)), vcat, _NT,
                                            preferred_element_type=F32)
            for a in range(2):
                for c in range(QB // ATT_RC):
                    r = slice(c * ATT_RC, (c + 1) * ATT_RC)
                    sc = sc_ref[a, r, :] + bias_ref[a, r, :]
                    if masked:
                        sc = jnp.where(valid, sc, NEG)
                    e = jnp.exp(sc - jnp.max(sc, axis=-1, keepdims=True))
                    p = e * (1.0 / jnp.sum(e, axis=-1, keepdims=True))
                    dp = dp_ref[a, r, :]
                    ds = p * (dp - jnp.sum(p * dp, axis=-1, keepdims=True))
                    db_ref[a, r, :] += ds
                    dsb_ref[a, r, :] = ds.astype(BF16)
                    pb_ref[a, r, :] = p.astype(BF16)
                hd = slice(a * HD, (a + 1) * HD)
                dq.append(jnp.dot(dsb_ref[a], kcat, preferred_element_type=F32))
                dka_ref[hd, :] += jnp.dot(qt[hd, :], dsb_ref[a], preferred_element_type=F32)
                dva_ref[hd, :] += jnp.dot(dot_[hd, :], pb_ref[a], preferred_element_type=F32)
            dq_ref[...] = jnp.where(lo, dq[0], dq[1]) * (HD ** -0.5)

        pl.when(j < npad)(functools.partial(block, True))
        pl.when(jnp.logical_and(j >= npad, j < nqb))(functools.partial(block, False))

        @pl.when(j == nqb - 1)
        def _():
            for a in range(2):
                dd_ref[a] = _diag_sums(db_ref, a)

        dk_ref[...] = dka_ref[:, 0:QB].T
        dv_ref[...] = dva_ref[:, 0:QB].T
        dka_ref[:, 0:KB - QB] = dka_ref[:, QB:KB]
        dva_ref[:, 0:KB - QB] = dva_ref[:, QB:KB]
        dka_ref[:, KB - QB:KB] = jnp.zeros((128, QB), F32)
        dva_ref[:, KB - QB:KB] = jnp.zeros((128, QB), F32)

    qspec, kspecs, dspec = _attn_specs(nqb)
    kout = pl.BlockSpec((QB, 128), lambda p, j: (jnp.maximum(j - npad, 0), p))
    sd = jax.ShapeDtypeStruct((s, D), F32)
    return pl.pallas_call(
        body, name="attn_bwd", grid=(D // 128, nqb + npad),
        in_specs=[qspec] + kspecs + kspecs + [qspec, dspec],
        out_specs=[qspec, kout, kout, pl.BlockSpec((None, 2, 8, _TOEP), lambda p, j: (p, 0, 0, 0))],
        out_shape=[sd, sd, sd, jax.ShapeDtypeStruct((NH // 2, 2, 8, _TOEP), F32)],
        scratch_shapes=[pltpu.VMEM((2, QB, KB), F32), pltpu.VMEM((2, QB, KB), F32),
                        pltpu.VMEM((128, KB), F32), pltpu.VMEM((128, KB), F32),
                        pltpu.VMEM((2, QB, KB), F32), pltpu.VMEM((2, QB, KB), F32),
                        pltpu.VMEM((2, QB, KB), BF16), pltpu.VMEM((2, QB, KB), BF16)],
        compiler_params=_cp(("parallel", "arbitrary")),
    )(q, kpad, kpad, kpad, vpad, vpad, vpad, do, diag)


def _head_norm_bwd(dy2, x2, g2, lo):
    rr = _head_rstd(x2, lo)
    xhat = x2 * rr
    t = dy2 * g2 * xhat
    m_lo = jnp.sum(jnp.where(lo, t, 0.0), axis=-1, keepdims=True)
    m_hi = jnp.sum(jnp.where(lo, 0.0, t), axis=-1, keepdims=True)
    m = jnp.where(lo, m_lo, m_hi) * (1.0 / HD)
    return rr * (dy2 * g2 - xhat * m), dy2 * xhat


def _kvq_bwd(dq, dk, dv, qraw, kraw, h2, g3, kv_norm, b_norm, w_kv, w_q, k_norm_t, q_norm_t):
    s = h2.shape[0]

    def body(dq_ref, dk_ref, dv_ref, qraw_ref, kraw_ref, h_ref, g3_ref, gkv_ref, gb_ref, wkv_ref, wq_ref,
             kn_ref, qn_ref, g2_ref, dqr_ref, dkv_ref, nb_ref, nk_ref, dgq_ref, dgk_ref, dgb_ref, dgkv_ref):
        @pl.when(pl.program_id(0) == 0)
        def _():
            for r in (dgq_ref, dgk_ref, dgb_ref, dgkv_ref):
                r[...] = jnp.zeros_like(r)

        lo = lax.broadcasted_iota(jnp.int32, (1, 128), 1) < HD
        for p in range(D // 128):
            sl = slice(p * 128, (p + 1) * 128)
            dx, dgp = _head_norm_bwd(dq_ref[:, sl], qraw_ref[:, sl], qn_ref[:, sl], lo)
            dqr_ref[:, sl] = dx.astype(BF16)
            dgq_ref[:, sl] += _sum8(dgp)
            dx, dgp = _head_norm_bwd(dk_ref[:, sl], kraw_ref[:, sl], kn_ref[:, sl], lo)
            dkv_ref[:, sl] = dx.astype(BF16)
            dgk_ref[:, sl] += _sum8(dgp)
        dkv_ref[:, D:] = dv_ref[...].astype(BF16)
        dnb = lax.dot_general(dqr_ref[...], wq_ref[...], _NT, preferred_element_type=F32)
        cw = 2 * D // NDEV
        dnk = lax.dot_general(dkv_ref[:, 0:cw], wkv_ref[0], _NT, preferred_element_type=F32)
        for d in range(1, NDEV):
            dnk = dnk + lax.dot_general(dkv_ref[:, d * cw:(d + 1) * cw], wkv_ref[d], _NT, preferred_element_type=F32)
        h = h_ref[...]
        r = _rstd(h)
        xhat = h * r
        dxg = dnb * gb_ref[...] + dnk * gkv_ref[...]
        g2_ref[...] = g3_ref[...] + r * (dxg - xhat * jnp.mean(dxg * xhat, axis=-1, keepdims=True))
        dgb_ref[...] += _sum8(dnb * xhat)
        dgkv_ref[...] += _sum8(dnk * xhat)
        nb_ref[...] = (xhat * gb_ref[...]).astype(BF16)
        nk_ref[...] = (xhat * gkv_ref[...]).astype(BF16)

    row = _rows(TM, D)
    sd = jax.ShapeDtypeStruct((s, D), BF16)
    acc = jax.ShapeDtypeStruct((8, D), F32)
    return pl.pallas_call(
        body, name="kvq_bwd", grid=(s // TM,),
        in_specs=[row] * 7 + [_full((1, D)), _full((1, D)), _full((NDEV, D, 2 * D // NDEV)), _full((D, D)),
                              _full((1, D)), _full((1, D))],
        out_specs=[row, row, _rows(TM, 2 * D), row, row] + [_acc_spec()] * 4,
        out_shape=[jax.ShapeDtypeStruct((s, D), F32), sd, jax.ShapeDtypeStruct((s, 2 * D), BF16), sd, sd,
                   acc, acc, acc, acc],
        compiler_params=_cp(("arbitrary",)),
    )(dq, dk, dv, qraw, kraw, h2, g3, kv_norm, b_norm, w_kv, w_q, k_norm_t, q_norm_t)


def _lru_bwd(g1, gate, rec, hs, w_out, conv_w, conv_b, wg, bg, lam):
    s = g1.shape[0]
    nt = s // TL

    def body(g1_ref, gate_ref, rec_ref, recp_ref, hs_ref, hsp_ref, wo_ref, cw_ref, cb_ref, wg_ref, bg_ref, lam_ref,
             du_ref, dcw_ref, dcb_ref, dwg_ref, dbg_ref, dlam_ref, ext_ref, dext_ref, cg_ref):
        i = pl.program_id(0)
        first_tile = i == nt - 1

        @pl.when(i == 0)
        def _():
            dext_ref[TL:TL + 8, :] = jnp.zeros((8, D), F32)
            cg_ref[...] = jnp.zeros_like(cg_ref)
            for r in (dcw_ref, dcb_ref, dwg_ref, dbg_ref, dlam_ref):
                r[...] = jnp.zeros_like(r)

        keep = jnp.where(first_tile, 0.0, 1.0)
        ext_ref[0:8, :] = recp_ref[...] * keep
        ext_ref[8:8 + TL, :] = rec_ref[...]
        dy = lax.dot_general(g1_ref[...].astype(BF16), wo_ref[...], _NT, preferred_element_type=F32)
        lam_v = lam_ref[...]
        sp = _softplus_neg(lam_v)
        dsp_dlam = -_sigmoid(-lam_v)
        rows = lax.broadcasted_iota(jnp.int32, (TL, BW), 0)
        for n in range(NBLK):
            sl = slice(n * BW, (n + 1) * BW)
            rc = _conv(ext_ref, cw_ref, cb_ref, sl, TL)
            rg, ig, a, mult = _lru_gates(rc, wg_ref[n], bg_ref[n:n + 1, :], sp[:, sl])
            h = hs_ref[:, sl]
            hprev = _shift_down(h, 1, hsp_ref[7:8, sl] * keep, rows)
            gt = gate_ref[:, sl]
            dyn = dy[:, sl]
            du_ref[:, sl] = (dyn * h * _gelu_grad(gt)).astype(BF16)
            dh = dyn * _gelu(gt)
            dh = dh + jnp.where(rows == TL - 1, cg_ref[0:1, sl], 0.0)
            _, gsc = _scan_bwd(_shift_up(a, 1, 0.0, rows, TL), dh)
            cg_ref[0:1, sl] = a[0:1, :] * gsc[0:1, :]
            da = gsc * hprev
            d_mult = gsc * ig * rc
            d_ig = gsc * mult * rc
            d_rc = gsc * mult * ig
            d_la = da * a - d_mult * (a * a) / mult
            d_rg = d_la * ((-LRU_C) * sp[:, sl])
            dlam_ref[:, sl] += _sum8(d_la * ((-LRU_C) * rg)) * dsp_dlam[:, sl]
            dg = jnp.concatenate([d_rg * rg * (1.0 - rg), d_ig * ig * (1.0 - ig)], axis=1)
            dgb = dg.astype(BF16)
            d_rc = d_rc + lax.dot_general(dgb, wg_ref[n], _NT, preferred_element_type=F32)
            dwg_ref[n] += lax.dot_general(rc.astype(BF16), dgb, _TN, preferred_element_type=F32)
            dbg_ref[n] += _sum8(dg)
            dext_ref[0:TL, sl] = d_rc
            dcb_ref[:, sl] += _sum8(d_rc)
            for k in range(4):
                dcw_ref[k, :, sl] += _sum8(d_rc * ext_ref[5 + k:5 + k + TL, sl])
        for k in range(4):
            part = cw_ref[3 - k:4 - k, :] * dext_ref[k:k + TL, :]
            acc = part if k == 0 else acc + part
        du_ref[:, D:] = acc.astype(BF16)
        dext_ref[TL:TL + 8, :] = dext_ref[0:8, :]

    rev = pl.BlockSpec((TL, D), lambda i: (nt - 1 - i, 0))
    rev8 = pl.BlockSpec((8, D), lambda i: (jnp.maximum((nt - 1 - i) * (TL // 8) - 1, 0), 0))
    acc = jax.ShapeDtypeStruct((8, D), F32)
    return pl.pallas_call(
        body, name="lru_bwd", grid=(nt,),
        in_specs=[rev, rev, rev, rev8, rev, rev8, _full((D, D)), _full((4, D)), _full((1, D)),
                  _full((NBLK, BW, 2 * BW)), _full((NBLK, 2 * BW)), _full((1, D))],
        out_specs=[pl.BlockSpec((TL, 2 * D), lambda i: (nt - 1 - i, 0)),
                   pl.BlockSpec((4, 8, D), lambda i: (0, 0, 0)), _acc_spec(),
                   pl.BlockSpec((NBLK, BW, 2 * BW), lambda i: (0, 0, 0)),
                   pl.BlockSpec((NBLK, 8, 2 * BW), lambda i: (0, 0, 0)), _acc_spec()],
        out_shape=[jax.ShapeDtypeStruct((s, 2 * D), BF16), jax.ShapeDtypeStruct((4, 8, D), F32), acc,
                   jax.ShapeDtypeStruct((NBLK, BW, 2 * BW), F32), jax.ShapeDtypeStruct((NBLK, 8, 2 * BW), F32), acc],
        scratch_shapes=[pltpu.VMEM((TL + 8, D), F32), pltpu.VMEM((TL + 8, D), F32), pltpu.VMEM((8, D), F32)],
        compiler_params=_cp(("arbitrary",)),
    )(g1, gate, rec, rec, hs, hs, w_out, conv_w, conv_b, wg, bg, lam)


def _a_in_bwd(du, h0, g1, a_norm, w_in):
    s = h0.shape[0]

    def body(du_ref, h_ref, g1_ref, an_ref, win_ref, gx_ref, n1_ref, dg_ref):
        @pl.when(pl.program_id(0) == 0)
        def _():
            dg_ref[...] = jnp.zeros_like(dg_ref)

        cw = 2 * D // NDEV
        dn = lax.dot_general(du_ref[:, 0:cw], win_ref[0], _NT, preferred_element_type=F32)
        for d in range(1, NDEV):
            dn = dn + lax.dot_general(du_ref[:, d * cw:(d + 1) * cw], win_ref[d], _NT, preferred_element_type=F32)
        h = h_ref[...]
        r = _rstd(h)
        xhat = h * r
        gx_ref[...] = g1_ref[...] + _rms_bwd(dn, xhat, r, an_ref[...])
        n1_ref[...] = (xhat * an_ref[...]).astype(BF16)
        dg_ref[...] += _sum8(dn * xhat)

    row = _rows(TM, D)
    return pl.pallas_call(
        body, name="a_in_bwd", grid=(s // TM,),
        in_specs=[_rows(TM, 2 * D), row, row, _full((1, D)), _full((NDEV, D, 2 * D // NDEV))],
        out_specs=[row, row, _acc_spec()],
        out_shape=[jax.ShapeDtypeStruct((s, D), F32), jax.ShapeDtypeStruct((s, D), BF16),
                   jax.ShapeDtypeStruct((8, D), F32)],
        compiler_params=_cp(("arbitrary",)),
    )(du, h0, g1, a_norm, w_in)


def _rel_onehot():
    m = np.arange(_TOEP)
    signed = np.where(m < KB, m, m - _TOEP)
    idx = np.clip(PADK - signed, -(CHUNK - 1), 2 * CHUNK) + (CHUNK - 1)
    return (idx[None, :] == np.arange(NREL)[:, None]).astype(np.float32)


def _bias_diagonals(rel_bias):
    diag = jnp.dot(rel_bias, jnp.asarray(_rel_onehot()), precision=lax.Precision.HIGHEST)
    return diag.reshape(NH // 2, 2, _TOEP)


def _rel_bias_grad(dd):
    rows = 8
    z = dd
    oh = np.zeros((_TOEP, 256), np.float32)
    oh[:, :NREL] = _rel_onehot().T

    def body(z_ref, oh_ref, o_ref):
        d = jnp.sum(z_ref[...], axis=0, keepdims=True)
        hi = d.astype(BF16)
        mid = (d - hi.astype(F32)).astype(BF16)
        lo = (d - hi.astype(F32) - mid.astype(F32)).astype(BF16)
        ohb = oh_ref[...].astype(BF16)
        acc = jnp.zeros((8, 256), F32)
        for piece in (lo, mid, hi):
            acc = acc + jnp.dot(jnp.broadcast_to(piece, (8, _TOEP)), ohb, preferred_element_type=F32)
        o_ref[...] = acc

    out = pl.pallas_call(
        body, name="rel_bias_grad", grid=(NH,),
        in_specs=[pl.BlockSpec((None, rows, _TOEP), lambda h: (h, 0, 0)), pl.BlockSpec((_TOEP, 256), lambda h: (0, 0))],
        out_specs=pl.BlockSpec((None, 8, 256), lambda h: (h, 0, 0)),
        out_shape=jax.ShapeDtypeStruct((NH, 8, 256), F32),
        compiler_params=_cp(("parallel",)),
    )(z, jnp.asarray(oh))
    return out[:, 0, :NREL]


def _exchange(arrays, scatter, name):
    n = len(arrays)

    def body(*refs):
        ins, outs = refs[:n], refs[n:2 * n]
        token, (send_sems, recv_sems, local_sems) = refs[2 * n], refs[2 * n + 1:]
        token[...] = jnp.zeros_like(token)
        x, y, c = lax.axis_index("x"), lax.axis_index("y"), lax.axis_index("c")
        me = 4 * x + 2 * y + c

        def peer_of(r):
            rx, ry, rc = (r >> 2) & 1, (r >> 1) & 1, r & 1
            px = 1 - x if rx else x
            py = 1 - y if ry else y
            pc = 1 - c if rc else c
            return (px, py, pc), 4 * px + 2 * py + pc

        local, sent = [], []
        for k in range(n):
            cp = pltpu.make_async_copy(ins[k].at[me] if scatter else ins[k], outs[k].at[me], local_sems.at[k])
            cp.start()
            local.append(cp)
            for r in range(1, NDEV):
                peer, peer_lin = peer_of(r)
                cp = pltpu.make_async_remote_copy(
                    src_ref=ins[k].at[peer_lin] if scatter else ins[k], dst_ref=outs[k].at[me],
                    send_sem=send_sems.at[k, r - 1], recv_sem=recv_sems.at[k, r - 1],
                    device_id=peer, device_id_type=pl.DeviceIdType.MESH)
                cp.start()
                sent.append(cp)
        for k in range(n):
            for r in range(1, NDEV):
                peer, peer_lin = peer_of(r)
                pltpu.make_async_remote_copy(
                    src_ref=ins[k].at[peer_lin] if scatter else ins[k], dst_ref=outs[k].at[peer_lin],
                    send_sem=send_sems.at[k, r - 1], recv_sem=recv_sems.at[k, r - 1],
                    device_id=peer, device_id_type=pl.DeviceIdType.MESH).wait_recv()
        for cp in sent:
            cp.wait_send()
        for cp in local:
            cp.wait()

    def slot_shape(a):
        return (NDEV,) + (a.shape[1:] if scatter else a.shape)

    anyspec = pl.BlockSpec(memory_space=pl.ANY)
    outs = pl.pallas_call(
        body, name=name,
        in_specs=[anyspec] * n, out_specs=[anyspec] * n + [pl.BlockSpec(memory_space=pltpu.VMEM)],
        out_shape=[jax.ShapeDtypeStruct(slot_shape(a), a.dtype) for a in arrays]
        + [jax.ShapeDtypeStruct((8, 128), F32)],
        scratch_shapes=[pltpu.SemaphoreType.DMA((n, NDEV - 1)), pltpu.SemaphoreType.DMA((n, NDEV - 1)),
                        pltpu.SemaphoreType.DMA((n,))],
        compiler_params=pltpu.CompilerParams(has_side_effects=True),
    )(*arrays)
    return outs[:n], outs[n]


def _peer(r):
    x, y, c = lax.axis_index("x"), lax.axis_index("y"), lax.axis_index("c")
    px = 1 - x if (r >> 2) & 1 else x
    py = 1 - y if (r >> 1) & 1 else y
    pc = 1 - c if r & 1 else c
    return (px, py, pc), 4 * px + 2 * py + pc


def _my_index():
    return 4 * lax.axis_index("x") + 2 * lax.axis_index("y") + lax.axis_index("c")


_HBM_SPEC = pl.BlockSpec(memory_space=pltpu.HBM)
_SEM_SPEC = pl.BlockSpec(memory_space=pltpu.SEMAPHORE)


_NPEER = NDEV - 1


def _exchange_start(arrays, scatter, name):
    n = len(arrays)
    ns = n * _NPEER
    slots = [(NDEV,) + (a.shape[1:] if scatter else a.shape) for a in arrays]

    def body(*refs):
        srcs, lands = refs[:n], refs[n:2 * n]
        send_sems, recv_sems = refs[2 * n:2 * n + ns], refs[2 * n + ns:2 * n + 2 * ns]
        token = refs[-1]
        me = _my_index()
        for k in range(n):
            for r in range(1, NDEV):
                peer, peer_lin = _peer(r)
                pltpu.make_async_remote_copy(
                    src_ref=srcs[k].at[peer_lin] if scatter else srcs[k], dst_ref=lands[k].at[me],
                    send_sem=send_sems[k * _NPEER + r - 1], recv_sem=recv_sems[k * _NPEER + r - 1],
                    device_id=peer, device_id_type=pl.DeviceIdType.MESH).start()
        token[...] = jnp.zeros_like(token)

    sem = pltpu.SemaphoreType.DMA(())
    outs = pl.pallas_call(
        body, name=name,
        out_shape=(*[sem] * (2 * ns), *[pltpu.HBM(a.shape, a.dtype) for a in arrays],
                   *[pltpu.HBM(s, a.dtype) for s, a in zip(slots, arrays)], jax.ShapeDtypeStruct((8, 128), F32)),
        in_specs=[_HBM_SPEC] * (2 * n),
        out_specs=(*[_SEM_SPEC] * (2 * ns), *[_HBM_SPEC] * (2 * n), pl.BlockSpec(memory_space=pltpu.VMEM)),
        input_output_aliases={k: 2 * ns + k for k in range(2 * n)},
        compiler_params=pltpu.CompilerParams(has_side_effects=pltpu.SideEffectType.DATAFLOW_SIDE_EFFECTING),
    )(*[pltpu.with_memory_space_constraint(a, pltpu.HBM) for a in arrays],
      *[pltpu.with_memory_space_constraint(lax.empty(s, a.dtype), pltpu.HBM) for s, a in zip(slots, arrays)])
    return outs[:ns], outs[ns:2 * ns], outs[2 * ns:2 * ns + n], outs[2 * ns + n:2 * ns + 2 * n], outs[-1]


def _exchange_wait(started, after, scatter, name):
    send_sems, recv_sems, srcs, lands, _ = started
    n = len(srcs)
    ns = n * _NPEER

    def body(*refs):
        src_refs, land_refs = refs[:n], refs[n:2 * n]
        ssem, rsem = refs[2 * n:2 * n + ns], refs[2 * n + ns:2 * n + 2 * ns]
        for k in range(n):
            for r in range(1, NDEV):
                peer, peer_lin = _peer(r)
                cp = pltpu.make_async_remote_copy(
                    src_ref=src_refs[k].at[peer_lin] if scatter else src_refs[k], dst_ref=land_refs[k].at[peer_lin],
                    send_sem=ssem[k * _NPEER + r - 1], recv_sem=rsem[k * _NPEER + r - 1],
                    device_id=peer, device_id_type=pl.DeviceIdType.MESH)
                cp.wait_send()
                cp.wait_recv()

    outs = pl.pallas_call(
        body, name=name,
        out_shape=tuple(pltpu.HBM(a.shape, a.dtype) for a in list(srcs) + list(lands)),
        in_specs=[_HBM_SPEC] * (2 * n) + [_SEM_SPEC] * (2 * ns) + [pl.BlockSpec(memory_space=pl.ANY)],
        out_specs=tuple([_HBM_SPEC] * (2 * n)),
        input_output_aliases={k: k for k in range(2 * n)},
        compiler_params=pltpu.CompilerParams(has_side_effects=pltpu.SideEffectType.DATAFLOW_SIDE_EFFECTING),
    )(*srcs, *lands, *send_sems, *recv_sems, after)
    return list(outs[:n]), list(outs[n:])


def _fill_own(lands, owns, me):
    return [lax.dynamic_update_slice(z, o, (me,) + (0,) * (z.ndim - 1)) for z, o in zip(lands, owns)]


def _sum_slots(st, name):
    _, r, c = st.shape

    def body(s_ref, o_ref):
        acc = s_ref[0]
        for d in range(1, NDEV):
            acc = acc + s_ref[d]
        o_ref[...] = acc

    return pl.pallas_call(
        body, name=name, out_shape=jax.ShapeDtypeStruct((r, c), F32),
        in_specs=[pl.BlockSpec((NDEV, r, c), lambda: (0, 0, 0))], out_specs=pl.BlockSpec((r, c), lambda: (0, 0)),
    )(st)


def _adamw(w, m, v, gst, name, transposed=False):
    r, c = w.shape
    ns = gst.shape[0]
    tr = min(r, 256)
    c1 = 1.0 - ADAM_B1 ** ADAM_STEP
    c2 = 1.0 - ADAM_B2 ** ADAM_STEP

    def body(w_ref, m_ref, v_ref, g_ref, go_ref, d_ref, mo_ref, vo_ref):
        g = g_ref[0].astype(F32)
        for d in range(1, ns):
            g = g + g_ref[d].astype(F32)
        if transposed:
            g = g.T
        m2 = ADAM_B1 * m_ref[...] + (1.0 - ADAM_B1) * g
        v2 = ADAM_B2 * v_ref[...] + (1.0 - ADAM_B2) * (g * g)
        go_ref[...] = g
        mo_ref[...] = m2
        vo_ref[...] = v2
        d_ref[...] = (-ADAM_LR) * ((m2 / c1) / (jnp.sqrt(v2 / c2) + ADAM_EPS) + ADAM_WD * w_ref[...])

    blk = pl.BlockSpec((tr, c), lambda i: (i, 0))
    sd = jax.ShapeDtypeStruct((r, c), F32)
    return pl.pallas_call(
        body, name=name, grid=(r // tr,),
        in_specs=[blk, blk, blk, pl.BlockSpec((ns, c, tr), lambda i: (0, 0, i)) if transposed
                  else pl.BlockSpec((ns, tr, c), lambda i: (0, i, 0))],
        out_specs=[blk, blk, blk, blk], out_shape=[sd, sd, sd, sd],
        compiler_params=_cp(("parallel",)),
    )(w, m, v, gst)


def _pack(pieces, rows):
    flat = jnp.concatenate([p.reshape(-1).astype(F32) for p in pieces])
    return jnp.pad(flat, (0, rows * 128 - flat.shape[0])).reshape(rows, 128)


def _unpack(flat, shapes):
    out, off = [], 0
    for shp in shapes:
        size = int(np.prod(shp))
        out.append(flat[off:off + size].reshape(shp))
        off += size
    return out


def _cols(full, me, width):
    return lax.dynamic_slice_in_dim(full, me * width, width, axis=full.ndim - 1)


def kernel(x, a_norm, a_w_in, a_conv_w, a_conv_b, a_w_gate, a_b_gate, a_lambda, a_w_out, kv_norm, w_kv, k_norm, b_norm, b_w_q, b_q_norm, b_rel_bias, b_w_o, mlp_norm, w_up, w_down, loss_target, m_a_norm, m_a_w_in, m_a_conv_w, m_a_conv_b, m_a_w_gate, m_a_b_gate, m_a_lambda, m_a_w_out, m_kv_norm, m_w_kv, m_k_norm, m_b_norm, m_b_w_q, m_b_q_norm, m_b_rel_bias, m_b_w_o, m_mlp_norm, m_w_up, m_w_down, v_a_norm, v_a_w_in, v_a_conv_w, v_a_conv_b, v_a_w_gate, v_a_b_gate, v_a_lambda, v_a_w_out, v_kv_norm, v_w_kv, v_k_norm, v_b_norm, v_b_w_q, v_b_q_norm, v_b_rel_bias, v_b_w_o, v_mlp_norm, v_w_up, v_w_down):
    me = 4 * lax.axis_index("x") + 2 * lax.axis_index("y") + lax.axis_index("c")
    sh = D // NDEV

    big_w = [a_w_in[0], a_w_out[0], w_kv, b_w_q[0], b_w_o[0], w_up[0], w_up[1], w_down[0], w_down[1]]
    small_sharded = [a_norm, a_conv_w, a_conv_b, a_b_gate, a_lambda, a_w_gate]
    small_rows = 272
    def to_bf16(w, token):
        return (w + token[0, 0]).astype(BF16)

    got, tok_a = _exchange([a_w_in[0].astype(BF16), _pack(small_sharded, small_rows)], False, "gather_a")
    own_b1 = [to_bf16(w, tok_a) for w in (a_w_out[0], w_up[0], w_down[0])]
    st_b1 = _exchange_start(own_b1, False, "gather_b1_start")
    own_b2 = [to_bf16(w, st_b1[4]) for w in (w_kv, b_w_q[0], b_w_o[0], w_up[1], w_down[1])]
    st_b2 = _exchange_start(own_b2, False, "gather_b2_start")
    w_in = got[0]
    sm = got[1].reshape(NDEV, small_rows * 128)
    an_f = sm[:, 0:128].reshape(1, D) + st_b2[4][0:1, 0:1]
    cw_f = sm[:, 128:640].reshape(NDEV, 4, sh).transpose(1, 0, 2).reshape(4, D)
    cb_f = sm[:, 640:768].reshape(1, D)
    bg_f = sm[:, 768:1024].reshape(NDEV, NBLK, 2 * BW // NDEV).transpose(1, 0, 2).reshape(NBLK, 2 * BW)
    lam_f = sm[:, 1024:1152].reshape(1, D)
    wg_f = sm[:, 1152:1152 + NBLK * BW * 32].reshape(NDEV, NBLK, BW, 32).transpose(1, 2, 0, 3)
    wg_f = wg_f.reshape(NBLK, BW, 2 * BW).astype(BF16)
    kn_t = jnp.tile(k_norm, NH).reshape(1, D)
    qn_t = jnp.tile(b_q_norm[0], NH).reshape(1, D)
    kvn = kv_norm.reshape(1, D)
    diag = _bias_diagonals(b_rel_bias[0])

    h0 = x[0]
    gate, rec, hs, y = _lru_fwd(h0, an_f, w_in, cw_f, cb_f, wg_f, bg_f, lam_f)
    own, land = _exchange_wait(st_b1, y, False, "gather_b1_wait")
    land = _fill_own(land, [o[None] for o in own], me)
    w_out = land[0].reshape(D, D)
    wu = [land[1], None]
    wd = [land[2].reshape(FF, D), None]
    h1, h2, up0 = _mlp_fwd(h0, y, w_out, mlp_norm[0:1], wu[0], wd[0], "mlp_fwd0")
    own, land = _exchange_wait(st_b2, h2, False, "gather_b2_wait")
    land = _fill_own(land, [o[None] for o in own], me)
    wkv = land[0]
    w_q = land[1].reshape(D, D)
    w_o = land[2].reshape(D, D)
    wu[1] = land[3]
    wd[1] = land[4].reshape(FF, D)
    kraw, qraw, kpad, vpad, q = _kvq_fwd(h2, kvn, b_norm, wkv, w_q, kn_t, qn_t)
    o = _attn_fwd(q, kpad, vpad, diag)
    h3, h4, up1 = _mlp_fwd(h2, o, w_o, mlp_norm[1:2], wu[1], wd[1], "mlp_fwd1")
    g4, lpart = _loss_grad(h4, loss_target[0])
    loss = lax.psum(jnp.sum(lpart), ("x", "y", "c"))

    g3, dup1, act1, n3, dgm1 = _mlp_bwd(g4, h3, up1, mlp_norm[1:2], wu[1], wd[1], "mlp_bwd1")
    d_wd1 = _matmul_tn(g4, act1, True, BF16, "dw_down1")
    d_wu1 = _matmul_tn(n3, dup1, True, BF16, "dw_up1")
    do = _matmul_nt(g3, w_o, "do_proj")
    d_wo = _matmul_tn(o, g3, False, BF16, "dw_o").reshape(NDEV, sh, D)
    dq, dk, dv, dd = _attn_bwd(q, kpad, vpad, do, diag)
    g2, dqr, dkv, nb, nk, dgq, dgk, dgb, dgkv = _kvq_bwd(dq, dk, dv, qraw, kraw, h2, g3, kvn, b_norm, wkv, w_q,
                                                       kn_t, qn_t)
    d_wq = _matmul_tn(nb, dqr, False, BF16, "dw_q").reshape(NDEV, sh, D)
    d_wkv = _matmul_tn(nk, dkv, True, BF16, "dw_kv")
    st_r1 = _exchange_start([d_wd1, d_wu1, d_wo, d_wq, d_wkv], True, "scatter_r1_start")
    g1, dup0, act0, n2, dgm0 = _mlp_bwd(g2, h1, up0, mlp_norm[0:1] + st_r1[4][0:1, 0:1], wu[0], wd[0], "mlp_bwd0")
    d_wd0 = _matmul_tn(g2, act0, True, BF16, "dw_down0")
    d_wu0 = _matmul_tn(n2, dup0, True, BF16, "dw_up0")
    st_r2 = _exchange_start([d_wu0, d_wd0], True, "scatter_r2_start")
    du, dcw, dcb, dwg, dbg, dlam = _lru_bwd(g1, gate, rec, hs, w_out, cw_f, cb_f, wg_f, bg_f,
                                            lam_f + st_r2[4][0:1, 0:1])
    d_wout = _matmul_tn(y, g1, False, BF16, "dw_out").reshape(NDEV, sh, D)
    gx, n1, dga = _a_in_bwd(du, h0, g1, an_f, w_in)
    d_win = _matmul_tn(n1, du, True, BF16, "dw_in")
    d_rel = _rel_bias_grad(dd.reshape(NH, 8, _TOEP))

    dwg_slab = dwg.reshape(NBLK, BW, NDEV, 32).transpose(2, 0, 1, 3).reshape(NDEV, 256, 128)
    recv3, _ = _exchange([d_win, d_wout, dwg_slab], True, "scatter_r3")
    small_full = [dga.sum(0), dcw.sum(1), dcb.sum(0), dbg.sum(1), dlam.sum(0), dgkv.sum(0),
                  dgk.sum(0).reshape(NH, HD).sum(0), dgb.sum(0), dgq.sum(0).reshape(NH, HD).sum(0), d_rel,
                  jnp.stack([dgm0.sum(0), dgm1.sum(0)])]
    small_g_rows = 136
    (gsm,), _ = _exchange([_pack(small_full, small_g_rows)], False, "gather_small_grads")
    src, recv1 = _exchange_wait(st_r1, gsm, True, "scatter_r1_wait")
    recv1 = _fill_own(recv1, [lax.dynamic_slice_in_dim(a, me, 1, 0) for a in src], me)
    src, recv2 = _exchange_wait(st_r2, recv1[0], True, "scatter_r2_wait")
    recv2 = _fill_own(recv2, [lax.dynamic_slice_in_dim(a, me, 1, 0) for a in src], me)
    recv = [recv3[0], recv3[1], recv1[4], recv1[3], recv1[2], recv2[0], recv1[1], recv2[1], recv1[0], recv3[2]]
    gs = _unpack(_sum_slots(gsm, "sum_small_grads").reshape(-1),
                 [(1, D), (4, D), (1, D), (NBLK, 2 * BW), (1, D), (D,), (HD,), (1, D), (1, HD), (1, NH, NREL), (2, D)])
    g_small = [_cols(gs[0], me, sh), _cols(gs[1], me, sh)[None], _cols(gs[2], me, sh),
               _cols(gs[3], me, 2 * BW // NDEV)[None], _cols(gs[4], me, sh)] + gs[5:]

    names = ["a_w_in", "a_w_out", "w_kv", "b_w_q", "b_w_o", "w_up0", "w_up1", "w_down0", "w_down1", "a_w_gate"]
    big_m = [m_a_w_in[0], m_a_w_out[0], m_w_kv, m_b_w_q[0], m_b_w_o[0], m_w_up[0], m_w_up[1], m_w_down[0],
             m_w_down[1], m_a_w_gate.reshape(256, 128)]
    big_v = [v_a_w_in[0], v_a_w_out[0], v_w_kv, v_b_w_q[0], v_b_w_o[0], v_w_up[0], v_w_up[1], v_w_down[0],
             v_w_down[1], v_a_w_gate.reshape(256, 128)]
    res = [_adamw(w, m, v, g, "adamw_" + nm, transposed=nm.startswith("w_down"))
           for w, m, v, g, nm in zip(big_w + [a_w_gate.reshape(256, 128)], big_m, big_v, recv, names)]
    small_w = [a_norm, a_conv_w, a_conv_b, a_b_gate, a_lambda, kv_norm, k_norm, b_norm, b_q_norm, b_rel_bias, mlp_norm]
    small_m = [m_a_norm, m_a_conv_w, m_a_conv_b, m_a_b_gate, m_a_lambda, m_kv_norm, m_k_norm, m_b_norm, m_b_q_norm,
               m_b_rel_bias, m_mlp_norm]
    small_v = [v_a_norm, v_a_conv_w, v_a_conv_b, v_a_b_gate, v_a_lambda, v_kv_norm, v_k_norm, v_b_norm, v_b_q_norm,
               v_b_rel_bias, v_mlp_norm]
    pr = 72
    res_small = _adamw(_pack(small_w, pr), _pack(small_m, pr), _pack(small_v, pr), _pack(g_small, pr)[None],
                       "adamw_small")
    small_shapes = [w.shape for w in small_w]
    res_small = [_unpack(r.reshape(-1), small_shapes) for r in res_small]

    def assemble(t):
        b = [r[t] for r in res]
        s_ = res_small[t]
        return [s_[0], b[0][None], s_[1], s_[2], b[9].reshape(a_w_gate.shape), s_[3], s_[4], b[1][None],
                s_[5], b[2], s_[6], s_[7], b[3][None], s_[8], s_[9], b[4][None], s_[10],
                jnp.stack([b[5], b[6]]), jnp.stack([b[7], b[8]])]

    return tuple([loss, gx[None]] + assemble(0) + assemble(1) + assemble(2) + assemble(3))
```

```python
import functools

import numpy as np
import jax
import jax.numpy as jnp
from jax import lax
from jax.experimental import pallas as pl
from jax.experimental.pallas import tpu as pltpu

F32 = jnp.float32
BF16 = jnp.bfloat16

D = 1024
NH = 16
HD = 64
FF = 4096
NBLK = 8
BW = 128
CHUNK = 64
PADK = 512
NREL = 192
EPS = 1e-6
LRU_C = 8.0
NDEV = 8

V7X_VMEM_LIMIT = 56 * 1024 * 1024
TM = 512
TMM = 256
TL = 256
QB = 256
ATT_RC = 32
HPS = 4
LW = HPS * HD
KB = QB + PADK
NEG = -1e30

ADAM_LR, ADAM_B1, ADAM_B2, ADAM_EPS, ADAM_WD, ADAM_STEP = 0.001, 0.9, 0.999, 1e-08, 0.01, 10

_NT = (((1,), (1,)), ((), ()))
_TN = (((0,), (0,)), ((), ()))


def _cp(sem=None):
    return pltpu.CompilerParams(dimension_semantics=sem, vmem_limit_bytes=V7X_VMEM_LIMIT)


def _full(shape):
    n = len(shape)
    return pl.BlockSpec(shape, lambda *a: (0,) * n, pipeline_mode=pl.Buffered(1))


def _rows(tm, width):
    return pl.BlockSpec((tm, width), lambda i: (i, 0))


def _rstd(h):
    return lax.rsqrt(jnp.mean(h * h, axis=-1, keepdims=True) + EPS)


def _sigmoid(x):
    return 1.0 / (1.0 + jnp.exp(-x))


def _expm1(x):
    small = x * (1.0 + x * (0.5 + x * (1.0 / 6.0 + x * (1.0 / 24.0))))
    return jnp.where(jnp.abs(x) < 0.03, small, jnp.exp(x) - 1.0)


def _softplus_neg(lam):
    e = jnp.exp(-jnp.abs(lam))
    series = e * (1.0 - e * (0.5 - e * (1.0 / 3.0 - e * 0.25)))
    return jnp.maximum(-lam, 0.0) + jnp.where(e < 0.01, series, jnp.log(1.0 + e))


_GELU_K = 0.7978845608028654


def _gelu(x):
    return 0.5 * x * (1.0 + jnp.tanh(_GELU_K * (x + 0.044715 * x * x * x)))


def _gelu_grad(x):
    t = jnp.tanh(_GELU_K * (x + 0.044715 * x * x * x))
    return 0.5 * (1.0 + t) + 0.5 * x * (1.0 - t * t) * _GELU_K * (1.0 + 3.0 * 0.044715 * x * x)


def _sum8(x):
    r, c = x.shape
    return jnp.sum(x.reshape(r // 8, 8, c), axis=0)


def _shift_down(x, s, fill, rows):
    return jnp.where(rows >= s, pltpu.roll(x, s, axis=0), fill)


def _shift_up(x, s, fill, rows, n):
    return jnp.where(rows < n - s, pltpu.roll(x, n - s, axis=0), fill)


def _lru_gates(rc, wg_n, bg_n, sp_n):
    g = jnp.dot(rc.astype(BF16), wg_n, preferred_element_type=F32) + bg_n
    rg = _sigmoid(g[:, :BW])
    ig = _sigmoid(g[:, BW:])
    la = (-LRU_C) * rg * sp_n
    a = jnp.exp(la)
    mult = jnp.sqrt(-_expm1(2.0 * la))
    return rg, ig, a, mult


def _conv(ext_ref, cw_ref, cb_ref, sl, n):
    out = cb_ref[:, sl] + cw_ref[0:1, sl] * ext_ref[5:5 + n, sl]
    for k in range(1, 4):
        out = out + cw_ref[k:k + 1, sl] * ext_ref[5 + k:5 + k + n, sl]
    return out


def _tile_scan(a_ref, b_ref, h_ref, carry, reverse):
    sub = lax.broadcasted_iota(jnp.int32, (TL, BW), 0) % 8
    for n in range(NBLK):
        a, b = a_ref[n], b_ref[n]
        for s in (1, 2, 4):
            if reverse:
                inside = sub < 8 - s
                a_sh = jnp.where(inside, pltpu.roll(a, TL - s, axis=0), 1.0)
                b_sh = jnp.where(inside, pltpu.roll(b, TL - s, axis=0), 0.0)
            else:
                inside = sub >= s
                a_sh = jnp.where(inside, pltpu.roll(a, s, axis=0), 1.0)
                b_sh = jnp.where(inside, pltpu.roll(b, s, axis=0), 0.0)
            b = a * b_sh + b
            a = a * a_sh
        a_ref[n], b_ref[n] = a, b
    carry = list(carry)
    groups = range(TL // 8 - 1, -1, -1) if reverse else range(TL // 8)
    for g in groups:
        r = slice(8 * g, 8 * g + 8)
        for n in range(NBLK):
            h = a_ref[n, r, :] * carry[n] + b_ref[n, r, :]
            h_ref[n, r, :] = h
            carry[n] = h[0:1, :] if reverse else h[7:8, :]
    return carry


def _lru_fwd(h0, a_norm, w_in, conv_w, conv_b, wg, bg, lam):
    s = h0.shape[0]

    def body(h0_ref, an_ref, win_ref, cw_ref, cb_ref, wg_ref, bg_ref, lam_ref,
             gate_ref, rec_ref, hs_ref, y_ref, ext_ref, hc_ref, a_sc, b_sc, hl_sc):
        i = pl.program_id(0)

        @pl.when(i == 0)
        def _():
            ext_ref[0:8, :] = jnp.zeros((8, D), F32)
            hc_ref[...] = jnp.zeros_like(hc_ref)

        h = h0_ref[...]
        n1 = (h * _rstd(h) * an_ref[...]).astype(BF16)
        cw = 2 * D // NDEV
        for d in range(NDEV):
            ud = jnp.dot(n1, win_ref[d], preferred_element_type=F32)
            if d < NDEV // 2:
                gate_ref[:, d * cw:(d + 1) * cw] = ud
            else:
                rec_ref[:, d * cw - D:(d + 1) * cw - D] = ud
                ext_ref[8:8 + TL, d * cw - D:(d + 1) * cw - D] = ud
        sp = _softplus_neg(lam_ref[...])
        for n in range(NBLK):
            sl = slice(n * BW, (n + 1) * BW)
            rc = _conv(ext_ref, cw_ref, cb_ref, sl, TL)
            rg, ig, a, mult = _lru_gates(rc, wg_ref[n], bg_ref[n:n + 1, :], sp[:, sl])
            a_sc[n] = a
            b_sc[n] = mult * (ig * rc)
        carry = _tile_scan(a_sc, b_sc, hl_sc, [hc_ref[0:1, n * BW:(n + 1) * BW] for n in range(NBLK)], reverse=False)
        for n in range(NBLK):
            sl = slice(n * BW, (n + 1) * BW)
            hh = hl_sc[n]
            hc_ref[0:1, sl] = carry[n]
            hs_ref[:, sl] = hh
            y_ref[:, sl] = (_gelu(gate_ref[:, sl]) * hh).astype(BF16)
        ext_ref[0:8, :] = ext_ref[TL:TL + 8, :]

    row = _rows(TL, D)
    return pl.pallas_call(
        body, name="lru_fwd", grid=(s // TL,),
        in_specs=[row, _full((1, D)), _full((NDEV, D, 2 * D // NDEV)), _full((4, D)), _full((1, D)),
                  _full((NBLK, BW, 2 * BW)), _full((NBLK, 2 * BW)), _full((1, D))],
        out_specs=[row, row, row, row],
        out_shape=[jax.ShapeDtypeStruct((s, D), F32), jax.ShapeDtypeStruct((s, D), F32),
                   jax.ShapeDtypeStruct((s, D), F32), jax.ShapeDtypeStruct((s, D), BF16)],
        scratch_shapes=[pltpu.VMEM((TL + 8, D), F32), pltpu.VMEM((8, D), F32)]
        + [pltpu.VMEM((NBLK, TL, BW), F32)] * 3,
        compiler_params=_cp(("arbitrary",)),
    )(h0, a_norm, w_in, conv_w, conv_b, wg, bg, lam)


def _mlp_fwd(res, px, pw, g, wu, wd, name):
    s = res.shape[0]
    fj = 512

    def body(res_ref, px_ref, pw_ref, g_ref, wu_ref, wd_ref, hin_ref, hout_ref, up_ref, n_ref):
        hin = res_ref[...] + jnp.dot(px_ref[...], pw_ref[...], preferred_element_type=F32)
        hin_ref[...] = hin
        hout_ref[...] = hin
        n_ref[...] = (hin * _rstd(hin) * g_ref[...]).astype(BF16)
        for j in range(FF // fj):
            sl = slice(j * fj, (j + 1) * fj)
            up = jnp.dot(n_ref[...], wu_ref[j], preferred_element_type=F32)
            up_ref[:, sl] = up
            rl = jnp.maximum(up, 0.0)
            hout_ref[...] += jnp.dot((rl * rl).astype(BF16), wd_ref[sl, :], preferred_element_type=F32)

    row = _rows(TMM, D)
    return pl.pallas_call(
        body, name=name, grid=(s // TMM,),
        in_specs=[row, row, _full((D, D)), _full((1, D)), _full((NDEV, D, fj)), _full((FF, D))],
        out_specs=[row, row, _rows(TMM, FF)],
        out_shape=[jax.ShapeDtypeStruct((s, D), F32), jax.ShapeDtypeStruct((s, D), F32),
                   jax.ShapeDtypeStruct((s, FF), F32)],
        scratch_shapes=[pltpu.VMEM((TMM, D), BF16)],
        compiler_params=_cp(("parallel",)),
    )(res, px, pw, g, wu, wd)


def _head_rstd(x2, lo):
    sq = x2 * x2
    s_lo = jnp.sum(jnp.where(lo, sq, 0.0), axis=-1, keepdims=True)
    s_hi = jnp.sum(jnp.where(lo, 0.0, sq), axis=-1, keepdims=True)
    return lax.rsqrt(jnp.where(lo, s_lo, s_hi) * (1.0 / HD) + EPS)


def _kvq_fwd(h2, kv_norm, b_norm, w_kv, w_q, k_norm_t, q_norm_t):
    s = h2.shape[0]
    assert PADK == TM

    def body(h_ref, gkv_ref, gb_ref, wkv_ref, wq_ref, kn_ref, qn_ref,
             kraw_ref, qraw_ref, k_ref, v_ref, q_ref):
        i = pl.program_id(0)

        @pl.when(i == 0)
        def _():
            k_ref[...] = jnp.zeros_like(k_ref)
            v_ref[...] = jnp.zeros_like(v_ref)

        @pl.when(i > 0)
        def _():
            h = h_ref[...]
            xhat = h * _rstd(h)
            nk = (xhat * gkv_ref[...]).astype(BF16)
            qr = jnp.dot((xhat * gb_ref[...]).astype(BF16), wq_ref[...], preferred_element_type=F32)
            qraw_ref[...] = qr
            lo = lax.broadcasted_iota(jnp.int32, (1, 128), 1) < HD
            cw = 2 * D // NDEV
            for d in range(NDEV):
                kvd = jnp.dot(nk, wkv_ref[d], preferred_element_type=F32)
                if d < NDEV // 2:
                    kraw_ref[:, d * cw:(d + 1) * cw] = kvd
                    for p in range(cw // 128):
                        sl = slice(d * cw + p * 128, d * cw + (p + 1) * 128)
                        k2 = kvd[:, p * 128:(p + 1) * 128]
                        k_ref[:, sl] = (k2 * _head_rstd(k2, lo) * kn_ref[:, sl]).astype(BF16)
                else:
                    v_ref[:, d * cw - D:(d + 1) * cw - D] = kvd.astype(BF16)
            for p in range(D // 128):
                sl = slice(p * 128, (p + 1) * 128)
                q2 = qr[:, sl]
                q_ref[:, sl] = (q2 * _head_rstd(q2, lo) * qn_ref[:, sl] * (HD ** -0.5)).astype(BF16)

    prev = pl.BlockSpec((TM, D), lambda i: (jnp.maximum(i - 1, 0), 0))
    cur = pl.BlockSpec((TM, D), lambda i: (i, 0))
    return pl.pallas_call(
        body, name="kvq_fwd", grid=(s // TM + 1,),
        in_specs=[prev, _full((1, D)), _full((1, D)), _full((NDEV, D, 2 * D // NDEV)), _full((D, D)), _full((1, D)),
                  _full((1, D))],
        out_specs=[prev, prev, cur, cur, prev],
        out_shape=[jax.ShapeDtypeStruct((s, D), F32), jax.ShapeDtypeStruct((s, D), F32),
                   jax.ShapeDtypeStruct((s + PADK, D), BF16), jax.ShapeDtypeStruct((s + PADK, D), BF16),
                   jax.ShapeDtypeStruct((s, D), BF16)],
        compiler_params=_cp(("arbitrary",)),
    )(h2, kv_norm, b_norm, w_kv, w_q, k_norm_t, q_norm_t)


_TOEP = QB + KB


def _bias_from_diag(diag_ref, bias_ref):
    row8 = lax.broadcasted_iota(jnp.int32, (8, _TOEP), 0)
    kchunk = lax.broadcasted_iota(jnp.int32, (8, KB), 1) // CHUNK
    for a in range(HPS):
        v = jnp.broadcast_to(diag_ref[a:a + 1, :], (8, _TOEP))
        z0 = v
        for b in range(1, 8):
            z0 = jnp.where(row8 == b, pltpu.roll(v, b, axis=1), z0)
        for t in range(QB // 8):
            slab = z0 if t == 0 else pltpu.roll(z0, 8 * t, axis=1)
            qchunk = (8 * t) // CHUNK
            band = jnp.logical_and(kchunk >= qchunk, kchunk <= qchunk + PADK // CHUNK)
            bias_ref[a, 8 * t:8 * t + 8, :] = jnp.where(band, slab[:, :KB], NEG)


def _diag_sums(db_ref, a):
    row8 = lax.broadcasted_iota(jnp.int32, (8, _TOEP), 0)
    z = jnp.zeros((8, _TOEP), F32)
    for t in range(QB // 8):
        slab = jnp.concatenate([db_ref[a, 8 * t:8 * t + 8, :], jnp.zeros((8, _TOEP - KB), F32)], axis=1)
        z = z + (slab if t == 0 else pltpu.roll(slab, _TOEP - 8 * t, axis=1))
    e = z
    for b in range(1, 8):
        e = jnp.where(row8 == b, pltpu.roll(z, _TOEP - b, axis=1), e)
    return e


def _attn_specs(nqb):
    qspec = pl.BlockSpec((QB, LW), lambda p, j: (jnp.minimum(j, nqb - 1), p))
    kspecs = [pl.BlockSpec((QB, LW), functools.partial(lambda p, j, t: (jnp.minimum(j, nqb - 1) + t, p), t=t))
              for t in range(KB // QB)]
    dspec = pl.BlockSpec((None, HPS, _TOEP), lambda p, j: (p, 0, 0))
    return qspec, kspecs, dspec


def _head_masks():
    head = lax.broadcasted_iota(jnp.int32, (1, LW), 1) // HD
    return [head == a for a in range(HPS)]


def _pick_heads(parts, masks):
    out = parts[HPS - 1]
    for a in range(HPS - 2, -1, -1):
        out = jnp.where(masks[a], parts[a], out)
    return out


def _attn_fwd(q, kpad, vpad, diag):
    s = q.shape[0]
    nqb = s // QB
    npad = PADK // QB

    def body(q_ref, k0, k1, k2, v0, v1, v2, diag_ref, o_ref, bias_ref, sc_ref, eb_ref, rl_ref):
        j = pl.program_id(1)

        @pl.when(j == 0)
        def _():
            _bias_from_diag(diag_ref, bias_ref)

        def block(masked):
            kcat = jnp.concatenate([k0[...], k1[...], k2[...]], axis=0)
            vcat = jnp.concatenate([v0[...], v1[...], v2[...]], axis=0)
            q2 = q_ref[...]
            masks = _head_masks()
            valid = (lax.broadcasted_iota(jnp.int32, (1, KB), 1) + j * QB >= PADK) if masked else None
            outs = []
            for a in range(HPS):
                sc_ref[a] = lax.dot_general(jnp.where(masks[a], q2, jnp.zeros_like(q2)), kcat, _NT,
                                            preferred_element_type=F32)
            for a in range(HPS):
                for c in range(QB // ATT_RC):
                    r = slice(c * ATT_RC, (c + 1) * ATT_RC)
                    sc = sc_ref[a, r, :] + bias_ref[a, r, :]
                    if masked:
                        sc = jnp.where(valid, sc, NEG)
                    e = jnp.exp(sc - jnp.max(sc, axis=-1, keepdims=True))
                    eb_ref[a, r, :] = e.astype(BF16)
                    rl_ref[a, r, :] = jnp.broadcast_to(1.0 / jnp.sum(e, axis=-1, keepdims=True), (ATT_RC, LW))
                outs.append(jnp.dot(eb_ref[a], vcat, preferred_element_type=F32) * rl_ref[a])
            o_ref[...] = _pick_heads(outs, masks).astype(BF16)

        pl.when(j < npad)(functools.partial(block, True))
        pl.when(j >= npad)(functools.partial(block, False))

    qspec, kspecs, dspec = _attn_specs(nqb)
    assert len(kspecs) == 3
    return pl.pallas_call(
        body, name="attn_fwd", grid=(D // LW, nqb),
        in_specs=[qspec] + kspecs + kspecs + [dspec],
        out_specs=qspec,
        out_shape=jax.ShapeDtypeStruct((s, D), BF16),
        scratch_shapes=[pltpu.VMEM((HPS, QB, KB), F32), pltpu.VMEM((HPS, QB, KB), F32),
                        pltpu.VMEM((HPS, QB, KB), BF16), pltpu.VMEM((HPS, QB, LW), F32)],
        compiler_params=_cp(("parallel", "arbitrary")),
    )(q, kpad, kpad, kpad, vpad, vpad, vpad, diag)


def _loss_grad(h4, tgt):
    s = h4.shape[0]

    def body(h_ref, t_ref, g_ref, l_ref):
        @pl.when(pl.program_id(0) == 0)
        def _():
            l_ref[...] = jnp.zeros_like(l_ref)

        d = h_ref[...] - t_ref[...]
        g_ref[...] = d * (1.0 / D)
        l_ref[...] += _sum8(d * d) * (0.5 / D)

    row = _rows(TM, D)
    return pl.pallas_call(
        body, name="loss_grad", grid=(s // TM,),
        in_specs=[row, row], out_specs=[row, pl.BlockSpec((8, D), lambda i: (0, 0))],
        out_shape=[jax.ShapeDtypeStruct((s, D), F32), jax.ShapeDtypeStruct((8, D), F32)],
        compiler_params=_cp(("arbitrary",)),
    )(h4, tgt)


def _rms_bwd(dn, xhat, r, g):
    dng = dn * g
    return r * (dng - xhat * jnp.mean(dng * xhat, axis=-1, keepdims=True))


def _acc_spec():
    return pl.BlockSpec((8, D), lambda i: (0, 0))


def _mlp_bwd(gout, hin, up, g, wu, wd, name):
    s = gout.shape[0]
    fj = 512

    def body(go_ref, hin_ref, up_ref, g_ref, wu_ref, wd_ref, gin_ref, dup_ref, act_ref, n_ref, dg_ref,
             gob_ref, dn_ref):
        @pl.when(pl.program_id(0) == 0)
        def _():
            dg_ref[...] = jnp.zeros_like(dg_ref)

        gob_ref[...] = go_ref[...].astype(BF16)
        for j in range(FF // fj):
            sl = slice(j * fj, (j + 1) * fj)
            rl = jnp.maximum(up_ref[:, sl], 0.0)
            act_ref[:, sl] = (rl * rl).astype(BF16)
            dact = lax.dot_general(gob_ref[...], wd_ref[sl, :], _NT, preferred_element_type=F32)
            dupj = (dact * (2.0 * rl)).astype(BF16)
            dup_ref[:, sl] = dupj
            part = lax.dot_general(dupj, wu_ref[j], _NT, preferred_element_type=F32)
            if j == 0:
                dn_ref[...] = part
            else:
                dn_ref[...] += part
        hin = hin_ref[...]
        r = _rstd(hin)
        xhat = hin * r
        n_ref[...] = (xhat * g_ref[...]).astype(BF16)
        dn = dn_ref[...]
        gin_ref[...] = go_ref[...] + _rms_bwd(dn, xhat, r, g_ref[...])
        dg_ref[...] += _sum8(dn * xhat)

    row = _rows(TMM, D)
    wide = _rows(TMM, FF)
    return pl.pallas_call(
        body, name=name, grid=(s // TMM,),
        in_specs=[row, row, wide, _full((1, D)), _full((NDEV, D, fj)), _full((FF, D))],
        out_specs=[row, wide, wide, row, _acc_spec()],
        out_shape=[jax.ShapeDtypeStruct((s, D), F32), jax.ShapeDtypeStruct((s, FF), BF16),
                   jax.ShapeDtypeStruct((s, FF), BF16), jax.ShapeDtypeStruct((s, D), BF16),
                   jax.ShapeDtypeStruct((8, D), F32)],
        scratch_shapes=[pltpu.VMEM((TMM, D), BF16), pltpu.VMEM((TMM, D), F32)],
        compiler_params=_cp(("arbitrary",)),
    )(gout, hin, up, g, wu, wd)


def _matmul_tn(a, b, slab, out_dtype, name):
    s, m = a.shape
    n = b.shape[1]
    ts = min(s, 512 if n > 2048 else 1024)
    nk = s // ts
    nc = 512
    w = n // NDEV

    def body(a_ref, b_ref, o_ref, at_ref, acc_ref):
        k = pl.program_id(0)
        at_ref[...] = a_ref[...].astype(BF16).T

        @pl.when(k == 0)
        def _():
            acc_ref[...] = jnp.zeros_like(acc_ref)

        for c in range(n // nc):
            sl = slice(c * nc, (c + 1) * nc)
            acc_ref[:, sl] += jnp.dot(at_ref[...], b_ref[:, sl].astype(BF16), preferred_element_type=F32)

        @pl.when(k == nk - 1)
        def _():
            if slab:
                for d in range(NDEV):
                    o_ref[d] = acc_ref[:, d * w:(d + 1) * w].astype(out_dtype)
            else:
                o_ref[...] = acc_ref[...].astype(out_dtype)

    if slab:
        out_shape = jax.ShapeDtypeStruct((NDEV, m, w), out_dtype)
        out_spec = pl.BlockSpec((NDEV, m, w), lambda k: (0, 0, 0), pipeline_mode=pl.Buffered(1))
    else:
        out_shape = jax.ShapeDtypeStruct((m, n), out_dtype)
        out_spec = pl.BlockSpec((m, n), lambda k: (0, 0), pipeline_mode=pl.Buffered(1))
    return pl.pallas_call(
        body, name=name, grid=(nk,),
        in_specs=[pl.BlockSpec((ts, m), lambda k: (k, 0)), pl.BlockSpec((ts, n), lambda k: (k, 0))],
        out_specs=out_spec, out_shape=out_shape,
        scratch_shapes=[pltpu.VMEM((m, ts), BF16), pltpu.VMEM((m, n), F32)],
        compiler_params=_cp(("arbitrary",)),
    )(a, b)


def _matmul_nt(x, w, name):
    s, n = x.shape
    k = w.shape[0]

    def body(x_ref, w_ref, o_ref):
        o_ref[...] = lax.dot_general(x_ref[...].astype(BF16), w_ref[...], _NT,
                                     preferred_element_type=F32).astype(BF16)

    return pl.pallas_call(
        body, name=name, grid=(s // TM,),
        in_specs=[_rows(TM, n), _full((k, n))], out_specs=_rows(TM, k),
        out_shape=jax.ShapeDtypeStruct((s, k), BF16),
        compiler_params=_cp(("parallel",)),
    )(x, w)


def _attn_bwd(q, kpad, vpad, do, diag):
    s = q.shape[0]
    nqb = s // QB
    npad = PADK // QB

    def body(q_ref, k0, k1, k2, v0, v1, v2, do_ref, diag_ref, dq_ref, dk_ref, dv_ref, dd_ref,
             bias_ref, db_ref, dka_ref, dva_ref, sc_ref, dp_ref, dsb_ref, pb_ref):
        j = pl.program_id(1)

        @pl.when(j == 0)
        def _():
            _bias_from_diag(diag_ref, bias_ref)
            dka_ref[...] = jnp.zeros_like(dka_ref)
            dva_ref[...] = jnp.zeros_like(dva_ref)
            db_ref[...] = jnp.zeros_like(db_ref)

        def block(masked):
            kcat = jnp.concatenate([k0[...], k1[...], k2[...]], axis=0)
            vcat = jnp.concatenate([v0[...], v1[...], v2[...]], axis=0)
            q2 = q_ref[...]
            do2 = do_ref[...]
            masks = _head_masks()
            valid = (lax.broadcasted_iota(jnp.int32, (1, KB), 1) + j * QB >= PADK) if masked else None
            qt = q2.T
            dot_ = do2.T
            dq = []
            for a in range(HPS):
                sc_ref[a] = lax.dot_general(jnp.where(masks[a], q2, jnp.zeros_like(q2)), kcat, _NT,
                                            preferred_element_type=F32)
                dp_ref[a] = lax.dot_general(jnp.where(masks[a], do2, jnp.zeros_like(do2)), vcat, _NT,
                                            preferred_element_type=F32)
            for a in range(HPS):
                for c in range(QB // ATT_RC):
                    r = slice(c * ATT_RC, (c + 1) * ATT_RC)
                    sc = sc_ref[a, r, :] + bias_ref[a, r, :]
                    if masked:
                        sc = jnp.where(valid, sc, NEG)
                    e = jnp.exp(sc - jnp.max(sc, axis=-1, keepdims=True))
                    p = e * (1.0 / jnp.sum(e, axis=-1, keepdims=True))
                    dp = dp_ref[a, r, :]
                    ds = p * (dp - jnp.sum(p * dp, axis=-1, keepdims=True))
                    db_ref[a, r, :] += ds
                    dsb_ref[a, r, :] = ds.astype(BF16)
                    pb_ref[a, r, :] = p.astype(BF16)
                hd = slice(a * HD, (a + 1) * HD)
                dq.append(jnp.dot(dsb_ref[a], kcat, preferred_element_type=F32))
                dka_ref[hd, :] += jnp.dot(qt[hd, :], dsb_ref[a], preferred_element_type=F32)
                dva_ref[hd, :] += jnp.dot(dot_[hd, :], pb_ref[a], preferred_element_type=F32)
            dq_ref[...] = _pick_heads(dq, masks) * (HD ** -0.5)

        pl.when(j < npad)(functools.partial(block, True))
        pl.when(jnp.logical_and(j >= npad, j < nqb))(functools.partial(block, False))

        @pl.when(j == nqb - 1)
        def _():
            for a in range(HPS):
                dd_ref[a] = _diag_sums(db_ref, a)

        dk_ref[...] = dka_ref[:, 0:QB].T
        dv_ref[...] = dva_ref[:, 0:QB].T
        dka_ref[:, 0:KB - QB] = dka_ref[:, QB:KB]
        dva_ref[:, 0:KB - QB] = dva_ref[:, QB:KB]
        dka_ref[:, KB - QB:KB] = jnp.zeros((LW, QB), F32)
        dva_ref[:, KB - QB:KB] = jnp.zeros((LW, QB), F32)

    qspec, kspecs, dspec = _attn_specs(nqb)
    kout = pl.BlockSpec((QB, LW), lambda p, j: (jnp.maximum(j - npad, 0), p))
    sd = jax.ShapeDtypeStruct((s, D), F32)
    return pl.pallas_call(
        body, name="attn_bwd", grid=(D // LW, nqb + npad),
        in_specs=[qspec] + kspecs + kspecs + [qspec, dspec],
        out_specs=[qspec, kout, kout, pl.BlockSpec((None, HPS, 8, _TOEP), lambda p, j: (p, 0, 0, 0))],
        out_shape=[sd, sd, sd, jax.ShapeDtypeStruct((NH // HPS, HPS, 8, _TOEP), F32)],
        scratch_shapes=[pltpu.VMEM((HPS, QB, KB), F32), pltpu.VMEM((HPS, QB, KB), F32),
                        pltpu.VMEM((LW, KB), F32), pltpu.VMEM((LW, KB), F32),
                        pltpu.VMEM((HPS, QB, KB), F32), pltpu.VMEM((HPS, QB, KB), F32),
                        pltpu.VMEM((HPS, QB, KB), BF16), pltpu.VMEM((HPS, QB, KB), BF16)],
        compiler_params=_cp(("parallel", "arbitrary")),
    )(q, kpad, kpad, kpad, vpad, vpad, vpad, do, diag)


def _head_norm_bwd(dy2, x2, g2, lo):
    rr = _head_rstd(x2, lo)
    xhat = x2 * rr
    t = dy2 * g2 * xhat
    m_lo = jnp.sum(jnp.where(lo, t, 0.0), axis=-1, keepdims=True)
    m_hi = jnp.sum(jnp.where(lo, 0.0, t), axis=-1, keepdims=True)
    m = jnp.where(lo, m_lo, m_hi) * (1.0 / HD)
    return rr * (dy2 * g2 - xhat * m), dy2 * xhat


def _kvq_bwd(dq, dk, dv, qraw, kraw, h2, g3, kv_norm, b_norm, w_kv, w_q, k_norm_t, q_norm_t):
    s = h2.shape[0]

    def body(dq_ref, dk_ref, dv_ref, qraw_ref, kraw_ref, h_ref, g3_ref, gkv_ref, gb_ref, wkv_ref, wq_ref,
             kn_ref, qn_ref, g2_ref, dqr_ref, dkv_ref, nb_ref, nk_ref, dgq_ref, dgk_ref, dgb_ref, dgkv_ref):
        @pl.when(pl.program_id(0) == 0)
        def _():
            for r in (dgq_ref, dgk_ref, dgb_ref, dgkv_ref):
                r[...] = jnp.zeros_like(r)

        lo = lax.broadcasted_iota(jnp.int32, (1, 128), 1) < HD
        for p in range(D // 128):
            sl = slice(p * 128, (p + 1) * 128)
            dx, dgp = _head_norm_bwd(dq_ref[:, sl], qraw_ref[:, sl], qn_ref[:, sl], lo)
            dqr_ref[:, sl] = dx.astype(BF16)
            dgq_ref[:, sl] += _sum8(dgp)
            dx, dgp = _head_norm_bwd(dk_ref[:, sl], kraw_ref[:, sl], kn_ref[:, sl], lo)
            dkv_ref[:, sl] = dx.astype(BF16)
            dgk_ref[:, sl] += _sum8(dgp)
        dkv_ref[:, D:] = dv_ref[...].astype(BF16)
        dnb = lax.dot_general(dqr_ref[...], wq_ref[...], _NT, preferred_element_type=F32)
        cw = 2 * D // NDEV
        dnk = lax.dot_general(dkv_ref[:, 0:cw], wkv_ref[0], _NT, preferred_element_type=F32)
        for d in range(1, NDEV):
            dnk = dnk + lax.dot_general(dkv_ref[:, d * cw:(d + 1) * cw], wkv_ref[d], _NT, preferred_element_type=F32)
        h = h_ref[...]
        r = _rstd(h)
        xhat = h * r
        dxg = dnb * gb_ref[...] + dnk * gkv_ref[...]
        g2_ref[...] = g3_ref[...] + r * (dxg - xhat * jnp.mean(dxg * xhat, axis=-1, keepdims=True))
        dgb_ref[...] += _sum8(dnb * xhat)
        dgkv_ref[...] += _sum8(dnk * xhat)
        nb_ref[...] = (xhat * gb_ref[...]).astype(BF16)
        nk_ref[...] = (xhat * gkv_ref[...]).astype(BF16)

    row = _rows(TM, D)
    sd = jax.ShapeDtypeStruct((s, D), BF16)
    acc = jax.ShapeDtypeStruct((8, D), F32)
    return pl.pallas_call(
        body, name="kvq_bwd", grid=(s // TM,),
        in_specs=[row] * 7 + [_full((1, D)), _full((1, D)), _full((NDEV, D, 2 * D // NDEV)), _full((D, D)),
                              _full((1, D)), _full((1, D))],
        out_specs=[row, row, _rows(TM, 2 * D), row, row] + [_acc_spec()] * 4,
        out_shape=[jax.ShapeDtypeStruct((s, D), F32), sd, jax.ShapeDtypeStruct((s, 2 * D), BF16), sd, sd,
                   acc, acc, acc, acc],
        compiler_params=_cp(("arbitrary",)),
    )(dq, dk, dv, qraw, kraw, h2, g3, kv_norm, b_norm, w_kv, w_q, k_norm_t, q_norm_t)


def _lru_bwd(g1, gate, rec, hs, w_out, conv_w, conv_b, wg, bg, lam):
    s = g1.shape[0]
    nt = s // TL

    def body(g1_ref, gate_ref, rec_ref, recp_ref, hs_ref, hsp_ref, wo_ref, cw_ref, cb_ref, wg_ref, bg_ref, lam_ref,
             du_ref, dcw_ref, dcb_ref, dwg_ref, dbg_ref, dlam_ref, ext_ref, dext_ref, cg_ref,
             a_sc, dh_sc, hl_sc, ac_sc, rg_sc, ig_sc, mult_sc, rc_sc):
        i = pl.program_id(0)
        first_tile = i == nt - 1

        @pl.when(i == 0)
        def _():
            dext_ref[TL:TL + 8, :] = jnp.zeros((8, D), F32)
            cg_ref[...] = jnp.zeros_like(cg_ref)
            for r in (dcw_ref, dcb_ref, dwg_ref, dbg_ref, dlam_ref):
                r[...] = jnp.zeros_like(r)

        keep = jnp.where(first_tile, 0.0, 1.0)
        ext_ref[0:8, :] = recp_ref[...] * keep
        ext_ref[8:8 + TL, :] = rec_ref[...]
        dy = lax.dot_general(g1_ref[...].astype(BF16), wo_ref[...], _NT, preferred_element_type=F32)
        lam_v = lam_ref[...]
        sp = _softplus_neg(lam_v)
        dsp_dlam = -_sigmoid(-lam_v)
        rows = lax.broadcasted_iota(jnp.int32, (TL, BW), 0)
        for n in range(NBLK):
            sl = slice(n * BW, (n + 1) * BW)
            rc = _conv(ext_ref, cw_ref, cb_ref, sl, TL)
            rg, ig, a, mult = _lru_gates(rc, wg_ref[n], bg_ref[n:n + 1, :], sp[:, sl])
            h = hs_ref[:, sl]
            gt = gate_ref[:, sl]
            dyn = dy[:, sl]
            du_ref[:, sl] = (dyn * h * _gelu_grad(gt)).astype(BF16)
            dh_sc[n] = dyn * _gelu(gt) + jnp.where(rows == TL - 1, cg_ref[0:1, sl], 0.0)
            a_sc[n], rg_sc[n], ig_sc[n], mult_sc[n], rc_sc[n] = a, rg, ig, mult, rc
            ac_sc[n] = _shift_up(a, 1, 0.0, rows, TL)
        _tile_scan(ac_sc, dh_sc, hl_sc, [jnp.zeros((1, BW), F32)] * NBLK, reverse=True)
        for n in range(NBLK):
            sl = slice(n * BW, (n + 1) * BW)
            gsc = hl_sc[n]
            a, rg, ig, mult, rc = a_sc[n], rg_sc[n], ig_sc[n], mult_sc[n], rc_sc[n]
            cg_ref[0:1, sl] = a[0:1, :] * gsc[0:1, :]
            hprev = _shift_down(hs_ref[:, sl], 1, hsp_ref[7:8, sl] * keep, rows)
            da = gsc * hprev
            d_mult = gsc * ig * rc
            d_ig = gsc * mult * rc
            d_rc = gsc * mult * ig
            d_la = da * a - d_mult * (a * a) / mult
            d_rg = d_la * ((-LRU_C) * sp[:, sl])
            dlam_ref[:, sl] += _sum8(d_la * ((-LRU_C) * rg)) * dsp_dlam[:, sl]
            dg = jnp.concatenate([d_rg * rg * (1.0 - rg), d_ig * ig * (1.0 - ig)], axis=1)
            dgb = dg.astype(BF16)
            d_rc = d_rc + lax.dot_general(dgb, wg_ref[n], _NT, preferred_element_type=F32)
            dwg_ref[n] += lax.dot_general(rc.astype(BF16), dgb, _TN, preferred_element_type=F32)
            dbg_ref[n] += _sum8(dg)
            dext_ref[0:TL, sl] = d_rc
            dcb_ref[:, sl] += _sum8(d_rc)
            for k in range(4):
                dcw_ref[k, :, sl] += _sum8(d_rc * ext_ref[5 + k:5 + k + TL, sl])
        for k in range(4):
            part = cw_ref[3 - k:4 - k, :] * dext_ref[k:k + TL, :]
            acc = part if k == 0 else acc + part
        du_ref[:, D:] = acc.astype(BF16)
        dext_ref[TL:TL + 8, :] = dext_ref[0:8, :]

    rev = pl.BlockSpec((TL, D), lambda i: (nt - 1 - i, 0))
    rev8 = pl.BlockSpec((8, D), lambda i: (jnp.maximum((nt - 1 - i) * (TL // 8) - 1, 0), 0))
    acc = jax.ShapeDtypeStruct((8, D), F32)
    return pl.pallas_call(
        body, name="lru_bwd", grid=(nt,),
        in_specs=[rev, rev, rev, rev8, rev, rev8, _full((D, D)), _full((4, D)), _full((1, D)),
                  _full((NBLK, BW, 2 * BW)), _full((NBLK, 2 * BW)), _full((1, D))],
        out_specs=[pl.BlockSpec((TL, 2 * D), lambda i: (nt - 1 - i, 0)),
                   pl.BlockSpec((4, 8, D), lambda i: (0, 0, 0)), _acc_spec(),
                   pl.BlockSpec((NBLK, BW, 2 * BW), lambda i: (0, 0, 0)),
                   pl.BlockSpec((NBLK, 8, 2 * BW), lambda i: (0, 0, 0)), _acc_spec()],
        out_shape=[jax.ShapeDtypeStruct((s, 2 * D), BF16), jax.ShapeDtypeStruct((4, 8, D), F32), acc,
                   jax.ShapeDtypeStruct((NBLK, BW, 2 * BW), F32), jax.ShapeDtypeStruct((NBLK, 8, 2 * BW), F32), acc],
        scratch_shapes=[pltpu.VMEM((TL + 8, D), F32), pltpu.VMEM((TL + 8, D), F32), pltpu.VMEM((8, D), F32)]
        + [pltpu.VMEM((NBLK, TL, BW), F32)] * 8,
        compiler_params=_cp(("arbitrary",)),
    )(g1, gate, rec, rec, hs, hs, w_out, conv_w, conv_b, wg, bg, lam)


def _a_in_bwd(du, h0, g1, a_norm, w_in):
    s = h0.shape[0]

    def body(du_ref, h_ref, g1_ref, an_ref, win_ref, gx_ref, n1_ref, dg_ref):
        @pl.when(pl.program_id(0) == 0)
        def _():
            dg_ref[...] = jnp.zeros_like(dg_ref)

        cw = 2 * D // NDEV
        dn = lax.dot_general(du_ref[:, 0:cw], win_ref[0], _NT, preferred_element_type=F32)
        for d in range(1, NDEV):
            dn = dn + lax.dot_general(du_ref[:, d * cw:(d + 1) * cw], win_ref[d], _NT, preferred_element_type=F32)
        h = h_ref[...]
        r = _rstd(h)
        xhat = h * r
        gx_ref[...] = g1_ref[...] + _rms_bwd(dn, xhat, r, an_ref[...])
        n1_ref[...] = (xhat * an_ref[...]).astype(BF16)
        dg_ref[...] += _sum8(dn * xhat)

    row = _rows(TM, D)
    return pl.pallas_call(
        body, name="a_in_bwd", grid=(s // TM,),
        in_specs=[_rows(TM, 2 * D), row, row, _full((1, D)), _full((NDEV, D, 2 * D // NDEV))],
        out_specs=[row, row, _acc_spec()],
        out_shape=[jax.ShapeDtypeStruct((s, D), F32), jax.ShapeDtypeStruct((s, D), BF16),
                   jax.ShapeDtypeStruct((8, D), F32)],
        compiler_params=_cp(("arbitrary",)),
    )(du, h0, g1, a_norm, w_in)


def _rel_onehot():
    m = np.arange(_TOEP)
    signed = np.where(m < KB, m, m - _TOEP)
    idx = np.clip(PADK - signed, -(CHUNK - 1), 2 * CHUNK) + (CHUNK - 1)
    return (idx[None, :] == np.arange(NREL)[:, None]).astype(np.float32)


def _bias_diagonals(rel_bias):
    diag = jnp.dot(rel_bias, jnp.asarray(_rel_onehot()), precision=lax.Precision.HIGHEST)
    return diag.reshape(NH // HPS, HPS, _TOEP)


def _rel_bias_grad(dd):
    rows = 8
    z = dd
    oh = np.zeros((_TOEP, 256), np.float32)
    oh[:, :NREL] = _rel_onehot().T

    def body(z_ref, oh_ref, o_ref):
        d = jnp.sum(z_ref[...], axis=0, keepdims=True)
        hi = d.astype(BF16)
        mid = (d - hi.astype(F32)).astype(BF16)
        lo = (d - hi.astype(F32) - mid.astype(F32)).astype(BF16)
        ohb = oh_ref[...].astype(BF16)
        acc = jnp.zeros((8, 256), F32)
        for piece in (lo, mid, hi):
            acc = acc + jnp.dot(jnp.broadcast_to(piece, (8, _TOEP)), ohb, preferred_element_type=F32)
        o_ref[...] = acc

    out = pl.pallas_call(
        body, name="rel_bias_grad", grid=(NH,),
        in_specs=[pl.BlockSpec((None, rows, _TOEP), lambda h: (h, 0, 0)), pl.BlockSpec((_TOEP, 256), lambda h: (0, 0))],
        out_specs=pl.BlockSpec((None, 8, 256), lambda h: (h, 0, 0)),
        out_shape=jax.ShapeDtypeStruct((NH, 8, 256), F32),
        compiler_params=_cp(("parallel",)),
    )(z, jnp.asarray(oh))
    return out[:, 0, :NREL]


def _exchange(arrays, scatter, name):
    n = len(arrays)

    def body(*refs):
        ins, outs = refs[:n], refs[n:2 * n]
        token, (send_sems, recv_sems, local_sems) = refs[2 * n], refs[2 * n + 1:]
        token[...] = jnp.zeros_like(token)
        x, y, c = lax.axis_index("x"), lax.axis_index("y"), lax.axis_index("c")
        me = 4 * x + 2 * y + c

        def peer_of(r):
            rx, ry, rc = (r >> 2) & 1, (r >> 1) & 1, r & 1
            px = 1 - x if rx else x
            py = 1 - y if ry else y
            pc = 1 - c if rc else c
            return (px, py, pc), 4 * px + 2 * py + pc

        local, sent = [], []
        for k in range(n):
            cp = pltpu.make_async_copy(ins[k].at[me] if scatter else ins[k], outs[k].at[me], local_sems.at[k])
            cp.start()
            local.append(cp)
            for r in range(1, NDEV):
                peer, peer_lin = peer_of(r)
                cp = pltpu.make_async_remote_copy(
                    src_ref=ins[k].at[peer_lin] if scatter else ins[k], dst_ref=outs[k].at[me],
                    send_sem=send_sems.at[k, r - 1], recv_sem=recv_sems.at[k, r - 1],
                    device_id=peer, device_id_type=pl.DeviceIdType.MESH)
                cp.start()
                sent.append(cp)
        for k in range(n):
            for r in range(1, NDEV):
                peer, peer_lin = peer_of(r)
                pltpu.make_async_remote_copy(
                    src_ref=ins[k].at[peer_lin] if scatter else ins[k], dst_ref=outs[k].at[peer_lin],
                    send_sem=send_sems.at[k, r - 1], recv_sem=recv_sems.at[k, r - 1],
                    device_id=peer, device_id_type=pl.DeviceIdType.MESH).wait_recv()
        for cp in sent:
            cp.wait_send()
        for cp in local:
            cp.wait()

    def slot_shape(a):
        return (NDEV,) + (a.shape[1:] if scatter else a.shape)

    anyspec = pl.BlockSpec(memory_space=pl.ANY)
    outs = pl.pallas_call(
        body, name=name,
        in_specs=[anyspec] * n, out_specs=[anyspec] * n + [pl.BlockSpec(memory_space=pltpu.VMEM)],
        out_shape=[jax.ShapeDtypeStruct(slot_shape(a), a.dtype) for a in arrays]
        + [jax.ShapeDtypeStruct((8, 128), F32)],
        scratch_shapes=[pltpu.SemaphoreType.DMA((n, NDEV - 1)), pltpu.SemaphoreType.DMA((n, NDEV - 1)),
                        pltpu.SemaphoreType.DMA((n,))],
        compiler_params=pltpu.CompilerParams(has_side_effects=True),
    )(*arrays)
    return outs[:n], outs[n]


def _peer(r):
    x, y, c = lax.axis_index("x"), lax.axis_index("y"), lax.axis_index("c")
    px = 1 - x if (r >> 2) & 1 else x
    py = 1 - y if (r >> 1) & 1 else y
    pc = 1 - c if r & 1 else c
    return (px, py, pc), 4 * px + 2 * py + pc


def _my_index():
    return 4 * lax.axis_index("x") + 2 * lax.axis_index("y") + lax.axis_index("c")


_HBM_SPEC = pl.BlockSpec(memory_space=pltpu.HBM)
_SEM_SPEC = pl.BlockSpec(memory_space=pltpu.SEMAPHORE)


_NPEER = NDEV - 1


def _exchange_start(arrays, scatter, name):
    n = len(arrays)
    ns = n * _NPEER
    slots = [(NDEV,) + (a.shape[1:] if scatter else a.shape) for a in arrays]

    def body(*refs):
        srcs, lands = refs[:n], refs[n:2 * n]
        send_sems, recv_sems = refs[2 * n:2 * n + ns], refs[2 * n + ns:2 * n + 2 * ns]
        token = refs[-1]
        me = _my_index()
        for k in range(n):
            for r in range(1, NDEV):
                peer, peer_lin = _peer(r)
                pltpu.make_async_remote_copy(
                    src_ref=srcs[k].at[peer_lin] if scatter else srcs[k], dst_ref=lands[k].at[me],
                    send_sem=send_sems[k * _NPEER + r - 1], recv_sem=recv_sems[k * _NPEER + r - 1],
                    device_id=peer, device_id_type=pl.DeviceIdType.MESH).start()
        token[...] = jnp.zeros_like(token)

    sem = pltpu.SemaphoreType.DMA(())
    outs = pl.pallas_call(
        body, name=name,
        out_shape=(*[sem] * (2 * ns), *[pltpu.HBM(a.shape, a.dtype) for a in arrays],
                   *[pltpu.HBM(s, a.dtype) for s, a in zip(slots, arrays)], jax.ShapeDtypeStruct((8, 128), F32)),
        in_specs=[_HBM_SPEC] * (2 * n),
        out_specs=(*[_SEM_SPEC] * (2 * ns), *[_HBM_SPEC] * (2 * n), pl.BlockSpec(memory_space=pltpu.VMEM)),
        input_output_aliases={k: 2 * ns + k for k in range(2 * n)},
        compiler_params=pltpu.CompilerParams(has_side_effects=pltpu.SideEffectType.DATAFLOW_SIDE_EFFECTING),
    )(*[pltpu.with_memory_space_constraint(a, pltpu.HBM) for a in arrays],
      *[pltpu.with_memory_space_constraint(lax.empty(s, a.dtype), pltpu.HBM) for s, a in zip(slots, arrays)])
    return outs[:ns], outs[ns:2 * ns], outs[2 * ns:2 * ns + n], outs[2 * ns + n:2 * ns + 2 * n], outs[-1]


def _exchange_wait(started, after, scatter, name):
    send_sems, recv_sems, srcs, lands, _ = started
    n = len(srcs)
    ns = n * _NPEER

    def body(*refs):
        src_refs, land_refs = refs[:n], refs[n:2 * n]
        ssem, rsem = refs[2 * n:2 * n + ns], refs[2 * n + ns:2 * n + 2 * ns]
        for k in range(n):
            for r in range(1, NDEV):
                peer, peer_lin = _peer(r)
                cp = pltpu.make_async_remote_copy(
                    src_ref=src_refs[k].at[peer_lin] if scatter else src_refs[k], dst_ref=land_refs[k].at[peer_lin],
                    send_sem=ssem[k * _NPEER + r - 1], recv_sem=rsem[k * _NPEER + r - 1],
                    device_id=peer, device_id_type=pl.DeviceIdType.MESH)
                cp.wait_send()
                cp.wait_recv()

    outs = pl.pallas_call(
        body, name=name,
        out_shape=tuple(pltpu.HBM(a.shape, a.dtype) for a in list(srcs) + list(lands)),
        in_specs=[_HBM_SPEC] * (2 * n) + [_SEM_SPEC] * (2 * ns) + [pl.BlockSpec(memory_space=pl.ANY)],
        out_specs=tuple([_HBM_SPEC] * (2 * n)),
        input_output_aliases={k: k for k in range(2 * n)},
        compiler_params=pltpu.CompilerParams(has_side_effects=pltpu.SideEffectType.DATAFLOW_SIDE_EFFECTING),
    )(*srcs, *lands, *send_sems, *recv_sems, after)
    return list(outs[:n]), list(outs[n:])


def _fill_own(lands, owns, me):
    return [lax.dynamic_update_slice(z, o, (me,) + (0,) * (z.ndim - 1)) for z, o in zip(lands, owns)]


def _sum_slots(st, name):
    _, r, c = st.shape

    def body(s_ref, o_ref):
        acc = s_ref[0]
        for d in range(1, NDEV):
            acc = acc + s_ref[d]
        o_ref[...] = acc

    return pl.pallas_call(
        body, name=name, out_shape=jax.ShapeDtypeStruct((r, c), F32),
        in_specs=[pl.BlockSpec((NDEV, r, c), lambda: (0, 0, 0))], out_specs=pl.BlockSpec((r, c), lambda: (0, 0)),
    )(st)


def _adamw(w, m, v, gst, name, transposed=False):
    r, c = w.shape
    ns = gst.shape[0]
    tr = min(r, 256)
    c1 = 1.0 - ADAM_B1 ** ADAM_STEP
    c2 = 1.0 - ADAM_B2 ** ADAM_STEP

    def body(w_ref, m_ref, v_ref, g_ref, go_ref, d_ref, mo_ref, vo_ref):
        g = g_ref[0].astype(F32)
        for d in range(1, ns):
            g = g + g_ref[d].astype(F32)
        if transposed:
            g = g.T
        m2 = ADAM_B1 * m_ref[...] + (1.0 - ADAM_B1) * g
        v2 = ADAM_B2 * v_ref[...] + (1.0 - ADAM_B2) * (g * g)
        go_ref[...] = g
        mo_ref[...] = m2
        vo_ref[...] = v2
        d_ref[...] = (-ADAM_LR) * ((m2 / c1) / (jnp.sqrt(v2 / c2) + ADAM_EPS) + ADAM_WD * w_ref[...])

    blk = pl.BlockSpec((tr, c), lambda i: (i, 0))
    sd = jax.ShapeDtypeStruct((r, c), F32)
    return pl.pallas_call(
        body, name=name, grid=(r // tr,),
        in_specs=[blk, blk, blk, pl.BlockSpec((ns, c, tr), lambda i: (0, 0, i)) if transposed
                  else pl.BlockSpec((ns, tr, c), lambda i: (0, i, 0))],
        out_specs=[blk, blk, blk, blk], out_shape=[sd, sd, sd, sd],
        compiler_params=_cp(("parallel",)),
    )(w, m, v, gst)


def _pack(pieces, rows):
    flat = jnp.concatenate([p.reshape(-1).astype(F32) for p in pieces])
    return jnp.pad(flat, (0, rows * 128 - flat.shape[0])).reshape(rows, 128)


def _unpack(flat, shapes):
    out, off = [], 0
    for shp in shapes:
        size = int(np.prod(shp))
        out.append(flat[off:off + size].reshape(shp))
        off += size
    return out


def _cols(full, me, width):
    return lax.dynamic_slice_in_dim(full, me * width, width, axis=full.ndim - 1)


def kernel(x, a_norm, a_w_in, a_conv_w, a_conv_b, a_w_gate, a_b_gate, a_lambda, a_w_out, kv_norm, w_kv, k_norm, b_norm, b_w_q, b_q_norm, b_rel_bias, b_w_o, mlp_norm, w_up, w_down, loss_target, m_a_norm, m_a_w_in, m_a_conv_w, m_a_conv_b, m_a_w_gate, m_a_b_gate, m_a_lambda, m_a_w_out, m_kv_norm, m_w_kv, m_k_norm, m_b_norm, m_b_w_q, m_b_q_norm, m_b_rel_bias, m_b_w_o, m_mlp_norm, m_w_up, m_w_down, v_a_norm, v_a_w_in, v_a_conv_w, v_a_conv_b, v_a_w_gate, v_a_b_gate, v_a_lambda, v_a_w_out, v_kv_norm, v_w_kv, v_k_norm, v_b_norm, v_b_w_q, v_b_q_norm, v_b_rel_bias, v_b_w_o, v_mlp_norm, v_w_up, v_w_down):
    me = 4 * lax.axis_index("x") + 2 * lax.axis_index("y") + lax.axis_index("c")
    sh = D // NDEV

    big_w = [a_w_in[0], a_w_out[0], w_kv, b_w_q[0], b_w_o[0], w_up[0], w_up[1], w_down[0], w_down[1]]
    small_sharded = [a_norm, a_conv_w, a_conv_b, a_b_gate, a_lambda, a_w_gate]
    small_rows = 272
    def to_bf16(w, token):
        return (w + token[0, 0]).astype(BF16)

    got, tok_a = _exchange([a_w_in[0].astype(BF16), _pack(small_sharded, small_rows)], False, "gather_a")
    own_b1 = [to_bf16(w, tok_a) for w in (a_w_out[0], w_up[0], w_down[0])]
    st_b1 = _exchange_start(own_b1, False, "gather_b1_start")
    own_b2 = [to_bf16(w, st_b1[4]) for w in (w_kv, b_w_q[0], b_w_o[0], w_up[1], w_down[1])]
    st_b2 = _exchange_start(own_b2, False, "gather_b2_start")
    w_in = got[0]
    sm = got[1].reshape(NDEV, small_rows * 128)
    an_f = sm[:, 0:128].reshape(1, D) + st_b2[4][0:1, 0:1]
    cw_f = sm[:, 128:640].reshape(NDEV, 4, sh).transpose(1, 0, 2).reshape(4, D)
    cb_f = sm[:, 640:768].reshape(1, D)
    bg_f = sm[:, 768:1024].reshape(NDEV, NBLK, 2 * BW // NDEV).transpose(1, 0, 2).reshape(NBLK, 2 * BW)
    lam_f = sm[:, 1024:1152].reshape(1, D)
    wg_f = sm[:, 1152:1152 + NBLK * BW * 32].reshape(NDEV, NBLK, BW, 32).transpose(1, 2, 0, 3)
    wg_f = wg_f.reshape(NBLK, BW, 2 * BW).astype(BF16)
    kn_t = jnp.tile(k_norm, NH).reshape(1, D)
    qn_t = jnp.tile(b_q_norm[0], NH).reshape(1, D)
    kvn = kv_norm.reshape(1, D)
    diag = _bias_diagonals(b_rel_bias[0])

    h0 = x[0]
    gate, rec, hs, y = _lru_fwd(h0, an_f, w_in, cw_f, cb_f, wg_f, bg_f, lam_f)
    own, land = _exchange_wait(st_b1, y, False, "gather_b1_wait")
    land = _fill_own(land, [o[None] for o in own], me)
    w_out = land[0].reshape(D, D)
    wu = [land[1], None]
    wd = [land[2].reshape(FF, D), None]
    h1, h2, up0 = _mlp_fwd(h0, y, w_out, mlp_norm[0:1], wu[0], wd[0], "mlp_fwd0")
    own, land = _exchange_wait(st_b2, h2, False, "gather_b2_wait")
    land = _fill_own(land, [o[None] for o in own], me)
    wkv = land[0]
    w_q = land[1].reshape(D, D)
    w_o = land[2].reshape(D, D)
    wu[1] = land[3]
    wd[1] = land[4].reshape(FF, D)
    kraw, qraw, kpad, vpad, q = _kvq_fwd(h2, kvn, b_norm, wkv, w_q, kn_t, qn_t)
    o = _attn_fwd(q, kpad, vpad, diag)
    h3, h4, up1 = _mlp_fwd(h2, o, w_o, mlp_norm[1:2], wu[1], wd[1], "mlp_fwd1")
    g4, lpart = _loss_grad(h4, loss_target[0])
    loss = lax.psum(jnp.sum(lpart), ("x", "y", "c"))

    g3, dup1, act1, n3, dgm1 = _mlp_bwd(g4, h3, up1, mlp_norm[1:2], wu[1], wd[1], "mlp_bwd1")
    d_wd1 = _matmul_tn(g4, act1, True, BF16, "dw_down1")
    d_wu1 = _matmul_tn(n3, dup1, True, BF16, "dw_up1")
    do = _matmul_nt(g3, w_o, "do_proj")
    d_wo = _matmul_tn(o, g3, False, BF16, "dw_o").reshape(NDEV, sh, D)
    dq, dk, dv, dd = _attn_bwd(q, kpad, vpad, do, diag)
    g2, dqr, dkv, nb, nk, dgq, dgk, dgb, dgkv = _kvq_bwd(dq, dk, dv, qraw, kraw, h2, g3, kvn, b_norm, wkv, w_q,
                                                       kn_t, qn_t)
    d_wq = _matmul_tn(nb, dqr, False, BF16, "dw_q").reshape(NDEV, sh, D)
    d_wkv = _matmul_tn(nk, dkv, True, BF16, "dw_kv")
    st_r1 = _exchange_start([d_wd1, d_wu1, d_wo, d_wq, d_wkv], True, "scatter_r1_start")
    g1, dup0, act0, n2, dgm0 = _mlp_bwd(g2, h1, up0, mlp_norm[0:1] + st_r1[4][0:1, 0:1], wu[0], wd[0], "mlp_bwd0")
    d_wd0 = _matmul_tn(g2, act0, True, BF16, "dw_down0")
    d_wu0 = _matmul_tn(n2, dup0, True, BF16, "dw_up0")
    d_wout = _matmul_tn(y, g1, False, BF16, "dw_out").reshape(NDEV, sh, D)
    st_r2 = _exchange_start([d_wu0, d_wd0, d_wout], True, "scatter_r2_start")
    du, dcw, dcb, dwg, dbg, dlam = _lru_bwd(g1, gate, rec, hs, w_out, cw_f, cb_f, wg_f, bg_f,
                                            lam_f + st_r2[4][0:1, 0:1])
    gx, n1, dga = _a_in_bwd(du, h0, g1, an_f, w_in)
    d_win = _matmul_tn(n1, du, True, BF16, "dw_in")
    d_rel = _rel_bias_grad(dd.reshape(NH, 8, _TOEP))

    dwg_slab = dwg.reshape(NBLK, BW, NDEV, 32).transpose(2, 0, 1, 3).reshape(NDEV, 256, 128).astype(BF16)
    st_r3 = _exchange_start([d_win, dwg_slab], True, "scatter_r3_start")
    small_full = [dga.sum(0) + st_r3[4][0, 0], dcw.sum(1), dcb.sum(0), dbg.sum(1), dlam.sum(0), dgkv.sum(0),
                  dgk.sum(0).reshape(NH, HD).sum(0), dgb.sum(0), dgq.sum(0).reshape(NH, HD).sum(0), d_rel,
                  jnp.stack([dgm0.sum(0), dgm1.sum(0)])]
    small_g_rows = 136
    (gsm,), _ = _exchange([_pack(small_full, small_g_rows)], False, "gather_small_grads")
    src, recv1 = _exchange_wait(st_r1, gsm, True, "scatter_r1_wait")
    recv1 = _fill_own(recv1, [lax.dynamic_slice_in_dim(a, me, 1, 0) for a in src], me)
    src, recv2 = _exchange_wait(st_r2, recv1[0], True, "scatter_r2_wait")
    recv2 = _fill_own(recv2, [lax.dynamic_slice_in_dim(a, me, 1, 0) for a in src], me)
    gs = _unpack(_sum_slots(gsm, "sum_small_grads").reshape(-1),
                 [(1, D), (4, D), (1, D), (NBLK, 2 * BW), (1, D), (D,), (HD,), (1, D), (1, HD), (1, NH, NREL), (2, D)])
    g_small = [_cols(gs[0], me, sh), _cols(gs[1], me, sh)[None], _cols(gs[2], me, sh),
               _cols(gs[3], me, 2 * BW // NDEV)[None], _cols(gs[4], me, sh)] + gs[5:]

    names = ["a_w_in", "a_w_out", "w_kv", "b_w_q", "b_w_o", "w_up0", "w_up1", "w_down0", "w_down1", "a_w_gate"]
    big_m = [m_a_w_in[0], m_a_w_out[0], m_w_kv, m_b_w_q[0], m_b_w_o[0], m_w_up[0], m_w_up[1], m_w_down[0],
             m_w_down[1], m_a_w_gate.reshape(256, 128)]
    big_v = [v_a_w_in[0], v_a_w_out[0], v_w_kv, v_b_w_q[0], v_b_w_o[0], v_w_up[0], v_w_up[1], v_w_down[0],
             v_w_down[1], v_a_w_gate.reshape(256, 128)]
    big_w = big_w + [a_w_gate.reshape(256, 128)]

    def update(k, g):
        return _adamw(big_w[k], big_m[k], big_v[k], g, "adamw_" + names[k], transposed=names[k].startswith("w_down"))

    early = {1: recv2[2], 2: recv1[4], 3: recv1[3], 4: recv1[2], 5: recv2[0], 6: recv1[1], 7: recv2[1], 8: recv1[0]}
    res = {k: update(k, g) for k, g in early.items()}
    src, recv3 = _exchange_wait(st_r3, res[8][1], True, "scatter_r3_wait")
    recv3 = _fill_own(recv3, [lax.dynamic_slice_in_dim(a, me, 1, 0) for a in src], me)
    res[0] = update(0, recv3[0])
    res[9] = update(9, recv3[1])
    res = [res[k] for k in range(len(names))]
    small_w = [a_norm, a_conv_w, a_conv_b, a_b_gate, a_lambda, kv_norm, k_norm, b_norm, b_q_norm, b_rel_bias, mlp_norm]
    small_m = [m_a_norm, m_a_conv_w, m_a_conv_b, m_a_b_gate, m_a_lambda, m_kv_norm, m_k_norm, m_b_norm, m_b_q_norm,
               m_b_rel_bias, m_mlp_norm]
    small_v = [v_a_norm, v_a_conv_w, v_a_conv_b, v_a_b_gate, v_a_lambda, v_kv_norm, v_k_norm, v_b_norm, v_b_q_norm,
               v_b_rel_bias, v_mlp_norm]
    pr = 72
    res_small = _adamw(_pack(small_w, pr), _pack(small_m, pr), _pack(small_v, pr), _pack(g_small, pr)[None],
                       "adamw_small")
    small_shapes = [w.shape for w in small_w]
    res_small = [_unpack(r.reshape(-1), small_shapes) for r in res_small]

    def assemble(t):
        b = [r[t] for r in res]
        s_ = res_small[t]
        return [s_[0], b[0][None], s_[1], s_[2], b[9].reshape(a_w_gate.shape), s_[3], s_[4], b[1][None],
                s_[5], b[2], s_[6], s_[7], b[3][None], s_[8], s_[9], b[4][None], s_[10],
                jnp.stack([b[5], b[6]]), jnp.stack([b[7], b[8]])]

    return tuple([loss, gx[None]] + assemble(0) + assemble(1) + assemble(2) + assemble(3))
```

```python
import functools

import numpy as np
import jax
import jax.numpy as jnp
from jax import lax
from jax.experimental import pallas as pl
from jax.experimental.pallas import tpu as pltpu

F32 = jnp.float32
BF16 = jnp.bfloat16

D = 1024
NH = 16
HD = 64
FF = 4096
NBLK = 8
BW = 128
CHUNK = 64
PADK = 512
NREL = 192
EPS = 1e-6
LRU_C = 8.0
NDEV = 8

V7X_VMEM_LIMIT = 56 * 1024 * 1024
TM = 512
TMM = 256
TMF = 512
TL = 256
QB = 256
ATT_RC = 32
HPS = 8
LW = HPS * HD
KB = QB + PADK
NEG = -1e30

ADAM_LR, ADAM_B1, ADAM_B2, ADAM_EPS, ADAM_WD, ADAM_STEP = 0.001, 0.9, 0.999, 1e-08, 0.01, 10

_NT = (((1,), (1,)), ((), ()))
_TN = (((0,), (0,)), ((), ()))


def _cp(sem=None):
    return pltpu.CompilerParams(dimension_semantics=sem, vmem_limit_bytes=V7X_VMEM_LIMIT)


def _full(shape):
    n = len(shape)
    return pl.BlockSpec(shape, lambda *a: (0,) * n, pipeline_mode=pl.Buffered(1))


def _rows(tm, width):
    return pl.BlockSpec((tm, width), lambda i: (i, 0))


def _rstd(h):
    return lax.rsqrt(jnp.mean(h * h, axis=-1, keepdims=True) + EPS)


def _sigmoid(x):
    return 1.0 / (1.0 + jnp.exp(-x))


def _expm1(x):
    small = x * (1.0 + x * (0.5 + x * (1.0 / 6.0 + x * (1.0 / 24.0))))
    return jnp.where(jnp.abs(x) < 0.03, small, jnp.exp(x) - 1.0)


def _softplus_neg(lam):
    e = jnp.exp(-jnp.abs(lam))
    series = e * (1.0 - e * (0.5 - e * (1.0 / 3.0 - e * 0.25)))
    return jnp.maximum(-lam, 0.0) + jnp.where(e < 0.01, series, jnp.log(1.0 + e))


_GELU_K = 0.7978845608028654


def _gelu(x):
    return 0.5 * x * (1.0 + jnp.tanh(_GELU_K * (x + 0.044715 * x * x * x)))


def _gelu_grad(x):
    t = jnp.tanh(_GELU_K * (x + 0.044715 * x * x * x))
    return 0.5 * (1.0 + t) + 0.5 * x * (1.0 - t * t) * _GELU_K * (1.0 + 3.0 * 0.044715 * x * x)


def _sum8(x):
    r, c = x.shape
    return jnp.sum(x.reshape(r // 8, 8, c), axis=0)


def _shift_down(x, s, fill, rows):
    return jnp.where(rows >= s, pltpu.roll(x, s, axis=0), fill)


def _shift_up(x, s, fill, rows, n):
    return jnp.where(rows < n - s, pltpu.roll(x, n - s, axis=0), fill)


def _lru_gates(rc, wg_n, bg_n, sp_n):
    g = jnp.dot(rc.astype(BF16), wg_n, preferred_element_type=F32) + bg_n
    rg = _sigmoid(g[:, :BW])
    ig = _sigmoid(g[:, BW:])
    la = (-LRU_C) * rg * sp_n
    a = jnp.exp(la)
    mult = jnp.sqrt(-_expm1(2.0 * la))
    return rg, ig, a, mult


def _conv(ext_ref, cw_ref, cb_ref, sl, n):
    out = cb_ref[:, sl] + cw_ref[0:1, sl] * ext_ref[5:5 + n, sl]
    for k in range(1, 4):
        out = out + cw_ref[k:k + 1, sl] * ext_ref[5 + k:5 + k + n, sl]
    return out


def _tile_scan(a_ref, b_ref, h_ref, carry, reverse):
    sub = lax.broadcasted_iota(jnp.int32, (TL, BW), 0) % 8
    for n in range(NBLK):
        a, b = a_ref[n], b_ref[n]
        for s in (1, 2, 4):
            if reverse:
                inside = sub < 8 - s
                a_sh = jnp.where(inside, pltpu.roll(a, TL - s, axis=0), 1.0)
                b_sh = jnp.where(inside, pltpu.roll(b, TL - s, axis=0), 0.0)
            else:
                inside = sub >= s
                a_sh = jnp.where(inside, pltpu.roll(a, s, axis=0), 1.0)
                b_sh = jnp.where(inside, pltpu.roll(b, s, axis=0), 0.0)
            b = a * b_sh + b
            a = a * a_sh
        a_ref[n], b_ref[n] = a, b
    carry = list(carry)
    groups = range(TL // 8 - 1, -1, -1) if reverse else range(TL // 8)
    for g in groups:
        r = slice(8 * g, 8 * g + 8)
        for n in range(NBLK):
            h = a_ref[n, r, :] * carry[n] + b_ref[n, r, :]
            h_ref[n, r, :] = h
            carry[n] = h[0:1, :] if reverse else h[7:8, :]
    return carry


def _lru_fwd(h0, a_norm, w_in, conv_w, conv_b, wg, bg, lam):
    s = h0.shape[0]

    def body(h0_ref, an_ref, win_ref, cw_ref, cb_ref, wg_ref, bg_ref, lam_ref,
             gate_ref, rec_ref, hs_ref, y_ref, ext_ref, hc_ref, a_sc, b_sc, hl_sc):
        i = pl.program_id(0)

        @pl.when(i == 0)
        def _():
            ext_ref[0:8, :] = jnp.zeros((8, D), F32)
            hc_ref[...] = jnp.zeros_like(hc_ref)

        h = h0_ref[...]
        n1 = (h * _rstd(h) * an_ref[...]).astype(BF16)
        cw = 2 * D // NDEV
        for d in range(NDEV):
            ud = jnp.dot(n1, win_ref[d], preferred_element_type=F32)
            if d < NDEV // 2:
                gate_ref[:, d * cw:(d + 1) * cw] = ud
            else:
                rec_ref[:, d * cw - D:(d + 1) * cw - D] = ud
                ext_ref[8:8 + TL, d * cw - D:(d + 1) * cw - D] = ud
        sp = _softplus_neg(lam_ref[...])
        for n in range(NBLK):
            sl = slice(n * BW, (n + 1) * BW)
            rc = _conv(ext_ref, cw_ref, cb_ref, sl, TL)
            rg, ig, a, mult = _lru_gates(rc, wg_ref[n], bg_ref[n:n + 1, :], sp[:, sl])
            a_sc[n] = a
            b_sc[n] = mult * (ig * rc)
        carry = _tile_scan(a_sc, b_sc, hl_sc, [hc_ref[0:1, n * BW:(n + 1) * BW] for n in range(NBLK)], reverse=False)
        for n in range(NBLK):
            sl = slice(n * BW, (n + 1) * BW)
            hh = hl_sc[n]
            hc_ref[0:1, sl] = carry[n]
            hs_ref[:, sl] = hh
            y_ref[:, sl] = (_gelu(gate_ref[:, sl]) * hh).astype(BF16)
        ext_ref[0:8, :] = ext_ref[TL:TL + 8, :]

    row = _rows(TL, D)
    return pl.pallas_call(
        body, name="lru_fwd", grid=(s // TL,),
        in_specs=[row, _full((1, D)), _full((NDEV, D, 2 * D // NDEV)), _full((4, D)), _full((1, D)),
                  _full((NBLK, BW, 2 * BW)), _full((NBLK, 2 * BW)), _full((1, D))],
        out_specs=[row, row, row, row],
        out_shape=[jax.ShapeDtypeStruct((s, D), F32), jax.ShapeDtypeStruct((s, D), F32),
                   jax.ShapeDtypeStruct((s, D), F32), jax.ShapeDtypeStruct((s, D), BF16)],
        scratch_shapes=[pltpu.VMEM((TL + 8, D), F32), pltpu.VMEM((8, D), F32)]
        + [pltpu.VMEM((NBLK, TL, BW), F32)] * 3,
        compiler_params=_cp(("arbitrary",)),
    )(h0, a_norm, w_in, conv_w, conv_b, wg, bg, lam)


def _mlp_fwd(res, px, pw, g, wu, wd, name):
    s = res.shape[0]
    fj = 512

    def body(res_ref, px_ref, pw_ref, g_ref, wu_ref, wd_ref, hin_ref, hout_ref, up_ref, n_ref):
        hin = res_ref[...] + jnp.dot(px_ref[...], pw_ref[...], preferred_element_type=F32)
        hin_ref[...] = hin
        hout_ref[...] = hin
        n_ref[...] = (hin * _rstd(hin) * g_ref[...]).astype(BF16)
        for j in range(FF // fj):
            sl = slice(j * fj, (j + 1) * fj)
            up = jnp.dot(n_ref[...], wu_ref[j], preferred_element_type=F32)
            up_ref[:, sl] = up
            rl = jnp.maximum(up, 0.0)
            hout_ref[...] += jnp.dot((rl * rl).astype(BF16), wd_ref[sl, :], preferred_element_type=F32)

    row = _rows(TMF, D)
    return pl.pallas_call(
        body, name=name, grid=(s // TMF,),
        in_specs=[row, row, _full((D, D)), _full((1, D)), _full((NDEV, D, fj)), _full((FF, D))],
        out_specs=[row, row, _rows(TMF, FF)],
        out_shape=[jax.ShapeDtypeStruct((s, D), F32), jax.ShapeDtypeStruct((s, D), F32),
                   jax.ShapeDtypeStruct((s, FF), F32)],
        scratch_shapes=[pltpu.VMEM((TMF, D), BF16)],
        compiler_params=_cp(("parallel",)),
    )(res, px, pw, g, wu, wd)


def _head_rstd(x2, lo):
    sq = x2 * x2
    s_lo = jnp.sum(jnp.where(lo, sq, 0.0), axis=-1, keepdims=True)
    s_hi = jnp.sum(jnp.where(lo, 0.0, sq), axis=-1, keepdims=True)
    return lax.rsqrt(jnp.where(lo, s_lo, s_hi) * (1.0 / HD) + EPS)


def _kvq_fwd(h2, kv_norm, b_norm, w_kv, w_q, k_norm_t, q_norm_t):
    s = h2.shape[0]
    assert PADK == TM

    def body(h_ref, gkv_ref, gb_ref, wkv_ref, wq_ref, kn_ref, qn_ref,
             kraw_ref, qraw_ref, k_ref, v_ref, q_ref):
        i = pl.program_id(0)

        @pl.when(i == 0)
        def _():
            k_ref[...] = jnp.zeros_like(k_ref)
            v_ref[...] = jnp.zeros_like(v_ref)

        @pl.when(i > 0)
        def _():
            h = h_ref[...]
            xhat = h * _rstd(h)
            nk = (xhat * gkv_ref[...]).astype(BF16)
            qr = jnp.dot((xhat * gb_ref[...]).astype(BF16), wq_ref[...], preferred_element_type=F32)
            qraw_ref[...] = qr
            lo = lax.broadcasted_iota(jnp.int32, (1, 128), 1) < HD
            cw = 2 * D // NDEV
            for d in range(NDEV):
                kvd = jnp.dot(nk, wkv_ref[d], preferred_element_type=F32)
                if d < NDEV // 2:
                    kraw_ref[:, d * cw:(d + 1) * cw] = kvd
                    for p in range(cw // 128):
                        sl = slice(d * cw + p * 128, d * cw + (p + 1) * 128)
                        k2 = kvd[:, p * 128:(p + 1) * 128]
                        k_ref[:, sl] = (k2 * _head_rstd(k2, lo) * kn_ref[:, sl]).astype(BF16)
                else:
                    v_ref[:, d * cw - D:(d + 1) * cw - D] = kvd.astype(BF16)
            for p in range(D // 128):
                sl = slice(p * 128, (p + 1) * 128)
                q2 = qr[:, sl]
                q_ref[:, sl] = (q2 * _head_rstd(q2, lo) * qn_ref[:, sl] * (HD ** -0.5)).astype(BF16)

    prev = pl.BlockSpec((TM, D), lambda i: (jnp.maximum(i - 1, 0), 0))
    cur = pl.BlockSpec((TM, D), lambda i: (i, 0))
    return pl.pallas_call(
        body, name="kvq_fwd", grid=(s // TM + 1,),
        in_specs=[prev, _full((1, D)), _full((1, D)), _full((NDEV, D, 2 * D // NDEV)), _full((D, D)), _full((1, D)),
                  _full((1, D))],
        out_specs=[prev, prev, cur, cur, prev],
        out_shape=[jax.ShapeDtypeStruct((s, D), F32), jax.ShapeDtypeStruct((s, D), F32),
                   jax.ShapeDtypeStruct((s + PADK, D), BF16), jax.ShapeDtypeStruct((s + PADK, D), BF16),
                   jax.ShapeDtypeStruct((s, D), BF16)],
        compiler_params=_cp(("arbitrary",)),
    )(h2, kv_norm, b_norm, w_kv, w_q, k_norm_t, q_norm_t)


_TOEP = QB + KB


def _bias_from_diag(diag_ref, bias_ref):
    row8 = lax.broadcasted_iota(jnp.int32, (8, _TOEP), 0)
    kchunk = lax.broadcasted_iota(jnp.int32, (8, KB), 1) // CHUNK
    for a in range(HPS):
        v = jnp.broadcast_to(diag_ref[a:a + 1, :], (8, _TOEP))
        z0 = v
        for b in range(1, 8):
            z0 = jnp.where(row8 == b, pltpu.roll(v, b, axis=1), z0)
        for t in range(QB // 8):
            slab = z0 if t == 0 else pltpu.roll(z0, 8 * t, axis=1)
            qchunk = (8 * t) // CHUNK
            band = jnp.logical_and(kchunk >= qchunk, kchunk <= qchunk + PADK // CHUNK)
            bias_ref[a, 8 * t:8 * t + 8, :] = jnp.where(band, slab[:, :KB], NEG)


def _diag_sums(db_ref, a):
    row8 = lax.broadcasted_iota(jnp.int32, (8, _TOEP), 0)
    z = jnp.zeros((8, _TOEP), F32)
    for t in range(QB // 8):
        slab = jnp.concatenate([db_ref[a, 8 * t:8 * t + 8, :], jnp.zeros((8, _TOEP - KB), F32)], axis=1)
        z = z + (slab if t == 0 else pltpu.roll(slab, _TOEP - 8 * t, axis=1))
    e = z
    for b in range(1, 8):
        e = jnp.where(row8 == b, pltpu.roll(z, _TOEP - b, axis=1), e)
    return e


def _attn_specs(nqb):
    qspec = pl.BlockSpec((QB, LW), lambda p, j: (jnp.minimum(j, nqb - 1), p))
    kspecs = [pl.BlockSpec((QB, LW), functools.partial(lambda p, j, t: (jnp.minimum(j, nqb - 1) + t, p), t=t))
              for t in range(KB // QB)]
    dspec = pl.BlockSpec((None, HPS, _TOEP), lambda p, j: (p, 0, 0))
    return qspec, kspecs, dspec


V7X_MXU = 256
HPT = V7X_MXU // HD


def _head_masks():
    head = lax.broadcasted_iota(jnp.int32, (1, V7X_MXU), 1) // HD
    return [head == t for t in range(HPT)]


def _tile_of(a):
    return slice((a // HPT) * V7X_MXU, (a // HPT + 1) * V7X_MXU)


def _pick_heads(parts, masks):
    tiles = []
    for g in range(HPS // HPT):
        out = parts[g * HPT + HPT - 1]
        for t in range(HPT - 2, -1, -1):
            out = jnp.where(masks[t], parts[g * HPT + t], out)
        tiles.append(out)
    return tiles[0] if len(tiles) == 1 else jnp.concatenate(tiles, axis=1)


def _attn_fwd(q, kpad, vpad, diag):
    s = q.shape[0]
    nqb = s // QB
    npad = PADK // QB

    def body(q_ref, k0, k1, k2, v0, v1, v2, diag_ref, o_ref, bias_ref, sc_ref, eb_ref, rl_ref):
        j = pl.program_id(1)

        @pl.when(j == 0)
        def _():
            _bias_from_diag(diag_ref, bias_ref)

        def block(masked):
            kcat = jnp.concatenate([k0[...], k1[...], k2[...]], axis=0)
            vcat = jnp.concatenate([v0[...], v1[...], v2[...]], axis=0)
            q2 = q_ref[...]
            masks = _head_masks()
            valid = (lax.broadcasted_iota(jnp.int32, (1, KB), 1) + j * QB >= PADK) if masked else None
            outs = []
            for a in range(HPS):
                qa = q2[:, _tile_of(a)]
                sc_ref[a] = lax.dot_general(jnp.where(masks[a % HPT], qa, jnp.zeros_like(qa)), kcat[:, _tile_of(a)],
                                            _NT, preferred_element_type=F32)
            for a in range(HPS):
                for c in range(QB // ATT_RC):
                    r = slice(c * ATT_RC, (c + 1) * ATT_RC)
                    sc = sc_ref[a, r, :] + bias_ref[a, r, :]
                    if masked:
                        sc = jnp.where(valid, sc, NEG)
                    e = jnp.exp(sc - jnp.max(sc, axis=-1, keepdims=True))
                    eb_ref[a, r, :] = e.astype(BF16)
                    rl_ref[a, r, :] = jnp.broadcast_to(1.0 / jnp.sum(e, axis=-1, keepdims=True),
                                                       (ATT_RC, V7X_MXU))
                outs.append(jnp.dot(eb_ref[a], vcat[:, _tile_of(a)], preferred_element_type=F32) * rl_ref[a])
            o_ref[...] = _pick_heads(outs, masks).astype(BF16)

        pl.when(j < npad)(functools.partial(block, True))
        pl.when(j >= npad)(functools.partial(block, False))

    qspec, kspecs, dspec = _attn_specs(nqb)
    assert len(kspecs) == 3
    return pl.pallas_call(
        body, name="attn_fwd", grid=(D // LW, nqb),
        in_specs=[qspec] + kspecs + kspecs + [dspec],
        out_specs=qspec,
        out_shape=jax.ShapeDtypeStruct((s, D), BF16),
        scratch_shapes=[pltpu.VMEM((HPS, QB, KB), F32), pltpu.VMEM((HPS, QB, KB), F32),
                        pltpu.VMEM((HPS, QB, KB), BF16), pltpu.VMEM((HPS, QB, V7X_MXU), F32)],
        compiler_params=_cp(("parallel", "arbitrary")),
    )(q, kpad, kpad, kpad, vpad, vpad, vpad, diag)


def _loss_grad(h4, tgt):
    s = h4.shape[0]

    def body(h_ref, t_ref, g_ref, l_ref):
        @pl.when(pl.program_id(0) == 0)
        def _():
            l_ref[...] = jnp.zeros_like(l_ref)

        d = h_ref[...] - t_ref[...]
        g_ref[...] = d * (1.0 / D)
        l_ref[...] += _sum8(d * d) * (0.5 / D)

    row = _rows(TM, D)
    return pl.pallas_call(
        body, name="loss_grad", grid=(s // TM,),
        in_specs=[row, row], out_specs=[row, pl.BlockSpec((8, D), lambda i: (0, 0))],
        out_shape=[jax.ShapeDtypeStruct((s, D), F32), jax.ShapeDtypeStruct((8, D), F32)],
        compiler_params=_cp(("arbitrary",)),
    )(h4, tgt)


def _rms_bwd(dn, xhat, r, g):
    dng = dn * g
    return r * (dng - xhat * jnp.mean(dng * xhat, axis=-1, keepdims=True))


def _acc_spec():
    return pl.BlockSpec((8, D), lambda i: (0, 0))


def _mlp_bwd(gout, hin, up, g, wu, wd, name):
    s = gout.shape[0]
    fj = 512

    def body(go_ref, hin_ref, up_ref, g_ref, wu_ref, wd_ref, gin_ref, dup_ref, act_ref, n_ref, dg_ref,
             gob_ref, dn_ref):
        @pl.when(pl.program_id(0) == 0)
        def _():
            dg_ref[...] = jnp.zeros_like(dg_ref)

        gob_ref[...] = go_ref[...].astype(BF16)
        for j in range(FF // fj):
            sl = slice(j * fj, (j + 1) * fj)
            rl = jnp.maximum(up_ref[:, sl], 0.0)
            act_ref[:, sl] = (rl * rl).astype(BF16)
            dact = lax.dot_general(gob_ref[...], wd_ref[sl, :], _NT, preferred_element_type=F32)
            dupj = (dact * (2.0 * rl)).astype(BF16)
            dup_ref[:, sl] = dupj
            part = lax.dot_general(dupj, wu_ref[j], _NT, preferred_element_type=F32)
            if j == 0:
                dn_ref[...] = part
            else:
                dn_ref[...] += part
        hin = hin_ref[...]
        r = _rstd(hin)
        xhat = hin * r
        n_ref[...] = (xhat * g_ref[...]).astype(BF16)
        dn = dn_ref[...]
        gin_ref[...] = go_ref[...] + _rms_bwd(dn, xhat, r, g_ref[...])
        dg_ref[...] += _sum8(dn * xhat)

    row = _rows(TMM, D)
    wide = _rows(TMM, FF)
    return pl.pallas_call(
        body, name=name, grid=(s // TMM,),
        in_specs=[row, row, wide, _full((1, D)), _full((NDEV, D, fj)), _full((FF, D))],
        out_specs=[row, wide, wide, row, _acc_spec()],
        out_shape=[jax.ShapeDtypeStruct((s, D), F32), jax.ShapeDtypeStruct((s, FF), BF16),
                   jax.ShapeDtypeStruct((s, FF), BF16), jax.ShapeDtypeStruct((s, D), BF16),
                   jax.ShapeDtypeStruct((8, D), F32)],
        scratch_shapes=[pltpu.VMEM((TMM, D), BF16), pltpu.VMEM((TMM, D), F32)],
        compiler_params=_cp(("arbitrary",)),
    )(gout, hin, up, g, wu, wd)


def _matmul_tn(a, b, slab, out_dtype, name):
    s, m = a.shape
    n = b.shape[1]
    ts = min(s, 512 if n > 2048 else 1024)
    nk = s // ts
    nc = 512
    w = n // NDEV

    def body(a_ref, b_ref, o_ref, at_ref, acc_ref):
        k = pl.program_id(0)
        at_ref[...] = a_ref[...].astype(BF16).T

        @pl.when(k == 0)
        def _():
            acc_ref[...] = jnp.zeros_like(acc_ref)

        for c in range(n // nc):
            sl = slice(c * nc, (c + 1) * nc)
            acc_ref[:, sl] += jnp.dot(at_ref[...], b_ref[:, sl].astype(BF16), preferred_element_type=F32)

        @pl.when(k == nk - 1)
        def _():
            if slab:
                for d in range(NDEV):
                    o_ref[d] = acc_ref[:, d * w:(d + 1) * w].astype(out_dtype)
            else:
                o_ref[...] = acc_ref[...].astype(out_dtype)

    if slab:
        out_shape = jax.ShapeDtypeStruct((NDEV, m, w), out_dtype)
        out_spec = pl.BlockSpec((NDEV, m, w), lambda k: (0, 0, 0), pipeline_mode=pl.Buffered(1))
    else:
        out_shape = jax.ShapeDtypeStruct((m, n), out_dtype)
        out_spec = pl.BlockSpec((m, n), lambda k: (0, 0), pipeline_mode=pl.Buffered(1))
    return pl.pallas_call(
        body, name=name, grid=(nk,),
        in_specs=[pl.BlockSpec((ts, m), lambda k: (k, 0)), pl.BlockSpec((ts, n), lambda k: (k, 0))],
        out_specs=out_spec, out_shape=out_shape,
        scratch_shapes=[pltpu.VMEM((m, ts), BF16), pltpu.VMEM((m, n), F32)],
        compiler_params=_cp(("arbitrary",)),
    )(a, b)


def _matmul_nt(x, w, name):
    s, n = x.shape
    k = w.shape[0]

    def body(x_ref, w_ref, o_ref):
        o_ref[...] = lax.dot_general(x_ref[...].astype(BF16), w_ref[...], _NT,
                                     preferred_element_type=F32).astype(BF16)

    return pl.pallas_call(
        body, name=name, grid=(s // TM,),
        in_specs=[_rows(TM, n), _full((k, n))], out_specs=_rows(TM, k),
        out_shape=jax.ShapeDtypeStruct((s, k), BF16),
        compiler_params=_cp(("parallel",)),
    )(x, w)


def _attn_bwd(q, kpad, vpad, do, diag):
    s = q.shape[0]
    nqb = s // QB
    npad = PADK // QB

    def body(q_ref, k0, k1, k2, v0, v1, v2, do_ref, diag_ref, dq_ref, dk_ref, dv_ref, dd_ref,
             bias_ref, db_ref, dka_ref, dva_ref, sc_ref, dp_ref, dsb_ref, pb_ref):
        j = pl.program_id(1)

        @pl.when(j == 0)
        def _():
            _bias_from_diag(diag_ref, bias_ref)
            dka_ref[...] = jnp.zeros_like(dka_ref)
            dva_ref[...] = jnp.zeros_like(dva_ref)
            db_ref[...] = jnp.zeros_like(db_ref)

        def block(masked):
            kcat = jnp.concatenate([k0[...], k1[...], k2[...]], axis=0)
            vcat = jnp.concatenate([v0[...], v1[...], v2[...]], axis=0)
            q2 = q_ref[...]
            do2 = do_ref[...]
            masks = _head_masks()
            valid = (lax.broadcasted_iota(jnp.int32, (1, KB), 1) + j * QB >= PADK) if masked else None
            qt = q2.T
            dot_ = do2.T
            dq = []
            for a in range(HPS):
                qa, doa = q2[:, _tile_of(a)], do2[:, _tile_of(a)]
                sc_ref[a] = lax.dot_general(jnp.where(masks[a % HPT], qa, jnp.zeros_like(qa)), kcat[:, _tile_of(a)],
                                            _NT, preferred_element_type=F32)
                dp_ref[a] = lax.dot_general(jnp.where(masks[a % HPT], doa, jnp.zeros_like(doa)),
                                            vcat[:, _tile_of(a)], _NT, preferred_element_type=F32)
            for a in range(HPS):
                for c in range(QB // ATT_RC):
                    r = slice(c * ATT_RC, (c + 1) * ATT_RC)
                    sc = sc_ref[a, r, :] + bias_ref[a, r, :]
                    if masked:
                        sc = jnp.where(valid, sc, NEG)
                    e = jnp.exp(sc - jnp.max(sc, axis=-1, keepdims=True))
                    p = e * (1.0 / jnp.sum(e, axis=-1, keepdims=True))
                    dp = dp_ref[a, r, :]
                    ds = p * (dp - jnp.sum(p * dp, axis=-1, keepdims=True))
                    db_ref[a, r, :] += ds
                    dsb_ref[a, r, :] = ds.astype(BF16)
                    pb_ref[a, r, :] = p.astype(BF16)
                hd = slice(a * HD, (a + 1) * HD)
                dq.append(jnp.dot(dsb_ref[a], kcat[:, _tile_of(a)], preferred_element_type=F32))
                dka_ref[hd, :] += jnp.dot(qt[hd, :], dsb_ref[a], preferred_element_type=F32)
                dva_ref[hd, :] += jnp.dot(dot_[hd, :], pb_ref[a], preferred_element_type=F32)
            dq_ref[...] = _pick_heads(dq, masks) * (HD ** -0.5)

        pl.when(j < npad)(functools.partial(block, True))
        pl.when(jnp.logical_and(j >= npad, j < nqb))(functools.partial(block, False))

        @pl.when(j == nqb - 1)
        def _():
            for a in range(HPS):
                dd_ref[a] = _diag_sums(db_ref, a)

        dk_ref[...] = dka_ref[:, 0:QB].T
        dv_ref[...] = dva_ref[:, 0:QB].T
        dka_ref[:, 0:KB - QB] = dka_ref[:, QB:KB]
        dva_ref[:, 0:KB - QB] = dva_ref[:, QB:KB]
        dka_ref[:, KB - QB:KB] = jnp.zeros((LW, QB), F32)
        dva_ref[:, KB - QB:KB] = jnp.zeros((LW, QB), F32)

    qspec, kspecs, dspec = _attn_specs(nqb)
    kout = pl.BlockSpec((QB, LW), lambda p, j: (jnp.maximum(j - npad, 0), p))
    sd = jax.ShapeDtypeStruct((s, D), F32)
    return pl.pallas_call(
        body, name="attn_bwd", grid=(D // LW, nqb + npad),
        in_specs=[qspec] + kspecs + kspecs + [qspec, dspec],
        out_specs=[qspec, kout, kout, pl.BlockSpec((None, HPS, 8, _TOEP), lambda p, j: (p, 0, 0, 0))],
        out_shape=[sd, sd, sd, jax.ShapeDtypeStruct((NH // HPS, HPS, 8, _TOEP), F32)],
        scratch_shapes=[pltpu.VMEM((HPS, QB, KB), F32), pltpu.VMEM((HPS, QB, KB), F32),
                        pltpu.VMEM((LW, KB), F32), pltpu.VMEM((LW, KB), F32),
                        pltpu.VMEM((HPS, QB, KB), F32), pltpu.VMEM((HPS, QB, KB), F32),
                        pltpu.VMEM((HPS, QB, KB), BF16), pltpu.VMEM((HPS, QB, KB), BF16)],
        compiler_params=_cp(("parallel", "arbitrary")),
    )(q, kpad, kpad, kpad, vpad, vpad, vpad, do, diag)


def _head_norm_bwd(dy2, x2, g2, lo):
    rr = _head_rstd(x2, lo)
    xhat = x2 * rr
    t = dy2 * g2 * xhat
    m_lo = jnp.sum(jnp.where(lo, t, 0.0), axis=-1, keepdims=True)
    m_hi = jnp.sum(jnp.where(lo, 0.0, t), axis=-1, keepdims=True)
    m = jnp.where(lo, m_lo, m_hi) * (1.0 / HD)
    return rr * (dy2 * g2 - xhat * m), dy2 * xhat


def _kvq_bwd(dq, dk, dv, qraw, kraw, h2, g3, kv_norm, b_norm, w_kv, w_q, k_norm_t, q_norm_t):
    s = h2.shape[0]

    def body(dq_ref, dk_ref, dv_ref, qraw_ref, kraw_ref, h_ref, g3_ref, gkv_ref, gb_ref, wkv_ref, wq_ref,
             kn_ref, qn_ref, g2_ref, dqr_ref, dkv_ref, nb_ref, nk_ref, dgq_ref, dgk_ref, dgb_ref, dgkv_ref):
        @pl.when(pl.program_id(0) == 0)
        def _():
            for r in (dgq_ref, dgk_ref, dgb_ref, dgkv_ref):
                r[...] = jnp.zeros_like(r)

        lo = lax.broadcasted_iota(jnp.int32, (1, 128), 1) < HD
        for p in range(D // 128):
            sl = slice(p * 128, (p + 1) * 128)
            dx, dgp = _head_norm_bwd(dq_ref[:, sl], qraw_ref[:, sl], qn_ref[:, sl], lo)
            dqr_ref[:, sl] = dx.astype(BF16)
            dgq_ref[:, sl] += _sum8(dgp)
            dx, dgp = _head_norm_bwd(dk_ref[:, sl], kraw_ref[:, sl], kn_ref[:, sl], lo)
            dkv_ref[:, sl] = dx.astype(BF16)
            dgk_ref[:, sl] += _sum8(dgp)
        dkv_ref[:, D:] = dv_ref[...].astype(BF16)
        dnb = lax.dot_general(dqr_ref[...], wq_ref[...], _NT, preferred_element_type=F32)
        cw = 2 * D // NDEV
        dnk = lax.dot_general(dkv_ref[:, 0:cw], wkv_ref[0], _NT, preferred_element_type=F32)
        for d in range(1, NDEV):
            dnk = dnk + lax.dot_general(dkv_ref[:, d * cw:(d + 1) * cw], wkv_ref[d], _NT, preferred_element_type=F32)
        h = h_ref[...]
        r = _rstd(h)
        xhat = h * r
        dxg = dnb * gb_ref[...] + dnk * gkv_ref[...]
        g2_ref[...] = g3_ref[...] + r * (dxg - xhat * jnp.mean(dxg * xhat, axis=-1, keepdims=True))
        dgb_ref[...] += _sum8(dnb * xhat)
        dgkv_ref[...] += _sum8(dnk * xhat)
        nb_ref[...] = (xhat * gb_ref[...]).astype(BF16)
        nk_ref[...] = (xhat * gkv_ref[...]).astype(BF16)

    row = _rows(TM, D)
    sd = jax.ShapeDtypeStruct((s, D), BF16)
    acc = jax.ShapeDtypeStruct((8, D), F32)
    return pl.pallas_call(
        body, name="kvq_bwd", grid=(s // TM,),
        in_specs=[row] * 7 + [_full((1, D)), _full((1, D)), _full((NDEV, D, 2 * D // NDEV)), _full((D, D)),
                              _full((1, D)), _full((1, D))],
        out_specs=[row, row, _rows(TM, 2 * D), row, row] + [_acc_spec()] * 4,
        out_shape=[jax.ShapeDtypeStruct((s, D), F32), sd, jax.ShapeDtypeStruct((s, 2 * D), BF16), sd, sd,
                   acc, acc, acc, acc],
        compiler_params=_cp(("arbitrary",)),
    )(dq, dk, dv, qraw, kraw, h2, g3, kv_norm, b_norm, w_kv, w_q, k_norm_t, q_norm_t)


def _lru_bwd(g1, gate, rec, hs, w_out, conv_w, conv_b, wg, bg, lam):
    s = g1.shape[0]
    nt = s // TL

    def body(g1_ref, gate_ref, rec_ref, recp_ref, hs_ref, hsp_ref, wo_ref, cw_ref, cb_ref, wg_ref, bg_ref, lam_ref,
             du_ref, dcw_ref, dcb_ref, dwg_ref, dbg_ref, dlam_ref, ext_ref, dext_ref, cg_ref,
             a_sc, dh_sc, hl_sc, ac_sc, rg_sc, ig_sc, mult_sc, rc_sc):
        i = pl.program_id(0)
        first_tile = i == nt - 1

        @pl.when(i == 0)
        def _():
            dext_ref[TL:TL + 8, :] = jnp.zeros((8, D), F32)
            cg_ref[...] = jnp.zeros_like(cg_ref)
            for r in (dcw_ref, dcb_ref, dwg_ref, dbg_ref, dlam_ref):
                r[...] = jnp.zeros_like(r)

        keep = jnp.where(first_tile, 0.0, 1.0)
        ext_ref[0:8, :] = recp_ref[...] * keep
        ext_ref[8:8 + TL, :] = rec_ref[...]
        dy = lax.dot_general(g1_ref[...].astype(BF16), wo_ref[...], _NT, preferred_element_type=F32)
        lam_v = lam_ref[...]
        sp = _softplus_neg(lam_v)
        dsp_dlam = -_sigmoid(-lam_v)
        rows = lax.broadcasted_iota(jnp.int32, (TL, BW), 0)
        for n in range(NBLK):
            sl = slice(n * BW, (n + 1) * BW)
            rc = _conv(ext_ref, cw_ref, cb_ref, sl, TL)
            rg, ig, a, mult = _lru_gates(rc, wg_ref[n], bg_ref[n:n + 1, :], sp[:, sl])
            h = hs_ref[:, sl]
            gt = gate_ref[:, sl]
            dyn = dy[:, sl]
            du_ref[:, sl] = (dyn * h * _gelu_grad(gt)).astype(BF16)
            dh_sc[n] = dyn * _gelu(gt) + jnp.where(rows == TL - 1, cg_ref[0:1, sl], 0.0)
            a_sc[n], rg_sc[n], ig_sc[n], mult_sc[n], rc_sc[n] = a, rg, ig, mult, rc
            ac_sc[n] = _shift_up(a, 1, 0.0, rows, TL)
        _tile_scan(ac_sc, dh_sc, hl_sc, [jnp.zeros((1, BW), F32)] * NBLK, reverse=True)
        for n in range(NBLK):
            sl = slice(n * BW, (n + 1) * BW)
            gsc = hl_sc[n]
            a, rg, ig, mult, rc = a_sc[n], rg_sc[n], ig_sc[n], mult_sc[n], rc_sc[n]
            cg_ref[0:1, sl] = a[0:1, :] * gsc[0:1, :]
            hprev = _shift_down(hs_ref[:, sl], 1, hsp_ref[7:8, sl] * keep, rows)
            da = gsc * hprev
            d_mult = gsc * ig * rc
            d_ig = gsc * mult * rc
            d_rc = gsc * mult * ig
            d_la = da * a - d_mult * (a * a) / mult
            d_rg = d_la * ((-LRU_C) * sp[:, sl])
            dlam_ref[:, sl] += _sum8(d_la * ((-LRU_C) * rg)) * dsp_dlam[:, sl]
            dg = jnp.concatenate([d_rg * rg * (1.0 - rg), d_ig * ig * (1.0 - ig)], axis=1)
            dgb = dg.astype(BF16)
            d_rc = d_rc + lax.dot_general(dgb, wg_ref[n], _NT, preferred_element_type=F32)
            dwg_ref[n] += lax.dot_general(rc.astype(BF16), dgb, _TN, preferred_element_type=F32)
            dbg_ref[n] += _sum8(dg)
            dext_ref[0:TL, sl] = d_rc
            dcb_ref[:, sl] += _sum8(d_rc)
            for k in range(4):
                dcw_ref[k, :, sl] += _sum8(d_rc * ext_ref[5 + k:5 + k + TL, sl])
        for k in range(4):
            part = cw_ref[3 - k:4 - k, :] * dext_ref[k:k + TL, :]
            acc = part if k == 0 else acc + part
        du_ref[:, D:] = acc.astype(BF16)
        dext_ref[TL:TL + 8, :] = dext_ref[0:8, :]

    rev = pl.BlockSpec((TL, D), lambda i: (nt - 1 - i, 0))
    rev8 = pl.BlockSpec((8, D), lambda i: (jnp.maximum((nt - 1 - i) * (TL // 8) - 1, 0), 0))
    acc = jax.ShapeDtypeStruct((8, D), F32)
    return pl.pallas_call(
        body, name="lru_bwd", grid=(nt,),
        in_specs=[rev, rev, rev, rev8, rev, rev8, _full((D, D)), _full((4, D)), _full((1, D)),
                  _full((NBLK, BW, 2 * BW)), _full((NBLK, 2 * BW)), _full((1, D))],
        out_specs=[pl.BlockSpec((TL, 2 * D), lambda i: (nt - 1 - i, 0)),
                   pl.BlockSpec((4, 8, D), lambda i: (0, 0, 0)), _acc_spec(),
                   pl.BlockSpec((NBLK, BW, 2 * BW), lambda i: (0, 0, 0)),
                   pl.BlockSpec((NBLK, 8, 2 * BW), lambda i: (0, 0, 0)), _acc_spec()],
        out_shape=[jax.ShapeDtypeStruct((s, 2 * D), BF16), jax.ShapeDtypeStruct((4, 8, D), F32), acc,
                   jax.ShapeDtypeStruct((NBLK, BW, 2 * BW), F32), jax.ShapeDtypeStruct((NBLK, 8, 2 * BW), F32), acc],
        scratch_shapes=[pltpu.VMEM((TL + 8, D), F32), pltpu.VMEM((TL + 8, D), F32), pltpu.VMEM((8, D), F32)]
        + [pltpu.VMEM((NBLK, TL, BW), F32)] * 8,
        compiler_params=_cp(("arbitrary",)),
    )(g1, gate, rec, rec, hs, hs, w_out, conv_w, conv_b, wg, bg, lam)


def _a_in_bwd(du, h0, g1, a_norm, w_in):
    s = h0.shape[0]

    def body(du_ref, h_ref, g1_ref, an_ref, win_ref, gx_ref, n1_ref, dg_ref):
        @pl.when(pl.program_id(0) == 0)
        def _():
            dg_ref[...] = jnp.zeros_like(dg_ref)

        cw = 2 * D // NDEV
        dn = lax.dot_general(du_ref[:, 0:cw], win_ref[0], _NT, preferred_element_type=F32)
        for d in range(1, NDEV):
            dn = dn + lax.dot_general(du_ref[:, d * cw:(d + 1) * cw], win_ref[d], _NT, preferred_element_type=F32)
        h = h_ref[...]
        r = _rstd(h)
        xhat = h * r
        gx_ref[...] = g1_ref[...] + _rms_bwd(dn, xhat, r, an_ref[...])
        n1_ref[...] = (xhat * an_ref[...]).astype(BF16)
        dg_ref[...] += _sum8(dn * xhat)

    row = _rows(TM, D)
    return pl.pallas_call(
        body, name="a_in_bwd", grid=(s // TM,),
        in_specs=[_rows(TM, 2 * D), row, row, _full((1, D)), _full((NDEV, D, 2 * D // NDEV))],
        out_specs=[row, row, _acc_spec()],
        out_shape=[jax.ShapeDtypeStruct((s, D), F32), jax.ShapeDtypeStruct((s, D), BF16),
                   jax.ShapeDtypeStruct((8, D), F32)],
        compiler_params=_cp(("arbitrary",)),
    )(du, h0, g1, a_norm, w_in)


def _rel_onehot():
    m = np.arange(_TOEP)
    signed = np.where(m < KB, m, m - _TOEP)
    idx = np.clip(PADK - signed, -(CHUNK - 1), 2 * CHUNK) + (CHUNK - 1)
    return (idx[None, :] == np.arange(NREL)[:, None]).astype(np.float32)


def _bias_diagonals(rel_bias):
    diag = jnp.dot(rel_bias, jnp.asarray(_rel_onehot()), precision=lax.Precision.HIGHEST)
    return diag.reshape(NH // HPS, HPS, _TOEP)


def _rel_bias_grad(dd):
    rows = 8
    z = dd
    oh = np.zeros((_TOEP, 256), np.float32)
    oh[:, :NREL] = _rel_onehot().T

    def body(z_ref, oh_ref, o_ref):
        d = jnp.sum(z_ref[...], axis=0, keepdims=True)
        hi = d.astype(BF16)
        mid = (d - hi.astype(F32)).astype(BF16)
        lo = (d - hi.astype(F32) - mid.astype(F32)).astype(BF16)
        ohb = oh_ref[...].astype(BF16)
        acc = jnp.zeros((8, 256), F32)
        for piece in (lo, mid, hi):
            acc = acc + jnp.dot(jnp.broadcast_to(piece, (8, _TOEP)), ohb, preferred_element_type=F32)
        o_ref[...] = acc

    out = pl.pallas_call(
        body, name="rel_bias_grad", grid=(NH,),
        in_specs=[pl.BlockSpec((None, rows, _TOEP), lambda h: (h, 0, 0)), pl.BlockSpec((_TOEP, 256), lambda h: (0, 0))],
        out_specs=pl.BlockSpec((None, 8, 256), lambda h: (h, 0, 0)),
        out_shape=jax.ShapeDtypeStruct((NH, 8, 256), F32),
        compiler_params=_cp(("parallel",)),
    )(z, jnp.asarray(oh))
    return out[:, 0, :NREL]


def _exchange(arrays, scatter, name):
    n = len(arrays)

    def body(*refs):
        ins, outs = refs[:n], refs[n:2 * n]
        token, (send_sems, recv_sems, local_sems) = refs[2 * n], refs[2 * n + 1:]
        token[...] = jnp.zeros_like(token)
        x, y, c = lax.axis_index("x"), lax.axis_index("y"), lax.axis_index("c")
        me = 4 * x + 2 * y + c

        def peer_of(r):
            rx, ry, rc = (r >> 2) & 1, (r >> 1) & 1, r & 1
            px = 1 - x if rx else x
            py = 1 - y if ry else y
            pc = 1 - c if rc else c
            return (px, py, pc), 4 * px + 2 * py + pc

        local, sent = [], []
        for k in range(n):
            cp = pltpu.make_async_copy(ins[k].at[me] if scatter else ins[k], outs[k].at[me], local_sems.at[k])
            cp.start()
            local.append(cp)
            for r in range(1, NDEV):
                peer, peer_lin = peer_of(r)
                cp = pltpu.make_async_remote_copy(
                    src_ref=ins[k].at[peer_lin] if scatter else ins[k], dst_ref=outs[k].at[me],
                    send_sem=send_sems.at[k, r - 1], recv_sem=recv_sems.at[k, r - 1],
                    device_id=peer, device_id_type=pl.DeviceIdType.MESH)
                cp.start()
                sent.append(cp)
        for k in range(n):
            for r in range(1, NDEV):
                peer, peer_lin = peer_of(r)
                pltpu.make_async_remote_copy(
                    src_ref=ins[k].at[peer_lin] if scatter else ins[k], dst_ref=outs[k].at[peer_lin],
                    send_sem=send_sems.at[k, r - 1], recv_sem=recv_sems.at[k, r - 1],
                    device_id=peer, device_id_type=pl.DeviceIdType.MESH).wait_recv()
        for cp in sent:
            cp.wait_send()
        for cp in local:
            cp.wait()

    def slot_shape(a):
        return (NDEV,) + (a.shape[1:] if scatter else a.shape)

    anyspec = pl.BlockSpec(memory_space=pl.ANY)
    outs = pl.pallas_call(
        body, name=name,
        in_specs=[anyspec] * n, out_specs=[anyspec] * n + [pl.BlockSpec(memory_space=pltpu.VMEM)],
        out_shape=[jax.ShapeDtypeStruct(slot_shape(a), a.dtype) for a in arrays]
        + [jax.ShapeDtypeStruct((8, 128), F32)],
        scratch_shapes=[pltpu.SemaphoreType.DMA((n, NDEV - 1)), pltpu.SemaphoreType.DMA((n, NDEV - 1)),
                        pltpu.SemaphoreType.DMA((n,))],
        compiler_params=pltpu.CompilerParams(has_side_effects=True),
    )(*arrays)
    return outs[:n], outs[n]


def _peer(r):
    x, y, c = lax.axis_index("x"), lax.axis_index("y"), lax.axis_index("c")
    px = 1 - x if (r >> 2) & 1 else x
    py = 1 - y if (r >> 1) & 1 else y
    pc = 1 - c if r & 1 else c
    return (px, py, pc), 4 * px + 2 * py + pc


def _my_index():
    return 4 * lax.axis_index("x") + 2 * lax.axis_index("y") + lax.axis_index("c")


_HBM_SPEC = pl.BlockSpec(memory_space=pltpu.HBM)
_SEM_SPEC = pl.BlockSpec(memory_space=pltpu.SEMAPHORE)


_NPEER = NDEV - 1


def _exchange_start(arrays, scatter, name):
    n = len(arrays)
    ns = n * _NPEER
    slots = [(NDEV,) + (a.shape[1:] if scatter else a.shape) for a in arrays]

    def body(*refs):
        srcs, lands = refs[:n], refs[n:2 * n]
        send_sems, recv_sems = refs[2 * n:2 * n + ns], refs[2 * n + ns:2 * n + 2 * ns]
        token = refs[-1]
        me = _my_index()
        for k in range(n):
            for r in range(1, NDEV):
                peer, peer_lin = _peer(r)
                pltpu.make_async_remote_copy(
                    src_ref=srcs[k].at[peer_lin] if scatter else srcs[k], dst_ref=lands[k].at[me],
                    send_sem=send_sems[k * _NPEER + r - 1], recv_sem=recv_sems[k * _NPEER + r - 1],
                    device_id=peer, device_id_type=pl.DeviceIdType.MESH).start()
        token[...] = jnp.zeros_like(token)

    sem = pltpu.SemaphoreType.DMA(())
    outs = pl.pallas_call(
        body, name=name,
        out_shape=(*[sem] * (2 * ns), *[pltpu.HBM(a.shape, a.dtype) for a in arrays],
                   *[pltpu.HBM(s, a.dtype) for s, a in zip(slots, arrays)], jax.ShapeDtypeStruct((8, 128), F32)),
        in_specs=[_HBM_SPEC] * (2 * n),
        out_specs=(*[_SEM_SPEC] * (2 * ns), *[_HBM_SPEC] * (2 * n), pl.BlockSpec(memory_space=pltpu.VMEM)),
        input_output_aliases={k: 2 * ns + k for k in range(2 * n)},
        compiler_params=pltpu.CompilerParams(has_side_effects=pltpu.SideEffectType.DATAFLOW_SIDE_EFFECTING),
    )(*[pltpu.with_memory_space_constraint(a, pltpu.HBM) for a in arrays],
      *[pltpu.with_memory_space_constraint(lax.empty(s, a.dtype), pltpu.HBM) for s, a in zip(slots, arrays)])
    return outs[:ns], outs[ns:2 * ns], outs[2 * ns:2 * ns + n], outs[2 * ns + n:2 * ns + 2 * n], outs[-1]


def _exchange_wait(started, after, scatter, name):
    send_sems, recv_sems, srcs, lands, _ = started
    n = len(srcs)
    ns = n * _NPEER

    def body(*refs):
        src_refs, land_refs = refs[:n], refs[n:2 * n]
        ssem, rsem = refs[2 * n:2 * n + ns], refs[2 * n + ns:2 * n + 2 * ns]
        for k in range(n):
            for r in range(1, NDEV):
                peer, peer_lin = _peer(r)
                cp = pltpu.make_async_remote_copy(
                    src_ref=src_refs[k].at[peer_lin] if scatter else src_refs[k], dst_ref=land_refs[k].at[peer_lin],
                    send_sem=ssem[k * _NPEER + r - 1], recv_sem=rsem[k * _NPEER + r - 1],
                    device_id=peer, device_id_type=pl.DeviceIdType.MESH)
                cp.wait_send()
                cp.wait_recv()

    outs = pl.pallas_call(
        body, name=name,
        out_shape=tuple(pltpu.HBM(a.shape, a.dtype) for a in list(srcs) + list(lands)),
        in_specs=[_HBM_SPEC] * (2 * n) + [_SEM_SPEC] * (2 * ns) + [pl.BlockSpec(memory_space=pl.ANY)],
        out_specs=tuple([_HBM_SPEC] * (2 * n)),
        input_output_aliases={k: k for k in range(2 * n)},
        compiler_params=pltpu.CompilerParams(has_side_effects=pltpu.SideEffectType.DATAFLOW_SIDE_EFFECTING),
    )(*srcs, *lands, *send_sems, *recv_sems, after)
    return list(outs[:n]), list(outs[n:])


def _fill_own(lands, owns, me):
    return [lax.dynamic_update_slice(z, o, (me,) + (0,) * (z.ndim - 1)) for z, o in zip(lands, owns)]


def _sum_slots(st, name):
    _, r, c = st.shape

    def body(s_ref, o_ref):
        acc = s_ref[0]
        for d in range(1, NDEV):
            acc = acc + s_ref[d]
        o_ref[...] = acc

    return pl.pallas_call(
        body, name=name, out_shape=jax.ShapeDtypeStruct((r, c), F32),
        in_specs=[pl.BlockSpec((NDEV, r, c), lambda: (0, 0, 0))], out_specs=pl.BlockSpec((r, c), lambda: (0, 0)),
    )(st)


def _adamw(w, m, v, gst, name, transposed=False):
    r, c = w.shape
    ns = gst.shape[0]
    tr = min(r, 256)
    c1 = 1.0 - ADAM_B1 ** ADAM_STEP
    c2 = 1.0 - ADAM_B2 ** ADAM_STEP

    def body(w_ref, m_ref, v_ref, g_ref, go_ref, d_ref, mo_ref, vo_ref):
        g = g_ref[0].astype(F32)
        for d in range(1, ns):
            g = g + g_ref[d].astype(F32)
        if transposed:
            g = g.T
        m2 = ADAM_B1 * m_ref[...] + (1.0 - ADAM_B1) * g
        v2 = ADAM_B2 * v_ref[...] + (1.0 - ADAM_B2) * (g * g)
        go_ref[...] = g
        mo_ref[...] = m2
        vo_ref[...] = v2
        d_ref[...] = (-ADAM_LR) * ((m2 / c1) / (jnp.sqrt(v2 / c2) + ADAM_EPS) + ADAM_WD * w_ref[...])

    blk = pl.BlockSpec((tr, c), lambda i: (i, 0))
    sd = jax.ShapeDtypeStruct((r, c), F32)
    return pl.pallas_call(
        body, name=name, grid=(r // tr,),
        in_specs=[blk, blk, blk, pl.BlockSpec((ns, c, tr), lambda i: (0, 0, i)) if transposed
                  else pl.BlockSpec((ns, tr, c), lambda i: (0, i, 0))],
        out_specs=[blk, blk, blk, blk], out_shape=[sd, sd, sd, sd],
        compiler_params=_cp(("parallel",)),
    )(w, m, v, gst)


def _pack(pieces, rows):
    flat = jnp.concatenate([p.reshape(-1).astype(F32) for p in pieces])
    return jnp.pad(flat, (0, rows * 128 - flat.shape[0])).reshape(rows, 128)


def _unpack(flat, shapes):
    out, off = [], 0
    for shp in shapes:
        size = int(np.prod(shp))
        out.append(flat[off:off + size].reshape(shp))
        off += size
    return out


def _cols(full, me, width):
    return lax.dynamic_slice_in_dim(full, me * width, width, axis=full.ndim - 1)


def kernel(x, a_norm, a_w_in, a_conv_w, a_conv_b, a_w_gate, a_b_gate, a_lambda, a_w_out, kv_norm, w_kv, k_norm, b_norm, b_w_q, b_q_norm, b_rel_bias, b_w_o, mlp_norm, w_up, w_down, loss_target, m_a_norm, m_a_w_in, m_a_conv_w, m_a_conv_b, m_a_w_gate, m_a_b_gate, m_a_lambda, m_a_w_out, m_kv_norm, m_w_kv, m_k_norm, m_b_norm, m_b_w_q, m_b_q_norm, m_b_rel_bias, m_b_w_o, m_mlp_norm, m_w_up, m_w_down, v_a_norm, v_a_w_in, v_a_conv_w, v_a_conv_b, v_a_w_gate, v_a_b_gate, v_a_lambda, v_a_w_out, v_kv_norm, v_w_kv, v_k_norm, v_b_norm, v_b_w_q, v_b_q_norm, v_b_rel_bias, v_b_w_o, v_mlp_norm, v_w_up, v_w_down):
    me = 4 * lax.axis_index("x") + 2 * lax.axis_index("y") + lax.axis_index("c")
    sh = D // NDEV

    big_w = [a_w_in[0], a_w_out[0], w_kv, b_w_q[0], b_w_o[0], w_up[0], w_up[1], w_down[0], w_down[1]]
    small_sharded = [a_norm, a_conv_w, a_conv_b, a_b_gate, a_lambda, a_w_gate]
    small_rows = 272
    def to_bf16(w, token):
        return (w + token[0, 0]).astype(BF16)

    got, tok_a = _exchange([a_w_in[0].astype(BF16), _pack(small_sharded, small_rows)], False, "gather_a")
    own_b1 = [to_bf16(w, tok_a) for w in (a_w_out[0], w_up[0], w_down[0])]
    st_b1 = _exchange_start(own_b1, False, "gather_b1_start")
    own_b2 = [to_bf16(w, st_b1[4]) for w in (w_kv, b_w_q[0], b_w_o[0], w_up[1], w_down[1])]
    st_b2 = _exchange_start(own_b2, False, "gather_b2_start")
    w_in = got[0]
    sm = got[1].reshape(NDEV, small_rows * 128)
    an_f = sm[:, 0:128].reshape(1, D) + st_b2[4][0:1, 0:1]
    cw_f = sm[:, 128:640].reshape(NDEV, 4, sh).transpose(1, 0, 2).reshape(4, D)
    cb_f = sm[:, 640:768].reshape(1, D)
    bg_f = sm[:, 768:1024].reshape(NDEV, NBLK, 2 * BW // NDEV).transpose(1, 0, 2).reshape(NBLK, 2 * BW)
    lam_f = sm[:, 1024:1152].reshape(1, D)
    wg_f = sm[:, 1152:1152 + NBLK * BW * 32].reshape(NDEV, NBLK, BW, 32).transpose(1, 2, 0, 3)
    wg_f = wg_f.reshape(NBLK, BW, 2 * BW).astype(BF16)
    kn_t = jnp.tile(k_norm, NH).reshape(1, D)
    qn_t = jnp.tile(b_q_norm[0], NH).reshape(1, D)
    kvn = kv_norm.reshape(1, D)
    diag = _bias_diagonals(b_rel_bias[0])

    h0 = x[0]
    gate, rec, hs, y = _lru_fwd(h0, an_f, w_in, cw_f, cb_f, wg_f, bg_f, lam_f)
    own, land = _exchange_wait(st_b1, y, False, "gather_b1_wait")
    land = _fill_own(land, [o[None] for o in own], me)
    w_out = land[0].reshape(D, D)
    wu = [land[1], None]
    wd = [land[2].reshape(FF, D), None]
    h1, h2, up0 = _mlp_fwd(h0, y, w_out, mlp_norm[0:1], wu[0], wd[0], "mlp_fwd0")
    own, land = _exchange_wait(st_b2, h2, False, "gather_b2_wait")
    land = _fill_own(land, [o[None] for o in own], me)
    wkv = land[0]
    w_q = land[1].reshape(D, D)
    w_o = land[2].reshape(D, D)
    wu[1] = land[3]
    wd[1] = land[4].reshape(FF, D)
    kraw, qraw, kpad, vpad, q = _kvq_fwd(h2, kvn, b_norm, wkv, w_q, kn_t, qn_t)
    o = _attn_fwd(q, kpad, vpad, diag)
    h3, h4, up1 = _mlp_fwd(h2, o, w_o, mlp_norm[1:2], wu[1], wd[1], "mlp_fwd1")
    g4, lpart = _loss_grad(h4, loss_target[0])
    loss = lax.psum(jnp.sum(lpart), ("x", "y", "c"))

    g3, dup1, act1, n3, dgm1 = _mlp_bwd(g4, h3, up1, mlp_norm[1:2], wu[1], wd[1], "mlp_bwd1")
    d_wd1 = _matmul_tn(g4, act1, True, BF16, "dw_down1")
    d_wu1 = _matmul_tn(n3, dup1, True, BF16, "dw_up1")
    do = _matmul_nt(g3, w_o, "do_proj")
    d_wo = _matmul_tn(o, g3, False, BF16, "dw_o").reshape(NDEV, sh, D)
    dq, dk, dv, dd = _attn_bwd(q, kpad, vpad, do, diag)
    g2, dqr, dkv, nb, nk, dgq, dgk, dgb, dgkv = _kvq_bwd(dq, dk, dv, qraw, kraw, h2, g3, kvn, b_norm, wkv, w_q,
                                                       kn_t, qn_t)
    d_wq = _matmul_tn(nb, dqr, False, BF16, "dw_q").reshape(NDEV, sh, D)
    d_wkv = _matmul_tn(nk, dkv, True, BF16, "dw_kv")
    st_r1 = _exchange_start([d_wd1, d_wu1, d_wo, d_wq, d_wkv], True, "scatter_r1_start")
    g1, dup0, act0, n2, dgm0 = _mlp_bwd(g2, h1, up0, mlp_norm[0:1] + st_r1[4][0:1, 0:1], wu[0], wd[0], "mlp_bwd0")
    d_wd0 = _matmul_tn(g2, act0, True, BF16, "dw_down0")
    d_wu0 = _matmul_tn(n2, dup0, True, BF16, "dw_up0")
    d_wout = _matmul_tn(y, g1, False, BF16, "dw_out").reshape(NDEV, sh, D)
    st_r2 = _exchange_start([d_wu0, d_wd0, d_wout], True, "scatter_r2_start")
    du, dcw, dcb, dwg, dbg, dlam = _lru_bwd(g1, gate, rec, hs, w_out, cw_f, cb_f, wg_f, bg_f,
                                            lam_f + st_r2[4][0:1, 0:1])
    gx, n1, dga = _a_in_bwd(du, h0, g1, an_f, w_in)
    d_win = _matmul_tn(n1, du, True, BF16, "dw_in")
    d_rel = _rel_bias_grad(dd.reshape(NH, 8, _TOEP))

    dwg_slab = dwg.reshape(NBLK, BW, NDEV, 32).transpose(2, 0, 1, 3).reshape(NDEV, 256, 128).astype(BF16)
    st_r3 = _exchange_start([d_win, dwg_slab], True, "scatter_r3_start")
    small_full = [dga.sum(0) + st_r3[4][0, 0], dcw.sum(1), dcb.sum(0), dbg.sum(1), dlam.sum(0), dgkv.sum(0),
                  dgk.sum(0).reshape(NH, HD).sum(0), dgb.sum(0), dgq.sum(0).reshape(NH, HD).sum(0), d_rel,
                  jnp.stack([dgm0.sum(0), dgm1.sum(0)])]
    small_g_rows = 136
    (gsm,), _ = _exchange([_pack(small_full, small_g_rows)], False, "gather_small_grads")
    src, recv1 = _exchange_wait(st_r1, gsm, True, "scatter_r1_wait")
    recv1 = _fill_own(recv1, [lax.dynamic_slice_in_dim(a, me, 1, 0) for a in src], me)
    src, recv2 = _exchange_wait(st_r2, recv1[0], True, "scatter_r2_wait")
    recv2 = _fill_own(recv2, [lax.dynamic_slice_in_dim(a, me, 1, 0) for a in src], me)
    gs = _unpack(_sum_slots(gsm, "sum_small_grads").reshape(-1),
                 [(1, D), (4, D), (1, D), (NBLK, 2 * BW), (1, D), (D,), (HD,), (1, D), (1, HD), (1, NH, NREL), (2, D)])
    g_small = [_cols(gs[0], me, sh), _cols(gs[1], me, sh)[None], _cols(gs[2], me, sh),
               _cols(gs[3], me, 2 * BW // NDEV)[None], _cols(gs[4], me, sh)] + gs[5:]

    names = ["a_w_in", "a_w_out", "w_kv", "b_w_q", "b_w_o", "w_up0", "w_up1", "w_down0", "w_down1", "a_w_gate"]
    big_m = [m_a_w_in[0], m_a_w_out[0], m_w_kv, m_b_w_q[0], m_b_w_o[0], m_w_up[0], m_w_up[1], m_w_down[0],
             m_w_down[1], m_a_w_gate.reshape(256, 128)]
    big_v = [v_a_w_in[0], v_a_w_out[0], v_w_kv, v_b_w_q[0], v_b_w_o[0], v_w_up[0], v_w_up[1], v_w_down[0],
             v_w_down[1], v_a_w_gate.reshape(256, 128)]
    big_w = big_w + [a_w_gate.reshape(256, 128)]

    def update(k, g):
        return _adamw(big_w[k], big_m[k], big_v[k], g, "adamw_" + names[k], transposed=names[k].startswith("w_down"))

    early = {1: recv2[2], 2: recv1[4], 3: recv1[3], 4: recv1[2], 5: recv2[0], 6: recv1[1], 7: recv2[1], 8: recv1[0]}
    res = {k: update(k, g) for k, g in early.items()}
    src, recv3 = _exchange_wait(st_r3, res[8][1], True, "scatter_r3_wait")
    recv3 = _fill_own(recv3, [lax.dynamic_slice_in_dim(a, me, 1, 0) for a in src], me)
    res[0] = update(0, recv3[0])
    res[9] = update(9, recv3[1])
    res = [res[k] for k in range(len(names))]
    small_w = [a_norm, a_conv_w, a_conv_b, a_b_gate, a_lambda, kv_norm, k_norm, b_norm, b_q_norm, b_rel_bias, mlp_norm]
    small_m = [m_a_norm, m_a_conv_w, m_a_conv_b, m_a_b_gate, m_a_lambda, m_kv_norm, m_k_norm, m_b_norm, m_b_q_norm,
               m_b_rel_bias, m_mlp_norm]
    small_v = [v_a_norm, v_a_conv_w, v_a_conv_b, v_a_b_gate, v_a_lambda, v_kv_norm, v_k_norm, v_b_norm, v_b_q_norm,
               v_b_rel_bias, v_mlp_norm]
    pr = 72
    res_small = _adamw(_pack(small_w, pr), _pack(small_m, pr), _pack(small_v, pr), _pack(g_small, pr)[None],
                       "adamw_small")
    small_shapes = [w.shape for w in small_w]
    res_small = [_unpack(r.reshape(-1), small_shapes) for r in res_small]

    def assemble(t):
        b = [r[t] for r in res]
        s_ = res_small[t]
        return [s_[0], b[0][None], s_[1], s_[2], b[9].reshape(a_w_gate.shape), s_[3], s_[4], b[1][None],
                s_[5], b[2], s_[6], s_[7], b[3][None], s_[8], s_[9], b[4][None], s_[10],
                jnp.stack([b[5], b[6]]), jnp.stack([b[7], b[8]])]

    return tuple([loss, gx[None]] + assemble(0) + assemble(1) + assemble(2) + assemble(3))
```

```python
import functools

import numpy as np
import jax
import jax.numpy as jnp
from jax import lax
from jax.experimental import pallas as pl
from jax.experimental.pallas import tpu as pltpu

F32 = jnp.float32
BF16 = jnp.bfloat16

D = 1024
NH = 16
HD = 64
FF = 4096
NBLK = 8
BW = 128
CHUNK = 64
PADK = 512
NREL = 192
EPS = 1e-6
LRU_C = 8.0
NDEV = 8

V7X_VMEM_LIMIT = 56 * 1024 * 1024
TM = 512
TMM = 512
TMF = 512
TL = 256
QB = 256
ATT_RC = 32
HPS = 8
LW = HPS * HD
KB = QB + PADK
NEG = -1e30

ADAM_LR, ADAM_B1, ADAM_B2, ADAM_EPS, ADAM_WD, ADAM_STEP = 0.001, 0.9, 0.999, 1e-08, 0.01, 10

_NT = (((1,), (1,)), ((), ()))
_TN = (((0,), (0,)), ((), ()))


def _cp(sem=None):
    return pltpu.CompilerParams(dimension_semantics=sem, vmem_limit_bytes=V7X_VMEM_LIMIT)


def _full(shape):
    n = len(shape)
    return pl.BlockSpec(shape, lambda *a: (0,) * n, pipeline_mode=pl.Buffered(1))


def _rows(tm, width):
    return pl.BlockSpec((tm, width), lambda i: (i, 0))


def _rstd(h):
    return lax.rsqrt(jnp.mean(h * h, axis=-1, keepdims=True) + EPS)


def _sigmoid(x):
    return 1.0 / (1.0 + jnp.exp(-x))


def _expm1(x):
    small = x * (1.0 + x * (0.5 + x * (1.0 / 6.0 + x * (1.0 / 24.0))))
    return jnp.where(jnp.abs(x) < 0.03, small, jnp.exp(x) - 1.0)


def _softplus_neg(lam):
    e = jnp.exp(-jnp.abs(lam))
    series = e * (1.0 - e * (0.5 - e * (1.0 / 3.0 - e * 0.25)))
    return jnp.maximum(-lam, 0.0) + jnp.where(e < 0.01, series, jnp.log(1.0 + e))


_GELU_K = 0.7978845608028654


def _gelu(x):
    return 0.5 * x * (1.0 + jnp.tanh(_GELU_K * (x + 0.044715 * x * x * x)))


def _gelu_grad(x):
    t = jnp.tanh(_GELU_K * (x + 0.044715 * x * x * x))
    return 0.5 * (1.0 + t) + 0.5 * x * (1.0 - t * t) * _GELU_K * (1.0 + 3.0 * 0.044715 * x * x)


def _sum8(x):
    r, c = x.shape
    return jnp.sum(x.reshape(r // 8, 8, c), axis=0)


def _shift_down(x, s, fill, rows):
    return jnp.where(rows >= s, pltpu.roll(x, s, axis=0), fill)


def _shift_up(x, s, fill, rows, n):
    return jnp.where(rows < n - s, pltpu.roll(x, n - s, axis=0), fill)


def _lru_gates(rc, wg_n, bg_n, sp_n):
    g = jnp.dot(rc.astype(BF16), wg_n, preferred_element_type=F32) + bg_n
    rg = _sigmoid(g[:, :BW])
    ig = _sigmoid(g[:, BW:])
    la = (-LRU_C) * rg * sp_n
    a = jnp.exp(la)
    mult = jnp.sqrt(-_expm1(2.0 * la))
    return rg, ig, a, mult


def _conv(ext_ref, cw_ref, cb_ref, sl, n):
    out = cb_ref[:, sl] + cw_ref[0:1, sl] * ext_ref[5:5 + n, sl]
    for k in range(1, 4):
        out = out + cw_ref[k:k + 1, sl] * ext_ref[5 + k:5 + k + n, sl]
    return out


def _tile_scan(a_ref, b_ref, h_ref, carry, reverse):
    sub = lax.broadcasted_iota(jnp.int32, (TL, BW), 0) % 8
    for n in range(NBLK):
        a, b = a_ref[n], b_ref[n]
        for s in (1, 2, 4):
            if reverse:
                inside = sub < 8 - s
                a_sh = jnp.where(inside, pltpu.roll(a, TL - s, axis=0), 1.0)
                b_sh = jnp.where(inside, pltpu.roll(b, TL - s, axis=0), 0.0)
            else:
                inside = sub >= s
                a_sh = jnp.where(inside, pltpu.roll(a, s, axis=0), 1.0)
                b_sh = jnp.where(inside, pltpu.roll(b, s, axis=0), 0.0)
            b = a * b_sh + b
            a = a * a_sh
        a_ref[n], b_ref[n] = a, b
    carry = list(carry)
    groups = range(TL // 8 - 1, -1, -1) if reverse else range(TL // 8)
    for g in groups:
        r = slice(8 * g, 8 * g + 8)
        for n in range(NBLK):
            h = a_ref[n, r, :] * carry[n] + b_ref[n, r, :]
            h_ref[n, r, :] = h
            carry[n] = h[0:1, :] if reverse else h[7:8, :]
    return carry


def _lru_fwd(h0, a_norm, w_in, conv_w, conv_b, wg, bg, lam):
    s = h0.shape[0]

    def body(h0_ref, an_ref, win_ref, cw_ref, cb_ref, wg_ref, bg_ref, lam_ref,
             gate_ref, rec_ref, hs_ref, y_ref, ext_ref, hc_ref, a_sc, b_sc, hl_sc):
        i = pl.program_id(0)

        @pl.when(i == 0)
        def _():
            ext_ref[0:8, :] = jnp.zeros((8, D), F32)
            hc_ref[...] = jnp.zeros_like(hc_ref)

        h = h0_ref[...]
        n1 = (h * _rstd(h) * an_ref[...]).astype(BF16)
        cw = 2 * D // NDEV
        for d in range(NDEV):
            ud = jnp.dot(n1, win_ref[d], preferred_element_type=F32)
            if d < NDEV // 2:
                gate_ref[:, d * cw:(d + 1) * cw] = ud
            else:
                rec_ref[:, d * cw - D:(d + 1) * cw - D] = ud
                ext_ref[8:8 + TL, d * cw - D:(d + 1) * cw - D] = ud
        sp = _softplus_neg(lam_ref[...])
        for n in range(NBLK):
            sl = slice(n * BW, (n + 1) * BW)
            rc = _conv(ext_ref, cw_ref, cb_ref, sl, TL)
            rg, ig, a, mult = _lru_gates(rc, wg_ref[n], bg_ref[n:n + 1, :], sp[:, sl])
            a_sc[n] = a
            b_sc[n] = mult * (ig * rc)
        carry = _tile_scan(a_sc, b_sc, hl_sc, [hc_ref[0:1, n * BW:(n + 1) * BW] for n in range(NBLK)], reverse=False)
        for n in range(NBLK):
            sl = slice(n * BW, (n + 1) * BW)
            hh = hl_sc[n]
            hc_ref[0:1, sl] = carry[n]
            hs_ref[:, sl] = hh
            y_ref[:, sl] = (_gelu(gate_ref[:, sl]) * hh).astype(BF16)
        ext_ref[0:8, :] = ext_ref[TL:TL + 8, :]

    row = _rows(TL, D)
    return pl.pallas_call(
        body, name="lru_fwd", grid=(s // TL,),
        in_specs=[row, _full((1, D)), _full((NDEV, D, 2 * D // NDEV)), _full((4, D)), _full((1, D)),
                  _full((NBLK, BW, 2 * BW)), _full((NBLK, 2 * BW)), _full((1, D))],
        out_specs=[row, row, row, row],
        out_shape=[jax.ShapeDtypeStruct((s, D), F32), jax.ShapeDtypeStruct((s, D), F32),
                   jax.ShapeDtypeStruct((s, D), F32), jax.ShapeDtypeStruct((s, D), BF16)],
        scratch_shapes=[pltpu.VMEM((TL + 8, D), F32), pltpu.VMEM((8, D), F32)]
        + [pltpu.VMEM((NBLK, TL, BW), F32)] * 3,
        compiler_params=_cp(("arbitrary",)),
    )(h0, a_norm, w_in, conv_w, conv_b, wg, bg, lam)


def _mlp_fwd(res, px, pw, g, wu, wd, name, tgt=None):
    s = res.shape[0]
    fj = 512
    with_loss = tgt is not None

    def body(res_ref, px_ref, pw_ref, g_ref, wu_ref, wd_ref, *rest):
        if with_loss:
            t_ref, hin_ref, hout_ref, up_ref, l_ref, n_ref = rest

            @pl.when(pl.program_id(0) == 0)
            def _():
                l_ref[...] = jnp.zeros_like(l_ref)
        else:
            hin_ref, hout_ref, up_ref, n_ref = rest
        hin = res_ref[...] + jnp.dot(px_ref[...], pw_ref[...], preferred_element_type=F32)
        hin_ref[...] = hin
        hout_ref[...] = hin
        n_ref[...] = (hin * _rstd(hin) * g_ref[...]).astype(BF16)
        for j in range(FF // fj):
            sl = slice(j * fj, (j + 1) * fj)
            up = jnp.dot(n_ref[...], wu_ref[j], preferred_element_type=F32)
            up_ref[:, sl] = up.astype(BF16)
            rl = jnp.maximum(up, 0.0)
            hout_ref[...] += jnp.dot((rl * rl).astype(BF16), wd_ref[sl, :], preferred_element_type=F32)
        if with_loss:
            d = hout_ref[...] - t_ref[...]
            hout_ref[...] = d * (1.0 / D)
            l_ref[...] += _sum8(d * d) * (0.5 / D)

    row = _rows(TMF, D)
    outs = pl.pallas_call(
        body, name=name, grid=(s // TMF,),
        in_specs=[row, row, _full((D, D)), _full((1, D)), _full((NDEV, D, fj)), _full((FF, D))]
        + ([row] if with_loss else []),
        out_specs=[row, row, _rows(TMF, FF)] + ([_acc_spec()] if with_loss else []),
        out_shape=[jax.ShapeDtypeStruct((s, D), F32), jax.ShapeDtypeStruct((s, D), F32),
                   jax.ShapeDtypeStruct((s, FF), BF16)] + ([jax.ShapeDtypeStruct((8, D), F32)] if with_loss else []),
        scratch_shapes=[pltpu.VMEM((TMF, D), BF16)],
        compiler_params=_cp(("arbitrary",) if with_loss else ("parallel",)),
    )(*([res, px, pw, g, wu, wd] + ([tgt] if with_loss else [])))
    return outs


def _head_rstd(x2, lo):
    sq = x2 * x2
    s_lo = jnp.sum(jnp.where(lo, sq, 0.0), axis=-1, keepdims=True)
    s_hi = jnp.sum(jnp.where(lo, 0.0, sq), axis=-1, keepdims=True)
    return lax.rsqrt(jnp.where(lo, s_lo, s_hi) * (1.0 / HD) + EPS)


def _kvq_fwd(h2, kv_norm, b_norm, w_kv, w_q, k_norm_t, q_norm_t):
    s = h2.shape[0]
    assert PADK == TM

    def body(h_ref, gkv_ref, gb_ref, wkv_ref, wq_ref, kn_ref, qn_ref,
             kraw_ref, qraw_ref, k_ref, v_ref, q_ref):
        i = pl.program_id(0)

        @pl.when(i == 0)
        def _():
            k_ref[...] = jnp.zeros_like(k_ref)
            v_ref[...] = jnp.zeros_like(v_ref)

        @pl.when(i > 0)
        def _():
            h = h_ref[...]
            xhat = h * _rstd(h)
            nk = (xhat * gkv_ref[...]).astype(BF16)
            qr = jnp.dot((xhat * gb_ref[...]).astype(BF16), wq_ref[...], preferred_element_type=F32)
            qraw_ref[...] = qr
            lo = lax.broadcasted_iota(jnp.int32, (1, 128), 1) < HD
            cw = 2 * D // NDEV
            for d in range(NDEV):
                kvd = jnp.dot(nk, wkv_ref[d], preferred_element_type=F32)
                if d < NDEV // 2:
                    kraw_ref[:, d * cw:(d + 1) * cw] = kvd
                    for p in range(cw // 128):
                        sl = slice(d * cw + p * 128, d * cw + (p + 1) * 128)
                        k2 = kvd[:, p * 128:(p + 1) * 128]
                        k_ref[:, sl] = (k2 * _head_rstd(k2, lo) * kn_ref[:, sl]).astype(BF16)
                else:
                    v_ref[:, d * cw - D:(d + 1) * cw - D] = kvd.astype(BF16)
            for p in range(D // 128):
                sl = slice(p * 128, (p + 1) * 128)
                q2 = qr[:, sl]
                q_ref[:, sl] = (q2 * _head_rstd(q2, lo) * qn_ref[:, sl] * (HD ** -0.5)).astype(BF16)

    prev = pl.BlockSpec((TM, D), lambda i: (jnp.maximum(i - 1, 0), 0))
    cur = pl.BlockSpec((TM, D), lambda i: (i, 0))
    return pl.pallas_call(
        body, name="kvq_fwd", grid=(s // TM + 1,),
        in_specs=[prev, _full((1, D)), _full((1, D)), _full((NDEV, D, 2 * D // NDEV)), _full((D, D)), _full((1, D)),
                  _full((1, D))],
        out_specs=[prev, prev, cur, cur, prev],
        out_shape=[jax.ShapeDtypeStruct((s, D), F32), jax.ShapeDtypeStruct((s, D), F32),
                   jax.ShapeDtypeStruct((s + PADK, D), BF16), jax.ShapeDtypeStruct((s + PADK, D), BF16),
                   jax.ShapeDtypeStruct((s, D), BF16)],
        compiler_params=_cp(("arbitrary",)),
    )(h2, kv_norm, b_norm, w_kv, w_q, k_norm_t, q_norm_t)


_TOEP = QB + KB


def _bias_from_diag(diag_ref, bias_ref):
    row8 = lax.broadcasted_iota(jnp.int32, (8, _TOEP), 0)
    kchunk = lax.broadcasted_iota(jnp.int32, (8, KB), 1) // CHUNK
    for a in range(HPS):
        v = jnp.broadcast_to(diag_ref[a:a + 1, :], (8, _TOEP))
        z0 = v
        for b in range(1, 8):
            z0 = jnp.where(row8 == b, pltpu.roll(v, b, axis=1), z0)
        for t in range(QB // 8):
            slab = z0 if t == 0 else pltpu.roll(z0, 8 * t, axis=1)
            qchunk = (8 * t) // CHUNK
            band = jnp.logical_and(kchunk >= qchunk, kchunk <= qchunk + PADK // CHUNK)
            bias_ref[a, 8 * t:8 * t + 8, :] = jnp.where(band, slab[:, :KB], NEG)


def _diag_sums(db_ref, a):
    row8 = lax.broadcasted_iota(jnp.int32, (8, _TOEP), 0)
    z = jnp.zeros((8, _TOEP), F32)
    for t in range(QB // 8):
        slab = jnp.concatenate([db_ref[a, 8 * t:8 * t + 8, :], jnp.zeros((8, _TOEP - KB), F32)], axis=1)
        z = z + (slab if t == 0 else pltpu.roll(slab, _TOEP - 8 * t, axis=1))
    e = z
    for b in range(1, 8):
        e = jnp.where(row8 == b, pltpu.roll(z, _TOEP - b, axis=1), e)
    return e


def _attn_specs(nqb):
    qspec = pl.BlockSpec((QB, LW), lambda p, j: (jnp.minimum(j, nqb - 1), p))
    kspecs = [pl.BlockSpec((QB, LW), functools.partial(lambda p, j, t: (jnp.minimum(j, nqb - 1) + t, p), t=t))
              for t in range(KB // QB)]
    dspec = pl.BlockSpec((None, HPS, _TOEP), lambda p, j: (p, 0, 0))
    return qspec, kspecs, dspec


V7X_MXU = 256
HPT = V7X_MXU // HD


def _head_masks():
    head = lax.broadcasted_iota(jnp.int32, (1, V7X_MXU), 1) // HD
    return [head == t for t in range(HPT)]


def _tile_of(a):
    return slice((a // HPT) * V7X_MXU, (a // HPT + 1) * V7X_MXU)


def _pick_heads(parts, masks):
    tiles = []
    for g in range(HPS // HPT):
        out = parts[g * HPT + HPT - 1]
        for t in range(HPT - 2, -1, -1):
            out = jnp.where(masks[t], parts[g * HPT + t], out)
        tiles.append(out)
    return tiles[0] if len(tiles) == 1 else jnp.concatenate(tiles, axis=1)


def _attn_fwd(q, kpad, vpad, diag):
    s = q.shape[0]
    nqb = s // QB
    npad = PADK // QB

    def body(q_ref, k0, k1, k2, v0, v1, v2, diag_ref, o_ref, bias_ref, sc_ref, eb_ref, rl_ref):
        j = pl.program_id(1)

        @pl.when(j == 0)
        def _():
            _bias_from_diag(diag_ref, bias_ref)

        def block(masked):
            kcat = jnp.concatenate([k0[...], k1[...], k2[...]], axis=0)
            vcat = jnp.concatenate([v0[...], v1[...], v2[...]], axis=0)
            q2 = q_ref[...]
            masks = _head_masks()
            valid = (lax.broadcasted_iota(jnp.int32, (1, KB), 1) + j * QB >= PADK) if masked else None
            outs = []
            for a in range(HPS):
                qa = q2[:, _tile_of(a)]
                sc_ref[a] = lax.dot_general(jnp.where(masks[a % HPT], qa, jnp.zeros_like(qa)), kcat[:, _tile_of(a)],
                                            _NT, preferred_element_type=F32)
            for a in range(HPS):
                for c in range(QB // ATT_RC):
                    r = slice(c * ATT_RC, (c + 1) * ATT_RC)
                    sc = sc_ref[a, r, :] + bias_ref[a, r, :]
                    if masked:
                        sc = jnp.where(valid, sc, NEG)
                    e = jnp.exp(sc - jnp.max(sc, axis=-1, keepdims=True))
                    eb_ref[a, r, :] = e.astype(BF16)
                    rl_ref[a, r, :] = jnp.broadcast_to(1.0 / jnp.sum(e, axis=-1, keepdims=True),
                                                       (ATT_RC, V7X_MXU))
                outs.append(jnp.dot(eb_ref[a], vcat[:, _tile_of(a)], preferred_element_type=F32) * rl_ref[a])
            o_ref[...] = _pick_heads(outs, masks).astype(BF16)

        pl.when(j < npad)(functools.partial(block, True))
        pl.when(j >= npad)(functools.partial(block, False))

    qspec, kspecs, dspec = _attn_specs(nqb)
    assert len(kspecs) == 3
    return pl.pallas_call(
        body, name="attn_fwd", grid=(D // LW, nqb),
        in_specs=[qspec] + kspecs + kspecs + [dspec],
        out_specs=qspec,
        out_shape=jax.ShapeDtypeStruct((s, D), BF16),
        scratch_shapes=[pltpu.VMEM((HPS, QB, KB), F32), pltpu.VMEM((HPS, QB, KB), F32),
                        pltpu.VMEM((HPS, QB, KB), BF16), pltpu.VMEM((HPS, QB, V7X_MXU), F32)],
        compiler_params=_cp(("parallel", "arbitrary")),
    )(q, kpad, kpad, kpad, vpad, vpad, vpad, diag)


def _rms_bwd(dn, xhat, r, g):
    dng = dn * g
    return r * (dng - xhat * jnp.mean(dng * xhat, axis=-1, keepdims=True))


def _acc_spec():
    return pl.BlockSpec((8, D), lambda i: (0, 0))


def _mlp_bwd(gout, hin, up, g, wu, wd, name):
    s = gout.shape[0]
    fj = 512

    def body(go_ref, hin_ref, up_ref, g_ref, wu_ref, wd_ref, gin_ref, dup_ref, n_ref, dg_ref, gob_ref, dn_ref):
        @pl.when(pl.program_id(0) == 0)
        def _():
            dg_ref[...] = jnp.zeros_like(dg_ref)

        gob_ref[...] = go_ref[...].astype(BF16)
        for j in range(FF // fj):
            sl = slice(j * fj, (j + 1) * fj)
            rl = jnp.maximum(up_ref[:, sl].astype(F32), 0.0)
            dact = lax.dot_general(gob_ref[...], wd_ref[sl, :], _NT, preferred_element_type=F32)
            dupj = (dact * (2.0 * rl)).astype(BF16)
            dup_ref[:, sl] = dupj
            part = lax.dot_general(dupj, wu_ref[j], _NT, preferred_element_type=F32)
            if j == 0:
                dn_ref[...] = part
            else:
                dn_ref[...] += part
        hin = hin_ref[...]
        r = _rstd(hin)
        xhat = hin * r
        n_ref[...] = (xhat * g_ref[...]).astype(BF16)
        dn = dn_ref[...]
        gin_ref[...] = go_ref[...] + _rms_bwd(dn, xhat, r, g_ref[...])
        dg_ref[...] += _sum8(dn * xhat)

    row = _rows(TMM, D)
    wide = _rows(TMM, FF)
    return pl.pallas_call(
        body, name=name, grid=(s // TMM,),
        in_specs=[row, row, wide, _full((1, D)), _full((NDEV, D, fj)), _full((FF, D))],
        out_specs=[row, wide, row, _acc_spec()],
        out_shape=[jax.ShapeDtypeStruct((s, D), F32), jax.ShapeDtypeStruct((s, FF), BF16),
                   jax.ShapeDtypeStruct((s, D), BF16), jax.ShapeDtypeStruct((8, D), F32)],
        scratch_shapes=[pltpu.VMEM((TMM, D), BF16), pltpu.VMEM((TMM, D), F32)],
        compiler_params=_cp(("arbitrary",)),
    )(gout, hin, up, g, wu, wd)


def _matmul_tn(a, b, slab, out_dtype, name, b_relu2=False):
    s, m = a.shape
    n = b.shape[1]
    ts = min(s, 512 if n > 2048 else 1024)
    nk = s // ts
    nc = 512
    w = n // NDEV

    def body(a_ref, b_ref, o_ref, at_ref, acc_ref):
        k = pl.program_id(0)
        at_ref[...] = a_ref[...].astype(BF16).T

        @pl.when(k == 0)
        def _():
            acc_ref[...] = jnp.zeros_like(acc_ref)

        for c in range(n // nc):
            sl = slice(c * nc, (c + 1) * nc)
            bc = b_ref[:, sl]
            if b_relu2:
                rl = jnp.maximum(bc.astype(F32), 0.0)
                bc = rl * rl
            acc_ref[:, sl] += jnp.dot(at_ref[...], bc.astype(BF16), preferred_element_type=F32)

        @pl.when(k == nk - 1)
        def _():
            if slab:
                for d in range(NDEV):
                    o_ref[d] = acc_ref[:, d * w:(d + 1) * w].astype(out_dtype)
            else:
                o_ref[...] = acc_ref[...].astype(out_dtype)

    if slab:
        out_shape = jax.ShapeDtypeStruct((NDEV, m, w), out_dtype)
        out_spec = pl.BlockSpec((NDEV, m, w), lambda k: (0, 0, 0), pipeline_mode=pl.Buffered(1))
    else:
        out_shape = jax.ShapeDtypeStruct((m, n), out_dtype)
        out_spec = pl.BlockSpec((m, n), lambda k: (0, 0), pipeline_mode=pl.Buffered(1))
    return pl.pallas_call(
        body, name=name, grid=(nk,),
        in_specs=[pl.BlockSpec((ts, m), lambda k: (k, 0)), pl.BlockSpec((ts, n), lambda k: (k, 0))],
        out_specs=out_spec, out_shape=out_shape,
        scratch_shapes=[pltpu.VMEM((m, ts), BF16), pltpu.VMEM((m, n), F32)],
        compiler_params=_cp(("arbitrary",)),
    )(a, b)


def _matmul_nt(x, w, name):
    s, n = x.shape
    k = w.shape[0]

    def body(x_ref, w_ref, o_ref):
        o_ref[...] = lax.dot_general(x_ref[...].astype(BF16), w_ref[...], _NT,
                                     preferred_element_type=F32).astype(BF16)

    return pl.pallas_call(
        body, name=name, grid=(s // TM,),
        in_specs=[_rows(TM, n), _full((k, n))], out_specs=_rows(TM, k),
        out_shape=jax.ShapeDtypeStruct((s, k), BF16),
        compiler_params=_cp(("parallel",)),
    )(x, w)


def _attn_bwd(q, kpad, vpad, do, diag):
    s = q.shape[0]
    nqb = s // QB
    npad = PADK // QB

    def body(q_ref, k0, k1, k2, v0, v1, v2, do_ref, diag_ref, dq_ref, dk_ref, dv_ref, dd_ref,
             bias_ref, db_ref, dka_ref, dva_ref, sc_ref, dp_ref, dsb_ref, pb_ref):
        j = pl.program_id(1)

        @pl.when(j == 0)
        def _():
            _bias_from_diag(diag_ref, bias_ref)
            dka_ref[...] = jnp.zeros_like(dka_ref)
            dva_ref[...] = jnp.zeros_like(dva_ref)
            db_ref[...] = jnp.zeros_like(db_ref)

        def block(masked):
            kcat = jnp.concatenate([k0[...], k1[...], k2[...]], axis=0)
            vcat = jnp.concatenate([v0[...], v1[...], v2[...]], axis=0)
            q2 = q_ref[...]
            do2 = do_ref[...]
            masks = _head_masks()
            valid = (lax.broadcasted_iota(jnp.int32, (1, KB), 1) + j * QB >= PADK) if masked else None
            qt = q2.T
            dot_ = do2.T
            dq = []
            for a in range(HPS):
                qa, doa = q2[:, _tile_of(a)], do2[:, _tile_of(a)]
                sc_ref[a] = lax.dot_general(jnp.where(masks[a % HPT], qa, jnp.zeros_like(qa)), kcat[:, _tile_of(a)],
                                            _NT, preferred_element_type=F32)
                dp_ref[a] = lax.dot_general(jnp.where(masks[a % HPT], doa, jnp.zeros_like(doa)),
                                            vcat[:, _tile_of(a)], _NT, preferred_element_type=F32)
            for a in range(HPS):
                for c in range(QB // ATT_RC):
                    r = slice(c * ATT_RC, (c + 1) * ATT_RC)
                    sc = sc_ref[a, r, :] + bias_ref[a, r, :]
                    if masked:
                        sc = jnp.where(valid, sc, NEG)
                    e = jnp.exp(sc - jnp.max(sc, axis=-1, keepdims=True))
                    p = e * (1.0 / jnp.sum(e, axis=-1, keepdims=True))
                    dp = dp_ref[a, r, :]
                    ds = p * (dp - jnp.sum(p * dp, axis=-1, keepdims=True))
                    db_ref[a, r, :] += ds
                    dsb_ref[a, r, :] = ds.astype(BF16)
                    pb_ref[a, r, :] = p.astype(BF16)
                hd = slice(a * HD, (a + 1) * HD)
                dq.append(jnp.dot(dsb_ref[a], kcat[:, _tile_of(a)], preferred_element_type=F32))
                dka_ref[hd, :] += jnp.dot(qt[hd, :], dsb_ref[a], preferred_element_type=F32)
                dva_ref[hd, :] += jnp.dot(dot_[hd, :], pb_ref[a], preferred_element_type=F32)
            dq_ref[...] = _pick_heads(dq, masks) * (HD ** -0.5)

        pl.when(j < npad)(functools.partial(block, True))
        pl.when(jnp.logical_and(j >= npad, j < nqb))(functools.partial(block, False))

        @pl.when(j == nqb - 1)
        def _():
            for a in range(HPS):
                dd_ref[a] = _diag_sums(db_ref, a)

        dk_ref[...] = dka_ref[:, 0:QB].T
        dv_ref[...] = dva_ref[:, 0:QB].T
        dka_ref[:, 0:KB - QB] = dka_ref[:, QB:KB]
        dva_ref[:, 0:KB - QB] = dva_ref[:, QB:KB]
        dka_ref[:, KB - QB:KB] = jnp.zeros((LW, QB), F32)
        dva_ref[:, KB - QB:KB] = jnp.zeros((LW, QB), F32)

    qspec, kspecs, dspec = _attn_specs(nqb)
    kout = pl.BlockSpec((QB, LW), lambda p, j: (jnp.maximum(j - npad, 0), p))
    sd = jax.ShapeDtypeStruct((s, D), F32)
    return pl.pallas_call(
        body, name="attn_bwd", grid=(D // LW, nqb + npad),
        in_specs=[qspec] + kspecs + kspecs + [qspec, dspec],
        out_specs=[qspec, kout, kout, pl.BlockSpec((None, HPS, 8, _TOEP), lambda p, j: (p, 0, 0, 0))],
        out_shape=[sd, sd, sd, jax.ShapeDtypeStruct((NH // HPS, HPS, 8, _TOEP), F32)],
        scratch_shapes=[pltpu.VMEM((HPS, QB, KB), F32), pltpu.VMEM((HPS, QB, KB), F32),
                        pltpu.VMEM((LW, KB), F32), pltpu.VMEM((LW, KB), F32),
                        pltpu.VMEM((HPS, QB, KB), F32), pltpu.VMEM((HPS, QB, KB), F32),
                        pltpu.VMEM((HPS, QB, KB), BF16), pltpu.VMEM((HPS, QB, KB), BF16)],
        compiler_params=_cp(("parallel", "arbitrary")),
    )(q, kpad, kpad, kpad, vpad, vpad, vpad, do, diag)


def _head_norm_bwd(dy2, x2, g2, lo):
    rr = _head_rstd(x2, lo)
    xhat = x2 * rr
    t = dy2 * g2 * xhat
    m_lo = jnp.sum(jnp.where(lo, t, 0.0), axis=-1, keepdims=True)
    m_hi = jnp.sum(jnp.where(lo, 0.0, t), axis=-1, keepdims=True)
    m = jnp.where(lo, m_lo, m_hi) * (1.0 / HD)
    return rr * (dy2 * g2 - xhat * m), dy2 * xhat


def _kvq_bwd(dq, dk, dv, qraw, kraw, h2, g3, kv_norm, b_norm, w_kv, w_q, k_norm_t, q_norm_t):
    s = h2.shape[0]

    def body(dq_ref, dk_ref, dv_ref, qraw_ref, kraw_ref, h_ref, g3_ref, gkv_ref, gb_ref, wkv_ref, wq_ref,
             kn_ref, qn_ref, g2_ref, dqr_ref, dkv_ref, nb_ref, nk_ref, dgq_ref, dgk_ref, dgb_ref, dgkv_ref):
        @pl.when(pl.program_id(0) == 0)
        def _():
            for r in (dgq_ref, dgk_ref, dgb_ref, dgkv_ref):
                r[...] = jnp.zeros_like(r)

        lo = lax.broadcasted_iota(jnp.int32, (1, 128), 1) < HD
        for p in range(D // 128):
            sl = slice(p * 128, (p + 1) * 128)
            dx, dgp = _head_norm_bwd(dq_ref[:, sl], qraw_ref[:, sl], qn_ref[:, sl], lo)
            dqr_ref[:, sl] = dx.astype(BF16)
            dgq_ref[:, sl] += _sum8(dgp)
            dx, dgp = _head_norm_bwd(dk_ref[:, sl], kraw_ref[:, sl], kn_ref[:, sl], lo)
            dkv_ref[:, sl] = dx.astype(BF16)
            dgk_ref[:, sl] += _sum8(dgp)
        dkv_ref[:, D:] = dv_ref[...].astype(BF16)
        dnb = lax.dot_general(dqr_ref[...], wq_ref[...], _NT, preferred_element_type=F32)
        cw = 2 * D // NDEV
        dnk = lax.dot_general(dkv_ref[:, 0:cw], wkv_ref[0], _NT, preferred_element_type=F32)
        for d in range(1, NDEV):
            dnk = dnk + lax.dot_general(dkv_ref[:, d * cw:(d + 1) * cw], wkv_ref[d], _NT, preferred_element_type=F32)
        h = h_ref[...]
        r = _rstd(h)
        xhat = h * r
        dxg = dnb * gb_ref[...] + dnk * gkv_ref[...]
        g2_ref[...] = g3_ref[...] + r * (dxg - xhat * jnp.mean(dxg * xhat, axis=-1, keepdims=True))
        dgb_ref[...] += _sum8(dnb * xhat)
        dgkv_ref[...] += _sum8(dnk * xhat)
        nb_ref[...] = (xhat * gb_ref[...]).astype(BF16)
        nk_ref[...] = (xhat * gkv_ref[...]).astype(BF16)

    row = _rows(TM, D)
    sd = jax.ShapeDtypeStruct((s, D), BF16)
    acc = jax.ShapeDtypeStruct((8, D), F32)
    return pl.pallas_call(
        body, name="kvq_bwd", grid=(s // TM,),
        in_specs=[row] * 7 + [_full((1, D)), _full((1, D)), _full((NDEV, D, 2 * D // NDEV)), _full((D, D)),
                              _full((1, D)), _full((1, D))],
        out_specs=[row, row, _rows(TM, 2 * D), row, row] + [_acc_spec()] * 4,
        out_shape=[jax.ShapeDtypeStruct((s, D), F32), sd, jax.ShapeDtypeStruct((s, 2 * D), BF16), sd, sd,
                   acc, acc, acc, acc],
        compiler_params=_cp(("arbitrary",)),
    )(dq, dk, dv, qraw, kraw, h2, g3, kv_norm, b_norm, w_kv, w_q, k_norm_t, q_norm_t)


def _lru_bwd(g1, gate, rec, hs, w_out, conv_w, conv_b, wg, bg, lam):
    s = g1.shape[0]
    nt = s // TL

    def body(g1_ref, gate_ref, rec_ref, recp_ref, hs_ref, hsp_ref, wo_ref, cw_ref, cb_ref, wg_ref, bg_ref, lam_ref,
             du_ref, dcw_ref, dcb_ref, dwg_ref, dbg_ref, dlam_ref, ext_ref, dext_ref, cg_ref,
             a_sc, dh_sc, hl_sc, ac_sc, rg_sc, ig_sc, mult_sc, rc_sc):
        i = pl.program_id(0)
        first_tile = i == nt - 1

        @pl.when(i == 0)
        def _():
            dext_ref[TL:TL + 8, :] = jnp.zeros((8, D), F32)
            cg_ref[...] = jnp.zeros_like(cg_ref)
            for r in (dcw_ref, dcb_ref, dwg_ref, dbg_ref, dlam_ref):
                r[...] = jnp.zeros_like(r)

        keep = jnp.where(first_tile, 0.0, 1.0)
        ext_ref[0:8, :] = recp_ref[...] * keep
        ext_ref[8:8 + TL, :] = rec_ref[...]
        dy = lax.dot_general(g1_ref[...].astype(BF16), wo_ref[...], _NT, preferred_element_type=F32)
        lam_v = lam_ref[...]
        sp = _softplus_neg(lam_v)
        dsp_dlam = -_sigmoid(-lam_v)
        rows = lax.broadcasted_iota(jnp.int32, (TL, BW), 0)
        for n in range(NBLK):
            sl = slice(n * BW, (n + 1) * BW)
            rc = _conv(ext_ref, cw_ref, cb_ref, sl, TL)
            rg, ig, a, mult = _lru_gates(rc, wg_ref[n], bg_ref[n:n + 1, :], sp[:, sl])
            h = hs_ref[:, sl]
            gt = gate_ref[:, sl]
            dyn = dy[:, sl]
            du_ref[:, sl] = (dyn * h * _gelu_grad(gt)).astype(BF16)
            dh_sc[n] = dyn * _gelu(gt) + jnp.where(rows == TL - 1, cg_ref[0:1, sl], 0.0)
            a_sc[n], rg_sc[n], ig_sc[n], mult_sc[n], rc_sc[n] = a, rg, ig, mult, rc
            ac_sc[n] = _shift_up(a, 1, 0.0, rows, TL)
        _tile_scan(ac_sc, dh_sc, hl_sc, [jnp.zeros((1, BW), F32)] * NBLK, reverse=True)
        for n in range(NBLK):
            sl = slice(n * BW, (n + 1) * BW)
            gsc = hl_sc[n]
            a, rg, ig, mult, rc = a_sc[n], rg_sc[n], ig_sc[n], mult_sc[n], rc_sc[n]
            cg_ref[0:1, sl] = a[0:1, :] * gsc[0:1, :]
            hprev = _shift_down(hs_ref[:, sl], 1, hsp_ref[7:8, sl] * keep, rows)
            da = gsc * hprev
            d_mult = gsc * ig * rc
            d_ig = gsc * mult * rc
            d_rc = gsc * mult * ig
            d_la = da * a - d_mult * (a * a) / mult
            d_rg = d_la * ((-LRU_C) * sp[:, sl])
            dlam_ref[:, sl] += _sum8(d_la * ((-LRU_C) * rg)) * dsp_dlam[:, sl]
            dg = jnp.concatenate([d_rg * rg * (1.0 - rg), d_ig * ig * (1.0 - ig)], axis=1)
            dgb = dg.astype(BF16)
            d_rc = d_rc + lax.dot_general(dgb, wg_ref[n], _NT, preferred_element_type=F32)
            dwg_ref[n] += lax.dot_general(rc.astype(BF16), dgb, _TN, preferred_element_type=F32)
            dbg_ref[n] += _sum8(dg)
            dext_ref[0:TL, sl] = d_rc
            dcb_ref[:, sl] += _sum8(d_rc)
            for k in range(4):
                dcw_ref[k, :, sl] += _sum8(d_rc * ext_ref[5 + k:5 + k + TL, sl])
        for k in range(4):
            part = cw_ref[3 - k:4 - k, :] * dext_ref[k:k + TL, :]
            acc = part if k == 0 else acc + part
        du_ref[:, D:] = acc.astype(BF16)
        dext_ref[TL:TL + 8, :] = dext_ref[0:8, :]

    rev = pl.BlockSpec((TL, D), lambda i: (nt - 1 - i, 0))
    rev8 = pl.BlockSpec((8, D), lambda i: (jnp.maximum((nt - 1 - i) * (TL // 8) - 1, 0), 0))
    acc = jax.ShapeDtypeStruct((8, D), F32)
    return pl.pallas_call(
        body, name="lru_bwd", grid=(nt,),
        in_specs=[rev, rev, rev, rev8, rev, rev8, _full((D, D)), _full((4, D)), _full((1, D)),
                  _full((NBLK, BW, 2 * BW)), _full((NBLK, 2 * BW)), _full((1, D))],
        out_specs=[pl.BlockSpec((TL, 2 * D), lambda i: (nt - 1 - i, 0)),
                   pl.BlockSpec((4, 8, D), lambda i: (0, 0, 0)), _acc_spec(),
                   pl.BlockSpec((NBLK, BW, 2 * BW), lambda i: (0, 0, 0)),
                   pl.BlockSpec((NBLK, 8, 2 * BW), lambda i: (0, 0, 0)), _acc_spec()],
        out_shape=[jax.ShapeDtypeStruct((s, 2 * D), BF16), jax.ShapeDtypeStruct((4, 8, D), F32), acc,
                   jax.ShapeDtypeStruct((NBLK, BW, 2 * BW), F32), jax.ShapeDtypeStruct((NBLK, 8, 2 * BW), F32), acc],
        scratch_shapes=[pltpu.VMEM((TL + 8, D), F32), pltpu.VMEM((TL + 8, D), F32), pltpu.VMEM((8, D), F32)]
        + [pltpu.VMEM((NBLK, TL, BW), F32)] * 8,
        compiler_params=_cp(("arbitrary",)),
    )(g1, gate, rec, rec, hs, hs, w_out, conv_w, conv_b, wg, bg, lam)


def _a_in_bwd(du, h0, g1, a_norm, w_in):
    s = h0.shape[0]

    def body(du_ref, h_ref, g1_ref, an_ref, win_ref, gx_ref, n1_ref, dg_ref):
        @pl.when(pl.program_id(0) == 0)
        def _():
            dg_ref[...] = jnp.zeros_like(dg_ref)

        cw = 2 * D // NDEV
        dn = lax.dot_general(du_ref[:, 0:cw], win_ref[0], _NT, preferred_element_type=F32)
        for d in range(1, NDEV):
            dn = dn + lax.dot_general(du_ref[:, d * cw:(d + 1) * cw], win_ref[d], _NT, preferred_element_type=F32)
        h = h_ref[...]
        r = _rstd(h)
        xhat = h * r
        gx_ref[...] = g1_ref[...] + _rms_bwd(dn, xhat, r, an_ref[...])
        n1_ref[...] = (xhat * an_ref[...]).astype(BF16)
        dg_ref[...] += _sum8(dn * xhat)

    row = _rows(TM, D)
    return pl.pallas_call(
        body, name="a_in_bwd", grid=(s // TM,),
        in_specs=[_rows(TM, 2 * D), row, row, _full((1, D)), _full((NDEV, D, 2 * D // NDEV))],
        out_specs=[row, row, _acc_spec()],
        out_shape=[jax.ShapeDtypeStruct((s, D), F32), jax.ShapeDtypeStruct((s, D), BF16),
                   jax.ShapeDtypeStruct((8, D), F32)],
        compiler_params=_cp(("arbitrary",)),
    )(du, h0, g1, a_norm, w_in)


def _rel_onehot():
    m = np.arange(_TOEP)
    signed = np.where(m < KB, m, m - _TOEP)
    idx = np.clip(PADK - signed, -(CHUNK - 1), 2 * CHUNK) + (CHUNK - 1)
    return (idx[None, :] == np.arange(NREL)[:, None]).astype(np.float32)


def _bias_diagonals(rel_bias):
    diag = jnp.dot(rel_bias, jnp.asarray(_rel_onehot()), precision=lax.Precision.HIGHEST)
    return diag.reshape(NH // HPS, HPS, _TOEP)


def _rel_bias_grad(dd):
    rows = 8
    z = dd
    oh = np.zeros((_TOEP, 256), np.float32)
    oh[:, :NREL] = _rel_onehot().T

    def body(z_ref, oh_ref, o_ref):
        d = jnp.sum(z_ref[...], axis=0, keepdims=True)
        hi = d.astype(BF16)
        mid = (d - hi.astype(F32)).astype(BF16)
        lo = (d - hi.astype(F32) - mid.astype(F32)).astype(BF16)
        ohb = oh_ref[...].astype(BF16)
        acc = jnp.zeros((8, 256), F32)
        for piece in (lo, mid, hi):
            acc = acc + jnp.dot(jnp.broadcast_to(piece, (8, _TOEP)), ohb, preferred_element_type=F32)
        o_ref[...] = acc

    out = pl.pallas_call(
        body, name="rel_bias_grad", grid=(NH,),
        in_specs=[pl.BlockSpec((None, rows, _TOEP), lambda h: (h, 0, 0)), pl.BlockSpec((_TOEP, 256), lambda h: (0, 0))],
        out_specs=pl.BlockSpec((None, 8, 256), lambda h: (h, 0, 0)),
        out_shape=jax.ShapeDtypeStruct((NH, 8, 256), F32),
        compiler_params=_cp(("parallel",)),
    )(z, jnp.asarray(oh))
    return out[:, 0, :NREL]


def _exchange(arrays, scatter, name):
    n = len(arrays)

    def body(*refs):
        ins, outs = refs[:n], refs[n:2 * n]
        token, (send_sems, recv_sems, local_sems) = refs[2 * n], refs[2 * n + 1:]
        token[...] = jnp.zeros_like(token)
        x, y, c = lax.axis_index("x"), lax.axis_index("y"), lax.axis_index("c")
        me = 4 * x + 2 * y + c

        def peer_of(r):
            rx, ry, rc = (r >> 2) & 1, (r >> 1) & 1, r & 1
            px = 1 - x if rx else x
            py = 1 - y if ry else y
            pc = 1 - c if rc else c
            return (px, py, pc), 4 * px + 2 * py + pc

        local, sent = [], []
        for k in range(n):
            cp = pltpu.make_async_copy(ins[k].at[me] if scatter else ins[k], outs[k].at[me], local_sems.at[k])
            cp.start()
            local.append(cp)
            for r in range(1, NDEV):
                peer, peer_lin = peer_of(r)
                cp = pltpu.make_async_remote_copy(
                    src_ref=ins[k].at[peer_lin] if scatter else ins[k], dst_ref=outs[k].at[me],
                    send_sem=send_sems.at[k, r - 1], recv_sem=recv_sems.at[k, r - 1],
                    device_id=peer, device_id_type=pl.DeviceIdType.MESH)
                cp.start()
                sent.append(cp)
        for k in range(n):
            for r in range(1, NDEV):
                peer, peer_lin = peer_of(r)
                pltpu.make_async_remote_copy(
                    src_ref=ins[k].at[peer_lin] if scatter else ins[k], dst_ref=outs[k].at[peer_lin],
                    send_sem=send_sems.at[k, r - 1], recv_sem=recv_sems.at[k, r - 1],
                    device_id=peer, device_id_type=pl.DeviceIdType.MESH).wait_recv()
        for cp in sent:
            cp.wait_send()
        for cp in local:
            cp.wait()

    def slot_shape(a):
        return (NDEV,) + (a.shape[1:] if scatter else a.shape)

    anyspec = pl.BlockSpec(memory_space=pl.ANY)
    outs = pl.pallas_call(
        body, name=name,
        in_specs=[anyspec] * n, out_specs=[anyspec] * n + [pl.BlockSpec(memory_space=pltpu.VMEM)],
        out_shape=[jax.ShapeDtypeStruct(slot_shape(a), a.dtype) for a in arrays]
        + [jax.ShapeDtypeStruct((8, 128), F32)],
        scratch_shapes=[pltpu.SemaphoreType.DMA((n, NDEV - 1)), pltpu.SemaphoreType.DMA((n, NDEV - 1)),
                        pltpu.SemaphoreType.DMA((n,))],
        compiler_params=pltpu.CompilerParams(has_side_effects=True),
    )(*arrays)
    return outs[:n], outs[n]


def _peer(r):
    x, y, c = lax.axis_index("x"), lax.axis_index("y"), lax.axis_index("c")
    px = 1 - x if (r >> 2) & 1 else x
    py = 1 - y if (r >> 1) & 1 else y
    pc = 1 - c if r & 1 else c
    return (px, py, pc), 4 * px + 2 * py + pc


def _my_index():
    return 4 * lax.axis_index("x") + 2 * lax.axis_index("y") + lax.axis_index("c")


_HBM_SPEC = pl.BlockSpec(memory_space=pltpu.HBM)
_SEM_SPEC = pl.BlockSpec(memory_space=pltpu.SEMAPHORE)


_NPEER = NDEV - 1


def _exchange_start(arrays, scatter, name):
    n = len(arrays)
    ns = n * _NPEER
    slots = [(NDEV,) + (a.shape[1:] if scatter else a.shape) for a in arrays]

    def body(*refs):
        srcs, lands = refs[:n], refs[n:2 * n]
        send_sems, recv_sems = refs[2 * n:2 * n + ns], refs[2 * n + ns:2 * n + 2 * ns]
        token = refs[-1]
        me = _my_index()
        for k in range(n):
            for r in range(1, NDEV):
                peer, peer_lin = _peer(r)
                pltpu.make_async_remote_copy(
                    src_ref=srcs[k].at[peer_lin] if scatter else srcs[k], dst_ref=lands[k].at[me],
                    send_sem=send_sems[k * _NPEER + r - 1], recv_sem=recv_sems[k * _NPEER + r - 1],
                    device_id=peer, device_id_type=pl.DeviceIdType.MESH).start()
        token[...] = jnp.zeros_like(token)

    sem = pltpu.SemaphoreType.DMA(())
    outs = pl.pallas_call(
        body, name=name,
        out_shape=(*[sem] * (2 * ns), *[pltpu.HBM(a.shape, a.dtype) for a in arrays],
                   *[pltpu.HBM(s, a.dtype) for s, a in zip(slots, arrays)], jax.ShapeDtypeStruct((8, 128), F32)),
        in_specs=[_HBM_SPEC] * (2 * n),
        out_specs=(*[_SEM_SPEC] * (2 * ns), *[_HBM_SPEC] * (2 * n), pl.BlockSpec(memory_space=pltpu.VMEM)),
        input_output_aliases={k: 2 * ns + k for k in range(2 * n)},
        compiler_params=pltpu.CompilerParams(has_side_effects=pltpu.SideEffectType.DATAFLOW_SIDE_EFFECTING),
    )(*[pltpu.with_memory_space_constraint(a, pltpu.HBM) for a in arrays],
      *[pltpu.with_memory_space_constraint(lax.empty(s, a.dtype), pltpu.HBM) for s, a in zip(slots, arrays)])
    return outs[:ns], outs[ns:2 * ns], outs[2 * ns:2 * ns + n], outs[2 * ns + n:2 * ns + 2 * n], outs[-1]


def _exchange_wait(started, after, scatter, name):
    send_sems, recv_sems, srcs, lands, _ = started
    n = len(srcs)
    ns = n * _NPEER

    def body(*refs):
        src_refs, land_refs = refs[:n], refs[n:2 * n]
        ssem, rsem = refs[2 * n:2 * n + ns], refs[2 * n + ns:2 * n + 2 * ns]
        for k in range(n):
            for r in range(1, NDEV):
                peer, peer_lin = _peer(r)
                cp = pltpu.make_async_remote_copy(
                    src_ref=src_refs[k].at[peer_lin] if scatter else src_refs[k], dst_ref=land_refs[k].at[peer_lin],
                    send_sem=ssem[k * _NPEER + r - 1], recv_sem=rsem[k * _NPEER + r - 1],
                    device_id=peer, device_id_type=pl.DeviceIdType.MESH)
                cp.wait_send()
                cp.wait_recv()

    outs = pl.pallas_call(
        body, name=name,
        out_shape=tuple(pltpu.HBM(a.shape, a.dtype) for a in list(srcs) + list(lands)),
        in_specs=[_HBM_SPEC] * (2 * n) + [_SEM_SPEC] * (2 * ns) + [pl.BlockSpec(memory_space=pl.ANY)],
        out_specs=tuple([_HBM_SPEC] * (2 * n)),
        input_output_aliases={k: k for k in range(2 * n)},
        compiler_params=pltpu.CompilerParams(has_side_effects=pltpu.SideEffectType.DATAFLOW_SIDE_EFFECTING),
    )(*srcs, *lands, *send_sems, *recv_sems, after)
    return list(outs[:n]), list(outs[n:])


def _fill_own(lands, owns, me):
    return [lax.dynamic_update_slice(z, o, (me,) + (0,) * (z.ndim - 1)) for z, o in zip(lands, owns)]


def _sum_slots(st, name):
    _, r, c = st.shape

    def body(s_ref, o_ref):
        acc = s_ref[0]
        for d in range(1, NDEV):
            acc = acc + s_ref[d]
        o_ref[...] = acc

    return pl.pallas_call(
        body, name=name, out_shape=jax.ShapeDtypeStruct((r, c), F32),
        in_specs=[pl.BlockSpec((NDEV, r, c), lambda: (0, 0, 0))], out_specs=pl.BlockSpec((r, c), lambda: (0, 0)),
    )(st)


def _adamw(w, m, v, gst, name, transposed=False):
    r, c = w.shape
    ns = gst.shape[0]
    tr = min(r, 256)
    c1 = 1.0 - ADAM_B1 ** ADAM_STEP
    c2 = 1.0 - ADAM_B2 ** ADAM_STEP

    def body(w_ref, m_ref, v_ref, g_ref, go_ref, d_ref, mo_ref, vo_ref):
        g = g_ref[0].astype(F32)
        for d in range(1, ns):
            g = g + g_ref[d].astype(F32)
        if transposed:
            g = g.T
        m2 = ADAM_B1 * m_ref[...] + (1.0 - ADAM_B1) * g
        v2 = ADAM_B2 * v_ref[...] + (1.0 - ADAM_B2) * (g * g)
        go_ref[...] = g
        mo_ref[...] = m2
        vo_ref[...] = v2
        d_ref[...] = (-ADAM_LR) * ((m2 / c1) / (jnp.sqrt(v2 / c2) + ADAM_EPS) + ADAM_WD * w_ref[...])

    blk = pl.BlockSpec((tr, c), lambda i: (i, 0))
    sd = jax.ShapeDtypeStruct((r, c), F32)
    return pl.pallas_call(
        body, name=name, grid=(r // tr,),
        in_specs=[blk, blk, blk, pl.BlockSpec((ns, c, tr), lambda i: (0, 0, i)) if transposed
                  else pl.BlockSpec((ns, tr, c), lambda i: (0, i, 0))],
        out_specs=[blk, blk, blk, blk], out_shape=[sd, sd, sd, sd],
        compiler_params=_cp(("parallel",)),
    )(w, m, v, gst)


def _pack(pieces, rows):
    flat = jnp.concatenate([p.reshape(-1).astype(F32) for p in pieces])
    return jnp.pad(flat, (0, rows * 128 - flat.shape[0])).reshape(rows, 128)


def _unpack(flat, shapes):
    out, off = [], 0
    for shp in shapes:
        size = int(np.prod(shp))
        out.append(flat[off:off + size].reshape(shp))
        off += size
    return out


def _cols(full, me, width):
    return lax.dynamic_slice_in_dim(full, me * width, width, axis=full.ndim - 1)


def kernel(x, a_norm, a_w_in, a_conv_w, a_conv_b, a_w_gate, a_b_gate, a_lambda, a_w_out, kv_norm, w_kv, k_norm, b_norm, b_w_q, b_q_norm, b_rel_bias, b_w_o, mlp_norm, w_up, w_down, loss_target, m_a_norm, m_a_w_in, m_a_conv_w, m_a_conv_b, m_a_w_gate, m_a_b_gate, m_a_lambda, m_a_w_out, m_kv_norm, m_w_kv, m_k_norm, m_b_norm, m_b_w_q, m_b_q_norm, m_b_rel_bias, m_b_w_o, m_mlp_norm, m_w_up, m_w_down, v_a_norm, v_a_w_in, v_a_conv_w, v_a_conv_b, v_a_w_gate, v_a_b_gate, v_a_lambda, v_a_w_out, v_kv_norm, v_w_kv, v_k_norm, v_b_norm, v_b_w_q, v_b_q_norm, v_b_rel_bias, v_b_w_o, v_mlp_norm, v_w_up, v_w_down):
    me = 4 * lax.axis_index("x") + 2 * lax.axis_index("y") + lax.axis_index("c")
    sh = D // NDEV

    big_w = [a_w_in[0], a_w_out[0], w_kv, b_w_q[0], b_w_o[0], w_up[0], w_up[1], w_down[0], w_down[1]]
    small_sharded = [a_norm, a_conv_w, a_conv_b, a_b_gate, a_lambda, a_w_gate]
    small_rows = 272
    def to_bf16(w, token):
        return (w + token[0, 0]).astype(BF16)

    got, tok_a = _exchange([a_w_in[0].astype(BF16), _pack(small_sharded, small_rows)], False, "gather_a")
    own_b1 = [to_bf16(w, tok_a) for w in (a_w_out[0], w_up[0], w_down[0])]
    st_b1 = _exchange_start(own_b1, False, "gather_b1_start")
    own_b2 = [to_bf16(w, st_b1[4]) for w in (w_kv, b_w_q[0], b_w_o[0], w_up[1], w_down[1])]
    st_b2 = _exchange_start(own_b2, False, "gather_b2_start")
    w_in = got[0]
    sm = got[1].reshape(NDEV, small_rows * 128)
    an_f = sm[:, 0:128].reshape(1, D) + st_b2[4][0:1, 0:1]
    cw_f = sm[:, 128:640].reshape(NDEV, 4, sh).transpose(1, 0, 2).reshape(4, D)
    cb_f = sm[:, 640:768].reshape(1, D)
    bg_f = sm[:, 768:1024].reshape(NDEV, NBLK, 2 * BW // NDEV).transpose(1, 0, 2).reshape(NBLK, 2 * BW)
    lam_f = sm[:, 1024:1152].reshape(1, D)
    wg_f = sm[:, 1152:1152 + NBLK * BW * 32].reshape(NDEV, NBLK, BW, 32).transpose(1, 2, 0, 3)
    wg_f = wg_f.reshape(NBLK, BW, 2 * BW).astype(BF16)
    kn_t = jnp.tile(k_norm, NH).reshape(1, D)
    qn_t = jnp.tile(b_q_norm[0], NH).reshape(1, D)
    kvn = kv_norm.reshape(1, D)
    diag = _bias_diagonals(b_rel_bias[0])

    h0 = x[0]
    gate, rec, hs, y = _lru_fwd(h0, an_f, w_in, cw_f, cb_f, wg_f, bg_f, lam_f)
    own, land = _exchange_wait(st_b1, y, False, "gather_b1_wait")
    land = _fill_own(land, [o[None] for o in own], me)
    w_out = land[0].reshape(D, D)
    wu = [land[1], None]
    wd = [land[2].reshape(FF, D), None]
    h1, h2, up0 = _mlp_fwd(h0, y, w_out, mlp_norm[0:1], wu[0], wd[0], "mlp_fwd0")
    own, land = _exchange_wait(st_b2, h2, False, "gather_b2_wait")
    land = _fill_own(land, [o[None] for o in own], me)
    wkv = land[0]
    w_q = land[1].reshape(D, D)
    w_o = land[2].reshape(D, D)
    wu[1] = land[3]
    wd[1] = land[4].reshape(FF, D)
    kraw, qraw, kpad, vpad, q = _kvq_fwd(h2, kvn, b_norm, wkv, w_q, kn_t, qn_t)
    o = _attn_fwd(q, kpad, vpad, diag)
    h3, g4, up1, lpart = _mlp_fwd(h2, o, w_o, mlp_norm[1:2], wu[1], wd[1], "mlp_fwd1", tgt=loss_target[0])
    loss = lax.psum(jnp.sum(lpart), ("x", "y", "c"))

    g3, dup1, n3, dgm1 = _mlp_bwd(g4, h3, up1, mlp_norm[1:2], wu[1], wd[1], "mlp_bwd1")
    d_wd1 = _matmul_tn(g4, up1, True, BF16, "dw_down1", b_relu2=True)
    d_wu1 = _matmul_tn(n3, dup1, True, BF16, "dw_up1")
    do = _matmul_nt(g3, w_o, "do_proj")
    d_wo = _matmul_tn(o, g3, False, BF16, "dw_o").reshape(NDEV, sh, D)
    dq, dk, dv, dd = _attn_bwd(q, kpad, vpad, do, diag)
    g2, dqr, dkv, nb, nk, dgq, dgk, dgb, dgkv = _kvq_bwd(dq, dk, dv, qraw, kraw, h2, g3, kvn, b_norm, wkv, w_q,
                                                       kn_t, qn_t)
    d_wq = _matmul_tn(nb, dqr, False, BF16, "dw_q").reshape(NDEV, sh, D)
    d_wkv = _matmul_tn(nk, dkv, True, BF16, "dw_kv")
    st_r1 = _exchange_start([d_wd1, d_wu1, d_wo, d_wq, d_wkv], True, "scatter_r1_start")
    g1, dup0, n2, dgm0 = _mlp_bwd(g2, h1, up0, mlp_norm[0:1] + st_r1[4][0:1, 0:1], wu[0], wd[0], "mlp_bwd0")
    d_wd0 = _matmul_tn(g2, up0, True, BF16, "dw_down0", b_relu2=True)
    d_wu0 = _matmul_tn(n2, dup0, True, BF16, "dw_up0")
    d_wout = _matmul_tn(y, g1, False, BF16, "dw_out").reshape(NDEV, sh, D)
    st_r2 = _exchange_start([d_wu0, d_wd0, d_wout], True, "scatter_r2_start")
    du, dcw, dcb, dwg, dbg, dlam = _lru_bwd(g1, gate, rec, hs, w_out, cw_f, cb_f, wg_f, bg_f,
                                            lam_f + st_r2[4][0:1, 0:1])
    gx, n1, dga = _a_in_bwd(du, h0, g1, an_f, w_in)
    d_win = _matmul_tn(n1, du, True, BF16, "dw_in")
    d_rel = _rel_bias_grad(dd.reshape(NH, 8, _TOEP))

    dwg_slab = dwg.reshape(NBLK, BW, NDEV, 32).transpose(2, 0, 1, 3).reshape(NDEV, 256, 128).astype(BF16)
    small_full = [dga.sum(0), dcw.sum(1), dcb.sum(0), dbg.sum(1), dlam.sum(0), dgkv.sum(0),
                  dgk.sum(0).reshape(NH, HD).sum(0), dgb.sum(0), dgq.sum(0).reshape(NH, HD).sum(0), d_rel,
                  jnp.stack([dgm0.sum(0), dgm1.sum(0)])]
    small_g_rows = 136
    small_slabs = jnp.broadcast_to(_pack(small_full, small_g_rows)[None], (NDEV, small_g_rows, 128))
    st_r3 = _exchange_start([d_win, dwg_slab, small_slabs], True, "scatter_r3_start")
    src, recv1 = _exchange_wait(st_r1, st_r3[4], True, "scatter_r1_wait")
    recv1 = _fill_own(recv1, [lax.dynamic_slice_in_dim(a, me, 1, 0) for a in src], me)
    src, recv2 = _exchange_wait(st_r2, recv1[0], True, "scatter_r2_wait")
    recv2 = _fill_own(recv2, [lax.dynamic_slice_in_dim(a, me, 1, 0) for a in src], me)

    names = ["a_w_in", "a_w_out", "w_kv", "b_w_q", "b_w_o", "w_up0", "w_up1", "w_down0", "w_down1", "a_w_gate"]
    big_m = [m_a_w_in[0], m_a_w_out[0], m_w_kv, m_b_w_q[0], m_b_w_o[0], m_w_up[0], m_w_up[1], m_w_down[0],
             m_w_down[1], m_a_w_gate.reshape(256, 128)]
    big_v = [v_a_w_in[0], v_a_w_out[0], v_w_kv, v_b_w_q[0], v_b_w_o[0], v_w_up[0], v_w_up[1], v_w_down[0],
             v_w_down[1], v_a_w_gate.reshape(256, 128)]
    big_w = big_w + [a_w_gate.reshape(256, 128)]

    def update(k, g):
        return _adamw(big_w[k], big_m[k], big_v[k], g, "adamw_" + names[k], transposed=names[k].startswith("w_down"))

    early = {1: recv2[2], 2: recv1[4], 3: recv1[3], 4: recv1[2], 5: recv2[0], 6: recv1[1], 7: recv2[1], 8: recv1[0]}
    res = {k: update(k, g) for k, g in early.items()}
    src, recv3 = _exchange_wait(st_r3, res[8][1], True, "scatter_r3_wait")
    recv3 = _fill_own(recv3, [lax.dynamic_slice_in_dim(a, me, 1, 0) for a in src], me)
    res[0] = update(0, recv3[0])
    res[9] = update(9, recv3[1])
    res = [res[k] for k in range(len(names))]
    gs = _unpack(_sum_slots(recv3[2], "sum_small_grads").reshape(-1),
                 [(1, D), (4, D), (1, D), (NBLK, 2 * BW), (1, D), (D,), (HD,), (1, D), (1, HD), (1, NH, NREL), (2, D)])
    g_small = [_cols(gs[0], me, sh), _cols(gs[1], me, sh)[None], _cols(gs[2], me, sh),
               _cols(gs[3], me, 2 * BW // NDEV)[None], _cols(gs[4], me, sh)] + gs[5:]
    small_w = [a_norm, a_conv_w, a_conv_b, a_b_gate, a_lambda, kv_norm, k_norm, b_norm, b_q_norm, b_rel_bias, mlp_norm]
    small_m = [m_a_norm, m_a_conv_w, m_a_conv_b, m_a_b_gate, m_a_lambda, m_kv_norm, m_k_norm, m_b_norm, m_b_q_norm,
               m_b_rel_bias, m_mlp_norm]
    small_v = [v_a_norm, v_a_conv_w, v_a_conv_b, v_a_b_gate, v_a_lambda, v_kv_norm, v_k_norm, v_b_norm, v_b_q_norm,
               v_b_rel_bias, v_mlp_norm]
    pr = 72
    res_small = _adamw(_pack(small_w, pr), _pack(small_m, pr), _pack(small_v, pr), _pack(g_small, pr)[None],
                       "adamw_small")
    small_shapes = [w.shape for w in small_w]
    res_small = [_unpack(r.reshape(-1), small_shapes) for r in res_small]

    def assemble(t):
        b = [r[t] for r in res]
        s_ = res_small[t]
        return [s_[0], b[0][None], s_[1], s_[2], b[9].reshape(a_w_gate.shape), s_[3], s_[4], b[1][None],
                s_[5], b[2], s_[6], s_[7], b[3][None], s_[8], s_[9], b[4][None], s_[10],
                jnp.stack([b[5], b[6]]), jnp.stack([b[7], b[8]])]

    return tuple([loss, gx[None]] + assemble(0) + assemble(1) + assemble(2) + assemble(3))
```

```python
import functools

import numpy as np
import jax
import jax.numpy as jnp
from jax import lax
from jax.experimental import pallas as pl
from jax.experimental.pallas import tpu as pltpu

F32 = jnp.float32
BF16 = jnp.bfloat16

D = 1024
NH = 16
HD = 64
FF = 4096
NBLK = 8
BW = 128
CHUNK = 64
PADK = 512
NREL = 192
EPS = 1e-6
LRU_C = 8.0
NDEV = 8

V7X_VMEM_LIMIT = 56 * 1024 * 1024
TM = 512
TMM = 512
TMF = 512
TL = 256
QB = 256
ATT_RC = 32
HPS = 8
LW = HPS * HD
KB = QB + PADK
NEG = -1e30

ADAM_LR, ADAM_B1, ADAM_B2, ADAM_EPS, ADAM_WD, ADAM_STEP = 0.001, 0.9, 0.999, 1e-08, 0.01, 10

_NT = (((1,), (1,)), ((), ()))
_TN = (((0,), (0,)), ((), ()))


def _cp(sem=None):
    return pltpu.CompilerParams(dimension_semantics=sem, vmem_limit_bytes=V7X_VMEM_LIMIT)


def _full(shape):
    n = len(shape)
    return pl.BlockSpec(shape, lambda *a: (0,) * n, pipeline_mode=pl.Buffered(1))


def _rows(tm, width):
    return pl.BlockSpec((tm, width), lambda i: (i, 0))


def _rstd(h):
    return lax.rsqrt(jnp.mean(h * h, axis=-1, keepdims=True) + EPS)


def _sigmoid(x):
    return 1.0 / (1.0 + jnp.exp(-x))


def _one_minus_sq(a, la):
    x = 2.0 * la
    series = -x * (1.0 + x * (0.5 + x * (1.0 / 6.0)))
    return jnp.where(x > -0.01, series, 1.0 - a * a)


def _softplus_neg(lam):
    e = jnp.exp(-jnp.abs(lam))
    series = e * (1.0 - e * (0.5 - e * (1.0 / 3.0 - e * 0.25)))
    return jnp.maximum(-lam, 0.0) + jnp.where(e < 0.01, series, jnp.log(1.0 + e))


_GELU_K = 0.7978845608028654


def _gelu(x):
    return 0.5 * x * (1.0 + jnp.tanh(_GELU_K * (x + 0.044715 * x * x * x)))


def _gelu_grad(x):
    t = jnp.tanh(_GELU_K * (x + 0.044715 * x * x * x))
    return 0.5 * (1.0 + t) + 0.5 * x * (1.0 - t * t) * _GELU_K * (1.0 + 3.0 * 0.044715 * x * x)


def _sum8(x):
    r, c = x.shape
    return jnp.sum(x.reshape(r // 8, 8, c), axis=0)


def _shift_down(x, s, fill, rows):
    return jnp.where(rows >= s, pltpu.roll(x, s, axis=0), fill)


def _shift_up(x, s, fill, rows, n):
    return jnp.where(rows < n - s, pltpu.roll(x, n - s, axis=0), fill)


def _lru_gates(rc, wg_n, bg_n, sp_n):
    g = jnp.dot(rc.astype(BF16), wg_n, preferred_element_type=F32) + bg_n
    rg = _sigmoid(g[:, :BW])
    ig = _sigmoid(g[:, BW:])
    la = (-LRU_C) * rg * sp_n
    a = jnp.exp(la)
    mult = jnp.sqrt(_one_minus_sq(a, la))
    return rg, ig, a, mult


def _conv(ext_ref, cw_ref, cb_ref, sl, n):
    out = cb_ref[:, sl] + cw_ref[0:1, sl] * ext_ref[5:5 + n, sl]
    for k in range(1, 4):
        out = out + cw_ref[k:k + 1, sl] * ext_ref[5 + k:5 + k + n, sl]
    return out


def _tile_scan(a_ref, b_ref, h_ref, carry, reverse):
    sub = lax.broadcasted_iota(jnp.int32, (TL, BW), 0) % 8
    for n in range(NBLK):
        a, b = a_ref[n], b_ref[n]
        for s in (1, 2, 4):
            if reverse:
                inside = sub < 8 - s
                a_sh = jnp.where(inside, pltpu.roll(a, TL - s, axis=0), 1.0)
                b_sh = jnp.where(inside, pltpu.roll(b, TL - s, axis=0), 0.0)
            else:
                inside = sub >= s
                a_sh = jnp.where(inside, pltpu.roll(a, s, axis=0), 1.0)
                b_sh = jnp.where(inside, pltpu.roll(b, s, axis=0), 0.0)
            b = a * b_sh + b
            a = a * a_sh
        a_ref[n], b_ref[n] = a, b
    carry = list(carry)
    groups = range(TL // 8 - 1, -1, -1) if reverse else range(TL // 8)
    for g in groups:
        r = slice(8 * g, 8 * g + 8)
        for n in range(NBLK):
            h = a_ref[n, r, :] * carry[n] + b_ref[n, r, :]
            h_ref[n, r, :] = h
            carry[n] = h[0:1, :] if reverse else h[7:8, :]
    return carry


def _lru_fwd(h0, a_norm, w_in, conv_w, conv_b, wg, bg, lam):
    s = h0.shape[0]

    def body(h0_ref, an_ref, win_ref, cw_ref, cb_ref, wg_ref, bg_ref, lam_ref,
             gate_ref, rec_ref, hs_ref, y_ref, n1_ref, ext_ref, hc_ref, a_sc, b_sc, hl_sc):
        i = pl.program_id(0)

        @pl.when(i == 0)
        def _():
            ext_ref[0:8, :] = jnp.zeros((8, D), F32)
            hc_ref[...] = jnp.zeros_like(hc_ref)

        h = h0_ref[...]
        n1 = (h * _rstd(h) * an_ref[...]).astype(BF16)
        n1_ref[...] = n1
        cw = 2 * D // NDEV
        for d in range(NDEV):
            ud = jnp.dot(n1, win_ref[d], preferred_element_type=F32)
            if d < NDEV // 2:
                gate_ref[:, d * cw:(d + 1) * cw] = ud
            else:
                rec_ref[:, d * cw - D:(d + 1) * cw - D] = ud
                ext_ref[8:8 + TL, d * cw - D:(d + 1) * cw - D] = ud
        sp = _softplus_neg(lam_ref[...])
        for n in range(NBLK):
            sl = slice(n * BW, (n + 1) * BW)
            rc = _conv(ext_ref, cw_ref, cb_ref, sl, TL)
            rg, ig, a, mult = _lru_gates(rc, wg_ref[n], bg_ref[n:n + 1, :], sp[:, sl])
            a_sc[n] = a
            b_sc[n] = mult * (ig * rc)
        carry = _tile_scan(a_sc, b_sc, hl_sc, [hc_ref[0:1, n * BW:(n + 1) * BW] for n in range(NBLK)], reverse=False)
        for n in range(NBLK):
            sl = slice(n * BW, (n + 1) * BW)
            hh = hl_sc[n]
            hc_ref[0:1, sl] = carry[n]
            hs_ref[:, sl] = hh
            y_ref[:, sl] = (_gelu(gate_ref[:, sl]) * hh).astype(BF16)
        ext_ref[0:8, :] = ext_ref[TL:TL + 8, :]

    row = _rows(TL, D)
    return pl.pallas_call(
        body, name="lru_fwd", grid=(s // TL,),
        in_specs=[row, _full((1, D)), _full((NDEV, D, 2 * D // NDEV)), _full((4, D)), _full((1, D)),
                  _full((NBLK, BW, 2 * BW)), _full((NBLK, 2 * BW)), _full((1, D))],
        out_specs=[row, row, row, row, row],
        out_shape=[jax.ShapeDtypeStruct((s, D), F32), jax.ShapeDtypeStruct((s, D), F32),
                   jax.ShapeDtypeStruct((s, D), F32), jax.ShapeDtypeStruct((s, D), BF16),
                   jax.ShapeDtypeStruct((s, D), BF16)],
        scratch_shapes=[pltpu.VMEM((TL + 8, D), F32), pltpu.VMEM((8, D), F32)]
        + [pltpu.VMEM((NBLK, TL, BW), F32)] * 3,
        compiler_params=_cp(("arbitrary",)),
    )(h0, a_norm, w_in, conv_w, conv_b, wg, bg, lam)


def _mlp_fwd(res, px, pw, g, wu, wd, name, tgt=None):
    s = res.shape[0]
    fj = 512
    with_loss = tgt is not None

    def body(res_ref, px_ref, pw_ref, g_ref, wu_ref, wd_ref, *rest):
        if with_loss:
            t_ref, hin_ref, hout_ref, up_ref, l_ref, n_ref = rest

            @pl.when(pl.program_id(0) == 0)
            def _():
                l_ref[...] = jnp.zeros_like(l_ref)
        else:
            hin_ref, hout_ref, up_ref, n_ref = rest
        hin = res_ref[...] + jnp.dot(px_ref[...], pw_ref[...], preferred_element_type=F32)
        hin_ref[...] = hin
        hout_ref[...] = hin
        n_ref[...] = (hin * _rstd(hin) * g_ref[...]).astype(BF16)
        for j in range(FF // fj):
            sl = slice(j * fj, (j + 1) * fj)
            up = jnp.dot(n_ref[...], wu_ref[j], preferred_element_type=F32)
            up_ref[:, sl] = up.astype(BF16)
            rl = jnp.maximum(up, 0.0)
            hout_ref[...] += jnp.dot((rl * rl).astype(BF16), wd_ref[sl, :], preferred_element_type=F32)
        if with_loss:
            d = hout_ref[...] - t_ref[...]
            hout_ref[...] = d * (1.0 / D)
            l_ref[...] += _sum8(d * d) * (0.5 / D)

    row = _rows(TMF, D)
    outs = pl.pallas_call(
        body, name=name, grid=(s // TMF,),
        in_specs=[row, row, _full((D, D)), _full((1, D)), _full((NDEV, D, fj)), _full((FF, D))]
        + ([row] if with_loss else []),
        out_specs=[row, row, _rows(TMF, FF)] + ([_acc_spec()] if with_loss else []),
        out_shape=[jax.ShapeDtypeStruct((s, D), F32), jax.ShapeDtypeStruct((s, D), F32),
                   jax.ShapeDtypeStruct((s, FF), BF16)] + ([jax.ShapeDtypeStruct((8, D), F32)] if with_loss else []),
        scratch_shapes=[pltpu.VMEM((TMF, D), BF16)],
        compiler_params=_cp(("arbitrary",) if with_loss else ("parallel",)),
    )(*([res, px, pw, g, wu, wd] + ([tgt] if with_loss else [])))
    return outs


def _head_rstd(x2, lo):
    sq = x2 * x2
    s_lo = jnp.sum(jnp.where(lo, sq, 0.0), axis=-1, keepdims=True)
    s_hi = jnp.sum(jnp.where(lo, 0.0, sq), axis=-1, keepdims=True)
    return lax.rsqrt(jnp.where(lo, s_lo, s_hi) * (1.0 / HD) + EPS)


def _kvq_fwd(h2, kv_norm, b_norm, w_kv, w_q, k_norm_t, q_norm_t):
    s = h2.shape[0]
    assert PADK == TM

    def body(h_ref, gkv_ref, gb_ref, wkv_ref, wq_ref, kn_ref, qn_ref,
             kraw_ref, qraw_ref, k_ref, v_ref, q_ref):
        i = pl.program_id(0)

        @pl.when(i == 0)
        def _():
            k_ref[...] = jnp.zeros_like(k_ref)
            v_ref[...] = jnp.zeros_like(v_ref)

        @pl.when(i > 0)
        def _():
            h = h_ref[...]
            xhat = h * _rstd(h)
            nk = (xhat * gkv_ref[...]).astype(BF16)
            qr = jnp.dot((xhat * gb_ref[...]).astype(BF16), wq_ref[...], preferred_element_type=F32)
            qraw_ref[...] = qr
            lo = lax.broadcasted_iota(jnp.int32, (1, 128), 1) < HD
            cw = 2 * D // NDEV
            for d in range(NDEV):
                kvd = jnp.dot(nk, wkv_ref[d], preferred_element_type=F32)
                if d < NDEV // 2:
                    kraw_ref[:, d * cw:(d + 1) * cw] = kvd
                    for p in range(cw // 128):
                        sl = slice(d * cw + p * 128, d * cw + (p + 1) * 128)
                        k2 = kvd[:, p * 128:(p + 1) * 128]
                        k_ref[:, sl] = (k2 * _head_rstd(k2, lo) * kn_ref[:, sl]).astype(BF16)
                else:
                    v_ref[:, d * cw - D:(d + 1) * cw - D] = kvd.astype(BF16)
            for p in range(D // 128):
                sl = slice(p * 128, (p + 1) * 128)
                q2 = qr[:, sl]
                q_ref[:, sl] = (q2 * _head_rstd(q2, lo) * qn_ref[:, sl] * (HD ** -0.5)).astype(BF16)

    prev = pl.BlockSpec((TM, D), lambda i: (jnp.maximum(i - 1, 0), 0))
    cur = pl.BlockSpec((TM, D), lambda i: (i, 0))
    return pl.pallas_call(
        body, name="kvq_fwd", grid=(s // TM + 1,),
        in_specs=[prev, _full((1, D)), _full((1, D)), _full((NDEV, D, 2 * D // NDEV)), _full((D, D)), _full((1, D)),
                  _full((1, D))],
        out_specs=[prev, prev, cur, cur, prev],
        out_shape=[jax.ShapeDtypeStruct((s, D), F32), jax.ShapeDtypeStruct((s, D), F32),
                   jax.ShapeDtypeStruct((s + PADK, D), BF16), jax.ShapeDtypeStruct((s + PADK, D), BF16),
                   jax.ShapeDtypeStruct((s, D), BF16)],
        compiler_params=_cp(("arbitrary",)),
    )(h2, kv_norm, b_norm, w_kv, w_q, k_norm_t, q_norm_t)


_TOEP = QB + KB


def _bias_from_diag(diag_ref, bias_ref):
    row8 = lax.broadcasted_iota(jnp.int32, (8, _TOEP), 0)
    kchunk = lax.broadcasted_iota(jnp.int32, (8, KB), 1) // CHUNK
    for a in range(HPS):
        v = jnp.broadcast_to(diag_ref[a:a + 1, :], (8, _TOEP))
        z0 = v
        for b in range(1, 8):
            z0 = jnp.where(row8 == b, pltpu.roll(v, b, axis=1), z0)
        for t in range(QB // 8):
            slab = z0 if t == 0 else pltpu.roll(z0, 8 * t, axis=1)
            qchunk = (8 * t) // CHUNK
            band = jnp.logical_and(kchunk >= qchunk, kchunk <= qchunk + PADK // CHUNK)
            bias_ref[a, 8 * t:8 * t + 8, :] = jnp.where(band, slab[:, :KB], NEG)


def _diag_sums(db_ref, a):
    row8 = lax.broadcasted_iota(jnp.int32, (8, _TOEP), 0)
    z = jnp.zeros((8, _TOEP), F32)
    for t in range(QB // 8):
        slab = jnp.concatenate([db_ref[a, 8 * t:8 * t + 8, :], jnp.zeros((8, _TOEP - KB), F32)], axis=1)
        z = z + (slab if t == 0 else pltpu.roll(slab, _TOEP - 8 * t, axis=1))
    e = z
    for b in range(1, 8):
        e = jnp.where(row8 == b, pltpu.roll(z, _TOEP - b, axis=1), e)
    return e


def _attn_specs(nqb):
    qspec = pl.BlockSpec((QB, LW), lambda p, j: (jnp.minimum(j, nqb - 1), p))
    kspecs = [pl.BlockSpec((QB, LW), functools.partial(lambda p, j, t: (jnp.minimum(j, nqb - 1) + t, p), t=t))
              for t in range(KB // QB)]
    dspec = pl.BlockSpec((None, HPS, _TOEP), lambda p, j: (p, 0, 0))
    return qspec, kspecs, dspec


V7X_MXU = 256
HPT = V7X_MXU // HD


def _head_masks():
    head = lax.broadcasted_iota(jnp.int32, (1, V7X_MXU), 1) // HD
    return [head == t for t in range(HPT)]


def _tile_of(a):
    return slice((a // HPT) * V7X_MXU, (a // HPT + 1) * V7X_MXU)


def _pick_heads(parts, masks):
    tiles = []
    for g in range(HPS // HPT):
        out = parts[g * HPT + HPT - 1]
        for t in range(HPT - 2, -1, -1):
            out = jnp.where(masks[t], parts[g * HPT + t], out)
        tiles.append(out)
    return tiles[0] if len(tiles) == 1 else jnp.concatenate(tiles, axis=1)


def _attn_fwd(q, kpad, vpad, diag):
    s = q.shape[0]
    nqb = s // QB
    npad = PADK // QB

    def body(q_ref, k0, k1, k2, v0, v1, v2, diag_ref, o_ref, bias_ref, sc_ref, eb_ref, rl_ref):
        j = pl.program_id(1)

        @pl.when(j == 0)
        def _():
            _bias_from_diag(diag_ref, bias_ref)

        def block(masked):
            kcat = jnp.concatenate([k0[...], k1[...], k2[...]], axis=0)
            vcat = jnp.concatenate([v0[...], v1[...], v2[...]], axis=0)
            q2 = q_ref[...]
            masks = _head_masks()
            valid = (lax.broadcasted_iota(jnp.int32, (1, KB), 1) + j * QB >= PADK) if masked else None
            outs = []
            for a in range(HPS):
                qa = q2[:, _tile_of(a)]
                sc_ref[a] = lax.dot_general(jnp.where(masks[a % HPT], qa, jnp.zeros_like(qa)), kcat[:, _tile_of(a)],
                                            _NT, preferred_element_type=F32)
            for a in range(HPS):
                for c in range(QB // ATT_RC):
                    r = slice(c * ATT_RC, (c + 1) * ATT_RC)
                    sc = sc_ref[a, r, :] + bias_ref[a, r, :]
                    if masked:
                        sc = jnp.where(valid, sc, NEG)
                    e = jnp.exp(sc - jnp.max(sc, axis=-1, keepdims=True))
                    eb_ref[a, r, :] = e.astype(BF16)
                    rl_ref[a, r, :] = jnp.broadcast_to(1.0 / jnp.sum(e, axis=-1, keepdims=True),
                                                       (ATT_RC, V7X_MXU))
                outs.append(jnp.dot(eb_ref[a], vcat[:, _tile_of(a)], preferred_element_type=F32) * rl_ref[a])
            o_ref[...] = _pick_heads(outs, masks).astype(BF16)

        pl.when(j < npad)(functools.partial(block, True))
        pl.when(j >= npad)(functools.partial(block, False))

    qspec, kspecs, dspec = _attn_specs(nqb)
    assert len(kspecs) == 3
    return pl.pallas_call(
        body, name="attn_fwd", grid=(D // LW, nqb),
        in_specs=[qspec] + kspecs + kspecs + [dspec],
        out_specs=qspec,
        out_shape=jax.ShapeDtypeStruct((s, D), BF16),
        scratch_shapes=[pltpu.VMEM((HPS, QB, KB), F32), pltpu.VMEM((HPS, QB, KB), F32),
                        pltpu.VMEM((HPS, QB, KB), BF16), pltpu.VMEM((HPS, QB, V7X_MXU), F32)],
        compiler_params=_cp(("parallel", "arbitrary")),
    )(q, kpad, kpad, kpad, vpad, vpad, vpad, diag)


def _rms_bwd(dn, xhat, r, g):
    dng = dn * g
    return r * (dng - xhat * jnp.mean(dng * xhat, axis=-1, keepdims=True))


def _acc_spec():
    return pl.BlockSpec((8, D), lambda i: (0, 0))


def _mlp_bwd(gout, hin, up, g, wu, wd, name):
    s = gout.shape[0]
    fj = 512

    def body(go_ref, hin_ref, up_ref, g_ref, wu_ref, wd_ref, gin_ref, dup_ref, n_ref, dg_ref, gob_ref, dn_ref):
        @pl.when(pl.program_id(0) == 0)
        def _():
            dg_ref[...] = jnp.zeros_like(dg_ref)

        gob_ref[...] = go_ref[...].astype(BF16)
        for j in range(FF // fj):
            sl = slice(j * fj, (j + 1) * fj)
            rl = jnp.maximum(up_ref[:, sl].astype(F32), 0.0)
            dact = lax.dot_general(gob_ref[...], wd_ref[sl, :], _NT, preferred_element_type=F32)
            dupj = (dact * (2.0 * rl)).astype(BF16)
            dup_ref[:, sl] = dupj
            part = lax.dot_general(dupj, wu_ref[j], _NT, preferred_element_type=F32)
            if j == 0:
                dn_ref[...] = part
            else:
                dn_ref[...] += part
        hin = hin_ref[...]
        r = _rstd(hin)
        xhat = hin * r
        n_ref[...] = (xhat * g_ref[...]).astype(BF16)
        dn = dn_ref[...]
        gin_ref[...] = go_ref[...] + _rms_bwd(dn, xhat, r, g_ref[...])
        dg_ref[...] += _sum8(dn * xhat)

    row = _rows(TMM, D)
    wide = _rows(TMM, FF)
    return pl.pallas_call(
        body, name=name, grid=(s // TMM,),
        in_specs=[row, row, wide, _full((1, D)), _full((NDEV, D, fj)), _full((FF, D))],
        out_specs=[row, wide, row, _acc_spec()],
        out_shape=[jax.ShapeDtypeStruct((s, D), F32), jax.ShapeDtypeStruct((s, FF), BF16),
                   jax.ShapeDtypeStruct((s, D), BF16), jax.ShapeDtypeStruct((8, D), F32)],
        scratch_shapes=[pltpu.VMEM((TMM, D), BF16), pltpu.VMEM((TMM, D), F32)],
        compiler_params=_cp(("arbitrary",)),
    )(gout, hin, up, g, wu, wd)


def _matmul_tn(a, b, slab, out_dtype, name, b_relu2=False):
    s, m = a.shape
    n = b.shape[1]
    ts = min(s, 512 if n > 2048 else 1024)
    nk = s // ts
    nc = 512
    w = n // NDEV

    def body(a_ref, b_ref, o_ref, at_ref, acc_ref):
        k = pl.program_id(0)
        at_ref[...] = a_ref[...].astype(BF16).T

        @pl.when(k == 0)
        def _():
            acc_ref[...] = jnp.zeros_like(acc_ref)

        for c in range(n // nc):
            sl = slice(c * nc, (c + 1) * nc)
            bc = b_ref[:, sl]
            if b_relu2:
                rl = jnp.maximum(bc.astype(F32), 0.0)
                bc = rl * rl
            acc_ref[:, sl] += jnp.dot(at_ref[...], bc.astype(BF16), preferred_element_type=F32)

        @pl.when(k == nk - 1)
        def _():
            if slab:
                for d in range(NDEV):
                    o_ref[d] = acc_ref[:, d * w:(d + 1) * w].astype(out_dtype)
            else:
                o_ref[...] = acc_ref[...].astype(out_dtype)

    if slab:
        out_shape = jax.ShapeDtypeStruct((NDEV, m, w), out_dtype)
        out_spec = pl.BlockSpec((NDEV, m, w), lambda k: (0, 0, 0), pipeline_mode=pl.Buffered(1))
    else:
        out_shape = jax.ShapeDtypeStruct((m, n), out_dtype)
        out_spec = pl.BlockSpec((m, n), lambda k: (0, 0), pipeline_mode=pl.Buffered(1))
    return pl.pallas_call(
        body, name=name, grid=(nk,),
        in_specs=[pl.BlockSpec((ts, m), lambda k: (k, 0)), pl.BlockSpec((ts, n), lambda k: (k, 0))],
        out_specs=out_spec, out_shape=out_shape,
        scratch_shapes=[pltpu.VMEM((m, ts), BF16), pltpu.VMEM((m, n), F32)],
        compiler_params=_cp(("arbitrary",)),
    )(a, b)


def _matmul_nt(x, w, name):
    s, n = x.shape
    k = w.shape[0]

    def body(x_ref, w_ref, o_ref):
        o_ref[...] = lax.dot_general(x_ref[...].astype(BF16), w_ref[...], _NT,
                                     preferred_element_type=F32).astype(BF16)

    return pl.pallas_call(
        body, name=name, grid=(s // TM,),
        in_specs=[_rows(TM, n), _full((k, n))], out_specs=_rows(TM, k),
        out_shape=jax.ShapeDtypeStruct((s, k), BF16),
        compiler_params=_cp(("parallel",)),
    )(x, w)


def _attn_bwd(q, kpad, vpad, do, diag):
    s = q.shape[0]
    nqb = s // QB
    npad = PADK // QB

    def body(q_ref, k0, k1, k2, v0, v1, v2, do_ref, diag_ref, dq_ref, dk_ref, dv_ref, dd_ref,
             bias_ref, db_ref, dka_ref, dva_ref, sc_ref, dp_ref, dsb_ref, pb_ref):
        j = pl.program_id(1)

        @pl.when(j == 0)
        def _():
            _bias_from_diag(diag_ref, bias_ref)
            dka_ref[...] = jnp.zeros_like(dka_ref)
            dva_ref[...] = jnp.zeros_like(dva_ref)
            db_ref[...] = jnp.zeros_like(db_ref)

        def block(masked):
            kcat = jnp.concatenate([k0[...], k1[...], k2[...]], axis=0)
            vcat = jnp.concatenate([v0[...], v1[...], v2[...]], axis=0)
            q2 = q_ref[...]
            do2 = do_ref[...]
            masks = _head_masks()
            valid = (lax.broadcasted_iota(jnp.int32, (1, KB), 1) + j * QB >= PADK) if masked else None
            qt = q2.T
            dot_ = do2.T
            dq = []
            for a in range(HPS):
                qa, doa = q2[:, _tile_of(a)], do2[:, _tile_of(a)]
                sc_ref[a] = lax.dot_general(jnp.where(masks[a % HPT], qa, jnp.zeros_like(qa)), kcat[:, _tile_of(a)],
                                            _NT, preferred_element_type=F32)
                dp_ref[a] = lax.dot_general(jnp.where(masks[a % HPT], doa, jnp.zeros_like(doa)),
                                            vcat[:, _tile_of(a)], _NT, preferred_element_type=F32)
            for a in range(HPS):
                for c in range(QB // ATT_RC):
                    r = slice(c * ATT_RC, (c + 1) * ATT_RC)
                    sc = sc_ref[a, r, :] + bias_ref[a, r, :]
                    if masked:
                        sc = jnp.where(valid, sc, NEG)
                    e = jnp.exp(sc - jnp.max(sc, axis=-1, keepdims=True))
                    p = e * (1.0 / jnp.sum(e, axis=-1, keepdims=True))
                    dp = dp_ref[a, r, :]
                    ds = p * (dp - jnp.sum(p * dp, axis=-1, keepdims=True))
                    db_ref[a, r, :] += ds
                    dsb_ref[a, r, :] = ds.astype(BF16)
                    pb_ref[a, r, :] = p.astype(BF16)
                hd = slice(a * HD, (a + 1) * HD)
                dq.append(jnp.dot(dsb_ref[a], kcat[:, _tile_of(a)], preferred_element_type=F32))
                dka_ref[hd, :] += jnp.dot(qt[hd, :], dsb_ref[a], preferred_element_type=F32)
                dva_ref[hd, :] += jnp.dot(dot_[hd, :], pb_ref[a], preferred_element_type=F32)
            dq_ref[...] = _pick_heads(dq, masks) * (HD ** -0.5)

        pl.when(j < npad)(functools.partial(block, True))
        pl.when(jnp.logical_and(j >= npad, j < nqb))(functools.partial(block, False))

        @pl.when(j == nqb - 1)
        def _():
            for a in range(HPS):
                dd_ref[a] = _diag_sums(db_ref, a)

        dk_ref[...] = dka_ref[:, 0:QB].T
        dv_ref[...] = dva_ref[:, 0:QB].T
        dka_ref[:, 0:KB - QB] = dka_ref[:, QB:KB]
        dva_ref[:, 0:KB - QB] = dva_ref[:, QB:KB]
        dka_ref[:, KB - QB:KB] = jnp.zeros((LW, QB), F32)
        dva_ref[:, KB - QB:KB] = jnp.zeros((LW, QB), F32)

    qspec, kspecs, dspec = _attn_specs(nqb)
    kout = pl.BlockSpec((QB, LW), lambda p, j: (jnp.maximum(j - npad, 0), p))
    sd = jax.ShapeDtypeStruct((s, D), F32)
    return pl.pallas_call(
        body, name="attn_bwd", grid=(D // LW, nqb + npad),
        in_specs=[qspec] + kspecs + kspecs + [qspec, dspec],
        out_specs=[qspec, kout, kout, pl.BlockSpec((None, HPS, 8, _TOEP), lambda p, j: (p, 0, 0, 0))],
        out_shape=[sd, sd, sd, jax.ShapeDtypeStruct((NH // HPS, HPS, 8, _TOEP), F32)],
        scratch_shapes=[pltpu.VMEM((HPS, QB, KB), F32), pltpu.VMEM((HPS, QB, KB), F32),
                        pltpu.VMEM((LW, KB), F32), pltpu.VMEM((LW, KB), F32),
                        pltpu.VMEM((HPS, QB, KB), F32), pltpu.VMEM((HPS, QB, KB), F32),
                        pltpu.VMEM((HPS, QB, KB), BF16), pltpu.VMEM((HPS, QB, KB), BF16)],
        compiler_params=_cp(("parallel", "arbitrary")),
    )(q, kpad, kpad, kpad, vpad, vpad, vpad, do, diag)


def _head_norm_bwd(dy2, x2, g2, lo):
    rr = _head_rstd(x2, lo)
    xhat = x2 * rr
    t = dy2 * g2 * xhat
    m_lo = jnp.sum(jnp.where(lo, t, 0.0), axis=-1, keepdims=True)
    m_hi = jnp.sum(jnp.where(lo, 0.0, t), axis=-1, keepdims=True)
    m = jnp.where(lo, m_lo, m_hi) * (1.0 / HD)
    return rr * (dy2 * g2 - xhat * m), dy2 * xhat


def _kvq_bwd(dq, dk, dv, qraw, kraw, h2, g3, kv_norm, b_norm, w_kv, w_q, k_norm_t, q_norm_t):
    s = h2.shape[0]

    def body(dq_ref, dk_ref, dv_ref, qraw_ref, kraw_ref, h_ref, g3_ref, gkv_ref, gb_ref, wkv_ref, wq_ref,
             kn_ref, qn_ref, g2_ref, dqr_ref, dkv_ref, nb_ref, nk_ref, dgq_ref, dgk_ref, dgb_ref, dgkv_ref):
        @pl.when(pl.program_id(0) == 0)
        def _():
            for r in (dgq_ref, dgk_ref, dgb_ref, dgkv_ref):
                r[...] = jnp.zeros_like(r)

        lo = lax.broadcasted_iota(jnp.int32, (1, 128), 1) < HD
        for p in range(D // 128):
            sl = slice(p * 128, (p + 1) * 128)
            dx, dgp = _head_norm_bwd(dq_ref[:, sl], qraw_ref[:, sl], qn_ref[:, sl], lo)
            dqr_ref[:, sl] = dx.astype(BF16)
            dgq_ref[:, sl] += _sum8(dgp)
            dx, dgp = _head_norm_bwd(dk_ref[:, sl], kraw_ref[:, sl], kn_ref[:, sl], lo)
            dkv_ref[:, sl] = dx.astype(BF16)
            dgk_ref[:, sl] += _sum8(dgp)
        dkv_ref[:, D:] = dv_ref[...].astype(BF16)
        dnb = lax.dot_general(dqr_ref[...], wq_ref[...], _NT, preferred_element_type=F32)
        cw = 2 * D // NDEV
        dnk = lax.dot_general(dkv_ref[:, 0:cw], wkv_ref[0], _NT, preferred_element_type=F32)
        for d in range(1, NDEV):
            dnk = dnk + lax.dot_general(dkv_ref[:, d * cw:(d + 1) * cw], wkv_ref[d], _NT, preferred_element_type=F32)
        h = h_ref[...]
        r = _rstd(h)
        xhat = h * r
        dxg = dnb * gb_ref[...] + dnk * gkv_ref[...]
        g2_ref[...] = g3_ref[...] + r * (dxg - xhat * jnp.mean(dxg * xhat, axis=-1, keepdims=True))
        dgb_ref[...] += _sum8(dnb * xhat)
        dgkv_ref[...] += _sum8(dnk * xhat)
        nb_ref[...] = (xhat * gb_ref[...]).astype(BF16)
        nk_ref[...] = (xhat * gkv_ref[...]).astype(BF16)

    row = _rows(TM, D)
    sd = jax.ShapeDtypeStruct((s, D), BF16)
    acc = jax.ShapeDtypeStruct((8, D), F32)
    return pl.pallas_call(
        body, name="kvq_bwd", grid=(s // TM,),
        in_specs=[row] * 7 + [_full((1, D)), _full((1, D)), _full((NDEV, D, 2 * D // NDEV)), _full((D, D)),
                              _full((1, D)), _full((1, D))],
        out_specs=[row, row, _rows(TM, 2 * D), row, row] + [_acc_spec()] * 4,
        out_shape=[jax.ShapeDtypeStruct((s, D), F32), sd, jax.ShapeDtypeStruct((s, 2 * D), BF16), sd, sd,
                   acc, acc, acc, acc],
        compiler_params=_cp(("arbitrary",)),
    )(dq, dk, dv, qraw, kraw, h2, g3, kv_norm, b_norm, w_kv, w_q, k_norm_t, q_norm_t)


def _lru_bwd(g1, gate, rec, hs, w_out, conv_w, conv_b, wg, bg, lam):
    s = g1.shape[0]
    nt = s // TL

    def body(g1_ref, gate_ref, rec_ref, recp_ref, hs_ref, hsp_ref, wo_ref, cw_ref, cb_ref, wg_ref, bg_ref, lam_ref,
             du_ref, dcw_ref, dcb_ref, dwg_ref, dbg_ref, dlam_ref, ext_ref, dext_ref, cg_ref,
             a_sc, dh_sc, hl_sc, ac_sc, rg_sc, ig_sc, mult_sc, rc_sc):
        i = pl.program_id(0)
        first_tile = i == nt - 1

        @pl.when(i == 0)
        def _():
            dext_ref[TL:TL + 8, :] = jnp.zeros((8, D), F32)
            cg_ref[...] = jnp.zeros_like(cg_ref)
            for r in (dcw_ref, dcb_ref, dwg_ref, dbg_ref, dlam_ref):
                r[...] = jnp.zeros_like(r)

        keep = jnp.where(first_tile, 0.0, 1.0)
        ext_ref[0:8, :] = recp_ref[...] * keep
        ext_ref[8:8 + TL, :] = rec_ref[...]
        dy = lax.dot_general(g1_ref[...].astype(BF16), wo_ref[...], _NT, preferred_element_type=F32)
        lam_v = lam_ref[...]
        sp = _softplus_neg(lam_v)
        dsp_dlam = -_sigmoid(-lam_v)
        rows = lax.broadcasted_iota(jnp.int32, (TL, BW), 0)
        for n in range(NBLK):
            sl = slice(n * BW, (n + 1) * BW)
            rc = _conv(ext_ref, cw_ref, cb_ref, sl, TL)
            rg, ig, a, mult = _lru_gates(rc, wg_ref[n], bg_ref[n:n + 1, :], sp[:, sl])
            h = hs_ref[:, sl]
            gt = gate_ref[:, sl]
            dyn = dy[:, sl]
            du_ref[:, sl] = (dyn * h * _gelu_grad(gt)).astype(BF16)
            dh_sc[n] = dyn * _gelu(gt) + jnp.where(rows == TL - 1, cg_ref[0:1, sl], 0.0)
            a_sc[n], rg_sc[n], ig_sc[n], mult_sc[n], rc_sc[n] = a, rg, ig, mult, rc
            ac_sc[n] = _shift_up(a, 1, 0.0, rows, TL)
        _tile_scan(ac_sc, dh_sc, hl_sc, [jnp.zeros((1, BW), F32)] * NBLK, reverse=True)
        for n in range(NBLK):
            sl = slice(n * BW, (n + 1) * BW)
            gsc = hl_sc[n]
            a, rg, ig, mult, rc = a_sc[n], rg_sc[n], ig_sc[n], mult_sc[n], rc_sc[n]
            cg_ref[0:1, sl] = a[0:1, :] * gsc[0:1, :]
            hprev = _shift_down(hs_ref[:, sl], 1, hsp_ref[7:8, sl] * keep, rows)
            da = gsc * hprev
            d_mult = gsc * ig * rc
            d_ig = gsc * mult * rc
            d_rc = gsc * mult * ig
            d_la = da * a - d_mult * (a * a) / mult
            d_rg = d_la * ((-LRU_C) * sp[:, sl])
            dlam_ref[:, sl] += _sum8(d_la * ((-LRU_C) * rg)) * dsp_dlam[:, sl]
            dg = jnp.concatenate([d_rg * rg * (1.0 - rg), d_ig * ig * (1.0 - ig)], axis=1)
            dgb = dg.astype(BF16)
            d_rc = d_rc + lax.dot_general(dgb, wg_ref[n], _NT, preferred_element_type=F32)
            dwg_ref[n] += lax.dot_general(rc.astype(BF16), dgb, _TN, preferred_element_type=F32)
            dbg_ref[n] += _sum8(dg)
            dext_ref[0:TL, sl] = d_rc
            dcb_ref[:, sl] += _sum8(d_rc)
            for k in range(4):
                dcw_ref[k, :, sl] += _sum8(d_rc * ext_ref[5 + k:5 + k + TL, sl])
        for k in range(4):
            part = cw_ref[3 - k:4 - k, :] * dext_ref[k:k + TL, :]
            acc = part if k == 0 else acc + part
        du_ref[:, D:] = acc.astype(BF16)
        dext_ref[TL:TL + 8, :] = dext_ref[0:8, :]

    rev = pl.BlockSpec((TL, D), lambda i: (nt - 1 - i, 0))
    rev8 = pl.BlockSpec((8, D), lambda i: (jnp.maximum((nt - 1 - i) * (TL // 8) - 1, 0), 0))
    acc = jax.ShapeDtypeStruct((8, D), F32)
    return pl.pallas_call(
        body, name="lru_bwd", grid=(nt,),
        in_specs=[rev, rev, rev, rev8, rev, rev8, _full((D, D)), _full((4, D)), _full((1, D)),
                  _full((NBLK, BW, 2 * BW)), _full((NBLK, 2 * BW)), _full((1, D))],
        out_specs=[pl.BlockSpec((TL, 2 * D), lambda i: (nt - 1 - i, 0)),
                   pl.BlockSpec((4, 8, D), lambda i: (0, 0, 0)), _acc_spec(),
                   pl.BlockSpec((NBLK, BW, 2 * BW), lambda i: (0, 0, 0)),
                   pl.BlockSpec((NBLK, 8, 2 * BW), lambda i: (0, 0, 0)), _acc_spec()],
        out_shape=[jax.ShapeDtypeStruct((s, 2 * D), BF16), jax.ShapeDtypeStruct((4, 8, D), F32), acc,
                   jax.ShapeDtypeStruct((NBLK, BW, 2 * BW), F32), jax.ShapeDtypeStruct((NBLK, 8, 2 * BW), F32), acc],
        scratch_shapes=[pltpu.VMEM((TL + 8, D), F32), pltpu.VMEM((TL + 8, D), F32), pltpu.VMEM((8, D), F32)]
        + [pltpu.VMEM((NBLK, TL, BW), F32)] * 8,
        compiler_params=_cp(("arbitrary",)),
    )(g1, gate, rec, rec, hs, hs, w_out, conv_w, conv_b, wg, bg, lam)


def _a_in_bwd(du, h0, g1, a_norm, w_in):
    s = h0.shape[0]

    def body(du_ref, h_ref, g1_ref, an_ref, win_ref, gx_ref, dg_ref):
        @pl.when(pl.program_id(0) == 0)
        def _():
            dg_ref[...] = jnp.zeros_like(dg_ref)

        cw = 2 * D // NDEV
        dn = lax.dot_general(du_ref[:, 0:cw], win_ref[0], _NT, preferred_element_type=F32)
        for d in range(1, NDEV):
            dn = dn + lax.dot_general(du_ref[:, d * cw:(d + 1) * cw], win_ref[d], _NT, preferred_element_type=F32)
        h = h_ref[...]
        r = _rstd(h)
        xhat = h * r
        gx_ref[...] = g1_ref[...] + _rms_bwd(dn, xhat, r, an_ref[...])
        dg_ref[...] += _sum8(dn * xhat)

    row = _rows(TM, D)
    return pl.pallas_call(
        body, name="a_in_bwd", grid=(s // TM,),
        in_specs=[_rows(TM, 2 * D), row, row, _full((1, D)), _full((NDEV, D, 2 * D // NDEV))],
        out_specs=[row, _acc_spec()],
        out_shape=[jax.ShapeDtypeStruct((s, D), F32), jax.ShapeDtypeStruct((8, D), F32)],
        compiler_params=_cp(("arbitrary",)),
    )(du, h0, g1, a_norm, w_in)


def _rel_onehot():
    m = np.arange(_TOEP)
    signed = np.where(m < KB, m, m - _TOEP)
    idx = np.clip(PADK - signed, -(CHUNK - 1), 2 * CHUNK) + (CHUNK - 1)
    return (idx[None, :] == np.arange(NREL)[:, None]).astype(np.float32)


def _bias_diagonals(rel_bias):
    diag = jnp.dot(rel_bias, jnp.asarray(_rel_onehot()), precision=lax.Precision.HIGHEST)
    return diag.reshape(NH // HPS, HPS, _TOEP)


def _rel_bias_grad(dd):
    rows = 8
    z = dd
    oh = np.zeros((_TOEP, 256), np.float32)
    oh[:, :NREL] = _rel_onehot().T

    def body(z_ref, oh_ref, o_ref):
        d = jnp.sum(z_ref[...], axis=0, keepdims=True)
        hi = d.astype(BF16)
        mid = (d - hi.astype(F32)).astype(BF16)
        lo = (d - hi.astype(F32) - mid.astype(F32)).astype(BF16)
        ohb = oh_ref[...].astype(BF16)
        acc = jnp.zeros((8, 256), F32)
        for piece in (lo, mid, hi):
            acc = acc + jnp.dot(jnp.broadcast_to(piece, (8, _TOEP)), ohb, preferred_element_type=F32)
        o_ref[...] = acc

    out = pl.pallas_call(
        body, name="rel_bias_grad", grid=(NH,),
        in_specs=[pl.BlockSpec((None, rows, _TOEP), lambda h: (h, 0, 0)), pl.BlockSpec((_TOEP, 256), lambda h: (0, 0))],
        out_specs=pl.BlockSpec((None, 8, 256), lambda h: (h, 0, 0)),
        out_shape=jax.ShapeDtypeStruct((NH, 8, 256), F32),
        compiler_params=_cp(("parallel",)),
    )(z, jnp.asarray(oh))
    return out[:, 0, :NREL]


def _exchange(arrays, scatter, name):
    n = len(arrays)

    def body(*refs):
        ins, outs = refs[:n], refs[n:2 * n]
        token, (send_sems, recv_sems, local_sems) = refs[2 * n], refs[2 * n + 1:]
        token[...] = jnp.zeros_like(token)
        x, y, c = lax.axis_index("x"), lax.axis_index("y"), lax.axis_index("c")
        me = 4 * x + 2 * y + c

        def peer_of(r):
            rx, ry, rc = (r >> 2) & 1, (r >> 1) & 1, r & 1
            px = 1 - x if rx else x
            py = 1 - y if ry else y
            pc = 1 - c if rc else c
            return (px, py, pc), 4 * px + 2 * py + pc

        local, sent = [], []
        for k in range(n):
            cp = pltpu.make_async_copy(ins[k].at[me] if scatter else ins[k], outs[k].at[me], local_sems.at[k])
            cp.start()
            local.append(cp)
            for r in range(1, NDEV):
                peer, peer_lin = peer_of(r)
                cp = pltpu.make_async_remote_copy(
                    src_ref=ins[k].at[peer_lin] if scatter else ins[k], dst_ref=outs[k].at[me],
                    send_sem=send_sems.at[k, r - 1], recv_sem=recv_sems.at[k, r - 1],
                    device_id=peer, device_id_type=pl.DeviceIdType.MESH)
                cp.start()
                sent.append(cp)
        for k in range(n):
            for r in range(1, NDEV):
                peer, peer_lin = peer_of(r)
                pltpu.make_async_remote_copy(
                    src_ref=ins[k].at[peer_lin] if scatter else ins[k], dst_ref=outs[k].at[peer_lin],
                    send_sem=send_sems.at[k, r - 1], recv_sem=recv_sems.at[k, r - 1],
                    device_id=peer, device_id_type=pl.DeviceIdType.MESH).wait_recv()
        for cp in sent:
            cp.wait_send()
        for cp in local:
            cp.wait()

    def slot_shape(a):
        return (NDEV,) + (a.shape[1:] if scatter else a.shape)

    anyspec = pl.BlockSpec(memory_space=pl.ANY)
    outs = pl.pallas_call(
        body, name=name,
        in_specs=[anyspec] * n, out_specs=[anyspec] * n + [pl.BlockSpec(memory_space=pltpu.VMEM)],
        out_shape=[jax.ShapeDtypeStruct(slot_shape(a), a.dtype) for a in arrays]
        + [jax.ShapeDtypeStruct((8, 128), F32)],
        scratch_shapes=[pltpu.SemaphoreType.DMA((n, NDEV - 1)), pltpu.SemaphoreType.DMA((n, NDEV - 1)),
                        pltpu.SemaphoreType.DMA((n,))],
        compiler_params=pltpu.CompilerParams(has_side_effects=True),
    )(*arrays)
    return outs[:n], outs[n]


def _peer(r):
    x, y, c = lax.axis_index("x"), lax.axis_index("y"), lax.axis_index("c")
    px = 1 - x if (r >> 2) & 1 else x
    py = 1 - y if (r >> 1) & 1 else y
    pc = 1 - c if r & 1 else c
    return (px, py, pc), 4 * px + 2 * py + pc


def _my_index():
    return 4 * lax.axis_index("x") + 2 * lax.axis_index("y") + lax.axis_index("c")


_HBM_SPEC = pl.BlockSpec(memory_space=pltpu.HBM)
_SEM_SPEC = pl.BlockSpec(memory_space=pltpu.SEMAPHORE)


_NPEER = NDEV - 1


def _exchange_start(arrays, scatter, name):
    n = len(arrays)
    ns = n * _NPEER
    slots = [(NDEV,) + (a.shape[1:] if scatter else a.shape) for a in arrays]

    def body(*refs):
        srcs, lands = refs[:n], refs[n:2 * n]
        send_sems, recv_sems = refs[2 * n:2 * n + ns], refs[2 * n + ns:2 * n + 2 * ns]
        token = refs[-1]
        me = _my_index()
        for k in range(n):
            for r in range(1, NDEV):
                peer, peer_lin = _peer(r)
                pltpu.make_async_remote_copy(
                    src_ref=srcs[k].at[peer_lin] if scatter else srcs[k], dst_ref=lands[k].at[me],
                    send_sem=send_sems[k * _NPEER + r - 1], recv_sem=recv_sems[k * _NPEER + r - 1],
                    device_id=peer, device_id_type=pl.DeviceIdType.MESH).start()
        token[...] = jnp.zeros_like(token)

    sem = pltpu.SemaphoreType.DMA(())
    outs = pl.pallas_call(
        body, name=name,
        out_shape=(*[sem] * (2 * ns), *[pltpu.HBM(a.shape, a.dtype) for a in arrays],
                   *[pltpu.HBM(s, a.dtype) for s, a in zip(slots, arrays)], jax.ShapeDtypeStruct((8, 128), F32)),
        in_specs=[_HBM_SPEC] * (2 * n),
        out_specs=(*[_SEM_SPEC] * (2 * ns), *[_HBM_SPEC] * (2 * n), pl.BlockSpec(memory_space=pltpu.VMEM)),
        input_output_aliases={k: 2 * ns + k for k in range(2 * n)},
        compiler_params=pltpu.CompilerParams(has_side_effects=pltpu.SideEffectType.DATAFLOW_SIDE_EFFECTING),
    )(*[pltpu.with_memory_space_constraint(a, pltpu.HBM) for a in arrays],
      *[pltpu.with_memory_space_constraint(lax.empty(s, a.dtype), pltpu.HBM) for s, a in zip(slots, arrays)])
    return outs[:ns], outs[ns:2 * ns], outs[2 * ns:2 * ns + n], outs[2 * ns + n:2 * ns + 2 * n], outs[-1]


def _exchange_wait(started, after, scatter, name):
    send_sems, recv_sems, srcs, lands, _ = started
    n = len(srcs)
    ns = n * _NPEER

    def body(*refs):
        src_refs, land_refs = refs[:n], refs[n:2 * n]
        ssem, rsem = refs[2 * n:2 * n + ns], refs[2 * n + ns:2 * n + 2 * ns]
        for k in range(n):
            for r in range(1, NDEV):
                peer, peer_lin = _peer(r)
                cp = pltpu.make_async_remote_copy(
                    src_ref=src_refs[k].at[peer_lin] if scatter else src_refs[k], dst_ref=land_refs[k].at[peer_lin],
                    send_sem=ssem[k * _NPEER + r - 1], recv_sem=rsem[k * _NPEER + r - 1],
                    device_id=peer, device_id_type=pl.DeviceIdType.MESH)
                cp.wait_send()
                cp.wait_recv()

    outs = pl.pallas_call(
        body, name=name,
        out_shape=tuple(pltpu.HBM(a.shape, a.dtype) for a in list(srcs) + list(lands)),
        in_specs=[_HBM_SPEC] * (2 * n) + [_SEM_SPEC] * (2 * ns) + [pl.BlockSpec(memory_space=pl.ANY)],
        out_specs=tuple([_HBM_SPEC] * (2 * n)),
        input_output_aliases={k: k for k in range(2 * n)},
        compiler_params=pltpu.CompilerParams(has_side_effects=pltpu.SideEffectType.DATAFLOW_SIDE_EFFECTING),
    )(*srcs, *lands, *send_sems, *recv_sems, after)
    return list(outs[:n]), list(outs[n:])


def _fill_own(lands, owns, me):
    return [lax.dynamic_update_slice(z, o, (me,) + (0,) * (z.ndim - 1)) for z, o in zip(lands, owns)]


def _sum_slots(st, name):
    _, r, c = st.shape

    def body(s_ref, o_ref):
        acc = s_ref[0]
        for d in range(1, NDEV):
            acc = acc + s_ref[d]
        o_ref[...] = acc

    return pl.pallas_call(
        body, name=name, out_shape=jax.ShapeDtypeStruct((r, c), F32),
        in_specs=[pl.BlockSpec((NDEV, r, c), lambda: (0, 0, 0))], out_specs=pl.BlockSpec((r, c), lambda: (0, 0)),
    )(st)


def _adamw(w, m, v, gst, name, transposed=False):
    r, c = w.shape
    ns = gst.shape[0]
    tr = min(r, 256)
    c1 = 1.0 - ADAM_B1 ** ADAM_STEP
    c2 = 1.0 - ADAM_B2 ** ADAM_STEP

    def body(w_ref, m_ref, v_ref, g_ref, go_ref, d_ref, mo_ref, vo_ref):
        g = g_ref[0].astype(F32)
        for d in range(1, ns):
            g = g + g_ref[d].astype(F32)
        if transposed:
            g = g.T
        m2 = ADAM_B1 * m_ref[...] + (1.0 - ADAM_B1) * g
        v2 = ADAM_B2 * v_ref[...] + (1.0 - ADAM_B2) * (g * g)
        go_ref[...] = g
        mo_ref[...] = m2
        vo_ref[...] = v2
        d_ref[...] = (-ADAM_LR) * ((m2 / c1) / (jnp.sqrt(v2 / c2) + ADAM_EPS) + ADAM_WD * w_ref[...])

    blk = pl.BlockSpec((tr, c), lambda i: (i, 0))
    sd = jax.ShapeDtypeStruct((r, c), F32)
    return pl.pallas_call(
        body, name=name, grid=(r // tr,),
        in_specs=[blk, blk, blk, pl.BlockSpec((ns, c, tr), lambda i: (0, 0, i)) if transposed
                  else pl.BlockSpec((ns, tr, c), lambda i: (0, i, 0))],
        out_specs=[blk, blk, blk, blk], out_shape=[sd, sd, sd, sd],
        compiler_params=_cp(("parallel",)),
    )(w, m, v, gst)


def _pack(pieces, rows):
    flat = jnp.concatenate([p.reshape(-1).astype(F32) for p in pieces])
    return jnp.pad(flat, (0, rows * 128 - flat.shape[0])).reshape(rows, 128)


def _unpack(flat, shapes):
    out, off = [], 0
    for shp in shapes:
        size = int(np.prod(shp))
        out.append(flat[off:off + size].reshape(shp))
        off += size
    return out


def _cols(full, me, width):
    return lax.dynamic_slice_in_dim(full, me * width, width, axis=full.ndim - 1)


def kernel(x, a_norm, a_w_in, a_conv_w, a_conv_b, a_w_gate, a_b_gate, a_lambda, a_w_out, kv_norm, w_kv, k_norm, b_norm, b_w_q, b_q_norm, b_rel_bias, b_w_o, mlp_norm, w_up, w_down, loss_target, m_a_norm, m_a_w_in, m_a_conv_w, m_a_conv_b, m_a_w_gate, m_a_b_gate, m_a_lambda, m_a_w_out, m_kv_norm, m_w_kv, m_k_norm, m_b_norm, m_b_w_q, m_b_q_norm, m_b_rel_bias, m_b_w_o, m_mlp_norm, m_w_up, m_w_down, v_a_norm, v_a_w_in, v_a_conv_w, v_a_conv_b, v_a_w_gate, v_a_b_gate, v_a_lambda, v_a_w_out, v_kv_norm, v_w_kv, v_k_norm, v_b_norm, v_b_w_q, v_b_q_norm, v_b_rel_bias, v_b_w_o, v_mlp_norm, v_w_up, v_w_down):
    me = 4 * lax.axis_index("x") + 2 * lax.axis_index("y") + lax.axis_index("c")
    sh = D // NDEV

    big_w = [a_w_in[0], a_w_out[0], w_kv, b_w_q[0], b_w_o[0], w_up[0], w_up[1], w_down[0], w_down[1]]
    small_sharded = [a_norm, a_conv_w, a_conv_b, a_b_gate, a_lambda, a_w_gate]
    small_rows = 272
    def to_bf16(w, token):
        return (w + token[0, 0]).astype(BF16)

    got, tok_a = _exchange([a_w_in[0].astype(BF16), _pack(small_sharded, small_rows)], False, "gather_a")
    own_b1 = [to_bf16(w, tok_a) for w in (a_w_out[0], w_up[0], w_down[0])]
    st_b1 = _exchange_start(own_b1, False, "gather_b1_start")
    own_b2 = [to_bf16(w, st_b1[4]) for w in (w_kv, b_w_q[0], b_w_o[0], w_up[1], w_down[1])]
    st_b2 = _exchange_start(own_b2, False, "gather_b2_start")
    w_in = got[0]
    sm = got[1].reshape(NDEV, small_rows * 128)
    an_f = sm[:, 0:128].reshape(1, D) + st_b2[4][0:1, 0:1]
    cw_f = sm[:, 128:640].reshape(NDEV, 4, sh).transpose(1, 0, 2).reshape(4, D)
    cb_f = sm[:, 640:768].reshape(1, D)
    bg_f = sm[:, 768:1024].reshape(NDEV, NBLK, 2 * BW // NDEV).transpose(1, 0, 2).reshape(NBLK, 2 * BW)
    lam_f = sm[:, 1024:1152].reshape(1, D)
    wg_f = sm[:, 1152:1152 + NBLK * BW * 32].reshape(NDEV, NBLK, BW, 32).transpose(1, 2, 0, 3)
    wg_f = wg_f.reshape(NBLK, BW, 2 * BW).astype(BF16)
    kn_t = jnp.tile(k_norm, NH).reshape(1, D)
    qn_t = jnp.tile(b_q_norm[0], NH).reshape(1, D)
    kvn = kv_norm.reshape(1, D)
    diag = _bias_diagonals(b_rel_bias[0])

    h0 = x[0]
    gate, rec, hs, y, n1 = _lru_fwd(h0, an_f, w_in, cw_f, cb_f, wg_f, bg_f, lam_f)
    own, land = _exchange_wait(st_b1, y, False, "gather_b1_wait")
    land = _fill_own(land, [o[None] for o in own], me)
    w_out = land[0].reshape(D, D)
    wu = [land[1], None]
    wd = [land[2].reshape(FF, D), None]
    h1, h2, up0 = _mlp_fwd(h0, y, w_out, mlp_norm[0:1], wu[0], wd[0], "mlp_fwd0")
    own, land = _exchange_wait(st_b2, h2, False, "gather_b2_wait")
    land = _fill_own(land, [o[None] for o in own], me)
    wkv = land[0]
    w_q = land[1].reshape(D, D)
    w_o = land[2].reshape(D, D)
    wu[1] = land[3]
    wd[1] = land[4].reshape(FF, D)
    kraw, qraw, kpad, vpad, q = _kvq_fwd(h2, kvn, b_norm, wkv, w_q, kn_t, qn_t)
    o = _attn_fwd(q, kpad, vpad, diag)
    h3, g4, up1, lpart = _mlp_fwd(h2, o, w_o, mlp_norm[1:2], wu[1], wd[1], "mlp_fwd1", tgt=loss_target[0])
    loss = lax.psum(jnp.sum(lpart), ("x", "y", "c"))

    g3, dup1, n3, dgm1 = _mlp_bwd(g4, h3, up1, mlp_norm[1:2], wu[1], wd[1], "mlp_bwd1")
    d_wd1 = _matmul_tn(g4, up1, True, BF16, "dw_down1", b_relu2=True)
    d_wu1 = _matmul_tn(n3, dup1, True, BF16, "dw_up1")
    do = _matmul_nt(g3, w_o, "do_proj")
    d_wo = _matmul_tn(o, g3, False, BF16, "dw_o").reshape(NDEV, sh, D)
    dq, dk, dv, dd = _attn_bwd(q, kpad, vpad, do, diag)
    g2, dqr, dkv, nb, nk, dgq, dgk, dgb, dgkv = _kvq_bwd(dq, dk, dv, qraw, kraw, h2, g3, kvn, b_norm, wkv, w_q,
                                                       kn_t, qn_t)
    d_wq = _matmul_tn(nb, dqr, False, BF16, "dw_q").reshape(NDEV, sh, D)
    d_wkv = _matmul_tn(nk, dkv, True, BF16, "dw_kv")
    st_r1 = _exchange_start([d_wd1, d_wu1, d_wo, d_wq, d_wkv], True, "scatter_r1_start")
    g1, dup0, n2, dgm0 = _mlp_bwd(g2, h1, up0, mlp_norm[0:1] + st_r1[4][0:1, 0:1], wu[0], wd[0], "mlp_bwd0")
    d_wd0 = _matmul_tn(g2, up0, True, BF16, "dw_down0", b_relu2=True)
    d_wu0 = _matmul_tn(n2, dup0, True, BF16, "dw_up0")
    d_wout = _matmul_tn(y, g1, False, BF16, "dw_out").reshape(NDEV, sh, D)
    st_r2 = _exchange_start([d_wu0, d_wd0, d_wout], True, "scatter_r2_start")
    du, dcw, dcb, dwg, dbg, dlam = _lru_bwd(g1, gate, rec, hs, w_out, cw_f, cb_f, wg_f, bg_f,
                                            lam_f + st_r2[4][0:1, 0:1])
    d_rel = _rel_bias_grad(dd.reshape(NH, 8, _TOEP))
    d_win = _matmul_tn(n1, du, True, BF16, "dw_in")
    dwg_slab = dwg.reshape(NBLK, BW, NDEV, 32).transpose(2, 0, 1, 3).reshape(NDEV, 256, 128).astype(BF16)
    st_r3 = _exchange_start([d_win, dwg_slab], True, "scatter_r3_start")
    gx, dga = _a_in_bwd(du, h0, g1, an_f + st_r3[4][0:1, 0:1], w_in)
    small_full = [dga.sum(0), dcw.sum(1), dcb.sum(0), dbg.sum(1), dlam.sum(0), dgkv.sum(0),
                  dgk.sum(0).reshape(NH, HD).sum(0), dgb.sum(0), dgq.sum(0).reshape(NH, HD).sum(0), d_rel,
                  jnp.stack([dgm0.sum(0), dgm1.sum(0)])]
    small_g_rows = 136
    small_slabs = jnp.broadcast_to(_pack(small_full, small_g_rows)[None], (NDEV, small_g_rows, 128))
    st_r4 = _exchange_start([small_slabs], True, "scatter_r4_start")
    src, recv1 = _exchange_wait(st_r1, st_r4[4], True, "scatter_r1_wait")
    recv1 = _fill_own(recv1, [lax.dynamic_slice_in_dim(a, me, 1, 0) for a in src], me)
    src, recv2 = _exchange_wait(st_r2, recv1[0], True, "scatter_r2_wait")
    recv2 = _fill_own(recv2, [lax.dynamic_slice_in_dim(a, me, 1, 0) for a in src], me)

    names = ["a_w_in", "a_w_out", "w_kv", "b_w_q", "b_w_o", "w_up0", "w_up1", "w_down0", "w_down1", "a_w_gate"]
    big_m = [m_a_w_in[0], m_a_w_out[0], m_w_kv, m_b_w_q[0], m_b_w_o[0], m_w_up[0], m_w_up[1], m_w_down[0],
             m_w_down[1], m_a_w_gate.reshape(256, 128)]
    big_v = [v_a_w_in[0], v_a_w_out[0], v_w_kv, v_b_w_q[0], v_b_w_o[0], v_w_up[0], v_w_up[1], v_w_down[0],
             v_w_down[1], v_a_w_gate.reshape(256, 128)]
    big_w = big_w + [a_w_gate.reshape(256, 128)]

    def update(k, g):
        return _adamw(big_w[k], big_m[k], big_v[k], g, "adamw_" + names[k], transposed=names[k].startswith("w_down"))

    early = {1: recv2[2], 2: recv1[4], 3: recv1[3], 4: recv1[2], 5: recv2[0], 6: recv1[1], 7: recv2[1], 8: recv1[0]}
    res = {k: update(k, g) for k, g in early.items()}
    src, recv3 = _exchange_wait(st_r3, res[8][1], True, "scatter_r3_wait")
    recv3 = _fill_own(recv3, [lax.dynamic_slice_in_dim(a, me, 1, 0) for a in src], me)
    res[0] = update(0, recv3[0])
    res[9] = update(9, recv3[1])
    res = [res[k] for k in range(len(names))]
    src, recv4 = _exchange_wait(st_r4, res[0][1], True, "scatter_r4_wait")
    recv4 = _fill_own(recv4, [lax.dynamic_slice_in_dim(a, me, 1, 0) for a in src], me)
    gs = _unpack(_sum_slots(recv4[0], "sum_small_grads").reshape(-1),
                 [(1, D), (4, D), (1, D), (NBLK, 2 * BW), (1, D), (D,), (HD,), (1, D), (1, HD), (1, NH, NREL), (2, D)])
    g_small = [_cols(gs[0], me, sh), _cols(gs[1], me, sh)[None], _cols(gs[2], me, sh),
               _cols(gs[3], me, 2 * BW // NDEV)[None], _cols(gs[4], me, sh)] + gs[5:]
    small_w = [a_norm, a_conv_w, a_conv_b, a_b_gate, a_lambda, kv_norm, k_norm, b_norm, b_q_norm, b_rel_bias, mlp_norm]
    small_m = [m_a_norm, m_a_conv_w, m_a_conv_b, m_a_b_gate, m_a_lambda, m_kv_norm, m_k_norm, m_b_norm, m_b_q_norm,
               m_b_rel_bias, m_mlp_norm]
    small_v = [v_a_norm, v_a_conv_w, v_a_conv_b, v_a_b_gate, v_a_lambda, v_kv_norm, v_k_norm, v_b_norm, v_b_q_norm,
               v_b_rel_bias, v_mlp_norm]
    pr = 72
    res_small = _adamw(_pack(small_w, pr), _pack(small_m, pr), _pack(small_v, pr), _pack(g_small, pr)[None],
                       "adamw_small")
    small_shapes = [w.shape for w in small_w]
    res_small = [_unpack(r.reshape(-1), small_shapes) for r in res_small]

    def assemble(t):
        b = [r[t] for r in res]
        s_ = res_small[t]
        return [s_[0], b[0][None], s_[1], s_[2], b[9].reshape(a_w_gate.shape), s_[3], s_[4], b[1][None],
                s_[5], b[2], s_[6], s_[7], b[3][None], s_[8], s_[9], b[4][None], s_[10],
                jnp.stack([b[5], b[6]]), jnp.stack([b[7], b[8]])]

    return tuple([loss, gx[None]] + assemble(0) + assemble(1) + assemble(2) + assemble(3))
```

```python
import functools

import numpy as np
import jax
import jax.numpy as jnp
from jax import lax
from jax.experimental import pallas as pl
from jax.experimental.pallas import tpu as pltpu

F32 = jnp.float32
BF16 = jnp.bfloat16

D = 1024
NH = 16
HD = 64
FF = 4096
NBLK = 8
BW = 128
CHUNK = 64
PADK = 512
NREL = 192
EPS = 1e-6
LRU_C = 8.0
NDEV = 8

V7X_VMEM_LIMIT = 56 * 1024 * 1024
TM = 512
TMM = 512
TMF = 512
TL = 256
QB = 256
ATT_RC = 32
HPS = 8
LW = HPS * HD
KB = QB + PADK
NEG = -1e30

ADAM_LR, ADAM_B1, ADAM_B2, ADAM_EPS, ADAM_WD, ADAM_STEP = 0.001, 0.9, 0.999, 1e-08, 0.01, 10

_NT = (((1,), (1,)), ((), ()))
_TN = (((0,), (0,)), ((), ()))


def _cp(sem=None):
    return pltpu.CompilerParams(dimension_semantics=sem, vmem_limit_bytes=V7X_VMEM_LIMIT)


def _full(shape):
    n = len(shape)
    return pl.BlockSpec(shape, lambda *a: (0,) * n, pipeline_mode=pl.Buffered(1))


def _rows(tm, width):
    return pl.BlockSpec((tm, width), lambda i: (i, 0))


def _rstd(h):
    return lax.rsqrt(jnp.mean(h * h, axis=-1, keepdims=True) + EPS)


def _sigmoid(x):
    return 1.0 / (1.0 + jnp.exp(-x))


def _one_minus_sq(a, la):
    x = 2.0 * la
    series = -x * (1.0 + x * (0.5 + x * (1.0 / 6.0)))
    return jnp.where(x > -0.01, series, 1.0 - a * a)


def _softplus_neg(lam):
    e = jnp.exp(-jnp.abs(lam))
    series = e * (1.0 - e * (0.5 - e * (1.0 / 3.0 - e * 0.25)))
    return jnp.maximum(-lam, 0.0) + jnp.where(e < 0.01, series, jnp.log(1.0 + e))


_GELU_K = 0.7978845608028654


def _gelu(x):
    return 0.5 * x * (1.0 + jnp.tanh(_GELU_K * (x + 0.044715 * x * x * x)))


def _gelu_and_grad(x):
    x2 = x * x
    t = jnp.tanh(_GELU_K * (x + 0.044715 * x * x2))
    half = 0.5 * (1.0 + t)
    return x * half, half + 0.5 * x * (1.0 - t * t) * (_GELU_K * (1.0 + 3.0 * 0.044715 * x2))


def _sum8(x):
    r, c = x.shape
    return jnp.sum(x.reshape(r // 8, 8, c), axis=0)


def _shift_down(x, s, fill, rows):
    return jnp.where(rows >= s, pltpu.roll(x, s, axis=0), fill)


def _shift_up(x, s, fill, rows, n):
    return jnp.where(rows < n - s, pltpu.roll(x, n - s, axis=0), fill)


def _lru_gates(rc, wg_n, bg_n, sp_n):
    g = jnp.dot(rc.astype(BF16), wg_n, preferred_element_type=F32) + bg_n
    rg = _sigmoid(g[:, :BW])
    ig = _sigmoid(g[:, BW:])
    la = (-LRU_C) * rg * sp_n
    a = jnp.exp(la)
    mult = jnp.sqrt(_one_minus_sq(a, la))
    return rg, ig, a, mult


def _conv(ext_ref, cw_ref, cb_ref, sl, n):
    out = cb_ref[:, sl] + cw_ref[0:1, sl] * ext_ref[5:5 + n, sl]
    for k in range(1, 4):
        out = out + cw_ref[k:k + 1, sl] * ext_ref[5 + k:5 + k + n, sl]
    return out


def _tile_scan(a_ref, b_ref, h_ref, carry, reverse):
    sub = lax.broadcasted_iota(jnp.int32, (TL, BW), 0) % 8
    for n in range(NBLK):
        a, b = a_ref[n], b_ref[n]
        for s in (1, 2, 4):
            if reverse:
                inside = sub < 8 - s
                a_sh = jnp.where(inside, pltpu.roll(a, TL - s, axis=0), 1.0)
                b_sh = jnp.where(inside, pltpu.roll(b, TL - s, axis=0), 0.0)
            else:
                inside = sub >= s
                a_sh = jnp.where(inside, pltpu.roll(a, s, axis=0), 1.0)
                b_sh = jnp.where(inside, pltpu.roll(b, s, axis=0), 0.0)
            b = a * b_sh + b
            a = a * a_sh
        a_ref[n], b_ref[n] = a, b
    carry = list(carry)
    groups = range(TL // 8 - 1, -1, -1) if reverse else range(TL // 8)
    for g in groups:
        r = slice(8 * g, 8 * g + 8)
        for n in range(NBLK):
            h = a_ref[n, r, :] * carry[n] + b_ref[n, r, :]
            h_ref[n, r, :] = h
            carry[n] = h[0:1, :] if reverse else h[7:8, :]
    return carry


def _lru_fwd(h0, a_norm, w_in, conv_w, conv_b, wg, bg, lam):
    s = h0.shape[0]

    def body(h0_ref, an_ref, win_ref, cw_ref, cb_ref, wg_ref, bg_ref, lam_ref,
             gate_ref, rec_ref, hs_ref, y_ref, n1_ref, ext_ref, hc_ref, a_sc, b_sc, hl_sc):
        i = pl.program_id(0)

        @pl.when(i == 0)
        def _():
            ext_ref[0:8, :] = jnp.zeros((8, D), F32)
            hc_ref[...] = jnp.zeros_like(hc_ref)

        h = h0_ref[...]
        n1 = (h * _rstd(h) * an_ref[...]).astype(BF16)
        n1_ref[...] = n1
        cw = 2 * D // NDEV
        for d in range(NDEV):
            ud = jnp.dot(n1, win_ref[d], preferred_element_type=F32)
            if d < NDEV // 2:
                gate_ref[:, d * cw:(d + 1) * cw] = ud
            else:
                rec_ref[:, d * cw - D:(d + 1) * cw - D] = ud
                ext_ref[8:8 + TL, d * cw - D:(d + 1) * cw - D] = ud
        sp = _softplus_neg(lam_ref[...])
        for n in range(NBLK):
            sl = slice(n * BW, (n + 1) * BW)
            rc = _conv(ext_ref, cw_ref, cb_ref, sl, TL)
            rg, ig, a, mult = _lru_gates(rc, wg_ref[n], bg_ref[n:n + 1, :], sp[:, sl])
            a_sc[n] = a
            b_sc[n] = mult * (ig * rc)
        carry = _tile_scan(a_sc, b_sc, hl_sc, [hc_ref[0:1, n * BW:(n + 1) * BW] for n in range(NBLK)], reverse=False)
        for n in range(NBLK):
            sl = slice(n * BW, (n + 1) * BW)
            hh = hl_sc[n]
            hc_ref[0:1, sl] = carry[n]
            hs_ref[:, sl] = hh
            y_ref[:, sl] = (_gelu(gate_ref[:, sl]) * hh).astype(BF16)
        ext_ref[0:8, :] = ext_ref[TL:TL + 8, :]

    row = _rows(TL, D)
    return pl.pallas_call(
        body, name="lru_fwd", grid=(s // TL,),
        in_specs=[row, _full((1, D)), _full((NDEV, D, 2 * D // NDEV)), _full((4, D)), _full((1, D)),
                  _full((NBLK, BW, 2 * BW)), _full((NBLK, 2 * BW)), _full((1, D))],
        out_specs=[row, row, row, row, row],
        out_shape=[jax.ShapeDtypeStruct((s, D), F32), jax.ShapeDtypeStruct((s, D), F32),
                   jax.ShapeDtypeStruct((s, D), F32), jax.ShapeDtypeStruct((s, D), BF16),
                   jax.ShapeDtypeStruct((s, D), BF16)],
        scratch_shapes=[pltpu.VMEM((TL + 8, D), F32), pltpu.VMEM((8, D), F32)]
        + [pltpu.VMEM((NBLK, TL, BW), F32)] * 3,
        compiler_params=_cp(("arbitrary",)),
    )(h0, a_norm, w_in, conv_w, conv_b, wg, bg, lam)


def _mlp_fwd(res, px, pw, g, wu, wd, name, tgt=None):
    s = res.shape[0]
    fj = 512
    with_loss = tgt is not None

    def body(res_ref, px_ref, pw_ref, g_ref, wu_ref, wd_ref, *rest):
        if with_loss:
            t_ref, hin_ref, hout_ref, up_ref, l_ref, n_ref = rest

            @pl.when(pl.program_id(0) == 0)
            def _():
                l_ref[...] = jnp.zeros_like(l_ref)
        else:
            hin_ref, hout_ref, up_ref, n_ref = rest
        hin = res_ref[...] + jnp.dot(px_ref[...], pw_ref[...], preferred_element_type=F32)
        hin_ref[...] = hin
        hout_ref[...] = hin
        n_ref[...] = (hin * _rstd(hin) * g_ref[...]).astype(BF16)
        for j in range(FF // fj):
            sl = slice(j * fj, (j + 1) * fj)
            up = jnp.dot(n_ref[...], wu_ref[j], preferred_element_type=F32)
            up_ref[:, sl] = up.astype(BF16)
            rl = jnp.maximum(up, 0.0)
            hout_ref[...] += jnp.dot((rl * rl).astype(BF16), wd_ref[sl, :], preferred_element_type=F32)
        if with_loss:
            d = hout_ref[...] - t_ref[...]
            hout_ref[...] = d * (1.0 / D)
            l_ref[...] += _sum8(d * d) * (0.5 / D)

    row = _rows(TMF, D)
    outs = pl.pallas_call(
        body, name=name, grid=(s // TMF,),
        in_specs=[row, row, _full((D, D)), _full((1, D)), _full((NDEV, D, fj)), _full((FF, D))]
        + ([row] if with_loss else []),
        out_specs=[row, row, _rows(TMF, FF)] + ([_acc_spec()] if with_loss else []),
        out_shape=[jax.ShapeDtypeStruct((s, D), F32), jax.ShapeDtypeStruct((s, D), F32),
                   jax.ShapeDtypeStruct((s, FF), BF16)] + ([jax.ShapeDtypeStruct((8, D), F32)] if with_loss else []),
        scratch_shapes=[pltpu.VMEM((TMF, D), BF16)],
        compiler_params=_cp(("arbitrary",) if with_loss else ("parallel",)),
    )(*([res, px, pw, g, wu, wd] + ([tgt] if with_loss else [])))
    return outs


def _head_rstd(x2, lo):
    sq = x2 * x2
    s_lo = jnp.sum(jnp.where(lo, sq, 0.0), axis=-1, keepdims=True)
    s_hi = jnp.sum(jnp.where(lo, 0.0, sq), axis=-1, keepdims=True)
    return lax.rsqrt(jnp.where(lo, s_lo, s_hi) * (1.0 / HD) + EPS)


def _kvq_fwd(h2, kv_norm, b_norm, w_kv, w_q, k_norm_t, q_norm_t):
    s = h2.shape[0]
    assert PADK == TM

    def body(h_ref, gkv_ref, gb_ref, wkv_ref, wq_ref, kn_ref, qn_ref,
             kraw_ref, qraw_ref, k_ref, v_ref, q_ref):
        i = pl.program_id(0)

        @pl.when(i == 0)
        def _():
            k_ref[...] = jnp.zeros_like(k_ref)
            v_ref[...] = jnp.zeros_like(v_ref)

        @pl.when(i > 0)
        def _():
            h = h_ref[...]
            xhat = h * _rstd(h)
            nk = (xhat * gkv_ref[...]).astype(BF16)
            qr = jnp.dot((xhat * gb_ref[...]).astype(BF16), wq_ref[...], preferred_element_type=F32)
            qraw_ref[...] = qr
            lo = lax.broadcasted_iota(jnp.int32, (1, 128), 1) < HD
            cw = 2 * D // NDEV
            for d in range(NDEV):
                kvd = jnp.dot(nk, wkv_ref[d], preferred_element_type=F32)
                if d < NDEV // 2:
                    kraw_ref[:, d * cw:(d + 1) * cw] = kvd
                    for p in range(cw // 128):
                        sl = slice(d * cw + p * 128, d * cw + (p + 1) * 128)
                        k2 = kvd[:, p * 128:(p + 1) * 128]
                        k_ref[:, sl] = (k2 * _head_rstd(k2, lo) * kn_ref[:, sl]).astype(BF16)
                else:
                    v_ref[:, d * cw - D:(d + 1) * cw - D] = kvd.astype(BF16)
            for p in range(D // 128):
                sl = slice(p * 128, (p + 1) * 128)
                q2 = qr[:, sl]
                q_ref[:, sl] = (q2 * _head_rstd(q2, lo) * qn_ref[:, sl] * (HD ** -0.5)).astype(BF16)

    prev = pl.BlockSpec((TM, D), lambda i: (jnp.maximum(i - 1, 0), 0))
    cur = pl.BlockSpec((TM, D), lambda i: (i, 0))
    return pl.pallas_call(
        body, name="kvq_fwd", grid=(s // TM + 1,),
        in_specs=[prev, _full((1, D)), _full((1, D)), _full((NDEV, D, 2 * D // NDEV)), _full((D, D)), _full((1, D)),
                  _full((1, D))],
        out_specs=[prev, prev, cur, cur, prev],
        out_shape=[jax.ShapeDtypeStruct((s, D), F32), jax.ShapeDtypeStruct((s, D), F32),
                   jax.ShapeDtypeStruct((s + PADK, D), BF16), jax.ShapeDtypeStruct((s + PADK, D), BF16),
                   jax.ShapeDtypeStruct((s, D), BF16)],
        compiler_params=_cp(("arbitrary",)),
    )(h2, kv_norm, b_norm, w_kv, w_q, k_norm_t, q_norm_t)


_TOEP = QB + KB


def _bias_from_diag(diag_ref, bias_ref):
    row8 = lax.broadcasted_iota(jnp.int32, (8, _TOEP), 0)
    kchunk = lax.broadcasted_iota(jnp.int32, (8, KB), 1) // CHUNK
    for a in range(HPS):
        v = jnp.broadcast_to(diag_ref[a:a + 1, :], (8, _TOEP))
        z0 = v
        for b in range(1, 8):
            z0 = jnp.where(row8 == b, pltpu.roll(v, b, axis=1), z0)
        for t in range(QB // 8):
            slab = z0 if t == 0 else pltpu.roll(z0, 8 * t, axis=1)
            qchunk = (8 * t) // CHUNK
            band = jnp.logical_and(kchunk >= qchunk, kchunk <= qchunk + PADK // CHUNK)
            bias_ref[a, 8 * t:8 * t + 8, :] = jnp.where(band, slab[:, :KB], NEG)


def _diag_sums(db_ref, a):
    row8 = lax.broadcasted_iota(jnp.int32, (8, _TOEP), 0)
    z = jnp.zeros((8, _TOEP), F32)
    for t in range(QB // 8):
        slab = jnp.concatenate([db_ref[a, 8 * t:8 * t + 8, :], jnp.zeros((8, _TOEP - KB), F32)], axis=1)
        z = z + (slab if t == 0 else pltpu.roll(slab, _TOEP - 8 * t, axis=1))
    e = z
    for b in range(1, 8):
        e = jnp.where(row8 == b, pltpu.roll(z, _TOEP - b, axis=1), e)
    return e


def _attn_specs(nqb):
    qspec = pl.BlockSpec((QB, LW), lambda p, j: (jnp.minimum(j, nqb - 1), p))
    kspecs = [pl.BlockSpec((QB, LW), functools.partial(lambda p, j, t: (jnp.minimum(j, nqb - 1) + t, p), t=t))
              for t in range(KB // QB)]
    dspec = pl.BlockSpec((None, HPS, _TOEP), lambda p, j: (p, 0, 0))
    return qspec, kspecs, dspec


V7X_MXU = 256
HPT = V7X_MXU // HD


def _head_masks():
    head = lax.broadcasted_iota(jnp.int32, (1, V7X_MXU), 1) // HD
    return [head == t for t in range(HPT)]


def _tile_of(a):
    return slice((a // HPT) * V7X_MXU, (a // HPT + 1) * V7X_MXU)


def _pick_heads(parts, masks):
    tiles = []
    for g in range(HPS // HPT):
        out = parts[g * HPT + HPT - 1]
        for t in range(HPT - 2, -1, -1):
            out = jnp.where(masks[t], parts[g * HPT + t], out)
        tiles.append(out)
    return tiles[0] if len(tiles) == 1 else jnp.concatenate(tiles, axis=1)


def _attn_fwd(q, kpad, vpad, diag):
    s = q.shape[0]
    nqb = s // QB
    npad = PADK // QB

    def body(q_ref, k0, k1, k2, v0, v1, v2, diag_ref, o_ref, bias_ref, sc_ref, eb_ref, rl_ref):
        j = pl.program_id(1)

        @pl.when(j == 0)
        def _():
            _bias_from_diag(diag_ref, bias_ref)

        def block(masked):
            kcat = jnp.concatenate([k0[...], k1[...], k2[...]], axis=0)
            vcat = jnp.concatenate([v0[...], v1[...], v2[...]], axis=0)
            q2 = q_ref[...]
            masks = _head_masks()
            valid = (lax.broadcasted_iota(jnp.int32, (1, KB), 1) + j * QB >= PADK) if masked else None
            outs = []
            for a in range(HPS):
                qa = q2[:, _tile_of(a)]
                sc_ref[a] = lax.dot_general(jnp.where(masks[a % HPT], qa, jnp.zeros_like(qa)), kcat[:, _tile_of(a)],
                                            _NT, preferred_element_type=F32)
            for a in range(HPS):
                for c in range(QB // ATT_RC):
                    r = slice(c * ATT_RC, (c + 1) * ATT_RC)
                    sc = sc_ref[a, r, :] + bias_ref[a, r, :]
                    if masked:
                        sc = jnp.where(valid, sc, NEG)
                    e = jnp.exp(sc - jnp.max(sc, axis=-1, keepdims=True))
                    eb_ref[a, r, :] = e.astype(BF16)
                    rl_ref[a, r, :] = jnp.broadcast_to(1.0 / jnp.sum(e, axis=-1, keepdims=True),
                                                       (ATT_RC, V7X_MXU))
                outs.append(jnp.dot(eb_ref[a], vcat[:, _tile_of(a)], preferred_element_type=F32) * rl_ref[a])
            o_ref[...] = _pick_heads(outs, masks).astype(BF16)

        pl.when(j < npad)(functools.partial(block, True))
        pl.when(j >= npad)(functools.partial(block, False))

    qspec, kspecs, dspec = _attn_specs(nqb)
    assert len(kspecs) == 3
    return pl.pallas_call(
        body, name="attn_fwd", grid=(D // LW, nqb),
        in_specs=[qspec] + kspecs + kspecs + [dspec],
        out_specs=qspec,
        out_shape=jax.ShapeDtypeStruct((s, D), BF16),
        scratch_shapes=[pltpu.VMEM((HPS, QB, KB), F32), pltpu.VMEM((HPS, QB, KB), F32),
                        pltpu.VMEM((HPS, QB, KB), BF16), pltpu.VMEM((HPS, QB, V7X_MXU), F32)],
        compiler_params=_cp(("parallel", "arbitrary")),
    )(q, kpad, kpad, kpad, vpad, vpad, vpad, diag)


def _rms_bwd(dn, xhat, r, g):
    dng = dn * g
    return r * (dng - xhat * jnp.mean(dng * xhat, axis=-1, keepdims=True))


def _acc_spec():
    return pl.BlockSpec((8, D), lambda i: (0, 0))


def _mlp_bwd(gout, hin, up, g, wu, wd, name):
    s = gout.shape[0]
    fj = 512

    def body(go_ref, hin_ref, up_ref, g_ref, wu_ref, wd_ref, gin_ref, dup_ref, n_ref, dg_ref, gob_ref, dn_ref):
        @pl.when(pl.program_id(0) == 0)
        def _():
            dg_ref[...] = jnp.zeros_like(dg_ref)

        gob_ref[...] = go_ref[...].astype(BF16)
        for j in range(FF // fj):
            sl = slice(j * fj, (j + 1) * fj)
            rl = jnp.maximum(up_ref[:, sl].astype(F32), 0.0)
            dact = lax.dot_general(gob_ref[...], wd_ref[sl, :], _NT, preferred_element_type=F32)
            dupj = (dact * (2.0 * rl)).astype(BF16)
            dup_ref[:, sl] = dupj
            part = lax.dot_general(dupj, wu_ref[j], _NT, preferred_element_type=F32)
            if j == 0:
                dn_ref[...] = part
            else:
                dn_ref[...] += part
        hin = hin_ref[...]
        r = _rstd(hin)
        xhat = hin * r
        n_ref[...] = (xhat * g_ref[...]).astype(BF16)
        dn = dn_ref[...]
        gin_ref[...] = go_ref[...] + _rms_bwd(dn, xhat, r, g_ref[...])
        dg_ref[...] += _sum8(dn * xhat)

    row = _rows(TMM, D)
    wide = _rows(TMM, FF)
    return pl.pallas_call(
        body, name=name, grid=(s // TMM,),
        in_specs=[row, row, wide, _full((1, D)), _full((NDEV, D, fj)), _full((FF, D))],
        out_specs=[row, wide, row, _acc_spec()],
        out_shape=[jax.ShapeDtypeStruct((s, D), F32), jax.ShapeDtypeStruct((s, FF), BF16),
                   jax.ShapeDtypeStruct((s, D), BF16), jax.ShapeDtypeStruct((8, D), F32)],
        scratch_shapes=[pltpu.VMEM((TMM, D), BF16), pltpu.VMEM((TMM, D), F32)],
        compiler_params=_cp(("arbitrary",)),
    )(gout, hin, up, g, wu, wd)


def _matmul_tn(a, b, slab, out_dtype, name, b_relu2=False):
    s, m = a.shape
    n = b.shape[1]
    ts = min(s, 512 if n > 2048 else 1024)
    nk = s // ts
    nc = 512
    w = n // NDEV

    def body(a_ref, b_ref, o_ref, at_ref, acc_ref):
        k = pl.program_id(0)
        at_ref[...] = a_ref[...].astype(BF16).T

        @pl.when(k == 0)
        def _():
            acc_ref[...] = jnp.zeros_like(acc_ref)

        for c in range(n // nc):
            sl = slice(c * nc, (c + 1) * nc)
            bc = b_ref[:, sl]
            if b_relu2:
                rl = jnp.maximum(bc.astype(F32), 0.0)
                bc = rl * rl
            acc_ref[:, sl] += jnp.dot(at_ref[...], bc.astype(BF16), preferred_element_type=F32)

        @pl.when(k == nk - 1)
        def _():
            if slab:
                for d in range(NDEV):
                    o_ref[d] = acc_ref[:, d * w:(d + 1) * w].astype(out_dtype)
            else:
                o_ref[...] = acc_ref[...].astype(out_dtype)

    if slab:
        out_shape = jax.ShapeDtypeStruct((NDEV, m, w), out_dtype)
        out_spec = pl.BlockSpec((NDEV, m, w), lambda k: (0, 0, 0), pipeline_mode=pl.Buffered(1))
    else:
        out_shape = jax.ShapeDtypeStruct((m, n), out_dtype)
        out_spec = pl.BlockSpec((m, n), lambda k: (0, 0), pipeline_mode=pl.Buffered(1))
    return pl.pallas_call(
        body, name=name, grid=(nk,),
        in_specs=[pl.BlockSpec((ts, m), lambda k: (k, 0)), pl.BlockSpec((ts, n), lambda k: (k, 0))],
        out_specs=out_spec, out_shape=out_shape,
        scratch_shapes=[pltpu.VMEM((m, ts), BF16), pltpu.VMEM((m, n), F32)],
        compiler_params=_cp(("arbitrary",)),
    )(a, b)


def _matmul_nt(x, w, name):
    s, n = x.shape
    k = w.shape[0]

    def body(x_ref, w_ref, o_ref):
        o_ref[...] = lax.dot_general(x_ref[...].astype(BF16), w_ref[...], _NT,
                                     preferred_element_type=F32).astype(BF16)

    return pl.pallas_call(
        body, name=name, grid=(s // TM,),
        in_specs=[_rows(TM, n), _full((k, n))], out_specs=_rows(TM, k),
        out_shape=jax.ShapeDtypeStruct((s, k), BF16),
        compiler_params=_cp(("parallel",)),
    )(x, w)


def _attn_bwd(q, kpad, vpad, do, diag):
    s = q.shape[0]
    nqb = s // QB
    npad = PADK // QB

    def body(q_ref, k0, k1, k2, v0, v1, v2, do_ref, diag_ref, dq_ref, dk_ref, dv_ref, dd_ref,
             bias_ref, db_ref, dka_ref, dva_ref, sc_ref, dp_ref, dsb_ref, pb_ref):
        j = pl.program_id(1)

        @pl.when(j == 0)
        def _():
            _bias_from_diag(diag_ref, bias_ref)
            dka_ref[...] = jnp.zeros_like(dka_ref)
            dva_ref[...] = jnp.zeros_like(dva_ref)
            db_ref[...] = jnp.zeros_like(db_ref)

        def block(masked):
            kcat = jnp.concatenate([k0[...], k1[...], k2[...]], axis=0)
            vcat = jnp.concatenate([v0[...], v1[...], v2[...]], axis=0)
            q2 = q_ref[...]
            do2 = do_ref[...]
            masks = _head_masks()
            valid = (lax.broadcasted_iota(jnp.int32, (1, KB), 1) + j * QB >= PADK) if masked else None
            qt = q2.T
            dot_ = do2.T
            dq = []
            for a in range(HPS):
                qa, doa = q2[:, _tile_of(a)], do2[:, _tile_of(a)]
                sc_ref[a] = lax.dot_general(jnp.where(masks[a % HPT], qa, jnp.zeros_like(qa)), kcat[:, _tile_of(a)],
                                            _NT, preferred_element_type=F32)
                dp_ref[a] = lax.dot_general(jnp.where(masks[a % HPT], doa, jnp.zeros_like(doa)),
                                            vcat[:, _tile_of(a)], _NT, preferred_element_type=F32)
            for a in range(HPS):
                for c in range(QB // ATT_RC):
                    r = slice(c * ATT_RC, (c + 1) * ATT_RC)
                    sc = sc_ref[a, r, :] + bias_ref[a, r, :]
                    if masked:
                        sc = jnp.where(valid, sc, NEG)
                    e = jnp.exp(sc - jnp.max(sc, axis=-1, keepdims=True))
                    p = e * (1.0 / jnp.sum(e, axis=-1, keepdims=True))
                    dp = dp_ref[a, r, :]
                    ds = p * (dp - jnp.sum(p * dp, axis=-1, keepdims=True))
                    db_ref[a, r, :] += ds
                    dsb_ref[a, r, :] = ds.astype(BF16)
                    pb_ref[a, r, :] = p.astype(BF16)
                hd = slice(a * HD, (a + 1) * HD)
                dq.append(jnp.dot(dsb_ref[a], kcat[:, _tile_of(a)], preferred_element_type=F32))
                dka_ref[hd, :] += jnp.dot(qt[hd, :], dsb_ref[a], preferred_element_type=F32)
                dva_ref[hd, :] += jnp.dot(dot_[hd, :], pb_ref[a], preferred_element_type=F32)
            dq_ref[...] = _pick_heads(dq, masks) * (HD ** -0.5)

        pl.when(j < npad)(functools.partial(block, True))
        pl.when(jnp.logical_and(j >= npad, j < nqb))(functools.partial(block, False))

        @pl.when(j == nqb - 1)
        def _():
            for a in range(HPS):
                dd_ref[a] = _diag_sums(db_ref, a)

        dk_ref[...] = dka_ref[:, 0:QB].T
        dv_ref[...] = dva_ref[:, 0:QB].T
        dka_ref[:, 0:KB - QB] = dka_ref[:, QB:KB]
        dva_ref[:, 0:KB - QB] = dva_ref[:, QB:KB]
        dka_ref[:, KB - QB:KB] = jnp.zeros((LW, QB), F32)
        dva_ref[:, KB - QB:KB] = jnp.zeros((LW, QB), F32)

    qspec, kspecs, dspec = _attn_specs(nqb)
    kout = pl.BlockSpec((QB, LW), lambda p, j: (jnp.maximum(j - npad, 0), p))
    sd = jax.ShapeDtypeStruct((s, D), F32)
    return pl.pallas_call(
        body, name="attn_bwd", grid=(D // LW, nqb + npad),
        in_specs=[qspec] + kspecs + kspecs + [qspec, dspec],
        out_specs=[qspec, kout, kout, pl.BlockSpec((None, HPS, 8, _TOEP), lambda p, j: (p, 0, 0, 0))],
        out_shape=[sd, sd, sd, jax.ShapeDtypeStruct((NH // HPS, HPS, 8, _TOEP), F32)],
        scratch_shapes=[pltpu.VMEM((HPS, QB, KB), F32), pltpu.VMEM((HPS, QB, KB), F32),
                        pltpu.VMEM((LW, KB), F32), pltpu.VMEM((LW, KB), F32),
                        pltpu.VMEM((HPS, QB, KB), F32), pltpu.VMEM((HPS, QB, KB), F32),
                        pltpu.VMEM((HPS, QB, KB), BF16), pltpu.VMEM((HPS, QB, KB), BF16)],
        compiler_params=_cp(("parallel", "arbitrary")),
    )(q, kpad, kpad, kpad, vpad, vpad, vpad, do, diag)


def _head_norm_bwd(dy2, x2, g2, lo):
    rr = _head_rstd(x2, lo)
    xhat = x2 * rr
    t = dy2 * g2 * xhat
    m_lo = jnp.sum(jnp.where(lo, t, 0.0), axis=-1, keepdims=True)
    m_hi = jnp.sum(jnp.where(lo, 0.0, t), axis=-1, keepdims=True)
    m = jnp.where(lo, m_lo, m_hi) * (1.0 / HD)
    return rr * (dy2 * g2 - xhat * m), dy2 * xhat


def _kvq_bwd(dq, dk, dv, qraw, kraw, h2, g3, kv_norm, b_norm, w_kv, w_q, k_norm_t, q_norm_t):
    s = h2.shape[0]

    def body(dq_ref, dk_ref, dv_ref, qraw_ref, kraw_ref, h_ref, g3_ref, gkv_ref, gb_ref, wkv_ref, wq_ref,
             kn_ref, qn_ref, g2_ref, dqr_ref, dkv_ref, nb_ref, nk_ref, dgq_ref, dgk_ref, dgb_ref, dgkv_ref):
        @pl.when(pl.program_id(0) == 0)
        def _():
            for r in (dgq_ref, dgk_ref, dgb_ref, dgkv_ref):
                r[...] = jnp.zeros_like(r)

        lo = lax.broadcasted_iota(jnp.int32, (1, 128), 1) < HD
        for p in range(D // 128):
            sl = slice(p * 128, (p + 1) * 128)
            dx, dgp = _head_norm_bwd(dq_ref[:, sl], qraw_ref[:, sl], qn_ref[:, sl], lo)
            dqr_ref[:, sl] = dx.astype(BF16)
            dgq_ref[:, sl] += _sum8(dgp)
            dx, dgp = _head_norm_bwd(dk_ref[:, sl], kraw_ref[:, sl], kn_ref[:, sl], lo)
            dkv_ref[:, sl] = dx.astype(BF16)
            dgk_ref[:, sl] += _sum8(dgp)
        dkv_ref[:, D:] = dv_ref[...].astype(BF16)
        dnb = lax.dot_general(dqr_ref[...], wq_ref[...], _NT, preferred_element_type=F32)
        cw = 2 * D // NDEV
        dnk = lax.dot_general(dkv_ref[:, 0:cw], wkv_ref[0], _NT, preferred_element_type=F32)
        for d in range(1, NDEV):
            dnk = dnk + lax.dot_general(dkv_ref[:, d * cw:(d + 1) * cw], wkv_ref[d], _NT, preferred_element_type=F32)
        h = h_ref[...]
        r = _rstd(h)
        xhat = h * r
        dxg = dnb * gb_ref[...] + dnk * gkv_ref[...]
        g2_ref[...] = g3_ref[...] + r * (dxg - xhat * jnp.mean(dxg * xhat, axis=-1, keepdims=True))
        dgb_ref[...] += _sum8(dnb * xhat)
        dgkv_ref[...] += _sum8(dnk * xhat)
        nb_ref[...] = (xhat * gb_ref[...]).astype(BF16)
        nk_ref[...] = (xhat * gkv_ref[...]).astype(BF16)

    row = _rows(TM, D)
    sd = jax.ShapeDtypeStruct((s, D), BF16)
    acc = jax.ShapeDtypeStruct((8, D), F32)
    return pl.pallas_call(
        body, name="kvq_bwd", grid=(s // TM,),
        in_specs=[row] * 7 + [_full((1, D)), _full((1, D)), _full((NDEV, D, 2 * D // NDEV)), _full((D, D)),
                              _full((1, D)), _full((1, D))],
        out_specs=[row, row, _rows(TM, 2 * D), row, row] + [_acc_spec()] * 4,
        out_shape=[jax.ShapeDtypeStruct((s, D), F32), sd, jax.ShapeDtypeStruct((s, 2 * D), BF16), sd, sd,
                   acc, acc, acc, acc],
        compiler_params=_cp(("arbitrary",)),
    )(dq, dk, dv, qraw, kraw, h2, g3, kv_norm, b_norm, w_kv, w_q, k_norm_t, q_norm_t)


def _lru_bwd(g1, gate, rec, hs, w_out, conv_w, conv_b, wg, bg, lam):
    s = g1.shape[0]
    nt = s // TL

    def body(g1_ref, gate_ref, rec_ref, recp_ref, hs_ref, hsp_ref, wo_ref, cw_ref, cb_ref, wg_ref, bg_ref, lam_ref,
             du_ref, dcw_ref, dcb_ref, dwg_ref, dbg_ref, dlam_ref, ext_ref, dext_ref, cg_ref,
             a_sc, dh_sc, hl_sc, ac_sc, rg_sc, ig_sc, mult_sc, rc_sc):
        i = pl.program_id(0)
        first_tile = i == nt - 1

        @pl.when(i == 0)
        def _():
            dext_ref[TL:TL + 8, :] = jnp.zeros((8, D), F32)
            cg_ref[...] = jnp.zeros_like(cg_ref)
            for r in (dcw_ref, dcb_ref, dwg_ref, dbg_ref, dlam_ref):
                r[...] = jnp.zeros_like(r)

        keep = jnp.where(first_tile, 0.0, 1.0)
        ext_ref[0:8, :] = recp_ref[...] * keep
        ext_ref[8:8 + TL, :] = rec_ref[...]
        dy = lax.dot_general(g1_ref[...].astype(BF16), wo_ref[...], _NT, preferred_element_type=F32)
        lam_v = lam_ref[...]
        sp = _softplus_neg(lam_v)
        dsp_dlam = -_sigmoid(-lam_v)
        rows = lax.broadcasted_iota(jnp.int32, (TL, BW), 0)
        for n in range(NBLK):
            sl = slice(n * BW, (n + 1) * BW)
            rc = _conv(ext_ref, cw_ref, cb_ref, sl, TL)
            rg, ig, a, mult = _lru_gates(rc, wg_ref[n], bg_ref[n:n + 1, :], sp[:, sl])
            h = hs_ref[:, sl]
            gt = gate_ref[:, sl]
            dyn = dy[:, sl]
            gl, gl_grad = _gelu_and_grad(gt)
            du_ref[:, sl] = (dyn * h * gl_grad).astype(BF16)
            dh_sc[n] = dyn * gl + jnp.where(rows == TL - 1, cg_ref[0:1, sl], 0.0)
            a_sc[n], rg_sc[n], ig_sc[n], mult_sc[n], rc_sc[n] = a, rg, ig, mult, rc
            ac_sc[n] = _shift_up(a, 1, 0.0, rows, TL)
        _tile_scan(ac_sc, dh_sc, hl_sc, [jnp.zeros((1, BW), F32)] * NBLK, reverse=True)
        for n in range(NBLK):
            sl = slice(n * BW, (n + 1) * BW)
            gsc = hl_sc[n]
            a, rg, ig, mult, rc = a_sc[n], rg_sc[n], ig_sc[n], mult_sc[n], rc_sc[n]
            cg_ref[0:1, sl] = a[0:1, :] * gsc[0:1, :]
            hprev = _shift_down(hs_ref[:, sl], 1, hsp_ref[7:8, sl] * keep, rows)
            da = gsc * hprev
            d_mult = gsc * ig * rc
            d_ig = gsc * mult * rc
            d_rc = gsc * mult * ig
            d_la = da * a - d_mult * (a * a) / mult
            d_rg = d_la * ((-LRU_C) * sp[:, sl])
            dlam_ref[:, sl] += _sum8(d_la * ((-LRU_C) * rg)) * dsp_dlam[:, sl]
            dg = jnp.concatenate([d_rg * rg * (1.0 - rg), d_ig * ig * (1.0 - ig)], axis=1)
            dgb = dg.astype(BF16)
            d_rc = d_rc + lax.dot_general(dgb, wg_ref[n], _NT, preferred_element_type=F32)
            dwg_ref[n] += lax.dot_general(rc.astype(BF16), dgb, _TN, preferred_element_type=F32)
            dbg_ref[n] += _sum8(dg)
            dext_ref[0:TL, sl] = d_rc
            dcb_ref[:, sl] += _sum8(d_rc)
            for k in range(4):
                dcw_ref[k, :, sl] += _sum8(d_rc * ext_ref[5 + k:5 + k + TL, sl])
        for k in range(4):
            part = cw_ref[3 - k:4 - k, :] * dext_ref[k:k + TL, :]
            acc = part if k == 0 else acc + part
        du_ref[:, D:] = acc.astype(BF16)
        dext_ref[TL:TL + 8, :] = dext_ref[0:8, :]

    rev = pl.BlockSpec((TL, D), lambda i: (nt - 1 - i, 0))
    rev8 = pl.BlockSpec((8, D), lambda i: (jnp.maximum((nt - 1 - i) * (TL // 8) - 1, 0), 0))
    acc = jax.ShapeDtypeStruct((8, D), F32)
    return pl.pallas_call(
        body, name="lru_bwd", grid=(nt,),
        in_specs=[rev, rev, rev, rev8, rev, rev8, _full((D, D)), _full((4, D)), _full((1, D)),
                  _full((NBLK, BW, 2 * BW)), _full((NBLK, 2 * BW)), _full((1, D))],
        out_specs=[pl.BlockSpec((TL, 2 * D), lambda i: (nt - 1 - i, 0)),
                   pl.BlockSpec((4, 8, D), lambda i: (0, 0, 0)), _acc_spec(),
                   pl.BlockSpec((NBLK, BW, 2 * BW), lambda i: (0, 0, 0)),
                   pl.BlockSpec((NBLK, 8, 2 * BW), lambda i: (0, 0, 0)), _acc_spec()],
        out_shape=[jax.ShapeDtypeStruct((s, 2 * D), BF16), jax.ShapeDtypeStruct((4, 8, D), F32), acc,
                   jax.ShapeDtypeStruct((NBLK, BW, 2 * BW), F32), jax.ShapeDtypeStruct((NBLK, 8, 2 * BW), F32), acc],
        scratch_shapes=[pltpu.VMEM((TL + 8, D), F32), pltpu.VMEM((TL + 8, D), F32), pltpu.VMEM((8, D), F32)]
        + [pltpu.VMEM((NBLK, TL, BW), F32)] * 8,
        compiler_params=_cp(("arbitrary",)),
    )(g1, gate, rec, rec, hs, hs, w_out, conv_w, conv_b, wg, bg, lam)


def _a_in_bwd(du, h0, g1, a_norm, w_in):
    s = h0.shape[0]

    def body(du_ref, h_ref, g1_ref, an_ref, win_ref, gx_ref, dg_ref):
        @pl.when(pl.program_id(0) == 0)
        def _():
            dg_ref[...] = jnp.zeros_like(dg_ref)

        cw = 2 * D // NDEV
        dn = lax.dot_general(du_ref[:, 0:cw], win_ref[0], _NT, preferred_element_type=F32)
        for d in range(1, NDEV):
            dn = dn + lax.dot_general(du_ref[:, d * cw:(d + 1) * cw], win_ref[d], _NT, preferred_element_type=F32)
        h = h_ref[...]
        r = _rstd(h)
        xhat = h * r
        gx_ref[...] = g1_ref[...] + _rms_bwd(dn, xhat, r, an_ref[...])
        dg_ref[...] += _sum8(dn * xhat)

    row = _rows(TM, D)
    return pl.pallas_call(
        body, name="a_in_bwd", grid=(s // TM,),
        in_specs=[_rows(TM, 2 * D), row, row, _full((1, D)), _full((NDEV, D, 2 * D // NDEV))],
        out_specs=[row, _acc_spec()],
        out_shape=[jax.ShapeDtypeStruct((s, D), F32), jax.ShapeDtypeStruct((8, D), F32)],
        compiler_params=_cp(("arbitrary",)),
    )(du, h0, g1, a_norm, w_in)


def _rel_onehot():
    m = np.arange(_TOEP)
    signed = np.where(m < KB, m, m - _TOEP)
    idx = np.clip(PADK - signed, -(CHUNK - 1), 2 * CHUNK) + (CHUNK - 1)
    return (idx[None, :] == np.arange(NREL)[:, None]).astype(np.float32)


def _bias_diagonals(rel_bias):
    diag = jnp.dot(rel_bias, jnp.asarray(_rel_onehot()), precision=lax.Precision.HIGHEST)
    return diag.reshape(NH // HPS, HPS, _TOEP)


def _rel_bias_grad(dd):
    rows = 8
    z = dd
    oh = np.zeros((_TOEP, 256), np.float32)
    oh[:, :NREL] = _rel_onehot().T

    def body(z_ref, oh_ref, o_ref):
        d = jnp.sum(z_ref[...], axis=0, keepdims=True)
        hi = d.astype(BF16)
        mid = (d - hi.astype(F32)).astype(BF16)
        lo = (d - hi.astype(F32) - mid.astype(F32)).astype(BF16)
        ohb = oh_ref[...].astype(BF16)
        acc = jnp.zeros((8, 256), F32)
        for piece in (lo, mid, hi):
            acc = acc + jnp.dot(jnp.broadcast_to(piece, (8, _TOEP)), ohb, preferred_element_type=F32)
        o_ref[...] = acc

    out = pl.pallas_call(
        body, name="rel_bias_grad", grid=(NH,),
        in_specs=[pl.BlockSpec((None, rows, _TOEP), lambda h: (h, 0, 0)), pl.BlockSpec((_TOEP, 256), lambda h: (0, 0))],
        out_specs=pl.BlockSpec((None, 8, 256), lambda h: (h, 0, 0)),
        out_shape=jax.ShapeDtypeStruct((NH, 8, 256), F32),
        compiler_params=_cp(("parallel",)),
    )(z, jnp.asarray(oh))
    return out[:, 0, :NREL]


def _exchange(arrays, scatter, name):
    n = len(arrays)

    def body(*refs):
        ins, outs = refs[:n], refs[n:2 * n]
        token, (send_sems, recv_sems, local_sems) = refs[2 * n], refs[2 * n + 1:]
        token[...] = jnp.zeros_like(token)
        x, y, c = lax.axis_index("x"), lax.axis_index("y"), lax.axis_index("c")
        me = 4 * x + 2 * y + c

        def peer_of(r):
            rx, ry, rc = (r >> 2) & 1, (r >> 1) & 1, r & 1
            px = 1 - x if rx else x
            py = 1 - y if ry else y
            pc = 1 - c if rc else c
            return (px, py, pc), 4 * px + 2 * py + pc

        local, sent = [], []
        for k in range(n):
            cp = pltpu.make_async_copy(ins[k].at[me] if scatter else ins[k], outs[k].at[me], local_sems.at[k])
            cp.start()
            local.append(cp)
            for r in range(1, NDEV):
                peer, peer_lin = peer_of(r)
                cp = pltpu.make_async_remote_copy(
                    src_ref=ins[k].at[peer_lin] if scatter else ins[k], dst_ref=outs[k].at[me],
                    send_sem=send_sems.at[k, r - 1], recv_sem=recv_sems.at[k, r - 1],
                    device_id=peer, device_id_type=pl.DeviceIdType.MESH)
                cp.start()
                sent.append(cp)
        for k in range(n):
            for r in range(1, NDEV):
                peer, peer_lin = peer_of(r)
                pltpu.make_async_remote_copy(
                    src_ref=ins[k].at[peer_lin] if scatter else ins[k], dst_ref=outs[k].at[peer_lin],
                    send_sem=send_sems.at[k, r - 1], recv_sem=recv_sems.at[k, r - 1],
                    device_id=peer, device_id_type=pl.DeviceIdType.MESH).wait_recv()
        for cp in sent:
            cp.wait_send()
        for cp in local:
            cp.wait()

    def slot_shape(a):
        return (NDEV,) + (a.shape[1:] if scatter else a.shape)

    anyspec = pl.BlockSpec(memory_space=pl.ANY)
    outs = pl.pallas_call(
        body, name=name,
        in_specs=[anyspec] * n, out_specs=[anyspec] * n + [pl.BlockSpec(memory_space=pltpu.VMEM)],
        out_shape=[jax.ShapeDtypeStruct(slot_shape(a), a.dtype) for a in arrays]
        + [jax.ShapeDtypeStruct((8, 128), F32)],
        scratch_shapes=[pltpu.SemaphoreType.DMA((n, NDEV - 1)), pltpu.SemaphoreType.DMA((n, NDEV - 1)),
                        pltpu.SemaphoreType.DMA((n,))],
        compiler_params=pltpu.CompilerParams(has_side_effects=True),
    )(*arrays)
    return outs[:n], outs[n]


def _peer(r):
    x, y, c = lax.axis_index("x"), lax.axis_index("y"), lax.axis_index("c")
    px = 1 - x if (r >> 2) & 1 else x
    py = 1 - y if (r >> 1) & 1 else y
    pc = 1 - c if r & 1 else c
    return (px, py, pc), 4 * px + 2 * py + pc


def _my_index():
    return 4 * lax.axis_index("x") + 2 * lax.axis_index("y") + lax.axis_index("c")


_HBM_SPEC = pl.BlockSpec(memory_space=pltpu.HBM)
_SEM_SPEC = pl.BlockSpec(memory_space=pltpu.SEMAPHORE)


_NPEER = NDEV - 1


def _exchange_start(arrays, scatter, name):
    n = len(arrays)
    ns = n * _NPEER
    slots = [(NDEV,) + (a.shape[1:] if scatter else a.shape) for a in arrays]

    def body(*refs):
        srcs, lands = refs[:n], refs[n:2 * n]
        send_sems, recv_sems = refs[2 * n:2 * n + ns], refs[2 * n + ns:2 * n + 2 * ns]
        token = refs[-1]
        me = _my_index()
        for k in range(n):
            for r in range(1, NDEV):
                peer, peer_lin = _peer(r)
                pltpu.make_async_remote_copy(
                    src_ref=srcs[k].at[peer_lin] if scatter else srcs[k], dst_ref=lands[k].at[me],
                    send_sem=send_sems[k * _NPEER + r - 1], recv_sem=recv_sems[k * _NPEER + r - 1],
                    device_id=peer, device_id_type=pl.DeviceIdType.MESH).start()
        token[...] = jnp.zeros_like(token)

    sem = pltpu.SemaphoreType.DMA(())
    outs = pl.pallas_call(
        body, name=name,
        out_shape=(*[sem] * (2 * ns), *[pltpu.HBM(a.shape, a.dtype) for a in arrays],
                   *[pltpu.HBM(s, a.dtype) for s, a in zip(slots, arrays)], jax.ShapeDtypeStruct((8, 128), F32)),
        in_specs=[_HBM_SPEC] * (2 * n),
        out_specs=(*[_SEM_SPEC] * (2 * ns), *[_HBM_SPEC] * (2 * n), pl.BlockSpec(memory_space=pltpu.VMEM)),
        input_output_aliases={k: 2 * ns + k for k in range(2 * n)},
        compiler_params=pltpu.CompilerParams(has_side_effects=pltpu.SideEffectType.DATAFLOW_SIDE_EFFECTING),
    )(*[pltpu.with_memory_space_constraint(a, pltpu.HBM) for a in arrays],
      *[pltpu.with_memory_space_constraint(lax.empty(s, a.dtype), pltpu.HBM) for s, a in zip(slots, arrays)])
    return outs[:ns], outs[ns:2 * ns], outs[2 * ns:2 * ns + n], outs[2 * ns + n:2 * ns + 2 * n], outs[-1]


def _exchange_wait(started, after, scatter, name):
    send_sems, recv_sems, srcs, lands, _ = started
    n = len(srcs)
    ns = n * _NPEER

    def body(*refs):
        src_refs, land_refs = refs[:n], refs[n:2 * n]
        ssem, rsem = refs[2 * n:2 * n + ns], refs[2 * n + ns:2 * n + 2 * ns]
        for k in range(n):
            for r in range(1, NDEV):
                peer, peer_lin = _peer(r)
                cp = pltpu.make_async_remote_copy(
                    src_ref=src_refs[k].at[peer_lin] if scatter else src_refs[k], dst_ref=land_refs[k].at[peer_lin],
                    send_sem=ssem[k * _NPEER + r - 1], recv_sem=rsem[k * _NPEER + r - 1],
                    device_id=peer, device_id_type=pl.DeviceIdType.MESH)
                cp.wait_send()
                cp.wait_recv()

    outs = pl.pallas_call(
        body, name=name,
        out_shape=tuple(pltpu.HBM(a.shape, a.dtype) for a in list(srcs) + list(lands)),
        in_specs=[_HBM_SPEC] * (2 * n) + [_SEM_SPEC] * (2 * ns) + [pl.BlockSpec(memory_space=pl.ANY)],
        out_specs=tuple([_HBM_SPEC] * (2 * n)),
        input_output_aliases={k: k for k in range(2 * n)},
        compiler_params=pltpu.CompilerParams(has_side_effects=pltpu.SideEffectType.DATAFLOW_SIDE_EFFECTING),
    )(*srcs, *lands, *send_sems, *recv_sems, after)
    return list(outs[:n]), list(outs[n:])


def _fill_own(lands, owns, me):
    return [lax.dynamic_update_slice(z, o, (me,) + (0,) * (z.ndim - 1)) for z, o in zip(lands, owns)]


def _sum_slots(st, name):
    _, r, c = st.shape

    def body(s_ref, o_ref):
        acc = s_ref[0]
        for d in range(1, NDEV):
            acc = acc + s_ref[d]
        o_ref[...] = acc

    return pl.pallas_call(
        body, name=name, out_shape=jax.ShapeDtypeStruct((r, c), F32),
        in_specs=[pl.BlockSpec((NDEV, r, c), lambda: (0, 0, 0))], out_specs=pl.BlockSpec((r, c), lambda: (0, 0)),
    )(st)


def _adamw(w, m, v, gst, name, transposed=False):
    r, c = w.shape
    ns = gst.shape[0]
    tr = min(r, 256)
    c1 = 1.0 - ADAM_B1 ** ADAM_STEP
    c2 = 1.0 - ADAM_B2 ** ADAM_STEP

    def body(w_ref, m_ref, v_ref, g_ref, go_ref, d_ref, mo_ref, vo_ref):
        g = g_ref[0].astype(F32)
        for d in range(1, ns):
            g = g + g_ref[d].astype(F32)
        if transposed:
            g = g.T
        m2 = ADAM_B1 * m_ref[...] + (1.0 - ADAM_B1) * g
        v2 = ADAM_B2 * v_ref[...] + (1.0 - ADAM_B2) * (g * g)
        go_ref[...] = g
        mo_ref[...] = m2
        vo_ref[...] = v2
        d_ref[...] = (-ADAM_LR) * ((m2 / c1) / (jnp.sqrt(v2 / c2) + ADAM_EPS) + ADAM_WD * w_ref[...])

    blk = pl.BlockSpec((tr, c), lambda i: (i, 0))
    sd = jax.ShapeDtypeStruct((r, c), F32)
    return pl.pallas_call(
        body, name=name, grid=(r // tr,),
        in_specs=[blk, blk, blk, pl.BlockSpec((ns, c, tr), lambda i: (0, 0, i)) if transposed
                  else pl.BlockSpec((ns, tr, c), lambda i: (0, i, 0))],
        out_specs=[blk, blk, blk, blk], out_shape=[sd, sd, sd, sd],
        compiler_params=_cp(("parallel",)),
    )(w, m, v, gst)


def _pack(pieces, rows):
    flat = jnp.concatenate([p.reshape(-1).astype(F32) for p in pieces])
    return jnp.pad(flat, (0, rows * 128 - flat.shape[0])).reshape(rows, 128)


def _unpack(flat, shapes):
    out, off = [], 0
    for shp in shapes:
        size = int(np.prod(shp))
        out.append(flat[off:off + size].reshape(shp))
        off += size
    return out


def _cols(full, me, width):
    return lax.dynamic_slice_in_dim(full, me * width, width, axis=full.ndim - 1)


def kernel(x, a_norm, a_w_in, a_conv_w, a_conv_b, a_w_gate, a_b_gate, a_lambda, a_w_out, kv_norm, w_kv, k_norm, b_norm, b_w_q, b_q_norm, b_rel_bias, b_w_o, mlp_norm, w_up, w_down, loss_target, m_a_norm, m_a_w_in, m_a_conv_w, m_a_conv_b, m_a_w_gate, m_a_b_gate, m_a_lambda, m_a_w_out, m_kv_norm, m_w_kv, m_k_norm, m_b_norm, m_b_w_q, m_b_q_norm, m_b_rel_bias, m_b_w_o, m_mlp_norm, m_w_up, m_w_down, v_a_norm, v_a_w_in, v_a_conv_w, v_a_conv_b, v_a_w_gate, v_a_b_gate, v_a_lambda, v_a_w_out, v_kv_norm, v_w_kv, v_k_norm, v_b_norm, v_b_w_q, v_b_q_norm, v_b_rel_bias, v_b_w_o, v_mlp_norm, v_w_up, v_w_down):
    me = 4 * lax.axis_index("x") + 2 * lax.axis_index("y") + lax.axis_index("c")
    sh = D // NDEV

    big_w = [a_w_in[0], a_w_out[0], w_kv, b_w_q[0], b_w_o[0], w_up[0], w_up[1], w_down[0], w_down[1]]
    small_sharded = [a_norm, a_conv_w, a_conv_b, a_b_gate, a_lambda, a_w_gate]
    small_rows = 272
    def to_bf16(w, token):
        return (w + token[0, 0]).astype(BF16)

    got, tok_a = _exchange([a_w_in[0].astype(BF16), _pack(small_sharded, small_rows)], False, "gather_a")
    own_b1 = [to_bf16(w, tok_a) for w in (a_w_out[0], w_up[0], w_down[0])]
    st_b1 = _exchange_start(own_b1, False, "gather_b1_start")
    own_b2 = [to_bf16(w, st_b1[4]) for w in (w_kv, b_w_q[0], b_w_o[0], w_up[1], w_down[1])]
    st_b2 = _exchange_start(own_b2, False, "gather_b2_start")
    w_in = got[0]
    sm = got[1].reshape(NDEV, small_rows * 128)
    an_f = sm[:, 0:128].reshape(1, D) + st_b2[4][0:1, 0:1]
    cw_f = sm[:, 128:640].reshape(NDEV, 4, sh).transpose(1, 0, 2).reshape(4, D)
    cb_f = sm[:, 640:768].reshape(1, D)
    bg_f = sm[:, 768:1024].reshape(NDEV, NBLK, 2 * BW // NDEV).transpose(1, 0, 2).reshape(NBLK, 2 * BW)
    lam_f = sm[:, 1024:1152].reshape(1, D)
    wg_f = sm[:, 1152:1152 + NBLK * BW * 32].reshape(NDEV, NBLK, BW, 32).transpose(1, 2, 0, 3)
    wg_f = wg_f.reshape(NBLK, BW, 2 * BW).astype(BF16)
    kn_t = jnp.tile(k_norm, NH).reshape(1, D)
    qn_t = jnp.tile(b_q_norm[0], NH).reshape(1, D)
    kvn = kv_norm.reshape(1, D)
    diag = _bias_diagonals(b_rel_bias[0])

    h0 = x[0]
    gate, rec, hs, y, n1 = _lru_fwd(h0, an_f, w_in, cw_f, cb_f, wg_f, bg_f, lam_f)
    own, land = _exchange_wait(st_b1, y, False, "gather_b1_wait")
    land = _fill_own(land, [o[None] for o in own], me)
    w_out = land[0].reshape(D, D)
    wu = [land[1], None]
    wd = [land[2].reshape(FF, D), None]
    h1, h2, up0 = _mlp_fwd(h0, y, w_out, mlp_norm[0:1], wu[0], wd[0], "mlp_fwd0")
    own, land = _exchange_wait(st_b2, h2, False, "gather_b2_wait")
    land = _fill_own(land, [o[None] for o in own], me)
    wkv = land[0]
    w_q = land[1].reshape(D, D)
    w_o = land[2].reshape(D, D)
    wu[1] = land[3]
    wd[1] = land[4].reshape(FF, D)
    kraw, qraw, kpad, vpad, q = _kvq_fwd(h2, kvn, b_norm, wkv, w_q, kn_t, qn_t)
    o = _attn_fwd(q, kpad, vpad, diag)
    h3, g4, up1, lpart = _mlp_fwd(h2, o, w_o, mlp_norm[1:2], wu[1], wd[1], "mlp_fwd1", tgt=loss_target[0])
    loss = lax.psum(jnp.sum(lpart), ("x", "y", "c"))

    g3, dup1, n3, dgm1 = _mlp_bwd(g4, h3, up1, mlp_norm[1:2], wu[1], wd[1], "mlp_bwd1")
    d_wd1 = _matmul_tn(g4, up1, True, BF16, "dw_down1", b_relu2=True)
    d_wu1 = _matmul_tn(n3, dup1, True, BF16, "dw_up1")
    do = _matmul_nt(g3, w_o, "do_proj")
    d_wo = _matmul_tn(o, g3, False, BF16, "dw_o").reshape(NDEV, sh, D)
    dq, dk, dv, dd = _attn_bwd(q, kpad, vpad, do, diag)
    g2, dqr, dkv, nb, nk, dgq, dgk, dgb, dgkv = _kvq_bwd(dq, dk, dv, qraw, kraw, h2, g3, kvn, b_norm, wkv, w_q,
                                                       kn_t, qn_t)
    d_wq = _matmul_tn(nb, dqr, False, BF16, "dw_q").reshape(NDEV, sh, D)
    d_wkv = _matmul_tn(nk, dkv, True, BF16, "dw_kv")
    st_r1 = _exchange_start([d_wd1, d_wu1, d_wo, d_wq, d_wkv], True, "scatter_r1_start")
    g1, dup0, n2, dgm0 = _mlp_bwd(g2, h1, up0, mlp_norm[0:1] + st_r1[4][0:1, 0:1], wu[0], wd[0], "mlp_bwd0")
    d_wd0 = _matmul_tn(g2, up0, True, BF16, "dw_down0", b_relu2=True)
    d_wu0 = _matmul_tn(n2, dup0, True, BF16, "dw_up0")
    d_wout = _matmul_tn(y, g1, False, BF16, "dw_out").reshape(NDEV, sh, D)
    st_r2 = _exchange_start([d_wu0, d_wd0, d_wout], True, "scatter_r2_start")
    du, dcw, dcb, dwg, dbg, dlam = _lru_bwd(g1, gate, rec, hs, w_out, cw_f, cb_f, wg_f, bg_f,
                                            lam_f + st_r2[4][0:1, 0:1])
    d_rel = _rel_bias_grad(dd.reshape(NH, 8, _TOEP))
    d_win = _matmul_tn(n1, du, True, BF16, "dw_in")
    dwg_slab = dwg.reshape(NBLK, BW, NDEV, 32).transpose(2, 0, 1, 3).reshape(NDEV, 256, 128).astype(BF16)
    st_r3 = _exchange_start([d_win, dwg_slab], True, "scatter_r3_start")
    gx, dga = _a_in_bwd(du, h0, g1, an_f + st_r3[4][0:1, 0:1], w_in)
    small_full = [dga.sum(0), dcw.sum(1), dcb.sum(0), dbg.sum(1), dlam.sum(0), dgkv.sum(0),
                  dgk.sum(0).reshape(NH, HD).sum(0), dgb.sum(0), dgq.sum(0).reshape(NH, HD).sum(0), d_rel,
                  jnp.stack([dgm0.sum(0), dgm1.sum(0)])]
    small_g_rows = 136
    small_slabs = jnp.broadcast_to(_pack(small_full, small_g_rows)[None], (NDEV, small_g_rows, 128))
    st_r4 = _exchange_start([small_slabs], True, "scatter_r4_start")
    src, recv1 = _exchange_wait(st_r1, st_r4[4], True, "scatter_r1_wait")
    recv1 = _fill_own(recv1, [lax.dynamic_slice_in_dim(a, me, 1, 0) for a in src], me)
    src, recv2 = _exchange_wait(st_r2, recv1[0], True, "scatter_r2_wait")
    recv2 = _fill_own(recv2, [lax.dynamic_slice_in_dim(a, me, 1, 0) for a in src], me)

    names = ["a_w_in", "a_w_out", "w_kv", "b_w_q", "b_w_o", "w_up0", "w_up1", "w_down0", "w_down1", "a_w_gate"]
    big_m = [m_a_w_in[0], m_a_w_out[0], m_w_kv, m_b_w_q[0], m_b_w_o[0], m_w_up[0], m_w_up[1], m_w_down[0],
             m_w_down[1], m_a_w_gate.reshape(256, 128)]
    big_v = [v_a_w_in[0], v_a_w_out[0], v_w_kv, v_b_w_q[0], v_b_w_o[0], v_w_up[0], v_w_up[1], v_w_down[0],
             v_w_down[1], v_a_w_gate.reshape(256, 128)]
    big_w = big_w + [a_w_gate.reshape(256, 128)]

    def update(k, g):
        return _adamw(big_w[k], big_m[k], big_v[k], g, "adamw_" + names[k], transposed=names[k].startswith("w_down"))

    early = {1: recv2[2], 2: recv1[4], 3: recv1[3], 4: recv1[2], 5: recv2[0], 6: recv1[1], 7: recv2[1], 8: recv1[0]}
    res = {k: update(k, g) for k, g in early.items()}
    src, recv3 = _exchange_wait(st_r3, res[8][1], True, "scatter_r3_wait")
    recv3 = _fill_own(recv3, [lax.dynamic_slice_in_dim(a, me, 1, 0) for a in src], me)
    res[0] = update(0, recv3[0])
    res[9] = update(9, recv3[1])
    res = [res[k] for k in range(len(names))]
    src, recv4 = _exchange_wait(st_r4, res[0][1], True, "scatter_r4_wait")
    recv4 = _fill_own(recv4, [lax.dynamic_slice_in_dim(a, me, 1, 0) for a in src], me)
    gs = _unpack(_sum_slots(recv4[0], "sum_small_grads").reshape(-1),
                 [(1, D), (4, D), (1, D), (NBLK, 2 * BW), (1, D), (D,), (HD,), (1, D), (1, HD), (1, NH, NREL), (2, D)])
    g_small = [_cols(gs[0], me, sh), _cols(gs[1], me, sh)[None], _cols(gs[2], me, sh),
               _cols(gs[3], me, 2 * BW // NDEV)[None], _cols(gs[4], me, sh)] + gs[5:]
    small_w = [a_norm, a_conv_w, a_conv_b, a_b_gate, a_lambda, kv_norm, k_norm, b_norm, b_q_norm, b_rel_bias, mlp_norm]
    small_m = [m_a_norm, m_a_conv_w, m_a_conv_b, m_a_b_gate, m_a_lambda, m_kv_norm, m_k_norm, m_b_norm, m_b_q_norm,
               m_b_rel_bias, m_mlp_norm]
    small_v = [v_a_norm, v_a_conv_w, v_a_conv_b, v_a_b_gate, v_a_lambda, v_kv_norm, v_k_norm, v_b_norm, v_b_q_norm,
               v_b_rel_bias, v_mlp_norm]
    pr = 72
    res_small = _adamw(_pack(small_w, pr), _pack(small_m, pr), _pack(small_v, pr), _pack(g_small, pr)[None],
                       "adamw_small")
    small_shapes = [w.shape for w in small_w]
    res_small = [_unpack(r.reshape(-1), small_shapes) for r in res_small]

    def assemble(t):
        b = [r[t] for r in res]
        s_ = res_small[t]
        return [s_[0], b[0][None], s_[1], s_[2], b[9].reshape(a_w_gate.shape), s_[3], s_[4], b[1][None],
                s_[5], b[2], s_[6], s_[7], b[3][None], s_[8], s_[9], b[4][None], s_[10],
                jnp.stack([b[5], b[6]]), jnp.stack([b[7], b[8]])]

    return tuple([loss, gx[None]] + assemble(0) + assemble(1) + assemble(2) + assemble(3))
```

```python
import functools

import numpy as np
import jax
import jax.numpy as jnp
from jax import lax
from jax.experimental import pallas as pl
from jax.experimental.pallas import tpu as pltpu

F32 = jnp.float32
BF16 = jnp.bfloat16

D = 1024
NH = 16
HD = 64
FF = 4096
NBLK = 8
BW = 128
CHUNK = 64
PADK = 512
NREL = 192
EPS = 1e-6
LRU_C = 8.0
NDEV = 8

V7X_VMEM_LIMIT = 56 * 1024 * 1024
TM = 512
TMM = 512
TMF = 512
TL = 256
QB = 256
ATT_RC = 32
HPS = 8
LW = HPS * HD
KB = QB + PADK
NEG = -1e30

ADAM_LR, ADAM_B1, ADAM_B2, ADAM_EPS, ADAM_WD, ADAM_STEP = 0.001, 0.9, 0.999, 1e-08, 0.01, 10

_NT = (((1,), (1,)), ((), ()))
_TN = (((0,), (0,)), ((), ()))


def _cp(sem=None):
    return pltpu.CompilerParams(dimension_semantics=sem, vmem_limit_bytes=V7X_VMEM_LIMIT)


def _full(shape):
    n = len(shape)
    return pl.BlockSpec(shape, lambda *a: (0,) * n, pipeline_mode=pl.Buffered(1))


def _rows(tm, width):
    return pl.BlockSpec((tm, width), lambda i: (i, 0))


def _rstd(h):
    return lax.rsqrt(jnp.mean(h * h, axis=-1, keepdims=True) + EPS)


def _sigmoid(x):
    return 1.0 / (1.0 + jnp.exp(-x))


def _one_minus_sq(a, la):
    x = 2.0 * la
    series = -x * (1.0 + x * (0.5 + x * (1.0 / 6.0)))
    return jnp.where(x > -0.01, series, 1.0 - a * a)


def _softplus_neg(lam):
    e = jnp.exp(-jnp.abs(lam))
    series = e * (1.0 - e * (0.5 - e * (1.0 / 3.0 - e * 0.25)))
    return jnp.maximum(-lam, 0.0) + jnp.where(e < 0.01, series, jnp.log(1.0 + e))


_GELU_K = 0.7978845608028654


def _gelu(x):
    return 0.5 * x * (1.0 + jnp.tanh(_GELU_K * (x + 0.044715 * x * x * x)))


def _gelu_and_grad(x):
    x2 = x * x
    t = jnp.tanh(_GELU_K * (x + 0.044715 * x * x2))
    half = 0.5 * (1.0 + t)
    return x * half, half + 0.5 * x * (1.0 - t * t) * (_GELU_K * (1.0 + 3.0 * 0.044715 * x2))


def _sum8(x):
    r, c = x.shape
    return jnp.sum(x.reshape(r // 8, 8, c), axis=0)


def _shift_down(x, s, fill, rows):
    return jnp.where(rows >= s, pltpu.roll(x, s, axis=0), fill)


def _shift_up(x, s, fill, rows, n):
    return jnp.where(rows < n - s, pltpu.roll(x, n - s, axis=0), fill)


def _lru_gates(rc, wg_n, bg_n, sp_n):
    g = jnp.dot(rc.astype(BF16), wg_n, preferred_element_type=F32) + bg_n
    rg = _sigmoid(g[:, :BW])
    ig = _sigmoid(g[:, BW:])
    la = (-LRU_C) * rg * sp_n
    a = jnp.exp(la)
    mult = jnp.sqrt(_one_minus_sq(a, la))
    return rg, ig, a, mult


def _conv(ext_ref, cw_ref, cb_ref, sl, n):
    out = cb_ref[:, sl] + cw_ref[0:1, sl] * ext_ref[5:5 + n, sl]
    for k in range(1, 4):
        out = out + cw_ref[k:k + 1, sl] * ext_ref[5 + k:5 + k + n, sl]
    return out


def _tile_scan(a_ref, b_ref, h_ref, carry, reverse):
    sub = lax.broadcasted_iota(jnp.int32, (TL, BW), 0) % 8
    for n in range(NBLK):
        a, b = a_ref[n], b_ref[n]
        for s in (1, 2, 4):
            if reverse:
                inside = sub < 8 - s
                a_sh = jnp.where(inside, pltpu.roll(a, TL - s, axis=0), 1.0)
                b_sh = jnp.where(inside, pltpu.roll(b, TL - s, axis=0), 0.0)
            else:
                inside = sub >= s
                a_sh = jnp.where(inside, pltpu.roll(a, s, axis=0), 1.0)
                b_sh = jnp.where(inside, pltpu.roll(b, s, axis=0), 0.0)
            b = a * b_sh + b
            a = a * a_sh
        a_ref[n], b_ref[n] = a, b
    carry = list(carry)
    groups = range(TL // 8 - 1, -1, -1) if reverse else range(TL // 8)
    for g in groups:
        r = slice(8 * g, 8 * g + 8)
        for n in range(NBLK):
            h = a_ref[n, r, :] * carry[n] + b_ref[n, r, :]
            h_ref[n, r, :] = h
            carry[n] = h[0:1, :] if reverse else h[7:8, :]
    return carry


def _lru_fwd(h0, a_norm, w_in, conv_w, conv_b, wg, bg, lam):
    s = h0.shape[0]

    def body(h0_ref, an_ref, win_ref, cw_ref, cb_ref, wg_ref, bg_ref, lam_ref,
             gate_ref, rec_ref, hs_ref, y_ref, n1_ref, ext_ref, hc_ref, a_sc, b_sc, hl_sc):
        i = pl.program_id(0)

        @pl.when(i == 0)
        def _():
            ext_ref[0:8, :] = jnp.zeros((8, D), F32)
            hc_ref[...] = jnp.zeros_like(hc_ref)

        h = h0_ref[...]
        n1 = (h * _rstd(h) * an_ref[...]).astype(BF16)
        n1_ref[...] = n1
        cw = 2 * D // NDEV
        for d in range(NDEV):
            ud = jnp.dot(n1, win_ref[d], preferred_element_type=F32)
            if d < NDEV // 2:
                gate_ref[:, d * cw:(d + 1) * cw] = ud
            else:
                rec_ref[:, d * cw - D:(d + 1) * cw - D] = ud
                ext_ref[8:8 + TL, d * cw - D:(d + 1) * cw - D] = ud
        sp = _softplus_neg(lam_ref[...])
        for n in range(NBLK):
            sl = slice(n * BW, (n + 1) * BW)
            rc = _conv(ext_ref, cw_ref, cb_ref, sl, TL)
            rg, ig, a, mult = _lru_gates(rc, wg_ref[n], bg_ref[n:n + 1, :], sp[:, sl])
            a_sc[n] = a
            b_sc[n] = mult * (ig * rc)
        carry = _tile_scan(a_sc, b_sc, hl_sc, [hc_ref[0:1, n * BW:(n + 1) * BW] for n in range(NBLK)], reverse=False)
        for n in range(NBLK):
            sl = slice(n * BW, (n + 1) * BW)
            hh = hl_sc[n]
            hc_ref[0:1, sl] = carry[n]
            hs_ref[:, sl] = hh
            y_ref[:, sl] = (_gelu(gate_ref[:, sl]) * hh).astype(BF16)
        ext_ref[0:8, :] = ext_ref[TL:TL + 8, :]

    row = _rows(TL, D)
    return pl.pallas_call(
        body, name="lru_fwd", grid=(s // TL,),
        in_specs=[row, _full((1, D)), _full((NDEV, D, 2 * D // NDEV)), _full((4, D)), _full((1, D)),
                  _full((NBLK, BW, 2 * BW)), _full((NBLK, 2 * BW)), _full((1, D))],
        out_specs=[row, row, row, row, row],
        out_shape=[jax.ShapeDtypeStruct((s, D), F32), jax.ShapeDtypeStruct((s, D), F32),
                   jax.ShapeDtypeStruct((s, D), F32), jax.ShapeDtypeStruct((s, D), BF16),
                   jax.ShapeDtypeStruct((s, D), BF16)],
        scratch_shapes=[pltpu.VMEM((TL + 8, D), F32), pltpu.VMEM((8, D), F32)]
        + [pltpu.VMEM((NBLK, TL, BW), F32)] * 3,
        compiler_params=_cp(("arbitrary",)),
    )(h0, a_norm, w_in, conv_w, conv_b, wg, bg, lam)


def _mlp_fwd(res, px, pw, g, wu, wd, name, tgt=None):
    s = res.shape[0]
    fj = 512
    with_loss = tgt is not None

    def body(res_ref, px_ref, pw_ref, g_ref, wu_ref, wd_ref, *rest):
        if with_loss:
            t_ref, hin_ref, hout_ref, up_ref, l_ref, n_ref = rest

            @pl.when(pl.program_id(0) == 0)
            def _():
                l_ref[...] = jnp.zeros_like(l_ref)
        else:
            hin_ref, hout_ref, up_ref, n_ref = rest
        hin = res_ref[...] + jnp.dot(px_ref[...], pw_ref[...], preferred_element_type=F32)
        hin_ref[...] = hin
        hout_ref[...] = hin
        n_ref[...] = (hin * _rstd(hin) * g_ref[...]).astype(BF16)
        for j in range(FF // fj):
            sl = slice(j * fj, (j + 1) * fj)
            up = jnp.dot(n_ref[...], wu_ref[j], preferred_element_type=F32)
            up_ref[:, sl] = up.astype(BF16)
            rl = jnp.maximum(up, 0.0)
            hout_ref[...] += jnp.dot((rl * rl).astype(BF16), wd_ref[sl, :], preferred_element_type=F32)
        if with_loss:
            d = hout_ref[...] - t_ref[...]
            hout_ref[...] = d * (1.0 / D)
            l_ref[...] += _sum8(d * d) * (0.5 / D)

    row = _rows(TMF, D)
    outs = pl.pallas_call(
        body, name=name, grid=(s // TMF,),
        in_specs=[row, row, _full((D, D)), _full((1, D)), _full((NDEV, D, fj)), _full((FF, D))]
        + ([row] if with_loss else []),
        out_specs=[row, row, _rows(TMF, FF)] + ([_acc_spec()] if with_loss else []),
        out_shape=[jax.ShapeDtypeStruct((s, D), F32), jax.ShapeDtypeStruct((s, D), F32),
                   jax.ShapeDtypeStruct((s, FF), BF16)] + ([jax.ShapeDtypeStruct((8, D), F32)] if with_loss else []),
        scratch_shapes=[pltpu.VMEM((TMF, D), BF16)],
        compiler_params=_cp(("arbitrary",) if with_loss else ("parallel",)),
    )(*([res, px, pw, g, wu, wd] + ([tgt] if with_loss else [])))
    return outs


def _head_rstd(x2, lo):
    sq = x2 * x2
    s_lo = jnp.sum(jnp.where(lo, sq, 0.0), axis=-1, keepdims=True)
    s_hi = jnp.sum(jnp.where(lo, 0.0, sq), axis=-1, keepdims=True)
    return lax.rsqrt(jnp.where(lo, s_lo, s_hi) * (1.0 / HD) + EPS)


def _kvq_fwd(h2, kv_norm, b_norm, w_kv, w_q, k_norm_t, q_norm_t):
    s = h2.shape[0]
    assert PADK == TM

    def body(h_ref, gkv_ref, gb_ref, wkv_ref, wq_ref, kn_ref, qn_ref,
             kraw_ref, qraw_ref, k_ref, v_ref, q_ref):
        i = pl.program_id(0)

        @pl.when(i == 0)
        def _():
            k_ref[...] = jnp.zeros_like(k_ref)
            v_ref[...] = jnp.zeros_like(v_ref)

        @pl.when(i > 0)
        def _():
            h = h_ref[...]
            xhat = h * _rstd(h)
            nk = (xhat * gkv_ref[...]).astype(BF16)
            qr = jnp.dot((xhat * gb_ref[...]).astype(BF16), wq_ref[...], preferred_element_type=F32)
            qraw_ref[...] = qr
            lo = lax.broadcasted_iota(jnp.int32, (1, 128), 1) < HD
            cw = 2 * D // NDEV
            for d in range(NDEV):
                kvd = jnp.dot(nk, wkv_ref[d], preferred_element_type=F32)
                if d < NDEV // 2:
                    kraw_ref[:, d * cw:(d + 1) * cw] = kvd
                    for p in range(cw // 128):
                        sl = slice(d * cw + p * 128, d * cw + (p + 1) * 128)
                        k2 = kvd[:, p * 128:(p + 1) * 128]
                        k_ref[:, sl] = (k2 * _head_rstd(k2, lo) * kn_ref[:, sl]).astype(BF16)
                else:
                    v_ref[:, d * cw - D:(d + 1) * cw - D] = kvd.astype(BF16)
            for p in range(D // 128):
                sl = slice(p * 128, (p + 1) * 128)
                q2 = qr[:, sl]
                q_ref[:, sl] = (q2 * _head_rstd(q2, lo) * qn_ref[:, sl] * (HD ** -0.5)).astype(BF16)

    prev = pl.BlockSpec((TM, D), lambda i: (jnp.maximum(i - 1, 0), 0))
    cur = pl.BlockSpec((TM, D), lambda i: (i, 0))
    return pl.pallas_call(
        body, name="kvq_fwd", grid=(s // TM + 1,),
        in_specs=[prev, _full((1, D)), _full((1, D)), _full((NDEV, D, 2 * D // NDEV)), _full((D, D)), _full((1, D)),
                  _full((1, D))],
        out_specs=[prev, prev, cur, cur, prev],
        out_shape=[jax.ShapeDtypeStruct((s, D), F32), jax.ShapeDtypeStruct((s, D), F32),
                   jax.ShapeDtypeStruct((s + PADK, D), BF16), jax.ShapeDtypeStruct((s + PADK, D), BF16),
                   jax.ShapeDtypeStruct((s, D), BF16)],
        compiler_params=_cp(("arbitrary",)),
    )(h2, kv_norm, b_norm, w_kv, w_q, k_norm_t, q_norm_t)


_TOEP = QB + KB


def _bias_from_diag(diag_ref, bias_ref):
    row8 = lax.broadcasted_iota(jnp.int32, (8, _TOEP), 0)
    kchunk = lax.broadcasted_iota(jnp.int32, (8, KB), 1) // CHUNK
    for a in range(HPS):
        v = jnp.broadcast_to(diag_ref[a:a + 1, :], (8, _TOEP))
        z0 = v
        for b in range(1, 8):
            z0 = jnp.where(row8 == b, pltpu.roll(v, b, axis=1), z0)
        for t in range(QB // 8):
            slab = z0 if t == 0 else pltpu.roll(z0, 8 * t, axis=1)
            qchunk = (8 * t) // CHUNK
            band = jnp.logical_and(kchunk >= qchunk, kchunk <= qchunk + PADK // CHUNK)
            bias_ref[a, 8 * t:8 * t + 8, :] = jnp.where(band, slab[:, :KB], NEG)


def _diag_sums(db_ref, a):
    row8 = lax.broadcasted_iota(jnp.int32, (8, _TOEP), 0)
    z = jnp.zeros((8, _TOEP), F32)
    for t in range(QB // 8):
        slab = jnp.concatenate([db_ref[a, 8 * t:8 * t + 8, :], jnp.zeros((8, _TOEP - KB), F32)], axis=1)
        z = z + (slab if t == 0 else pltpu.roll(slab, _TOEP - 8 * t, axis=1))
    e = z
    for b in range(1, 8):
        e = jnp.where(row8 == b, pltpu.roll(z, _TOEP - b, axis=1), e)
    return e


def _attn_specs(nqb):
    qspec = pl.BlockSpec((QB, LW), lambda p, j: (jnp.minimum(j, nqb - 1), p))
    kspecs = [pl.BlockSpec((QB, LW), functools.partial(lambda p, j, t: (jnp.minimum(j, nqb - 1) + t, p), t=t))
              for t in range(KB // QB)]
    dspec = pl.BlockSpec((None, HPS, _TOEP), lambda p, j: (p, 0, 0))
    return qspec, kspecs, dspec


V7X_MXU = 256
HPT = V7X_MXU // HD


def _head_masks():
    head = lax.broadcasted_iota(jnp.int32, (1, V7X_MXU), 1) // HD
    return [head == t for t in range(HPT)]


def _tile_of(a):
    return slice((a // HPT) * V7X_MXU, (a // HPT + 1) * V7X_MXU)


def _pick_heads(parts, masks):
    tiles = []
    for g in range(HPS // HPT):
        out = parts[g * HPT + HPT - 1]
        for t in range(HPT - 2, -1, -1):
            out = jnp.where(masks[t], parts[g * HPT + t], out)
        tiles.append(out)
    return tiles[0] if len(tiles) == 1 else jnp.concatenate(tiles, axis=1)


def _attn_fwd(q, kpad, vpad, diag):
    s = q.shape[0]
    nqb = s // QB
    npad = PADK // QB

    def body(q_ref, k0, k1, k2, v0, v1, v2, diag_ref, o_ref, bias_ref, sc_ref, eb_ref, rl_ref):
        j = pl.program_id(1)

        @pl.when(j == 0)
        def _():
            _bias_from_diag(diag_ref, bias_ref)

        def block(masked):
            kcat = jnp.concatenate([k0[...], k1[...], k2[...]], axis=0)
            vcat = jnp.concatenate([v0[...], v1[...], v2[...]], axis=0)
            q2 = q_ref[...]
            masks = _head_masks()
            valid = (lax.broadcasted_iota(jnp.int32, (1, KB), 1) + j * QB >= PADK) if masked else None
            outs = []
            for a in range(HPS):
                qa = q2[:, _tile_of(a)]
                sc_ref[a] = lax.dot_general(jnp.where(masks[a % HPT], qa, jnp.zeros_like(qa)), kcat[:, _tile_of(a)],
                                            _NT, preferred_element_type=F32)
            for a in range(HPS):
                for c in range(QB // ATT_RC):
                    r = slice(c * ATT_RC, (c + 1) * ATT_RC)
                    sc = sc_ref[a, r, :] + bias_ref[a, r, :]
                    if masked:
                        sc = jnp.where(valid, sc, NEG)
                    e = jnp.exp(sc - jnp.max(sc, axis=-1, keepdims=True))
                    eb_ref[a, r, :] = e.astype(BF16)
                    rl_ref[a, r, :] = jnp.broadcast_to(1.0 / jnp.sum(e, axis=-1, keepdims=True),
                                                       (ATT_RC, V7X_MXU))
                outs.append(jnp.dot(eb_ref[a], vcat[:, _tile_of(a)], preferred_element_type=F32) * rl_ref[a])
            o_ref[...] = _pick_heads(outs, masks).astype(BF16)

        pl.when(j < npad)(functools.partial(block, True))
        pl.when(j >= npad)(functools.partial(block, False))

    qspec, kspecs, dspec = _attn_specs(nqb)
    assert len(kspecs) == 3
    return pl.pallas_call(
        body, name="attn_fwd", grid=(D // LW, nqb),
        in_specs=[qspec] + kspecs + kspecs + [dspec],
        out_specs=qspec,
        out_shape=jax.ShapeDtypeStruct((s, D), BF16),
        scratch_shapes=[pltpu.VMEM((HPS, QB, KB), F32), pltpu.VMEM((HPS, QB, KB), F32),
                        pltpu.VMEM((HPS, QB, KB), BF16), pltpu.VMEM((HPS, QB, V7X_MXU), F32)],
        compiler_params=_cp(("parallel", "arbitrary")),
    )(q, kpad, kpad, kpad, vpad, vpad, vpad, diag)


def _rms_bwd(dn, xhat, r, g):
    dng = dn * g
    return r * (dng - xhat * jnp.mean(dng * xhat, axis=-1, keepdims=True))


def _acc_spec():
    return pl.BlockSpec((8, D), lambda i: (0, 0))


def _mlp_bwd(gout, hin, up, g, wu, wd, name, post_w=None):
    s = gout.shape[0]
    fj = 512

    def body(go_ref, hin_ref, up_ref, g_ref, wu_ref, wd_ref, *rest):
        if post_w is None:
            gin_ref, dup_ref, n_ref, dg_ref, gob_ref, dn_ref = rest
        else:
            pw_ref, gin_ref, dup_ref, n_ref, dg_ref, dpx_ref, gob_ref, dn_ref = rest

        @pl.when(pl.program_id(0) == 0)
        def _():
            dg_ref[...] = jnp.zeros_like(dg_ref)

        gob_ref[...] = go_ref[...].astype(BF16)
        for j in range(FF // fj):
            sl = slice(j * fj, (j + 1) * fj)
            rl = jnp.maximum(up_ref[:, sl].astype(F32), 0.0)
            dact = lax.dot_general(gob_ref[...], wd_ref[sl, :], _NT, preferred_element_type=F32)
            dupj = (dact * (2.0 * rl)).astype(BF16)
            dup_ref[:, sl] = dupj
            part = lax.dot_general(dupj, wu_ref[j], _NT, preferred_element_type=F32)
            if j == 0:
                dn_ref[...] = part
            else:
                dn_ref[...] += part
        hin = hin_ref[...]
        r = _rstd(hin)
        xhat = hin * r
        n_ref[...] = (xhat * g_ref[...]).astype(BF16)
        dn = dn_ref[...]
        gin = go_ref[...] + _rms_bwd(dn, xhat, r, g_ref[...])
        gin_ref[...] = gin
        dg_ref[...] += _sum8(dn * xhat)
        if post_w is not None:
            dpx_ref[...] = lax.dot_general(gin.astype(BF16), pw_ref[...], _NT,
                                           preferred_element_type=F32).astype(BF16)

    row = _rows(TMM, D)
    wide = _rows(TMM, FF)
    fused = post_w is not None
    return pl.pallas_call(
        body, name=name, grid=(s // TMM,),
        in_specs=[row, row, wide, _full((1, D)), _full((NDEV, D, fj)), _full((FF, D))]
        + ([_full((D, D))] if fused else []),
        out_specs=[row, wide, row, _acc_spec()] + ([row] if fused else []),
        out_shape=[jax.ShapeDtypeStruct((s, D), F32), jax.ShapeDtypeStruct((s, FF), BF16),
                   jax.ShapeDtypeStruct((s, D), BF16), jax.ShapeDtypeStruct((8, D), F32)]
        + ([jax.ShapeDtypeStruct((s, D), BF16)] if fused else []),
        scratch_shapes=[pltpu.VMEM((TMM, D), BF16), pltpu.VMEM((TMM, D), F32)],
        compiler_params=_cp(("arbitrary",)),
    )(*([gout, hin, up, g, wu, wd] + ([post_w] if fused else [])))


def _matmul_tn(a, b, slab, out_dtype, name, b_relu2=False):
    s, m = a.shape
    n = b.shape[1]
    ts = min(s, 512 if n > 2048 else 1024)
    nk = s // ts
    nc = 512
    w = n // NDEV

    def body(a_ref, b_ref, o_ref, at_ref, acc_ref):
        k = pl.program_id(0)
        at_ref[...] = a_ref[...].astype(BF16).T

        @pl.when(k == 0)
        def _():
            acc_ref[...] = jnp.zeros_like(acc_ref)

        for c in range(n // nc):
            sl = slice(c * nc, (c + 1) * nc)
            bc = b_ref[:, sl]
            if b_relu2:
                rl = jnp.maximum(bc.astype(F32), 0.0)
                bc = rl * rl
            acc_ref[:, sl] += jnp.dot(at_ref[...], bc.astype(BF16), preferred_element_type=F32)

        @pl.when(k == nk - 1)
        def _():
            if slab:
                for d in range(NDEV):
                    o_ref[d] = acc_ref[:, d * w:(d + 1) * w].astype(out_dtype)
            else:
                o_ref[...] = acc_ref[...].astype(out_dtype)

    if slab:
        out_shape = jax.ShapeDtypeStruct((NDEV, m, w), out_dtype)
        out_spec = pl.BlockSpec((NDEV, m, w), lambda k: (0, 0, 0), pipeline_mode=pl.Buffered(1))
    else:
        out_shape = jax.ShapeDtypeStruct((m, n), out_dtype)
        out_spec = pl.BlockSpec((m, n), lambda k: (0, 0), pipeline_mode=pl.Buffered(1))
    return pl.pallas_call(
        body, name=name, grid=(nk,),
        in_specs=[pl.BlockSpec((ts, m), lambda k: (k, 0)), pl.BlockSpec((ts, n), lambda k: (k, 0))],
        out_specs=out_spec, out_shape=out_shape,
        scratch_shapes=[pltpu.VMEM((m, ts), BF16), pltpu.VMEM((m, n), F32)],
        compiler_params=_cp(("arbitrary",)),
    )(a, b)


def _attn_bwd(q, kpad, vpad, do, diag):
    s = q.shape[0]
    nqb = s // QB
    npad = PADK // QB

    def body(q_ref, k0, k1, k2, v0, v1, v2, do_ref, diag_ref, dq_ref, dk_ref, dv_ref, dd_ref,
             bias_ref, db_ref, dka_ref, dva_ref, sc_ref, dp_ref, dsb_ref, pb_ref):
        j = pl.program_id(1)

        @pl.when(j == 0)
        def _():
            _bias_from_diag(diag_ref, bias_ref)
            dka_ref[...] = jnp.zeros_like(dka_ref)
            dva_ref[...] = jnp.zeros_like(dva_ref)
            db_ref[...] = jnp.zeros_like(db_ref)

        def block(masked):
            kcat = jnp.concatenate([k0[...], k1[...], k2[...]], axis=0)
            vcat = jnp.concatenate([v0[...], v1[...], v2[...]], axis=0)
            q2 = q_ref[...]
            do2 = do_ref[...]
            masks = _head_masks()
            valid = (lax.broadcasted_iota(jnp.int32, (1, KB), 1) + j * QB >= PADK) if masked else None
            qt = q2.T
            dot_ = do2.T
            dq = []
            for a in range(HPS):
                qa, doa = q2[:, _tile_of(a)], do2[:, _tile_of(a)]
                sc_ref[a] = lax.dot_general(jnp.where(masks[a % HPT], qa, jnp.zeros_like(qa)), kcat[:, _tile_of(a)],
                                            _NT, preferred_element_type=F32)
                dp_ref[a] = lax.dot_general(jnp.where(masks[a % HPT], doa, jnp.zeros_like(doa)),
                                            vcat[:, _tile_of(a)], _NT, preferred_element_type=F32)
            for a in range(HPS):
                for c in range(QB // ATT_RC):
                    r = slice(c * ATT_RC, (c + 1) * ATT_RC)
                    sc = sc_ref[a, r, :] + bias_ref[a, r, :]
                    if masked:
                        sc = jnp.where(valid, sc, NEG)
                    e = jnp.exp(sc - jnp.max(sc, axis=-1, keepdims=True))
                    p = e * (1.0 / jnp.sum(e, axis=-1, keepdims=True))
                    dp = dp_ref[a, r, :]
                    ds = p * (dp - jnp.sum(p * dp, axis=-1, keepdims=True))
                    db_ref[a, r, :] += ds
                    dsb_ref[a, r, :] = ds.astype(BF16)
                    pb_ref[a, r, :] = p.astype(BF16)
                hd = slice(a * HD, (a + 1) * HD)
                dq.append(jnp.dot(dsb_ref[a], kcat[:, _tile_of(a)], preferred_element_type=F32))
                dka_ref[hd, :] += jnp.dot(qt[hd, :], dsb_ref[a], preferred_element_type=F32)
                dva_ref[hd, :] += jnp.dot(dot_[hd, :], pb_ref[a], preferred_element_type=F32)
            dq_ref[...] = _pick_heads(dq, masks) * (HD ** -0.5)

        pl.when(j < npad)(functools.partial(block, True))
        pl.when(jnp.logical_and(j >= npad, j < nqb))(functools.partial(block, False))

        @pl.when(j == nqb - 1)
        def _():
            for a in range(HPS):
                dd_ref[a] = _diag_sums(db_ref, a)

        dk_ref[...] = dka_ref[:, 0:QB].T
        dv_ref[...] = dva_ref[:, 0:QB].T
        dka_ref[:, 0:KB - QB] = dka_ref[:, QB:KB]
        dva_ref[:, 0:KB - QB] = dva_ref[:, QB:KB]
        dka_ref[:, KB - QB:KB] = jnp.zeros((LW, QB), F32)
        dva_ref[:, KB - QB:KB] = jnp.zeros((LW, QB), F32)

    qspec, kspecs, dspec = _attn_specs(nqb)
    kout = pl.BlockSpec((QB, LW), lambda p, j: (jnp.maximum(j - npad, 0), p))
    sd = jax.ShapeDtypeStruct((s, D), F32)
    return pl.pallas_call(
        body, name="attn_bwd", grid=(D // LW, nqb + npad),
        in_specs=[qspec] + kspecs + kspecs + [qspec, dspec],
        out_specs=[qspec, kout, kout, pl.BlockSpec((None, HPS, 8, _TOEP), lambda p, j: (p, 0, 0, 0))],
        out_shape=[sd, sd, sd, jax.ShapeDtypeStruct((NH // HPS, HPS, 8, _TOEP), F32)],
        scratch_shapes=[pltpu.VMEM((HPS, QB, KB), F32), pltpu.VMEM((HPS, QB, KB), F32),
                        pltpu.VMEM((LW, KB), F32), pltpu.VMEM((LW, KB), F32),
                        pltpu.VMEM((HPS, QB, KB), F32), pltpu.VMEM((HPS, QB, KB), F32),
                        pltpu.VMEM((HPS, QB, KB), BF16), pltpu.VMEM((HPS, QB, KB), BF16)],
        compiler_params=_cp(("parallel", "arbitrary")),
    )(q, kpad, kpad, kpad, vpad, vpad, vpad, do, diag)


def _head_norm_bwd(dy2, x2, g2, lo):
    rr = _head_rstd(x2, lo)
    xhat = x2 * rr
    t = dy2 * g2 * xhat
    m_lo = jnp.sum(jnp.where(lo, t, 0.0), axis=-1, keepdims=True)
    m_hi = jnp.sum(jnp.where(lo, 0.0, t), axis=-1, keepdims=True)
    m = jnp.where(lo, m_lo, m_hi) * (1.0 / HD)
    return rr * (dy2 * g2 - xhat * m), dy2 * xhat


def _kvq_bwd(dq, dk, dv, qraw, kraw, h2, g3, kv_norm, b_norm, w_kv, w_q, k_norm_t, q_norm_t):
    s = h2.shape[0]

    def body(dq_ref, dk_ref, dv_ref, qraw_ref, kraw_ref, h_ref, g3_ref, gkv_ref, gb_ref, wkv_ref, wq_ref,
             kn_ref, qn_ref, g2_ref, dqr_ref, dkv_ref, nb_ref, nk_ref, dgq_ref, dgk_ref, dgb_ref, dgkv_ref):
        @pl.when(pl.program_id(0) == 0)
        def _():
            for r in (dgq_ref, dgk_ref, dgb_ref, dgkv_ref):
                r[...] = jnp.zeros_like(r)

        lo = lax.broadcasted_iota(jnp.int32, (1, 128), 1) < HD
        for p in range(D // 128):
            sl = slice(p * 128, (p + 1) * 128)
            dx, dgp = _head_norm_bwd(dq_ref[:, sl], qraw_ref[:, sl], qn_ref[:, sl], lo)
            dqr_ref[:, sl] = dx.astype(BF16)
            dgq_ref[:, sl] += _sum8(dgp)
            dx, dgp = _head_norm_bwd(dk_ref[:, sl], kraw_ref[:, sl], kn_ref[:, sl], lo)
            dkv_ref[:, sl] = dx.astype(BF16)
            dgk_ref[:, sl] += _sum8(dgp)
        dkv_ref[:, D:] = dv_ref[...].astype(BF16)
        dnb = lax.dot_general(dqr_ref[...], wq_ref[...], _NT, preferred_element_type=F32)
        cw = 2 * D // NDEV
        dnk = lax.dot_general(dkv_ref[:, 0:cw], wkv_ref[0], _NT, preferred_element_type=F32)
        for d in range(1, NDEV):
            dnk = dnk + lax.dot_general(dkv_ref[:, d * cw:(d + 1) * cw], wkv_ref[d], _NT, preferred_element_type=F32)
        h = h_ref[...]
        r = _rstd(h)
        xhat = h * r
        dxg = dnb * gb_ref[...] + dnk * gkv_ref[...]
        g2_ref[...] = g3_ref[...] + r * (dxg - xhat * jnp.mean(dxg * xhat, axis=-1, keepdims=True))
        dgb_ref[...] += _sum8(dnb * xhat)
        dgkv_ref[...] += _sum8(dnk * xhat)
        nb_ref[...] = (xhat * gb_ref[...]).astype(BF16)
        nk_ref[...] = (xhat * gkv_ref[...]).astype(BF16)

    row = _rows(TM, D)
    sd = jax.ShapeDtypeStruct((s, D), BF16)
    acc = jax.ShapeDtypeStruct((8, D), F32)
    return pl.pallas_call(
        body, name="kvq_bwd", grid=(s // TM,),
        in_specs=[row] * 7 + [_full((1, D)), _full((1, D)), _full((NDEV, D, 2 * D // NDEV)), _full((D, D)),
                              _full((1, D)), _full((1, D))],
        out_specs=[row, row, _rows(TM, 2 * D), row, row] + [_acc_spec()] * 4,
        out_shape=[jax.ShapeDtypeStruct((s, D), F32), sd, jax.ShapeDtypeStruct((s, 2 * D), BF16), sd, sd,
                   acc, acc, acc, acc],
        compiler_params=_cp(("arbitrary",)),
    )(dq, dk, dv, qraw, kraw, h2, g3, kv_norm, b_norm, w_kv, w_q, k_norm_t, q_norm_t)


def _lru_bwd(g1, gate, rec, hs, w_out, conv_w, conv_b, wg, bg, lam):
    s = g1.shape[0]
    nt = s // TL

    def body(g1_ref, gate_ref, rec_ref, recp_ref, hs_ref, hsp_ref, wo_ref, cw_ref, cb_ref, wg_ref, bg_ref, lam_ref,
             du_ref, dcw_ref, dcb_ref, dwg_ref, dbg_ref, dlam_ref, ext_ref, dext_ref, cg_ref,
             a_sc, dh_sc, hl_sc, ac_sc, rg_sc, ig_sc, mult_sc, rc_sc):
        i = pl.program_id(0)
        first_tile = i == nt - 1

        @pl.when(i == 0)
        def _():
            dext_ref[TL:TL + 8, :] = jnp.zeros((8, D), F32)
            cg_ref[...] = jnp.zeros_like(cg_ref)
            for r in (dcw_ref, dcb_ref, dwg_ref, dbg_ref, dlam_ref):
                r[...] = jnp.zeros_like(r)

        keep = jnp.where(first_tile, 0.0, 1.0)
        ext_ref[0:8, :] = recp_ref[...] * keep
        ext_ref[8:8 + TL, :] = rec_ref[...]
        dy = lax.dot_general(g1_ref[...].astype(BF16), wo_ref[...], _NT, preferred_element_type=F32)
        lam_v = lam_ref[...]
        sp = _softplus_neg(lam_v)
        dsp_dlam = -_sigmoid(-lam_v)
        rows = lax.broadcasted_iota(jnp.int32, (TL, BW), 0)
        for n in range(NBLK):
            sl = slice(n * BW, (n + 1) * BW)
            rc = _conv(ext_ref, cw_ref, cb_ref, sl, TL)
            rg, ig, a, mult = _lru_gates(rc, wg_ref[n], bg_ref[n:n + 1, :], sp[:, sl])
            h = hs_ref[:, sl]
            gt = gate_ref[:, sl]
            dyn = dy[:, sl]
            gl, gl_grad = _gelu_and_grad(gt)
            du_ref[:, sl] = (dyn * h * gl_grad).astype(BF16)
            dh_sc[n] = dyn * gl + jnp.where(rows == TL - 1, cg_ref[0:1, sl], 0.0)
            a_sc[n], rg_sc[n], ig_sc[n], mult_sc[n], rc_sc[n] = a, rg, ig, mult, rc
            ac_sc[n] = _shift_up(a, 1, 0.0, rows, TL)
        _tile_scan(ac_sc, dh_sc, hl_sc, [jnp.zeros((1, BW), F32)] * NBLK, reverse=True)
        for n in range(NBLK):
            sl = slice(n * BW, (n + 1) * BW)
            gsc = hl_sc[n]
            a, rg, ig, mult, rc = a_sc[n], rg_sc[n], ig_sc[n], mult_sc[n], rc_sc[n]
            cg_ref[0:1, sl] = a[0:1, :] * gsc[0:1, :]
            hprev = _shift_down(hs_ref[:, sl], 1, hsp_ref[7:8, sl] * keep, rows)
            da = gsc * hprev
            d_mult = gsc * ig * rc
            d_ig = gsc * mult * rc
            d_rc = gsc * mult * ig
            d_la = da * a - d_mult * (a * a) / mult
            d_rg = d_la * ((-LRU_C) * sp[:, sl])
            dlam_ref[:, sl] += _sum8(d_la * ((-LRU_C) * rg)) * dsp_dlam[:, sl]
            dg = jnp.concatenate([d_rg * rg * (1.0 - rg), d_ig * ig * (1.0 - ig)], axis=1)
            dgb = dg.astype(BF16)
            d_rc = d_rc + lax.dot_general(dgb, wg_ref[n], _NT, preferred_element_type=F32)
            dwg_ref[n] += lax.dot_general(rc.astype(BF16), dgb, _TN, preferred_element_type=F32)
            dbg_ref[n] += _sum8(dg)
            dext_ref[0:TL, sl] = d_rc
            dcb_ref[:, sl] += _sum8(d_rc)
            for k in range(4):
                dcw_ref[k, :, sl] += _sum8(d_rc * ext_ref[5 + k:5 + k + TL, sl])
        for k in range(4):
            part = cw_ref[3 - k:4 - k, :] * dext_ref[k:k + TL, :]
            acc = part if k == 0 else acc + part
        du_ref[:, D:] = acc.astype(BF16)
        dext_ref[TL:TL + 8, :] = dext_ref[0:8, :]

    rev = pl.BlockSpec((TL, D), lambda i: (nt - 1 - i, 0))
    rev8 = pl.BlockSpec((8, D), lambda i: (jnp.maximum((nt - 1 - i) * (TL // 8) - 1, 0), 0))
    acc = jax.ShapeDtypeStruct((8, D), F32)
    return pl.pallas_call(
        body, name="lru_bwd", grid=(nt,),
        in_specs=[rev, rev, rev, rev8, rev, rev8, _full((D, D)), _full((4, D)), _full((1, D)),
                  _full((NBLK, BW, 2 * BW)), _full((NBLK, 2 * BW)), _full((1, D))],
        out_specs=[pl.BlockSpec((TL, 2 * D), lambda i: (nt - 1 - i, 0)),
                   pl.BlockSpec((4, 8, D), lambda i: (0, 0, 0)), _acc_spec(),
                   pl.BlockSpec((NBLK, BW, 2 * BW), lambda i: (0, 0, 0)),
                   pl.BlockSpec((NBLK, 8, 2 * BW), lambda i: (0, 0, 0)), _acc_spec()],
        out_shape=[jax.ShapeDtypeStruct((s, 2 * D), BF16), jax.ShapeDtypeStruct((4, 8, D), F32), acc,
                   jax.ShapeDtypeStruct((NBLK, BW, 2 * BW), F32), jax.ShapeDtypeStruct((NBLK, 8, 2 * BW), F32), acc],
        scratch_shapes=[pltpu.VMEM((TL + 8, D), F32), pltpu.VMEM((TL + 8, D), F32), pltpu.VMEM((8, D), F32)]
        + [pltpu.VMEM((NBLK, TL, BW), F32)] * 8,
        compiler_params=_cp(("arbitrary",)),
    )(g1, gate, rec, rec, hs, hs, w_out, conv_w, conv_b, wg, bg, lam)


def _a_in_bwd(du, h0, g1, a_norm, w_in):
    s = h0.shape[0]

    def body(du_ref, h_ref, g1_ref, an_ref, win_ref, gx_ref, dg_ref):
        @pl.when(pl.program_id(0) == 0)
        def _():
            dg_ref[...] = jnp.zeros_like(dg_ref)

        cw = 2 * D // NDEV
        dn = lax.dot_general(du_ref[:, 0:cw], win_ref[0], _NT, preferred_element_type=F32)
        for d in range(1, NDEV):
            dn = dn + lax.dot_general(du_ref[:, d * cw:(d + 1) * cw], win_ref[d], _NT, preferred_element_type=F32)
        h = h_ref[...]
        r = _rstd(h)
        xhat = h * r
        gx_ref[...] = g1_ref[...] + _rms_bwd(dn, xhat, r, an_ref[...])
        dg_ref[...] += _sum8(dn * xhat)

    row = _rows(TM, D)
    return pl.pallas_call(
        body, name="a_in_bwd", grid=(s // TM,),
        in_specs=[_rows(TM, 2 * D), row, row, _full((1, D)), _full((NDEV, D, 2 * D // NDEV))],
        out_specs=[row, _acc_spec()],
        out_shape=[jax.ShapeDtypeStruct((s, D), F32), jax.ShapeDtypeStruct((8, D), F32)],
        compiler_params=_cp(("arbitrary",)),
    )(du, h0, g1, a_norm, w_in)


def _rel_onehot():
    m = np.arange(_TOEP)
    signed = np.where(m < KB, m, m - _TOEP)
    idx = np.clip(PADK - signed, -(CHUNK - 1), 2 * CHUNK) + (CHUNK - 1)
    return (idx[None, :] == np.arange(NREL)[:, None]).astype(np.float32)


def _bias_diagonals(rel_bias):
    diag = jnp.dot(rel_bias, jnp.asarray(_rel_onehot()), precision=lax.Precision.HIGHEST)
    return diag.reshape(NH // HPS, HPS, _TOEP)


def _rel_bias_grad(dd):
    rows = 8
    z = dd
    oh = np.zeros((_TOEP, 256), np.float32)
    oh[:, :NREL] = _rel_onehot().T

    def body(z_ref, oh_ref, o_ref):
        d = jnp.sum(z_ref[...], axis=0, keepdims=True)
        hi = d.astype(BF16)
        mid = (d - hi.astype(F32)).astype(BF16)
        lo = (d - hi.astype(F32) - mid.astype(F32)).astype(BF16)
        ohb = oh_ref[...].astype(BF16)
        acc = jnp.zeros((8, 256), F32)
        for piece in (lo, mid, hi):
            acc = acc + jnp.dot(jnp.broadcast_to(piece, (8, _TOEP)), ohb, preferred_element_type=F32)
        o_ref[...] = acc

    out = pl.pallas_call(
        body, name="rel_bias_grad", grid=(NH,),
        in_specs=[pl.BlockSpec((None, rows, _TOEP), lambda h: (h, 0, 0)), pl.BlockSpec((_TOEP, 256), lambda h: (0, 0))],
        out_specs=pl.BlockSpec((None, 8, 256), lambda h: (h, 0, 0)),
        out_shape=jax.ShapeDtypeStruct((NH, 8, 256), F32),
        compiler_params=_cp(("parallel",)),
    )(z, jnp.asarray(oh))
    return out[:, 0, :NREL]


def _exchange(arrays, scatter, name):
    n = len(arrays)

    def body(*refs):
        ins, outs = refs[:n], refs[n:2 * n]
        token, (send_sems, recv_sems, local_sems) = refs[2 * n], refs[2 * n + 1:]
        token[...] = jnp.zeros_like(token)
        x, y, c = lax.axis_index("x"), lax.axis_index("y"), lax.axis_index("c")
        me = 4 * x + 2 * y + c

        def peer_of(r):
            rx, ry, rc = (r >> 2) & 1, (r >> 1) & 1, r & 1
            px = 1 - x if rx else x
            py = 1 - y if ry else y
            pc = 1 - c if rc else c
            return (px, py, pc), 4 * px + 2 * py + pc

        local, sent = [], []
        for k in range(n):
            cp = pltpu.make_async_copy(ins[k].at[me] if scatter else ins[k], outs[k].at[me], local_sems.at[k])
            cp.start()
            local.append(cp)
            for r in range(1, NDEV):
                peer, peer_lin = peer_of(r)
                cp = pltpu.make_async_remote_copy(
                    src_ref=ins[k].at[peer_lin] if scatter else ins[k], dst_ref=outs[k].at[me],
                    send_sem=send_sems.at[k, r - 1], recv_sem=recv_sems.at[k, r - 1],
                    device_id=peer, device_id_type=pl.DeviceIdType.MESH)
                cp.start()
                sent.append(cp)
        for k in range(n):
            for r in range(1, NDEV):
                peer, peer_lin = peer_of(r)
                pltpu.make_async_remote_copy(
                    src_ref=ins[k].at[peer_lin] if scatter else ins[k], dst_ref=outs[k].at[peer_lin],
                    send_sem=send_sems.at[k, r - 1], recv_sem=recv_sems.at[k, r - 1],
                    device_id=peer, device_id_type=pl.DeviceIdType.MESH).wait_recv()
        for cp in sent:
            cp.wait_send()
        for cp in local:
            cp.wait()

    def slot_shape(a):
        return (NDEV,) + (a.shape[1:] if scatter else a.shape)

    anyspec = pl.BlockSpec(memory_space=pl.ANY)
    outs = pl.pallas_call(
        body, name=name,
        in_specs=[anyspec] * n, out_specs=[anyspec] * n + [pl.BlockSpec(memory_space=pltpu.VMEM)],
        out_shape=[jax.ShapeDtypeStruct(slot_shape(a), a.dtype) for a in arrays]
        + [jax.ShapeDtypeStruct((8, 128), F32)],
        scratch_shapes=[pltpu.SemaphoreType.DMA((n, NDEV - 1)), pltpu.SemaphoreType.DMA((n, NDEV - 1)),
                        pltpu.SemaphoreType.DMA((n,))],
        compiler_params=pltpu.CompilerParams(has_side_effects=True),
    )(*arrays)
    return outs[:n], outs[n]


def _peer(r):
    x, y, c = lax.axis_index("x"), lax.axis_index("y"), lax.axis_index("c")
    px = 1 - x if (r >> 2) & 1 else x
    py = 1 - y if (r >> 1) & 1 else y
    pc = 1 - c if r & 1 else c
    return (px, py, pc), 4 * px + 2 * py + pc


def _my_index():
    return 4 * lax.axis_index("x") + 2 * lax.axis_index("y") + lax.axis_index("c")


_HBM_SPEC = pl.BlockSpec(memory_space=pltpu.HBM)
_SEM_SPEC = pl.BlockSpec(memory_space=pltpu.SEMAPHORE)


_NPEER = NDEV - 1


def _exchange_start(arrays, scatter, name):
    n = len(arrays)
    ns = n * _NPEER
    slots = [(NDEV,) + (a.shape[1:] if scatter else a.shape) for a in arrays]

    def body(*refs):
        srcs, lands = refs[:n], refs[n:2 * n]
        send_sems, recv_sems = refs[2 * n:2 * n + ns], refs[2 * n + ns:2 * n + 2 * ns]
        token = refs[-1]
        me = _my_index()
        for k in range(n):
            for r in range(1, NDEV):
                peer, peer_lin = _peer(r)
                pltpu.make_async_remote_copy(
                    src_ref=srcs[k].at[peer_lin] if scatter else srcs[k], dst_ref=lands[k].at[me],
                    send_sem=send_sems[k * _NPEER + r - 1], recv_sem=recv_sems[k * _NPEER + r - 1],
                    device_id=peer, device_id_type=pl.DeviceIdType.MESH).start()
        token[...] = jnp.zeros_like(token)

    sem = pltpu.SemaphoreType.DMA(())
    outs = pl.pallas_call(
        body, name=name,
        out_shape=(*[sem] * (2 * ns), *[pltpu.HBM(a.shape, a.dtype) for a in arrays],
                   *[pltpu.HBM(s, a.dtype) for s, a in zip(slots, arrays)], jax.ShapeDtypeStruct((8, 128), F32)),
        in_specs=[_HBM_SPEC] * (2 * n),
        out_specs=(*[_SEM_SPEC] * (2 * ns), *[_HBM_SPEC] * (2 * n), pl.BlockSpec(memory_space=pltpu.VMEM)),
        input_output_aliases={k: 2 * ns + k for k in range(2 * n)},
        compiler_params=pltpu.CompilerParams(has_side_effects=pltpu.SideEffectType.DATAFLOW_SIDE_EFFECTING),
    )(*[pltpu.with_memory_space_constraint(a, pltpu.HBM) for a in arrays],
      *[pltpu.with_memory_space_constraint(lax.empty(s, a.dtype), pltpu.HBM) for s, a in zip(slots, arrays)])
    return outs[:ns], outs[ns:2 * ns], outs[2 * ns:2 * ns + n], outs[2 * ns + n:2 * ns + 2 * n], outs[-1]


def _exchange_wait(started, after, scatter, name):
    send_sems, recv_sems, srcs, lands, _ = started
    n = len(srcs)
    ns = n * _NPEER

    def body(*refs):
        src_refs, land_refs = refs[:n], refs[n:2 * n]
        ssem, rsem = refs[2 * n:2 * n + ns], refs[2 * n + ns:2 * n + 2 * ns]
        for k in range(n):
            for r in range(1, NDEV):
                peer, peer_lin = _peer(r)
                cp = pltpu.make_async_remote_copy(
                    src_ref=src_refs[k].at[peer_lin] if scatter else src_refs[k], dst_ref=land_refs[k].at[peer_lin],
                    send_sem=ssem[k * _NPEER + r - 1], recv_sem=rsem[k * _NPEER + r - 1],
                    device_id=peer, device_id_type=pl.DeviceIdType.MESH)
                cp.wait_send()
                cp.wait_recv()

    outs = pl.pallas_call(
        body, name=name,
        out_shape=tuple(pltpu.HBM(a.shape, a.dtype) for a in list(srcs) + list(lands)),
        in_specs=[_HBM_SPEC] * (2 * n) + [_SEM_SPEC] * (2 * ns) + [pl.BlockSpec(memory_space=pl.ANY)],
        out_specs=tuple([_HBM_SPEC] * (2 * n)),
        input_output_aliases={k: k for k in range(2 * n)},
        compiler_params=pltpu.CompilerParams(has_side_effects=pltpu.SideEffectType.DATAFLOW_SIDE_EFFECTING),
    )(*srcs, *lands, *send_sems, *recv_sems, after)
    return list(outs[:n]), list(outs[n:])


def _fill_own(lands, owns, me):
    return [lax.dynamic_update_slice(z, o, (me,) + (0,) * (z.ndim - 1)) for z, o in zip(lands, owns)]


def _sum_slots(st, name):
    _, r, c = st.shape

    def body(s_ref, o_ref):
        acc = s_ref[0]
        for d in range(1, NDEV):
            acc = acc + s_ref[d]
        o_ref[...] = acc

    return pl.pallas_call(
        body, name=name, out_shape=jax.ShapeDtypeStruct((r, c), F32),
        in_specs=[pl.BlockSpec((NDEV, r, c), lambda: (0, 0, 0))], out_specs=pl.BlockSpec((r, c), lambda: (0, 0)),
    )(st)


def _adamw(w, m, v, gst, name, transposed=False):
    r, c = w.shape
    ns = gst.shape[0]
    tr = min(r, 256)
    c1 = 1.0 - ADAM_B1 ** ADAM_STEP
    c2 = 1.0 - ADAM_B2 ** ADAM_STEP

    def body(w_ref, m_ref, v_ref, g_ref, go_ref, d_ref, mo_ref, vo_ref):
        g = g_ref[0].astype(F32)
        for d in range(1, ns):
            g = g + g_ref[d].astype(F32)
        if transposed:
            g = g.T
        m2 = ADAM_B1 * m_ref[...] + (1.0 - ADAM_B1) * g
        v2 = ADAM_B2 * v_ref[...] + (1.0 - ADAM_B2) * (g * g)
        go_ref[...] = g
        mo_ref[...] = m2
        vo_ref[...] = v2
        d_ref[...] = (-ADAM_LR) * ((m2 / c1) / (jnp.sqrt(v2 / c2) + ADAM_EPS) + ADAM_WD * w_ref[...])

    blk = pl.BlockSpec((tr, c), lambda i: (i, 0))
    sd = jax.ShapeDtypeStruct((r, c), F32)
    return pl.pallas_call(
        body, name=name, grid=(r // tr,),
        in_specs=[blk, blk, blk, pl.BlockSpec((ns, c, tr), lambda i: (0, 0, i)) if transposed
                  else pl.BlockSpec((ns, tr, c), lambda i: (0, i, 0))],
        out_specs=[blk, blk, blk, blk], out_shape=[sd, sd, sd, sd],
        compiler_params=_cp(("parallel",)),
    )(w, m, v, gst)


def _pack(pieces, rows):
    flat = jnp.concatenate([p.reshape(-1).astype(F32) for p in pieces])
    return jnp.pad(flat, (0, rows * 128 - flat.shape[0])).reshape(rows, 128)


def _unpack(flat, shapes):
    out, off = [], 0
    for shp in shapes:
        size = int(np.prod(shp))
        out.append(flat[off:off + size].reshape(shp))
        off += size
    return out


def _cols(full, me, width):
    return lax.dynamic_slice_in_dim(full, me * width, width, axis=full.ndim - 1)


def kernel(x, a_norm, a_w_in, a_conv_w, a_conv_b, a_w_gate, a_b_gate, a_lambda, a_w_out, kv_norm, w_kv, k_norm, b_norm, b_w_q, b_q_norm, b_rel_bias, b_w_o, mlp_norm, w_up, w_down, loss_target, m_a_norm, m_a_w_in, m_a_conv_w, m_a_conv_b, m_a_w_gate, m_a_b_gate, m_a_lambda, m_a_w_out, m_kv_norm, m_w_kv, m_k_norm, m_b_norm, m_b_w_q, m_b_q_norm, m_b_rel_bias, m_b_w_o, m_mlp_norm, m_w_up, m_w_down, v_a_norm, v_a_w_in, v_a_conv_w, v_a_conv_b, v_a_w_gate, v_a_b_gate, v_a_lambda, v_a_w_out, v_kv_norm, v_w_kv, v_k_norm, v_b_norm, v_b_w_q, v_b_q_norm, v_b_rel_bias, v_b_w_o, v_mlp_norm, v_w_up, v_w_down):
    me = 4 * lax.axis_index("x") + 2 * lax.axis_index("y") + lax.axis_index("c")
    sh = D // NDEV

    big_w = [a_w_in[0], a_w_out[0], w_kv, b_w_q[0], b_w_o[0], w_up[0], w_up[1], w_down[0], w_down[1]]
    small_sharded = [a_norm, a_conv_w, a_conv_b, a_b_gate, a_lambda, a_w_gate]
    small_rows = 272
    def to_bf16(w, token):
        return (w + token[0, 0]).astype(BF16)

    got, tok_a = _exchange([a_w_in[0].astype(BF16), _pack(small_sharded, small_rows)], False, "gather_a")
    own_b1 = [to_bf16(w, tok_a) for w in (a_w_out[0], w_up[0], w_down[0])]
    st_b1 = _exchange_start(own_b1, False, "gather_b1_start")
    own_b2 = [to_bf16(w, st_b1[4]) for w in (w_kv, b_w_q[0], b_w_o[0], w_up[1], w_down[1])]
    st_b2 = _exchange_start(own_b2, False, "gather_b2_start")
    w_in = got[0]
    sm = got[1].reshape(NDEV, small_rows * 128)
    an_f = sm[:, 0:128].reshape(1, D) + st_b2[4][0:1, 0:1]
    cw_f = sm[:, 128:640].reshape(NDEV, 4, sh).transpose(1, 0, 2).reshape(4, D)
    cb_f = sm[:, 640:768].reshape(1, D)
    bg_f = sm[:, 768:1024].reshape(NDEV, NBLK, 2 * BW // NDEV).transpose(1, 0, 2).reshape(NBLK, 2 * BW)
    lam_f = sm[:, 1024:1152].reshape(1, D)
    wg_f = sm[:, 1152:1152 + NBLK * BW * 32].reshape(NDEV, NBLK, BW, 32).transpose(1, 2, 0, 3)
    wg_f = wg_f.reshape(NBLK, BW, 2 * BW).astype(BF16)
    kn_t = jnp.tile(k_norm, NH).reshape(1, D)
    qn_t = jnp.tile(b_q_norm[0], NH).reshape(1, D)
    kvn = kv_norm.reshape(1, D)
    diag = _bias_diagonals(b_rel_bias[0])

    h0 = x[0]
    gate, rec, hs, y, n1 = _lru_fwd(h0, an_f, w_in, cw_f, cb_f, wg_f, bg_f, lam_f)
    own, land = _exchange_wait(st_b1, y, False, "gather_b1_wait")
    land = _fill_own(land, [o[None] for o in own], me)
    w_out = land[0].reshape(D, D)
    wu = [land[1], None]
    wd = [land[2].reshape(FF, D), None]
    h1, h2, up0 = _mlp_fwd(h0, y, w_out, mlp_norm[0:1], wu[0], wd[0], "mlp_fwd0")
    own, land = _exchange_wait(st_b2, h2, False, "gather_b2_wait")
    land = _fill_own(land, [o[None] for o in own], me)
    wkv = land[0]
    w_q = land[1].reshape(D, D)
    w_o = land[2].reshape(D, D)
    wu[1] = land[3]
    wd[1] = land[4].reshape(FF, D)
    kraw, qraw, kpad, vpad, q = _kvq_fwd(h2, kvn, b_norm, wkv, w_q, kn_t, qn_t)
    o = _attn_fwd(q, kpad, vpad, diag)
    h3, g4, up1, lpart = _mlp_fwd(h2, o, w_o, mlp_norm[1:2], wu[1], wd[1], "mlp_fwd1", tgt=loss_target[0])
    loss = lax.psum(jnp.sum(lpart), ("x", "y", "c"))

    g3, dup1, n3, dgm1, do = _mlp_bwd(g4, h3, up1, mlp_norm[1:2], wu[1], wd[1], "mlp_bwd1", post_w=w_o)
    d_wd1 = _matmul_tn(g4, up1, True, BF16, "dw_down1", b_relu2=True)
    d_wu1 = _matmul_tn(n3, dup1, True, BF16, "dw_up1")
    d_wo = _matmul_tn(o, g3, False, BF16, "dw_o").reshape(NDEV, sh, D)
    dq, dk, dv, dd = _attn_bwd(q, kpad, vpad, do, diag)
    g2, dqr, dkv, nb, nk, dgq, dgk, dgb, dgkv = _kvq_bwd(dq, dk, dv, qraw, kraw, h2, g3, kvn, b_norm, wkv, w_q,
                                                       kn_t, qn_t)
    d_wq = _matmul_tn(nb, dqr, False, BF16, "dw_q").reshape(NDEV, sh, D)
    d_wkv = _matmul_tn(nk, dkv, True, BF16, "dw_kv")
    st_r1 = _exchange_start([d_wd1, d_wu1, d_wo, d_wq, d_wkv], True, "scatter_r1_start")
    g1, dup0, n2, dgm0 = _mlp_bwd(g2, h1, up0, mlp_norm[0:1] + st_r1[4][0:1, 0:1], wu[0], wd[0], "mlp_bwd0")
    d_wd0 = _matmul_tn(g2, up0, True, BF16, "dw_down0", b_relu2=True)
    d_wu0 = _matmul_tn(n2, dup0, True, BF16, "dw_up0")
    d_wout = _matmul_tn(y, g1, False, BF16, "dw_out").reshape(NDEV, sh, D)
    st_r2 = _exchange_start([d_wu0, d_wd0, d_wout], True, "scatter_r2_start")
    du, dcw, dcb, dwg, dbg, dlam = _lru_bwd(g1, gate, rec, hs, w_out, cw_f, cb_f, wg_f, bg_f,
                                            lam_f + st_r2[4][0:1, 0:1])
    d_rel = _rel_bias_grad(dd.reshape(NH, 8, _TOEP))
    d_win = _matmul_tn(n1, du, True, BF16, "dw_in")
    dwg_slab = dwg.reshape(NBLK, BW, NDEV, 32).transpose(2, 0, 1, 3).reshape(NDEV, 256, 128).astype(BF16)
    st_r3 = _exchange_start([d_win, dwg_slab], True, "scatter_r3_start")
    gx, dga = _a_in_bwd(du, h0, g1, an_f + st_r3[4][0:1, 0:1], w_in)
    small_full = [dga.sum(0), dcw.sum(1), dcb.sum(0), dbg.sum(1), dlam.sum(0), dgkv.sum(0),
                  dgk.sum(0).reshape(NH, HD).sum(0), dgb.sum(0), dgq.sum(0).reshape(NH, HD).sum(0), d_rel,
                  jnp.stack([dgm0.sum(0), dgm1.sum(0)])]
    small_g_rows = 136
    small_slabs = jnp.broadcast_to(_pack(small_full, small_g_rows)[None], (NDEV, small_g_rows, 128))
    st_r4 = _exchange_start([small_slabs], True, "scatter_r4_start")
    src, recv1 = _exchange_wait(st_r1, st_r4[4], True, "scatter_r1_wait")
    recv1 = _fill_own(recv1, [lax.dynamic_slice_in_dim(a, me, 1, 0) for a in src], me)
    src, recv2 = _exchange_wait(st_r2, recv1[0], True, "scatter_r2_wait")
    recv2 = _fill_own(recv2, [lax.dynamic_slice_in_dim(a, me, 1, 0) for a in src], me)

    names = ["a_w_in", "a_w_out", "w_kv", "b_w_q", "b_w_o", "w_up0", "w_up1", "w_down0", "w_down1", "a_w_gate"]
    big_m = [m_a_w_in[0], m_a_w_out[0], m_w_kv, m_b_w_q[0], m_b_w_o[0], m_w_up[0], m_w_up[1], m_w_down[0],
             m_w_down[1], m_a_w_gate.reshape(256, 128)]
    big_v = [v_a_w_in[0], v_a_w_out[0], v_w_kv, v_b_w_q[0], v_b_w_o[0], v_w_up[0], v_w_up[1], v_w_down[0],
             v_w_down[1], v_a_w_gate.reshape(256, 128)]
    big_w = big_w + [a_w_gate.reshape(256, 128)]

    def update(k, g):
        return _adamw(big_w[k], big_m[k], big_v[k], g, "adamw_" + names[k], transposed=names[k].startswith("w_down"))

    early = {1: recv2[2], 2: recv1[4], 3: recv1[3], 4: recv1[2], 5: recv2[0], 6: recv1[1], 7: recv2[1], 8: recv1[0]}
    res = {k: update(k, g) for k, g in early.items()}
    src, recv3 = _exchange_wait(st_r3, res[8][1], True, "scatter_r3_wait")
    recv3 = _fill_own(recv3, [lax.dynamic_slice_in_dim(a, me, 1, 0) for a in src], me)
    res[0] = update(0, recv3[0])
    res[9] = update(9, recv3[1])
    res = [res[k] for k in range(len(names))]
    src, recv4 = _exchange_wait(st_r4, res[0][1], True, "scatter_r4_wait")
    recv4 = _fill_own(recv4, [lax.dynamic_slice_in_dim(a, me, 1, 0) for a in src], me)
    gs = _unpack(_sum_slots(recv4[0], "sum_small_grads").reshape(-1),
                 [(1, D), (4, D), (1, D), (NBLK, 2 * BW), (1, D), (D,), (HD,), (1, D), (1, HD), (1, NH, NREL), (2, D)])
    g_small = [_cols(gs[0], me, sh), _cols(gs[1], me, sh)[None], _cols(gs[2], me, sh),
               _cols(gs[3], me, 2 * BW // NDEV)[None], _cols(gs[4], me, sh)] + gs[5:]
    small_w = [a_norm, a_conv_w, a_conv_b, a_b_gate, a_lambda, kv_norm, k_norm, b_norm, b_q_norm, b_rel_bias, mlp_norm]
    small_m = [m_a_norm, m_a_conv_w, m_a_conv_b, m_a_b_gate, m_a_lambda, m_kv_norm, m_k_norm, m_b_norm, m_b_q_norm,
               m_b_rel_bias, m_mlp_norm]
    small_v = [v_a_norm, v_a_conv_w, v_a_conv_b, v_a_b_gate, v_a_lambda, v_kv_norm, v_k_norm, v_b_norm, v_b_q_norm,
               v_b_rel_bias, v_mlp_norm]
    pr = 72
    res_small = _adamw(_pack(small_w, pr), _pack(small_m, pr), _pack(small_v, pr), _pack(g_small, pr)[None],
                       "adamw_small")
    small_shapes = [w.shape for w in small_w]
    res_small = [_unpack(r.reshape(-1), small_shapes) for r in res_small]

    def assemble(t):
        b = [r[t] for r in res]
        s_ = res_small[t]
        return [s_[0], b[0][None], s_[1], s_[2], b[9].reshape(a_w_gate.shape), s_[3], s_[4], b[1][None],
                s_[5], b[2], s_[6], s_[7], b[3][None], s_[8], s_[9], b[4][None], s_[10],
                jnp.stack([b[5], b[6]]), jnp.stack([b[7], b[8]])]

    return tuple([loss, gx[None]] + assemble(0) + assemble(1) + assemble(2) + assemble(3))
```
